```python
import jax, jax.numpy as jnp
from jax import lax
import numpy as np

D_MODEL = 2048
BATCH = 4
SEQ = 8192
DEPTH = 1

GLA_HEADS = 4
GLA_DK = 256
GLA_DV = 512
GLA_GATE_RANK = 16
GLA_TAU = 16.0
GLA_CHUNK = 64
GLA_QK = GLA_HEADS * GLA_DK
GLA_VW = GLA_HEADS * GLA_DV

DIL_PATTERNS = ((128, 1), (512, 4), (2048, 16))
DIL_GROUPS = len(DIL_PATTERNS)
DIL_HEADS = 8
DIL_HEAD_DIM = 128
DIL_BLOCK = 128
DIL_W = DIL_GROUPS * DIL_HEADS * DIL_HEAD_DIM
DIL_OUT = DIL_HEADS * DIL_HEAD_DIM
ROPE_THETA = 10000.0

IN_SPLITS = (GLA_QK, GLA_QK, GLA_VW, GLA_VW, GLA_GATE_RANK, DIL_W, DIL_W, DIL_W, D_MODEL, D_MODEL)
IN_WIDTH = 2 * GLA_QK + 2 * GLA_VW + GLA_GATE_RANK + 3 * DIL_W + 2 * D_MODEL
VALUE_SEGMENTS = (2, 7)

D_FF = 5632
N_MOD = 9
LN_EPS = 1e-5
DN_ALPHA = (2.0 * DEPTH) ** 0.25
DN_BETA = (8.0 * DEPTH) ** -0.25

kernel_name = "hybrid_gla_dilated_macaron_deepnorm_adaln"


def layer_norm(x, g, b):
    xf = x.astype(jnp.float32)
    mu = jnp.mean(xf, -1, keepdims=True)
    var = jnp.mean(jnp.square(xf - mu), -1, keepdims=True)
    return ((xf - mu) * lax.rsqrt(var + LN_EPS) * g + b).astype(x.dtype)


def modulate(x, shift, scale):
    return x * (1.0 + scale[:, None, :]) + shift[:, None, :]


def swiglu(h, w_gu, w_down):
    gate, up = jnp.split(h @ w_gu, 2, axis=-1)
    return (jax.nn.silu(gate) * up) @ w_down


def rope(t, positions):
    half = t.shape[-1] // 2
    freq = ROPE_THETA ** (-jnp.arange(half, dtype=jnp.float32) / half)
    ang = positions.astype(jnp.float32)[..., None] * freq
    cos = jnp.cos(ang)[:, :, None, :]
    sin = jnp.sin(ang)[:, :, None, :]
    t1, t2 = t[..., :half].astype(jnp.float32), t[..., half:].astype(jnp.float32)
    return jnp.concatenate([t1 * cos - t2 * sin, t2 * cos + t1 * sin], -1).astype(t.dtype)


def gla_chunked(q, k, v, log_a):
    B, S, H, Dk = q.shape
    Dv = v.shape[-1]
    C = GLA_CHUNK
    nc = S // C

    def to_chunks(t):
        return t.astype(jnp.float32).reshape(B, nc, C, H, -1).transpose(1, 0, 3, 2, 4)

    qc, kc, vc, ac = to_chunks(q), to_chunks(k), to_chunks(v), to_chunks(log_a)
    b = jnp.cumsum(ac, axis=3)
    b_last = b[:, :, :, -1:, :]
    q_dec = qc * jnp.exp(b) * (Dk ** -0.5)
    k_inv = kc * jnp.exp(-b)
    k_end = kc * jnp.exp(b_last - b)
    causal = jnp.tril(jnp.ones((C, C), dtype=bool))
    a_intra = jnp.where(causal, jnp.einsum('nbhcd,nbhsd->nbhcs', q_dec, k_inv), 0.0)
    o_intra = jnp.einsum('nbhcs,nbhse->nbhce', a_intra, vc)
    decay = jnp.exp(b_last[:, :, :, 0, :])

    def step(state, inp):
        q_i, k_i, v_i, d_i = inp
        o_i = jnp.einsum('bhcd,bhde->bhce', q_i, state)
        state = d_i[..., None] * state + jnp.einsum('bhcd,bhce->bhde', k_i, v_i)
        return state, o_i

    state0 = jnp.zeros((B, H, Dk, Dv), jnp.float32)
    _, o_inter = lax.scan(step, state0, (q_dec, k_end, vc, decay))
    return (o_intra + o_inter).transpose(1, 0, 3, 2, 4).reshape(B, S, H, Dv)


def dilated_window_attention(q, k, v, window, dilation):
    B, S, H, Dh = q.shape
    r = dilation
    L = S // r
    W = window // r
    nb = -(-L // DIL_BLOCK)
    Lp = nb * DIL_BLOCK

    def strided_blocks(t):
        t = t.reshape(B, L, r, H, Dh).transpose(0, 2, 1, 3, 4)
        t = jnp.pad(t, ((0, 0), (0, 0), (0, Lp - L), (0, 0), (0, 0)))
        return t.reshape(B, r, nb, DIL_BLOCK, H, Dh)

    def with_previous_block(t):
        prev = jnp.pad(t, ((0, 0), (0, 0), (1, 0), (0, 0), (0, 0), (0, 0)))[:, :, :-1]
        return jnp.concatenate([prev, t], axis=3)

    qb = strided_blocks(q)
    kk = with_previous_block(strided_blocks(k))
    vv = with_previous_block(strided_blocks(v))
    s = jnp.einsum('brnqhd,brnkhd->brnhqk', qb, kk).astype(jnp.float32) * (Dh ** -0.5)
    qi = jnp.arange(DIL_BLOCK)[:, None]
    kj = jnp.arange(2 * DIL_BLOCK)[None, :]
    rel = qi + DIL_BLOCK - kj
    band = (rel >= 0) & (rel <= W)
    after_start = (jnp.arange(nb)[:, None, None] > 0) | (kj >= DIL_BLOCK)[None]
    mask = band[None] & after_start
    s = jnp.where(mask[None, None, :, None], s, -jnp.inf)
    m = jnp.max(s, -1, keepdims=True)
    p = jnp.exp(s - m)
    den = jnp.sum(p, -1, keepdims=True)
    o = jnp.einsum('brnhqk,brnkhd->brnhqd', p, vv.astype(jnp.float32)) / den
    lse = (m + jnp.log(den))[..., 0]
    o = o.transpose(0, 1, 2, 4, 3, 5).reshape(B, r, Lp, H, Dh)[:, :, :L]
    o = o.transpose(0, 2, 1, 3, 4).reshape(B, S, H, Dh)
    lse = lse.transpose(0, 1, 2, 4, 3).reshape(B, r, Lp, H)[:, :, :L]
    lse = lse.transpose(0, 2, 1, 3).reshape(B, S, H)
    return o, lse


def hybrid_mixer(h, positions, w_in, w_alpha2, b_alpha, gla_norm_g, w_branch_a, w_branch_b, w_out):
    B, S, _ = h.shape
    split_points = [int(p) for p in np.cumsum(IN_SPLITS)[:-1]]
    (gq, gk, gv, gr, g_lr, dq, dk, dv, gate_a, gate_b) = jnp.split(h @ w_in, split_points, axis=-1)

    gate_logits = (g_lr @ w_alpha2 + b_alpha).astype(jnp.float32)
    log_a = jax.nn.log_sigmoid(gate_logits) / GLA_TAU
    o_a = gla_chunked(gq.reshape(B, S, GLA_HEADS, GLA_DK),
                      gk.reshape(B, S, GLA_HEADS, GLA_DK),
                      gv.reshape(B, S, GLA_HEADS, GLA_DV),
                      log_a.reshape(B, S, GLA_HEADS, GLA_DK))
    o_a = o_a * lax.rsqrt(jnp.mean(jnp.square(o_a), -1, keepdims=True) + LN_EPS) * gla_norm_g
    o_a = o_a.reshape(B, S, GLA_VW).astype(h.dtype) * jax.nn.silu(gr)
    y_a = o_a @ w_branch_a

    dq = rope(dq.reshape(B, S, DIL_GROUPS * DIL_HEADS, DIL_HEAD_DIM), positions)
    dk = rope(dk.reshape(B, S, DIL_GROUPS * DIL_HEADS, DIL_HEAD_DIM), positions)
    dq = dq.reshape(B, S, DIL_GROUPS, DIL_HEADS, DIL_HEAD_DIM)
    dk = dk.reshape(B, S, DIL_GROUPS, DIL_HEADS, DIL_HEAD_DIM)
    dv = dv.reshape(B, S, DIL_GROUPS, DIL_HEADS, DIL_HEAD_DIM)
    outs, lses = [], []
    for g, (window, dilation) in enumerate(DIL_PATTERNS):
        o_g, lse_g = dilated_window_attention(dq[:, :, g], dk[:, :, g], dv[:, :, g], window, dilation)
        outs.append(o_g)
        lses.append(lse_g)
    wts = jax.nn.softmax(jnp.stack(lses, 0), axis=0)
    o_b = jnp.sum(wts[..., None] * jnp.stack(outs, 0), axis=0)
    y_b = o_b.reshape(B, S, DIL_OUT).astype(h.dtype) @ w_branch_b

    merged = jax.nn.sigmoid(gate_a) * y_a + jax.nn.sigmoid(gate_b) * y_b
    return merged @ w_out


def setup_inputs(seed: int = 0) -> dict:
    key = jax.random.key(seed)
    ks = jax.random.split(key, 24)
    f32 = jnp.float32
    nrm = lambda k, shape, s: jax.random.normal(k, shape, f32) * s
    col_scale = jnp.concatenate([jnp.full((w,), DN_BETA if i in VALUE_SEGMENTS else 1.0, f32)
                                 for i, w in enumerate(IN_SPLITS)])
    return {
        "x": nrm(ks[0], (BATCH, SEQ, D_MODEL), 1.0),
        "c": nrm(ks[1], (BATCH, D_MODEL), 1.0),
        "positions": jnp.broadcast_to(jnp.arange(SEQ, dtype=jnp.int32), (BATCH, SEQ)),
        "w_ada": nrm(ks[2], (DEPTH, D_MODEL, N_MOD * D_MODEL), D_MODEL ** -0.5),
        "b_ada": nrm(ks[3], (DEPTH, N_MOD * D_MODEL), 0.02),
        "ln1_g": 1.0 + nrm(ks[4], (DEPTH, D_MODEL), 0.02),
        "ln1_b": nrm(ks[5], (DEPTH, D_MODEL), 0.02),
        "w_ffn1_gu": nrm(ks[6], (DEPTH, D_MODEL, 2 * D_FF), D_MODEL ** -0.5),
        "w_ffn1_down": nrm(ks[7], (DEPTH, D_FF, D_MODEL), DN_BETA * D_FF ** -0.5),
        "w_in": nrm(ks[8], (DEPTH, D_MODEL, IN_WIDTH), D_MODEL ** -0.5) * col_scale,
        "w_alpha2": nrm(ks[9], (DEPTH, GLA_GATE_RANK, GLA_QK), GLA_GATE_RANK ** -0.5),
        "b_alpha": nrm(ks[10], (DEPTH, GLA_QK), 0.1),
        "gla_norm_g": 1.0 + nrm(ks[11], (DEPTH, GLA_HEADS, GLA_DV), 0.02),
        "w_branch_a": nrm(ks[12], (DEPTH, GLA_VW, D_MODEL), GLA_VW ** -0.5),
        "w_branch_b": nrm(ks[13], (DEPTH, DIL_OUT, D_MODEL), DIL_OUT ** -0.5),
        "w_out": nrm(ks[14], (DEPTH, D_MODEL, D_MODEL), DN_BETA * D_MODEL ** -0.5),
        "ln2_g": 1.0 + nrm(ks[15], (DEPTH, D_MODEL), 0.02),
        "ln2_b": nrm(ks[16], (DEPTH, D_MODEL), 0.02),
        "w_ffn2_gu": nrm(ks[17], (DEPTH, D_MODEL, 2 * D_FF), D_MODEL ** -0.5),
        "w_ffn2_down": nrm(ks[18], (DEPTH, D_FF, D_MODEL), DN_BETA * D_FF ** -0.5),
        "ln3_g": 1.0 + nrm(ks[19], (DEPTH, D_MODEL), 0.02),
        "ln3_b": nrm(ks[20], (DEPTH, D_MODEL), 0.02),
    }


def reference(x, c, positions, w_ada, b_ada, ln1_g, ln1_b, w_ffn1_gu, w_ffn1_down,
              w_in, w_alpha2, b_alpha, gla_norm_g, w_branch_a, w_branch_b, w_out,
              ln2_g, ln2_b, w_ffn2_gu, w_ffn2_down, ln3_g, ln3_b):
    c_act = jax.nn.silu(c)
    for l in range(DEPTH):
        mods = c_act @ w_ada[l] + b_ada[l]
        sh1, sc1, g1, sh2, sc2, g2, sh3, sc3, g3 = jnp.split(mods, N_MOD, axis=-1)
        f1 = swiglu(modulate(x, sh1, sc1), w_ffn1_gu[l], w_ffn1_down[l])
        x = layer_norm(DN_ALPHA * x + 0.5 * g1[:, None, :] * f1, ln1_g[l], ln1_b[l])
        m = hybrid_mixer(modulate(x, sh2, sc2), positions, w_in[l], w_alpha2[l], b_alpha[l],
                         gla_norm_g[l], w_branch_a[l], w_branch_b[l], w_out[l])
        x = layer_norm(DN_ALPHA * x + g2[:, None, :] * m, ln2_g[l], ln2_b[l])
        f2 = swiglu(modulate(x, sh3, sc3), w_ffn2_gu[l], w_ffn2_down[l])
        x = layer_norm(DN_ALPHA * x + 0.5 * g3[:, None, :] * f2, ln3_g[l], ln3_b[l])
    return x
```

```python
import functools

import jax
import jax.numpy as jnp
from jax import lax
from jax.experimental import pallas as pl
from jax.experimental.pallas import tpu as pltpu

F32 = jnp.float32
BF16 = jnp.bfloat16

D_MODEL = 2048
DEPTH = 1
D_FF = 5632
N_MOD = 9
LN_EPS = 1e-5
DN_ALPHA = (2.0 * DEPTH) ** 0.25

GLA_HEADS = 4
GLA_DK = 256
GLA_DV = 512
GLA_GATE_RANK = 16
GLA_TAU = 16.0
GLA_CHUNK = 64
GLA_QK = GLA_HEADS * GLA_DK
GLA_VW = GLA_HEADS * GLA_DV

DIL_PATTERNS = ((128, 1), (512, 4), (2048, 16))
DIL_GROUPS = len(DIL_PATTERNS)
DIL_HEADS = 8
DIL_HEAD_DIM = 128
DIL_BLOCK = 128
DIL_W = DIL_GROUPS * DIL_HEADS * DIL_HEAD_DIM
DIL_OUT = DIL_HEADS * DIL_HEAD_DIM
ROPE_THETA = 10000.0

LANES = 128
LR_PAD = LANES

PROJ_W = 2 * GLA_QK + 2 * GLA_VW + 2 * D_MODEL
OFF_GQ = 0
OFF_GK = GLA_QK
OFF_GV = 2 * GLA_QK
OFF_GR = OFF_GV + GLA_VW
OFF_GA = OFF_GR + GLA_VW
OFF_GB = OFF_GA + D_MODEL
DIL_QKV = 3 * DIL_OUT

NEG_BIG = -1e30

VMEM_LIMIT = 56 * 1024 * 1024


def _cparams(sem, vmem=VMEM_LIMIT):
    return pltpu.CompilerParams(dimension_semantics=sem, vmem_limit_bytes=vmem)


def _sigmoid(x):
    return 1.0 / (1.0 + jnp.exp(-x))


def _layer_norm(y, g, b):
    mu = jnp.mean(y, axis=-1, keepdims=True)
    d = y - mu
    var = jnp.mean(d * d, axis=-1, keepdims=True)
    return d * lax.rsqrt(var + LN_EPS) * g + b


def _mods_kernel(c_ref, w_ref, b_ref, o_ref):
    c = c_ref[...]
    c_act = (c * _sigmoid(c)).astype(BF16)
    o_ref[...] = jnp.dot(c_act, w_ref[...].astype(BF16), preferred_element_type=F32) + b_ref[...]


def _mods(c_pad, w_ada, b_ada, tn=1024):
    rows, d = c_pad.shape
    n = w_ada.shape[1]
    return pl.pallas_call(
        _mods_kernel,
        grid=(n // tn,),
        in_specs=[
            pl.BlockSpec((rows, d), lambda j: (0, 0)),
            pl.BlockSpec((d, tn), lambda j: (0, j)),
            pl.BlockSpec((1, tn), lambda j: (0, j)),
        ],
        out_specs=pl.BlockSpec((rows, tn), lambda j: (0, j)),
        out_shape=jax.ShapeDtypeStruct((rows, n), F32),
        compiler_params=_cparams(("arbitrary",)),
        name="mods",
    )(c_pad, w_ada, b_ada)


def _ffn_kernel(x_ref, sh_ref, sc_ref, g_ref, wg_ref, wu_ref, wd_ref, lng_ref, lnb_ref,
                o_ref, h_scr, acc_scr):
    j = pl.program_id(1)

    @pl.when(j == 0)
    def _():
        h_scr[...] = (x_ref[...] * (1.0 + sc_ref[0]) + sh_ref[0]).astype(BF16)
        acc_scr[...] = jnp.zeros_like(acc_scr)

    h = h_scr[...]
    gate = jnp.dot(h, wg_ref[...], preferred_element_type=F32)
    up = jnp.dot(h, wu_ref[...], preferred_element_type=F32)
    act = (gate * _sigmoid(gate) * up).astype(BF16)
    acc_scr[...] += jnp.dot(act, wd_ref[...], preferred_element_type=F32)

    @pl.when(j == pl.num_programs(1) - 1)
    def _():
        y = DN_ALPHA * x_ref[...] + 0.5 * g_ref[0] * acc_scr[...]
        o_ref[...] = _layer_norm(y, lng_ref[...], lnb_ref[...])


def _ffn(x2d, mods3, k_shift, k_scale, k_gate, w_gu, w_down, ln_g, ln_b, seq, tm=512, tf=512):
    m, d = x2d.shape
    n_ff = w_down.shape[0]
    nj = n_ff // tf
    tiles_per_seq = seq // tm

    def mod_spec(k):
        return pl.BlockSpec((1, 1, d), lambda i, j: ((i // tiles_per_seq) * N_MOD + k, 0, 0))

    return pl.pallas_call(
        _ffn_kernel,
        grid=(m // tm, nj),
        in_specs=[
            pl.BlockSpec((tm, d), lambda i, j: (i, 0)),
            mod_spec(k_shift), mod_spec(k_scale), mod_spec(k_gate),
            pl.BlockSpec((d, tf), lambda i, j: (0, j)),
            pl.BlockSpec((d, tf), lambda i, j: (0, j + nj)),
            pl.BlockSpec((tf, d), lambda i, j: (j, 0)),
            pl.BlockSpec((1, d), lambda i, j: (0, 0)),
            pl.BlockSpec((1, d), lambda i, j: (0, 0)),
        ],
        out_specs=pl.BlockSpec((tm, d), lambda i, j: (i, 0)),
        out_shape=jax.ShapeDtypeStruct((m, d), F32),
        scratch_shapes=[pltpu.VMEM((tm, d), BF16), pltpu.VMEM((tm, d), F32)],
        compiler_params=_cparams(("parallel", "arbitrary")),
        name="ffn",
    )(x2d, mods3, mods3, mods3, w_gu, w_gu, w_down, ln_g, ln_b)


PROJ_TM = 1024


def _rope_kernel(pos_ref, freq_ref, *out_refs, dilations):
    ang = pos_ref[...].astype(F32) * freq_ref[...]
    lane = lax.broadcasted_iota(jnp.int32, ang.shape, 1)
    s = jnp.sin(ang)
    cos_ref, sin_ref = out_refs[0], out_refs[1]
    cos_ref[...] = jnp.cos(ang)
    sin_ref[...] = jnp.where(lane < DIL_HEAD_DIM // 2, -s, s)
    tm = ang.shape[0]
    for gi, r in enumerate(dilations[1:]):
        tl = tm // r
        for c in range(r):
            out_refs[2 + 2 * gi][c * tl:(c + 1) * tl, :] = cos_ref[pl.ds(c, tl, stride=r), :]
            out_refs[3 + 2 * gi][c * tl:(c + 1) * tl, :] = sin_ref[pl.ds(c, tl, stride=r), :]


def _rope_tables(pos_col, freq2, dilations, tm=PROJ_TM):
    assert dilations[0] == 1
    m = pos_col.shape[0]
    n_out = 2 * len(dilations)
    return pl.pallas_call(
        functools.partial(_rope_kernel, dilations=dilations),
        grid=(m // tm,),
        in_specs=[
            pl.BlockSpec((tm, 1), lambda i: (i, 0)),
            pl.BlockSpec((1, DIL_HEAD_DIM), lambda i: (0, 0)),
        ],
        out_specs=[pl.BlockSpec((tm, DIL_HEAD_DIM), lambda i: (i, 0))] * n_out,
        out_shape=[jax.ShapeDtypeStruct((m, DIL_HEAD_DIM), F32)] * n_out,
        compiler_params=_cparams(("parallel",)),
        name="rope_tables",
    )(pos_col, freq2)


def _proj_kernel(x_ref, sh_ref, sc_ref, w_ref, wlr_ref, o_ref, lr_ref, h_scr):
    @pl.when(pl.program_id(1) == 0)
    def _():
        h = (x_ref[...] * (1.0 + sc_ref[0]) + sh_ref[0]).astype(BF16)
        h_scr[...] = h
        lr_ref[...] = jnp.dot(h, wlr_ref[...], preferred_element_type=F32)

    o_ref[...] = jnp.dot(h_scr[...], w_ref[...], preferred_element_type=F32).astype(BF16)


def _proj(x2d, mods3, k_shift, k_scale, w_p, w_lr, seq, tm=PROJ_TM, tn=1024):
    m, d = x2d.shape
    n = w_p.shape[1]
    tiles_per_seq = seq // tm

    def mod_spec(k):
        return pl.BlockSpec((1, 1, d), lambda i, j: ((i // tiles_per_seq) * N_MOD + k, 0, 0))

    return pl.pallas_call(
        _proj_kernel,
        grid=(m // tm, n // tn),
        in_specs=[
            pl.BlockSpec((tm, d), lambda i, j: (i, 0)),
            mod_spec(k_shift), mod_spec(k_scale),
            pl.BlockSpec((d, tn), lambda i, j: (0, j)),
            pl.BlockSpec((d, LR_PAD), lambda i, j: (0, 0)),
        ],
        out_specs=[
            pl.BlockSpec((tm, tn), lambda i, j: (i, j)),
            pl.BlockSpec((tm, LR_PAD), lambda i, j: (i, 0)),
        ],
        out_shape=[
            jax.ShapeDtypeStruct((m, n), BF16),
            jax.ShapeDtypeStruct((m, LR_PAD), F32),
        ],
        scratch_shapes=[pltpu.VMEM((tm, d), BF16)],
        compiler_params=_cparams(("parallel", "arbitrary")),
        name="proj",
    )(x2d, mods3, mods3, w_p, w_lr)


def _proj_dil_kernel(x_ref, sh_ref, sc_ref, w_ref, cos_ref, sin_ref, o_ref, h_scr, slab_scr, *, r):
    j = pl.program_id(1)
    tm, d = x_ref.shape
    tl = tm // r

    @pl.when(j == 0)
    def _():
        if r == 1:
            h_scr[...] = (x_ref[...] * (1.0 + sc_ref[0]) + sh_ref[0]).astype(BF16)
        else:
            for k in range(d // LANES):
                sl = slice(k * LANES, (k + 1) * LANES)
                slab_scr[k] = x_ref[:, sl] * (1.0 + sc_ref[0, :, sl]) + sh_ref[0, :, sl]
            for c in range(r):
                for k in range(d // LANES):
                    sl = slice(k * LANES, (k + 1) * LANES)
                    h_scr[c * tl:(c + 1) * tl, sl] = slab_scr[k, pl.ds(c, tl, stride=r), :].astype(BF16)

    t = jnp.dot(h_scr[...], w_ref[...], preferred_element_type=F32)
    n_slab = t.shape[1] // DIL_HEAD_DIM

    @pl.when(j == 2)
    def _():
        for c in range(r):
            o_ref[0, c] = t[c * tl:(c + 1) * tl, :].astype(BF16)

    @pl.when(j < 2)
    def _():
        scale = jnp.where(j == 0, DIL_HEAD_DIM ** -0.5, 1.0).astype(F32)
        cos = cos_ref[...] * scale
        sin = sin_ref[...] * scale
        for s in range(n_slab):
            sl = slice(s * DIL_HEAD_DIM, (s + 1) * DIL_HEAD_DIM)
            ts = t[:, sl]
            rot = (ts * cos + pltpu.roll(ts, DIL_HEAD_DIM // 2, 1) * sin).astype(BF16)
            for c in range(r):
                o_ref[0, c, :, sl] = rot[c * tl:(c + 1) * tl, :]


def _proj_dil(x2d, mods3, k_shift, k_scale, w_g, cos_t, sin_t, bsz, seq, r, tm=PROJ_TM):
    m, d = x2d.shape
    tiles_per_seq = seq // tm
    tl = tm // r

    def mod_spec(k):
        return pl.BlockSpec((1, 1, d), lambda i, j: ((i // tiles_per_seq) * N_MOD + k, 0, 0))

    return pl.pallas_call(
        functools.partial(_proj_dil_kernel, r=r),
        grid=(m // tm, 3),
        in_specs=[
            pl.BlockSpec((tm, d), lambda i, j: (i, 0)),
            mod_spec(k_shift), mod_spec(k_scale),
            pl.BlockSpec((d, DIL_OUT), lambda i, j: (0, j)),
            pl.BlockSpec((tm, DIL_HEAD_DIM), lambda i, j: (i, 0)),
            pl.BlockSpec((tm, DIL_HEAD_DIM), lambda i, j: (i, 0)),
        ],
        out_specs=pl.BlockSpec((1, r, tl, DIL_OUT),
                               lambda i, j: (i // tiles_per_seq, 0, i % tiles_per_seq, j)),
        out_shape=jax.ShapeDtypeStruct((bsz, r, seq // r, DIL_QKV), BF16),
        scratch_shapes=[pltpu.VMEM((tm, d), BF16),
                        pltpu.VMEM((d // LANES, tm, LANES) if r > 1 else (1, 8, LANES), F32)],
        compiler_params=_cparams(("parallel", "arbitrary")),
        name=f"proj_dil_r{r}",
    )(x2d, mods3, mods3, w_g, cos_t, sin_t)


def _gla_kernel(q_ref, k_ref, v_ref, r_ref, lr_ref, wa_ref, ba_ref, ng_ref, o_ref, st_scr,
                *, n_chunks):
    c_len = GLA_CHUNK

    @pl.when(pl.program_id(2) == 0)
    def _():
        st_scr[...] = jnp.zeros_like(st_scr)

    row = lax.broadcasted_iota(jnp.int32, (c_len, c_len), 0)
    col = lax.broadcasted_iota(jnp.int32, (c_len, c_len), 1)
    causal = row >= col
    tril = causal.astype(F32)

    def chunk(c, carry):
        r0 = pl.multiple_of(c * c_len, c_len)
        rows = pl.ds(r0, c_len)
        q = q_ref[0, rows, :].astype(F32)
        k = k_ref[0, rows, :].astype(F32)
        v = v_ref[0, rows, :]
        lr = lr_ref[0, rows, :].astype(BF16)
        logits = jnp.dot(lr, wa_ref[...], preferred_element_type=F32) + ba_ref[...]
        log_a = (jnp.minimum(logits, 0.0) - jnp.log(1.0 + jnp.exp(-jnp.abs(logits)))) / GLA_TAU
        b = jnp.dot(tril, log_a, preferred_element_type=F32, precision=lax.Precision.HIGHEST)
        b_last = b[c_len - 1:c_len, :]
        q_dec = (q * jnp.exp(b) * (GLA_DK ** -0.5)).astype(BF16)
        k_inv = (k * jnp.exp(-b)).astype(BF16)
        k_end = (k * jnp.exp(b_last - b)).astype(BF16)
        a = lax.dot_general(q_dec, k_inv, (((1,), (1,)), ((), ())), preferred_element_type=F32)
        a = jnp.where(causal, a, 0.0).astype(BF16)
        st = st_scr[...]
        o = jnp.dot(a, v, preferred_element_type=F32)
        o += lax.dot_general(q_dec, st.astype(BF16), (((1,), (1,)), ((), ())),
                             preferred_element_type=F32)
        st_scr[...] = st * jnp.exp(b_last) + lax.dot_general(
            v, k_end, (((0,), (0,)), ((), ())), preferred_element_type=F32)
        o = o * lax.rsqrt(jnp.mean(o * o, axis=-1, keepdims=True) + LN_EPS) * ng_ref[0]
        r = r_ref[0, rows, :].astype(F32)
        o_ref[0, rows, :] = (o * (r * _sigmoid(r))).astype(BF16)
        return carry

    lax.fori_loop(0, n_chunks, chunk, 0)


def _gla(p3, lr3, w_a2, b_a, norm_g3, tc=512):
    bsz, seq, _ = p3.shape
    nq, nv = OFF_GK // GLA_DK, OFF_GV // GLA_DV
    nr = OFF_GR // GLA_DV
    return pl.pallas_call(
        functools.partial(_gla_kernel, n_chunks=tc // GLA_CHUNK),
        grid=(bsz, GLA_HEADS, seq // tc),
        in_specs=[
            pl.BlockSpec((1, tc, GLA_DK), lambda b, h, t: (b, t, h)),
            pl.BlockSpec((1, tc, GLA_DK), lambda b, h, t: (b, t, nq + h)),
            pl.BlockSpec((1, tc, GLA_DV), lambda b, h, t: (b, t, nv + h)),
            pl.BlockSpec((1, tc, GLA_DV), lambda b, h, t: (b, t, nr + h)),
            pl.BlockSpec((1, tc, LR_PAD), lambda b, h, t: (b, t, 0)),
            pl.BlockSpec((LR_PAD, GLA_DK), lambda b, h, t: (0, h)),
            pl.BlockSpec((1, GLA_DK), lambda b, h, t: (0, h)),
            pl.BlockSpec((1, 1, GLA_DV), lambda b, h, t: (h, 0, 0)),
        ],
        out_specs=pl.BlockSpec((1, tc, GLA_DV), lambda b, h, t: (b, t, h)),
        out_shape=jax.ShapeDtypeStruct((bsz, seq, GLA_VW), BF16),
        scratch_shapes=[pltpu.VMEM((GLA_DV, GLA_DK), F32)],
        compiler_params=_cparams(("parallel", "parallel", "arbitrary")),
        name="gla",
    )(p3, p3, p3, p3, lr3, w_a2, b_a, norm_g3)


def _dil_kernel(q_ref, k_ref, v_ref, o_ref, lse_ref, kp_scr, vp_scr):
    n = pl.program_id(2)

    @pl.when(n == 0)
    def _():
        kp_scr[...] = jnp.zeros_like(kp_scr)
        vp_scr[...] = jnp.zeros_like(vp_scr)

    blk = DIL_BLOCK
    qi = lax.broadcasted_iota(jnp.int32, (blk, blk), 0)
    kj = lax.broadcasted_iota(jnp.int32, (blk, blk), 1)
    mask_cur = kj <= qi
    mask_prev = jnp.logical_and(kj >= qi, n > 0)
    lane = lax.broadcasted_iota(jnp.int32, (blk, LANES), 1)
    lse_tile = jnp.zeros((blk, LANES), F32)
    nt = (((1,), (1,)), ((), ()))
    for h in range(DIL_HEADS):
        sl = slice(h * DIL_HEAD_DIM, (h + 1) * DIL_HEAD_DIM)
        q = q_ref[0, 0, :, sl]
        s_c = lax.dot_general(q, k_ref[0, 0, :, sl], nt, preferred_element_type=F32)
        s_p = lax.dot_general(q, kp_scr[:, sl], nt, preferred_element_type=F32)
        s_c = jnp.where(mask_cur, s_c, NEG_BIG)
        s_p = jnp.where(mask_prev, s_p, NEG_BIG)
        m = jnp.maximum(jnp.max(s_c, axis=-1, keepdims=True), jnp.max(s_p, axis=-1, keepdims=True))
        p_c = jnp.exp(s_c - m)
        p_p = jnp.exp(s_p - m)
        den = jnp.sum(p_c, axis=-1, keepdims=True) + jnp.sum(p_p, axis=-1, keepdims=True)
        acc = jnp.dot(p_c.astype(BF16), v_ref[0, 0, :, sl], preferred_element_type=F32)
        acc += jnp.dot(p_p.astype(BF16), vp_scr[:, sl], preferred_element_type=F32)
        o_ref[0, 0, :, sl] = (acc / den).astype(BF16)
        lse_tile = jnp.where(lane == h, m + jnp.log(den), lse_tile)
    lse_ref[0, 0] = lse_tile
    kp_scr[...] = k_ref[0, 0]
    vp_scr[...] = v_ref[0, 0]


def _dil_group(qkv):
    bsz, r, length, _ = qkv.shape
    nb = length // DIL_BLOCK

    def in_spec(u):
        return pl.BlockSpec((1, 1, DIL_BLOCK, DIL_OUT), lambda b, c, n: (b, c, n, u))

    return pl.pallas_call(
        _dil_kernel,
        grid=(bsz, r, nb),
        in_specs=[in_spec(0), in_spec(1), in_spec(2)],
        out_specs=[
            pl.BlockSpec((1, 1, DIL_BLOCK, DIL_OUT), lambda b, c, n: (b, c, n, 0)),
            pl.BlockSpec((1, 1, DIL_BLOCK, LANES), lambda b, c, n: (b, c, n, 0)),
        ],
        out_shape=[
            jax.ShapeDtypeStruct((bsz, r, length, DIL_OUT), BF16),
            jax.ShapeDtypeStruct((bsz, r, length, LANES), F32),
        ],
        scratch_shapes=[pltpu.VMEM((DIL_BLOCK, DIL_OUT), BF16),
                        pltpu.VMEM((DIL_BLOCK, DIL_OUT), BF16)],
        compiler_params=_cparams(("parallel", "parallel", "arbitrary")),
        name=f"dilattn_r{r}",
    )(qkv, qkv, qkv)


def _merge_kernel(oa_ref, o1_ref, o2_ref, o3_ref, l1_ref, l2_ref, l3_ref, ga_ref, gb_ref, x_ref,
                  g2_ref, wa_ref, wb_ref, wo_ref, lng_ref, lnb_ref, out_ref, l_scr, w_scr, ob_scr,
                  *, dilations):
    y_a = jnp.dot(oa_ref[...], wa_ref[...], preferred_element_type=F32)
    o_refs = (o1_ref, o2_ref, o3_ref)
    l_refs = (l1_ref, l2_ref, l3_ref)
    tm = oa_ref.shape[0]

    def tok_rows(c, r):
        return pl.ds(c, tm // r, stride=r) if r > 1 else slice(None)

    for g, r in enumerate(dilations):
        for c in range(r):
            l_scr[g, tok_rows(c, r), :] = l_refs[g][0, c]
    l1, l2, l3 = l_scr[0], l_scr[1], l_scr[2]
    mx = jnp.maximum(jnp.maximum(l1, l2), l3)
    e1, e2, e3 = jnp.exp(l1 - mx), jnp.exp(l2 - mx), jnp.exp(l3 - mx)
    inv = 1.0 / (e1 + e2 + e3)
    w_scr[0], w_scr[1], w_scr[2] = e1 * inv, e2 * inv, e3 * inv
    for g, r in enumerate(dilations):
        for c in range(r):
            rows = tok_rows(c, r)
            wr = w_scr[g, rows, :]
            for h in range(DIL_HEADS):
                sl = slice(h * DIL_HEAD_DIM, (h + 1) * DIL_HEAD_DIM)
                part = wr[:, h:h + 1] * o_refs[g][0, c, :, sl].astype(F32)
                if g == 0:
                    ob_scr[h, rows, :] = part
                else:
                    ob_scr[h, rows, :] += part
    o_b = jnp.concatenate([ob_scr[h].astype(BF16) for h in range(DIL_HEADS)], axis=1)
    y_b = jnp.dot(o_b, wb_ref[...], preferred_element_type=F32)
    merged = _sigmoid(ga_ref[...].astype(F32)) * y_a + _sigmoid(gb_ref[...].astype(F32)) * y_b
    mix = jnp.dot(merged.astype(BF16), wo_ref[...], preferred_element_type=F32)
    y = DN_ALPHA * x_ref[...] + g2_ref[0] * mix
    out_ref[...] = _layer_norm(y, lng_ref[...], lnb_ref[...])


def _merge(oa2d, o_groups, lse_groups, p2d, x2d, mods3, k_gate, w_a, w_b, w_o, ln_g, ln_b, seq,
           dilations, tm=256):
    m, d = x2d.shape
    tiles_per_seq = seq // tm
    resident = pl.Buffered(1)
    n_g = len(dilations)

    def row_spec(width, col=0):
        return pl.BlockSpec((tm, width), lambda i: (i, col))

    def class_spec(r, width):
        return pl.BlockSpec((1, r, tm // r, width),
                            lambda i: (i // tiles_per_seq, 0, i % tiles_per_seq, 0))

    def const_spec(shape):
        return pl.BlockSpec(shape, lambda i: (0, 0), pipeline_mode=resident)

    return pl.pallas_call(
        functools.partial(_merge_kernel, dilations=dilations),
        grid=(m // tm,),
        in_specs=[
            row_spec(GLA_VW),
            *[class_spec(r, DIL_OUT) for r in dilations],
            *[class_spec(r, LANES) for r in dilations],
            row_spec(D_MODEL, OFF_GA // D_MODEL), row_spec(D_MODEL, OFF_GB // D_MODEL),
            row_spec(d),
            pl.BlockSpec((1, 1, d), lambda i: ((i // tiles_per_seq) * N_MOD + k_gate, 0, 0)),
            const_spec(w_a.shape), const_spec(w_b.shape), const_spec(w_o.shape),
            const_spec((1, d)), const_spec((1, d)),
        ],
        out_specs=row_spec(d),
        out_shape=jax.ShapeDtypeStruct((m, d), F32),
        scratch_shapes=[pltpu.VMEM((n_g, tm, LANES), F32), pltpu.VMEM((n_g, tm, LANES), F32),
                        pltpu.VMEM((DIL_HEADS, tm, DIL_HEAD_DIM), F32)],
        compiler_params=_cparams(("parallel",)),
        name="merge",
    )(oa2d, *o_groups, *lse_groups, p2d, p2d, x2d, mods3, w_a, w_b, w_o, ln_g, ln_b)


def _pack_w_in(w_in):
    o = 0
    segs = []
    for wdt in (GLA_QK, GLA_QK, GLA_VW, GLA_VW, GLA_GATE_RANK, DIL_W, DIL_W, DIL_W, D_MODEL, D_MODEL):
        segs.append(w_in[:, o:o + wdt])
        o += wdt
    gq, gk, gv, gr, glr, dq, dk, dv, ga, gb = segs
    w_p = jnp.concatenate([gq, gk, gv, gr, ga, gb], axis=1).astype(BF16)
    w_lr = jnp.pad(glr, ((0, 0), (0, LR_PAD - GLA_GATE_RANK))).astype(BF16)
    w_groups = []
    for g in range(DIL_GROUPS):
        sl = slice(g * DIL_OUT, (g + 1) * DIL_OUT)
        w_groups.append(jnp.concatenate([dq[:, sl], dk[:, sl], dv[:, sl]], axis=1).astype(BF16))
    return w_p, w_lr, w_groups


def kernel(x, c, positions, w_ada, b_ada, ln1_g, ln1_b, w_ffn1_gu, w_ffn1_down, w_in, w_alpha2,
           b_alpha, gla_norm_g, w_branch_a, w_branch_b, w_out, ln2_g, ln2_b, w_ffn2_gu, w_ffn2_down,
           ln3_g, ln3_b):
    bsz, seq, d = x.shape
    m = bsz * seq
    x2d = x.reshape(m, d)
    c_pad = jnp.pad(c, ((0, 8 - bsz % 8 if bsz % 8 else 0), (0, 0)))
    dilations = tuple(r for _, r in DIL_PATTERNS)

    half = DIL_HEAD_DIM // 2
    freq = ROPE_THETA ** (-jnp.arange(half, dtype=F32) / half)
    freq2 = jnp.concatenate([freq, freq]).reshape(1, DIL_HEAD_DIM)
    rope_t = _rope_tables(positions.reshape(m, 1), freq2, dilations)

    for l in range(DEPTH):
        mods = _mods(c_pad, w_ada[l], b_ada[l].reshape(1, -1))[:bsz]
        mods3 = mods.reshape(bsz * N_MOD, 1, d)

        x1 = _ffn(x2d, mods3, 0, 1, 2, w_ffn1_gu[l].astype(BF16), w_ffn1_down[l].astype(BF16),
                  ln1_g[l].reshape(1, d), ln1_b[l].reshape(1, d), seq)

        w_p, w_lr, w_groups = _pack_w_in(w_in[l])
        p2d, lr2d = _proj(x1, mods3, 3, 4, w_p, w_lr, seq)
        p3 = p2d.reshape(bsz, seq, PROJ_W)

        w_a2 = jnp.pad(w_alpha2[l], ((0, LR_PAD - GLA_GATE_RANK), (0, 0))).astype(BF16)
        o_a = _gla(p3, lr2d.reshape(bsz, seq, LR_PAD), w_a2, b_alpha[l].reshape(1, GLA_QK),
                   gla_norm_g[l].reshape(GLA_HEADS, 1, GLA_DV))

        o_groups, lse_groups = [], []
        for g, r in enumerate(dilations):
            qkv = _proj_dil(x1, mods3, 3, 4, w_groups[g], rope_t[2 * g], rope_t[2 * g + 1],
                            bsz, seq, r)
            o_g, lse_g = _dil_group(qkv)
            o_groups.append(o_g)
            lse_groups.append(lse_g)

        x2 = _merge(o_a.reshape(m, GLA_VW), o_groups, lse_groups, p2d, x1, mods3, 5,
                    w_branch_a[l].astype(BF16), w_branch_b[l].astype(BF16), w_out[l].astype(BF16),
                    ln2_g[l].reshape(1, d), ln2_b[l].reshape(1, d), seq, dilations)

        x2d = _ffn(x2, mods3, 6, 7, 8, w_ffn2_gu[l].astype(BF16), w_ffn2_down[l].astype(BF16),
                   ln3_g[l].reshape(1, d), ln3_b[l].reshape(1, d), seq)
    return x2d.reshape(bsz, seq, d)
```

```python
import functools

import jax
import jax.numpy as jnp
from jax import lax
from jax.experimental import pallas as pl
from jax.experimental.pallas import tpu as pltpu

F32 = jnp.float32
BF16 = jnp.bfloat16

D_MODEL = 2048
DEPTH = 1
D_FF = 5632
N_MOD = 9
LN_EPS = 1e-5
DN_ALPHA = (2.0 * DEPTH) ** 0.25

GLA_HEADS = 4
GLA_DK = 256
GLA_DV = 512
GLA_GATE_RANK = 16
GLA_TAU = 16.0
GLA_CHUNK = 64
GLA_QK = GLA_HEADS * GLA_DK
GLA_VW = GLA_HEADS * GLA_DV

DIL_PATTERNS = ((128, 1), (512, 4), (2048, 16))
DIL_GROUPS = len(DIL_PATTERNS)
DIL_HEADS = 8
DIL_HEAD_DIM = 128
DIL_BLOCK = 128
DIL_W = DIL_GROUPS * DIL_HEADS * DIL_HEAD_DIM
DIL_OUT = DIL_HEADS * DIL_HEAD_DIM
ROPE_THETA = 10000.0

LANES = 128
LR_PAD = LANES

PROJ_W = 2 * GLA_QK + 2 * GLA_VW + 2 * D_MODEL
OFF_GQ = 0
OFF_GK = GLA_QK
OFF_GV = 2 * GLA_QK
OFF_GR = OFF_GV + GLA_VW
OFF_GA = OFF_GR + GLA_VW
OFF_GB = OFF_GA + D_MODEL
DIL_QKV = 3 * DIL_OUT

NEG_BIG = -1e30

VMEM_LIMIT = 56 * 1024 * 1024


def _cparams(sem, vmem=VMEM_LIMIT):
    return pltpu.CompilerParams(dimension_semantics=sem, vmem_limit_bytes=vmem)


def _sigmoid(x):
    return 1.0 / (1.0 + jnp.exp(-x))


def _layer_norm(y, g, b):
    mu = jnp.mean(y, axis=-1, keepdims=True)
    d = y - mu
    var = jnp.mean(d * d, axis=-1, keepdims=True)
    return d * lax.rsqrt(var + LN_EPS) * g + b


def _mods_kernel(c_ref, w_ref, b_ref, o_ref):
    c = c_ref[...]
    c_act = (c * _sigmoid(c)).astype(BF16)
    o_ref[...] = jnp.dot(c_act, w_ref[...].astype(BF16), preferred_element_type=F32) + b_ref[...]


def _mods(c_pad, w_ada, b_ada, tn=1024):
    rows, d = c_pad.shape
    n = w_ada.shape[1]
    return pl.pallas_call(
        _mods_kernel,
        grid=(n // tn,),
        in_specs=[
            pl.BlockSpec((rows, d), lambda j: (0, 0)),
            pl.BlockSpec((d, tn), lambda j: (0, j)),
            pl.BlockSpec((1, tn), lambda j: (0, j)),
        ],
        out_specs=pl.BlockSpec((rows, tn), lambda j: (0, j)),
        out_shape=jax.ShapeDtypeStruct((rows, n), F32),
        compiler_params=_cparams(("arbitrary",)),
        name="mods",
    )(c_pad, w_ada, b_ada)


def _ffn_kernel(x_ref, sh_ref, sc_ref, g_ref, wg_ref, wu_ref, wd_ref, lng_ref, lnb_ref,
                o_ref, h_scr, acc_scr):
    j = pl.program_id(1)

    @pl.when(j == 0)
    def _():
        h_scr[...] = (x_ref[...] * (1.0 + sc_ref[0]) + sh_ref[0]).astype(BF16)
        acc_scr[...] = jnp.zeros_like(acc_scr)

    h = h_scr[...]
    gate = jnp.dot(h, wg_ref[...], preferred_element_type=F32)
    up = jnp.dot(h, wu_ref[...], preferred_element_type=F32)
    act = (gate * _sigmoid(gate) * up).astype(BF16)
    acc_scr[...] += jnp.dot(act, wd_ref[...], preferred_element_type=F32)

    @pl.when(j == pl.num_programs(1) - 1)
    def _():
        y = DN_ALPHA * x_ref[...] + 0.5 * g_ref[0] * acc_scr[...]
        o_ref[...] = _layer_norm(y, lng_ref[...], lnb_ref[...])


def _ffn(x2d, mods3, k_shift, k_scale, k_gate, w_gu, w_down, ln_g, ln_b, seq, tm=512, tf=512):
    m, d = x2d.shape
    n_ff = w_down.shape[0]
    nj = n_ff // tf
    tiles_per_seq = seq // tm

    def mod_spec(k):
        return pl.BlockSpec((1, 1, d), lambda i, j: ((i // tiles_per_seq) * N_MOD + k, 0, 0))

    return pl.pallas_call(
        _ffn_kernel,
        grid=(m // tm, nj),
        in_specs=[
            pl.BlockSpec((tm, d), lambda i, j: (i, 0)),
            mod_spec(k_shift), mod_spec(k_scale), mod_spec(k_gate),
            pl.BlockSpec((d, tf), lambda i, j: (0, j)),
            pl.BlockSpec((d, tf), lambda i, j: (0, j + nj)),
            pl.BlockSpec((tf, d), lambda i, j: (j, 0)),
            pl.BlockSpec((1, d), lambda i, j: (0, 0)),
            pl.BlockSpec((1, d), lambda i, j: (0, 0)),
        ],
        out_specs=pl.BlockSpec((tm, d), lambda i, j: (i, 0)),
        out_shape=jax.ShapeDtypeStruct((m, d), F32),
        scratch_shapes=[pltpu.VMEM((tm, d), BF16), pltpu.VMEM((tm, d), F32)],
        compiler_params=_cparams(("parallel", "arbitrary")),
        name="ffn",
    )(x2d, mods3, mods3, mods3, w_gu, w_gu, w_down, ln_g, ln_b)


PROJ_TM = 1024


def _rope_kernel(pos_ref, freq_ref, *out_refs, dilations):
    ang = pos_ref[...].astype(F32) * freq_ref[...]
    lane = lax.broadcasted_iota(jnp.int32, ang.shape, 1)
    s = jnp.sin(ang)
    cos_ref, sin_ref = out_refs[0], out_refs[1]
    cos_ref[...] = jnp.cos(ang)
    sin_ref[...] = jnp.where(lane < DIL_HEAD_DIM // 2, -s, s)
    tm = ang.shape[0]
    for gi, r in enumerate(dilations[1:]):
        tl = tm // r
        for c in range(r):
            out_refs[2 + 2 * gi][c * tl:(c + 1) * tl, :] = cos_ref[pl.ds(c, tl, stride=r), :]
            out_refs[3 + 2 * gi][c * tl:(c + 1) * tl, :] = sin_ref[pl.ds(c, tl, stride=r), :]


def _rope_tables(pos_col, freq2, dilations, tm=PROJ_TM):
    assert dilations[0] == 1
    m = pos_col.shape[0]
    n_out = 2 * len(dilations)
    return pl.pallas_call(
        functools.partial(_rope_kernel, dilations=dilations),
        grid=(m // tm,),
        in_specs=[
            pl.BlockSpec((tm, 1), lambda i: (i, 0)),
            pl.BlockSpec((1, DIL_HEAD_DIM), lambda i: (0, 0)),
        ],
        out_specs=[pl.BlockSpec((tm, DIL_HEAD_DIM), lambda i: (i, 0))] * n_out,
        out_shape=[jax.ShapeDtypeStruct((m, DIL_HEAD_DIM), F32)] * n_out,
        compiler_params=_cparams(("parallel",)),
        name="rope_tables",
    )(pos_col, freq2)


def _proj_kernel(x_ref, sh_ref, sc_ref, w_ref, wlr_ref, o_ref, lr_ref, h_scr):
    @pl.when(pl.program_id(1) == 0)
    def _():
        h = (x_ref[...] * (1.0 + sc_ref[0]) + sh_ref[0]).astype(BF16)
        h_scr[...] = h
        lr_ref[...] = jnp.dot(h, wlr_ref[...], preferred_element_type=F32)

    o_ref[...] = jnp.dot(h_scr[...], w_ref[...], preferred_element_type=F32).astype(BF16)


def _proj(x2d, mods3, k_shift, k_scale, w_p, w_lr, seq, tm=PROJ_TM, tn=1024):
    m, d = x2d.shape
    n = w_p.shape[1]
    tiles_per_seq = seq // tm

    def mod_spec(k):
        return pl.BlockSpec((1, 1, d), lambda i, j: ((i // tiles_per_seq) * N_MOD + k, 0, 0))

    return pl.pallas_call(
        _proj_kernel,
        grid=(m // tm, n // tn),
        in_specs=[
            pl.BlockSpec((tm, d), lambda i, j: (i, 0)),
            mod_spec(k_shift), mod_spec(k_scale),
            pl.BlockSpec((d, tn), lambda i, j: (0, j)),
            pl.BlockSpec((d, LR_PAD), lambda i, j: (0, 0)),
        ],
        out_specs=[
            pl.BlockSpec((tm, tn), lambda i, j: (i, j)),
            pl.BlockSpec((tm, LR_PAD), lambda i, j: (i, 0)),
        ],
        out_shape=[
            jax.ShapeDtypeStruct((m, n), BF16),
            jax.ShapeDtypeStruct((m, LR_PAD), F32),
        ],
        scratch_shapes=[pltpu.VMEM((tm, d), BF16)],
        compiler_params=_cparams(("parallel", "arbitrary")),
        name="proj",
    )(x2d, mods3, mods3, w_p, w_lr)


def _proj_dil_kernel(x_ref, sh_ref, sc_ref, w_ref, cos_ref, sin_ref, o_ref, h_scr, slab_scr, *, r):
    j = pl.program_id(1)
    tm, d = x_ref.shape
    tl = tm // r

    @pl.when(j == 0)
    def _():
        if r == 1:
            h_scr[...] = (x_ref[...] * (1.0 + sc_ref[0]) + sh_ref[0]).astype(BF16)
        else:
            for k in range(d // LANES):
                sl = slice(k * LANES, (k + 1) * LANES)
                slab_scr[k] = x_ref[:, sl] * (1.0 + sc_ref[0, :, sl]) + sh_ref[0, :, sl]
            for c in range(r):
                for k in range(d // LANES):
                    sl = slice(k * LANES, (k + 1) * LANES)
                    h_scr[c * tl:(c + 1) * tl, sl] = slab_scr[k, pl.ds(c, tl, stride=r), :].astype(BF16)

    t = jnp.dot(h_scr[...], w_ref[...], preferred_element_type=F32)
    n_slab = t.shape[1] // DIL_HEAD_DIM

    @pl.when(j == 2)
    def _():
        for c in range(r):
            o_ref[0, c] = t[c * tl:(c + 1) * tl, :].astype(BF16)

    @pl.when(j < 2)
    def _():
        scale = jnp.where(j == 0, DIL_HEAD_DIM ** -0.5, 1.0).astype(F32)
        cos = cos_ref[...] * scale
        sin = sin_ref[...] * scale
        for s in range(n_slab):
            sl = slice(s * DIL_HEAD_DIM, (s + 1) * DIL_HEAD_DIM)
            ts = t[:, sl]
            rot = (ts * cos + pltpu.roll(ts, DIL_HEAD_DIM // 2, 1) * sin).astype(BF16)
            for c in range(r):
                o_ref[0, c, :, sl] = rot[c * tl:(c + 1) * tl, :]


def _proj_dil(x2d, mods3, k_shift, k_scale, w_g, cos_t, sin_t, bsz, seq, r, tm=PROJ_TM):
    m, d = x2d.shape
    tiles_per_seq = seq // tm
    tl = tm // r

    def mod_spec(k):
        return pl.BlockSpec((1, 1, d), lambda i, j: ((i // tiles_per_seq) * N_MOD + k, 0, 0))

    return pl.pallas_call(
        functools.partial(_proj_dil_kernel, r=r),
        grid=(m // tm, 3),
        in_specs=[
            pl.BlockSpec((tm, d), lambda i, j: (i, 0)),
            mod_spec(k_shift), mod_spec(k_scale),
            pl.BlockSpec((d, DIL_OUT), lambda i, j: (0, j)),
            pl.BlockSpec((tm, DIL_HEAD_DIM), lambda i, j: (i, 0)),
            pl.BlockSpec((tm, DIL_HEAD_DIM), lambda i, j: (i, 0)),
        ],
        out_specs=pl.BlockSpec((1, r, tl, DIL_OUT),
                               lambda i, j: (i // tiles_per_seq, 0, i % tiles_per_seq, j)),
        out_shape=jax.ShapeDtypeStruct((bsz, r, seq // r, DIL_QKV), BF16),
        scratch_shapes=[pltpu.VMEM((tm, d), BF16),
                        pltpu.VMEM((d // LANES, tm, LANES) if r > 1 else (1, 8, LANES), F32)],
        compiler_params=_cparams(("parallel", "arbitrary")),
        name=f"proj_dil_r{r}",
    )(x2d, mods3, mods3, w_g, cos_t, sin_t)


def _gla_kernel(q_ref, k_ref, v_ref, r_ref, lr_ref, wa_ref, ba_ref, ng_ref, o_ref, st_scr, la_scr,
                *, n_chunks):
    c_len = GLA_CHUNK

    @pl.when(pl.program_id(2) == 0)
    def _():
        st_scr[...] = jnp.zeros_like(st_scr)

    row = lax.broadcasted_iota(jnp.int32, (c_len, c_len), 0)
    col = lax.broadcasted_iota(jnp.int32, (c_len, c_len), 1)
    causal = row >= col
    tril = causal.astype(F32)

    logits = jnp.dot(lr_ref[0].astype(BF16), wa_ref[...], preferred_element_type=F32) + ba_ref[...]
    la_scr[...] = (jnp.minimum(logits, 0.0) - jnp.log(1.0 + jnp.exp(-jnp.abs(logits)))) / GLA_TAU

    for c in range(n_chunks):
        rows = slice(c * c_len, (c + 1) * c_len)
        q = q_ref[0, rows, :].astype(F32)
        k = k_ref[0, rows, :].astype(F32)
        v = v_ref[0, rows, :]
        b = jnp.dot(tril, la_scr[rows, :], preferred_element_type=F32,
                    precision=lax.Precision.HIGHEST)
        b_last = b[c_len - 1:c_len, :]
        q_dec = (q * jnp.exp(b) * (GLA_DK ** -0.5)).astype(BF16)
        k_inv = (k * jnp.exp(-b)).astype(BF16)
        k_end = (k * jnp.exp(b_last - b)).astype(BF16)
        a = lax.dot_general(q_dec, k_inv, (((1,), (1,)), ((), ())), preferred_element_type=F32)
        a = jnp.where(causal, a, 0.0).astype(BF16)
        st = st_scr[...]
        o = jnp.dot(a, v, preferred_element_type=F32)
        o += lax.dot_general(q_dec, st.astype(BF16), (((1,), (1,)), ((), ())),
                             preferred_element_type=F32)
        st_scr[...] = st * jnp.exp(b_last) + lax.dot_general(
            v, k_end, (((0,), (0,)), ((), ())), preferred_element_type=F32)
        o = o * lax.rsqrt(jnp.mean(o * o, axis=-1, keepdims=True) + LN_EPS) * ng_ref[0]
        r = r_ref[0, rows, :].astype(F32)
        o_ref[0, rows, :] = (o * (r * _sigmoid(r))).astype(BF16)


def _gla(p3, lr3, w_a2, b_a, norm_g3, tc=512):
    bsz, seq, _ = p3.shape
    nq, nv = OFF_GK // GLA_DK, OFF_GV // GLA_DV
    nr = OFF_GR // GLA_DV
    return pl.pallas_call(
        functools.partial(_gla_kernel, n_chunks=tc // GLA_CHUNK),
        grid=(bsz, GLA_HEADS, seq // tc),
        in_specs=[
            pl.BlockSpec((1, tc, GLA_DK), lambda b, h, t: (b, t, h)),
            pl.BlockSpec((1, tc, GLA_DK), lambda b, h, t: (b, t, nq + h)),
            pl.BlockSpec((1, tc, GLA_DV), lambda b, h, t: (b, t, nv + h)),
            pl.BlockSpec((1, tc, GLA_DV), lambda b, h, t: (b, t, nr + h)),
            pl.BlockSpec((1, tc, LR_PAD), lambda b, h, t: (b, t, 0)),
            pl.BlockSpec((LR_PAD, GLA_DK), lambda b, h, t: (0, h)),
            pl.BlockSpec((1, GLA_DK), lambda b, h, t: (0, h)),
            pl.BlockSpec((1, 1, GLA_DV), lambda b, h, t: (h, 0, 0)),
        ],
        out_specs=pl.BlockSpec((1, tc, GLA_DV), lambda b, h, t: (b, t, h)),
        out_shape=jax.ShapeDtypeStruct((bsz, seq, GLA_VW), BF16),
        scratch_shapes=[pltpu.VMEM((GLA_DV, GLA_DK), F32), pltpu.VMEM((tc, GLA_DK), F32)],
        compiler_params=_cparams(("parallel", "parallel", "arbitrary")),
        name="gla",
    )(p3, p3, p3, p3, lr3, w_a2, b_a, norm_g3)


def _dil_kernel(q_ref, k_ref, v_ref, kp_ref, vp_ref, o_ref, lse_ref, s_scr, m_scr, p_scr, *, nblk):
    n = pl.program_id(2)
    blk = DIL_BLOCK
    qi = lax.broadcasted_iota(jnp.int32, (blk, 2 * blk), 0)
    kj = lax.broadcasted_iota(jnp.int32, (blk, 2 * blk), 1)
    band = jnp.logical_or(jnp.logical_and(kj < blk, kj >= qi),
                          jnp.logical_and(kj >= blk, kj - blk <= qi))
    band_first = jnp.logical_and(band, jnp.logical_or(kj >= blk, n > 0))
    nt = (((1,), (1,)), ((), ()))

    def rows(i):
        return slice(i * blk, (i + 1) * blk)

    def cols(h):
        return slice(h * DIL_HEAD_DIM, (h + 1) * DIL_HEAD_DIM)

    for i in range(nblk):
        for h in range(DIL_HEADS):
            q = q_ref[0, 0, rows(i), cols(h)]
            k_prev = kp_ref[0, 0, :, cols(h)] if i == 0 else k_ref[0, 0, rows(i - 1), cols(h)]
            s_p = lax.dot_general(q, k_prev, nt, preferred_element_type=F32)
            s_c = lax.dot_general(q, k_ref[0, 0, rows(i), cols(h)], nt, preferred_element_type=F32)
            s = jnp.concatenate([s_p, s_c], axis=1)
            s_scr[rows(i * DIL_HEADS + h), :] = jnp.where(band_first if i == 0 else band, s, NEG_BIG)

    s_all = s_scr[...]
    m_all = jnp.max(s_all, axis=-1, keepdims=True)
    p_scr[...] = jnp.exp(s_all - m_all).astype(BF16)
    m_scr[...] = jnp.broadcast_to(m_all, m_scr.shape)

    lane = lax.broadcasted_iota(jnp.int32, (blk, LANES), 1)
    ones = jnp.ones((2 * blk, DIL_HEAD_DIM), BF16)
    for i in range(nblk):
        lse_tile = jnp.zeros((blk, LANES), F32)
        for h in range(DIL_HEADS):
            u = i * DIL_HEADS + h
            v_prev = vp_ref[0, 0, :, cols(h)] if i == 0 else v_ref[0, 0, rows(i - 1), cols(h)]
            v_aug = jnp.concatenate(
                [jnp.concatenate([v_prev, v_ref[0, 0, rows(i), cols(h)]], axis=0), ones], axis=1)
            acc = jnp.dot(p_scr[rows(u), :], v_aug, preferred_element_type=F32)
            den = acc[:, DIL_HEAD_DIM:]
            o_ref[0, 0, rows(i), cols(h)] = (acc[:, :DIL_HEAD_DIM] / den).astype(BF16)
            lse_tile = jnp.where(lane == h, m_scr[rows(u), :] + jnp.log(den), lse_tile)
        lse_ref[0, 0, rows(i), :] = lse_tile


def _dil_group(qkv, nblk=4):
    bsz, r, length, _ = qkv.shape
    nb = length // DIL_BLOCK
    nblk = min(nblk, nb)
    assert nb % nblk == 0
    tq = nblk * DIL_BLOCK
    units = nblk * DIL_HEADS * DIL_BLOCK

    def cur_spec(u):
        return pl.BlockSpec((1, 1, tq, DIL_OUT), lambda b, c, n: (b, c, n, u))

    def prev_spec(u):
        return pl.BlockSpec((1, 1, DIL_BLOCK, DIL_OUT),
                            lambda b, c, n: (b, c, jnp.maximum(n * nblk - 1, 0), u))

    return pl.pallas_call(
        functools.partial(_dil_kernel, nblk=nblk),
        grid=(bsz, r, nb // nblk),
        in_specs=[cur_spec(0), cur_spec(1), cur_spec(2), prev_spec(1), prev_spec(2)],
        out_specs=[
            pl.BlockSpec((1, 1, tq, DIL_OUT), lambda b, c, n: (b, c, n, 0)),
            pl.BlockSpec((1, 1, tq, LANES), lambda b, c, n: (b, c, n, 0)),
        ],
        out_shape=[
            jax.ShapeDtypeStruct((bsz, r, length, DIL_OUT), BF16),
            jax.ShapeDtypeStruct((bsz, r, length, LANES), F32),
        ],
        scratch_shapes=[pltpu.VMEM((units, 2 * DIL_BLOCK), F32),
                        pltpu.VMEM((units, LANES), F32),
                        pltpu.VMEM((units, 2 * DIL_BLOCK), BF16)],
        compiler_params=_cparams(("parallel", "parallel", "parallel")),
        name=f"dilattn_r{r}",
    )(qkv, qkv, qkv, qkv, qkv)


def _merge_kernel(oa_ref, o1_ref, o2_ref, o3_ref, l1_ref, l2_ref, l3_ref, ga_ref, gb_ref, x_ref,
                  g2_ref, wa_ref, wb_ref, wo_ref, lng_ref, lnb_ref, out_ref, l_scr, w_scr, ob_scr,
                  *, dilations):
    y_a = jnp.dot(oa_ref[...], wa_ref[...], preferred_element_type=F32)
    o_refs = (o1_ref, o2_ref, o3_ref)
    l_refs = (l1_ref, l2_ref, l3_ref)
    tm = oa_ref.shape[0]

    def tok_rows(c, r):
        return pl.ds(c, tm // r, stride=r) if r > 1 else slice(None)

    for g, r in enumerate(dilations):
        for c in range(r):
            l_scr[g, tok_rows(c, r), :] = l_refs[g][0, c]
    l1, l2, l3 = l_scr[0], l_scr[1], l_scr[2]
    mx = jnp.maximum(jnp.maximum(l1, l2), l3)
    e1, e2, e3 = jnp.exp(l1 - mx), jnp.exp(l2 - mx), jnp.exp(l3 - mx)
    inv = 1.0 / (e1 + e2 + e3)
    w_scr[0], w_scr[1], w_scr[2] = e1 * inv, e2 * inv, e3 * inv
    for g, r in enumerate(dilations):
        for c in range(r):
            rows = tok_rows(c, r)
            wr = w_scr[g, rows, :]
            for h in range(DIL_HEADS):
                sl = slice(h * DIL_HEAD_DIM, (h + 1) * DIL_HEAD_DIM)
                part = wr[:, h:h + 1] * o_refs[g][0, c, :, sl].astype(F32)
                if g == 0:
                    ob_scr[h, rows, :] = part
                else:
                    ob_scr[h, rows, :] += part
    o_b = jnp.concatenate([ob_scr[h].astype(BF16) for h in range(DIL_HEADS)], axis=1)
    y_b = jnp.dot(o_b, wb_ref[...], preferred_element_type=F32)
    merged = _sigmoid(ga_ref[...].astype(F32)) * y_a + _sigmoid(gb_ref[...].astype(F32)) * y_b
    mix = jnp.dot(merged.astype(BF16), wo_ref[...], preferred_element_type=F32)
    y = DN_ALPHA * x_ref[...] + g2_ref[0] * mix
    out_ref[...] = _layer_norm(y, lng_ref[...], lnb_ref[...])


def _merge(oa2d, o_groups, lse_groups, p2d, x2d, mods3, k_gate, w_a, w_b, w_o, ln_g, ln_b, seq,
           dilations, tm=256):
    m, d = x2d.shape
    tiles_per_seq = seq // tm
    resident = pl.Buffered(1)
    n_g = len(dilations)

    def row_spec(width, col=0):
        return pl.BlockSpec((tm, width), lambda i: (i, col))

    def class_spec(r, width):
        return pl.BlockSpec((1, r, tm // r, width),
                            lambda i: (i // tiles_per_seq, 0, i % tiles_per_seq, 0))

    def const_spec(shape):
        return pl.BlockSpec(shape, lambda i: (0, 0), pipeline_mode=resident)

    return pl.pallas_call(
        functools.partial(_merge_kernel, dilations=dilations),
        grid=(m // tm,),
        in_specs=[
            row_spec(GLA_VW),
            *[class_spec(r, DIL_OUT) for r in dilations],
            *[class_spec(r, LANES) for r in dilations],
            row_spec(D_MODEL, OFF_GA // D_MODEL), row_spec(D_MODEL, OFF_GB // D_MODEL),
            row_spec(d),
            pl.BlockSpec((1, 1, d), lambda i: ((i // tiles_per_seq) * N_MOD + k_gate, 0, 0)),
            const_spec(w_a.shape), const_spec(w_b.shape), const_spec(w_o.shape),
            const_spec((1, d)), const_spec((1, d)),
        ],
        out_specs=row_spec(d),
        out_shape=jax.ShapeDtypeStruct((m, d), F32),
        scratch_shapes=[pltpu.VMEM((n_g, tm, LANES), F32), pltpu.VMEM((n_g, tm, LANES), F32),
                        pltpu.VMEM((DIL_HEADS, tm, DIL_HEAD_DIM), F32)],
        compiler_params=_cparams(("parallel",)),
        name="merge",
    )(oa2d, *o_groups, *lse_groups, p2d, p2d, x2d, mods3, w_a, w_b, w_o, ln_g, ln_b)


def _pack_w_in(w_in):
    o = 0
    segs = []
    for wdt in (GLA_QK, GLA_QK, GLA_VW, GLA_VW, GLA_GATE_RANK, DIL_W, DIL_W, DIL_W, D_MODEL, D_MODEL):
        segs.append(w_in[:, o:o + wdt])
        o += wdt
    gq, gk, gv, gr, glr, dq, dk, dv, ga, gb = segs
    w_p = jnp.concatenate([gq, gk, gv, gr, ga, gb], axis=1).astype(BF16)
    w_lr = jnp.pad(glr, ((0, 0), (0, LR_PAD - GLA_GATE_RANK))).astype(BF16)
    w_groups = []
    for g in range(DIL_GROUPS):
        sl = slice(g * DIL_OUT, (g + 1) * DIL_OUT)
        w_groups.append(jnp.concatenate([dq[:, sl], dk[:, sl], dv[:, sl]], axis=1).astype(BF16))
    return w_p, w_lr, w_groups


def kernel(x, c, positions, w_ada, b_ada, ln1_g, ln1_b, w_ffn1_gu, w_ffn1_down, w_in, w_alpha2,
           b_alpha, gla_norm_g, w_branch_a, w_branch_b, w_out, ln2_g, ln2_b, w_ffn2_gu, w_ffn2_down,
           ln3_g, ln3_b):
    bsz, seq, d = x.shape
    m = bsz * seq
    x2d = x.reshape(m, d)
    c_pad = jnp.pad(c, ((0, 8 - bsz % 8 if bsz % 8 else 0), (0, 0)))
    dilations = tuple(r for _, r in DIL_PATTERNS)

    half = DIL_HEAD_DIM // 2
    freq = ROPE_THETA ** (-jnp.arange(half, dtype=F32) / half)
    freq2 = jnp.concatenate([freq, freq]).reshape(1, DIL_HEAD_DIM)
    rope_t = _rope_tables(positions.reshape(m, 1), freq2, dilations)

    for l in range(DEPTH):
        mods = _mods(c_pad, w_ada[l], b_ada[l].reshape(1, -1))[:bsz]
        mods3 = mods.reshape(bsz * N_MOD, 1, d)

        x1 = _ffn(x2d, mods3, 0, 1, 2, w_ffn1_gu[l].astype(BF16), w_ffn1_down[l].astype(BF16),
                  ln1_g[l].reshape(1, d), ln1_b[l].reshape(1, d), seq)

        w_p, w_lr, w_groups = _pack_w_in(w_in[l])
        p2d, lr2d = _proj(x1, mods3, 3, 4, w_p, w_lr, seq)
        p3 = p2d.reshape(bsz, seq, PROJ_W)

        w_a2 = jnp.pad(w_alpha2[l], ((0, LR_PAD - GLA_GATE_RANK), (0, 0))).astype(BF16)
        o_a = _gla(p3, lr2d.reshape(bsz, seq, LR_PAD), w_a2, b_alpha[l].reshape(1, GLA_QK),
                   gla_norm_g[l].reshape(GLA_HEADS, 1, GLA_DV))

        o_groups, lse_groups = [], []
        for g, r in enumerate(dilations):
            qkv = _proj_dil(x1, mods3, 3, 4, w_groups[g], rope_t[2 * g], rope_t[2 * g + 1],
                            bsz, seq, r)
            o_g, lse_g = _dil_group(qkv)
            o_groups.append(o_g)
            lse_groups.append(lse_g)

        x2 = _merge(o_a.reshape(m, GLA_VW), o_groups, lse_groups, p2d, x1, mods3, 5,
                    w_branch_a[l].astype(BF16), w_branch_b[l].astype(BF16), w_out[l].astype(BF16),
                    ln2_g[l].reshape(1, d), ln2_b[l].reshape(1, d), seq, dilations)

        x2d = _ffn(x2, mods3, 6, 7, 8, w_ffn2_gu[l].astype(BF16), w_ffn2_down[l].astype(BF16),
                   ln3_g[l].reshape(1, d), ln3_b[l].reshape(1, d), seq)
    return x2d.reshape(bsz, seq, d)
```

```python
import functools

import jax
import jax.numpy as jnp
from jax import lax
from jax.experimental import pallas as pl
from jax.experimental.pallas import tpu as pltpu

F32 = jnp.float32
BF16 = jnp.bfloat16

D_MODEL = 2048
DEPTH = 1
D_FF = 5632
N_MOD = 9
LN_EPS = 1e-5
DN_ALPHA = (2.0 * DEPTH) ** 0.25

GLA_HEADS = 4
GLA_DK = 256
GLA_DV = 512
GLA_GATE_RANK = 16
GLA_TAU = 16.0
GLA_CHUNK = 64
GLA_QK = GLA_HEADS * GLA_DK
GLA_VW = GLA_HEADS * GLA_DV

DIL_PATTERNS = ((128, 1), (512, 4), (2048, 16))
DIL_GROUPS = len(DIL_PATTERNS)
DIL_HEADS = 8
DIL_HEAD_DIM = 128
DIL_BLOCK = 128
DIL_W = DIL_GROUPS * DIL_HEADS * DIL_HEAD_DIM
DIL_OUT = DIL_HEADS * DIL_HEAD_DIM
ROPE_THETA = 10000.0

LANES = 128
LR_PAD = LANES

PROJ_W = 2 * GLA_QK + 2 * GLA_VW + 2 * D_MODEL
OFF_GQ = 0
OFF_GK = GLA_QK
OFF_GV = 2 * GLA_QK
OFF_GR = OFF_GV + GLA_VW
OFF_GA = OFF_GR + GLA_VW
OFF_GB = OFF_GA + D_MODEL
DIL_QKV = 3 * DIL_OUT

NEG_BIG = -1e30

VMEM_LIMIT = 56 * 1024 * 1024
FFN_VMEM_LIMIT = 56 * 1024 * 1024


def _cparams(sem, vmem=VMEM_LIMIT):
    return pltpu.CompilerParams(dimension_semantics=sem, vmem_limit_bytes=vmem)


def _sigmoid(x):
    return 1.0 / (1.0 + jnp.exp(-x))


def _layer_norm(y, g, b):
    mu = jnp.mean(y, axis=-1, keepdims=True)
    d = y - mu
    var = jnp.mean(d * d, axis=-1, keepdims=True)
    return d * lax.rsqrt(var + LN_EPS) * g + b


def _mods_kernel(c_ref, w_ref, b_ref, o_ref):
    c = c_ref[...]
    c_act = (c * _sigmoid(c)).astype(BF16)
    o_ref[...] = jnp.dot(c_act, w_ref[...].astype(BF16), preferred_element_type=F32) + b_ref[...]


def _mods(c_pad, w_ada, b_ada, tn=1024):
    rows, d = c_pad.shape
    n = w_ada.shape[1]
    return pl.pallas_call(
        _mods_kernel,
        grid=(n // tn,),
        in_specs=[
            pl.BlockSpec((rows, d), lambda j: (0, 0)),
            pl.BlockSpec((d, tn), lambda j: (0, j)),
            pl.BlockSpec((1, tn), lambda j: (0, j)),
        ],
        out_specs=pl.BlockSpec((rows, tn), lambda j: (0, j)),
        out_shape=jax.ShapeDtypeStruct((rows, n), F32),
        compiler_params=_cparams(("arbitrary",)),
        name="mods",
    )(c_pad, w_ada, b_ada)


def _ffn_kernel(x_ref, sh_ref, sc_ref, g_ref, wg_ref, wu_ref, wd_ref, lng_ref, lnb_ref,
                o_ref, h_scr):
    j = pl.program_id(1)

    @pl.when(j == 0)
    def _():
        h_scr[...] = (x_ref[...] * (1.0 + sc_ref[0]) + sh_ref[0]).astype(BF16)
        o_ref[...] = jnp.zeros_like(o_ref)

    h = h_scr[...]
    gate = jnp.dot(h, wg_ref[...], preferred_element_type=F32)
    up = jnp.dot(h, wu_ref[...], preferred_element_type=F32)
    act = (gate * _sigmoid(gate) * up).astype(BF16)
    o_ref[...] += jnp.dot(act, wd_ref[...], preferred_element_type=F32)

    @pl.when(j == pl.num_programs(1) - 1)
    def _():
        y = DN_ALPHA * x_ref[...] + 0.5 * g_ref[0] * o_ref[...]
        o_ref[...] = _layer_norm(y, lng_ref[...], lnb_ref[...])


def _ffn(x2d, mods3, k_shift, k_scale, k_gate, w_gu, w_down, ln_g, ln_b, seq, tm=1024, tf=512):
    m, d = x2d.shape
    n_ff = w_down.shape[0]
    nj = n_ff // tf
    tiles_per_seq = seq // tm

    def mod_spec(k):
        return pl.BlockSpec((1, 1, d), lambda i, j: ((i // tiles_per_seq) * N_MOD + k, 0, 0))

    return pl.pallas_call(
        _ffn_kernel,
        grid=(m // tm, nj),
        in_specs=[
            pl.BlockSpec((tm, d), lambda i, j: (i, 0), pipeline_mode=pl.Buffered(1)),
            mod_spec(k_shift), mod_spec(k_scale), mod_spec(k_gate),
            pl.BlockSpec((d, tf), lambda i, j: (0, j)),
            pl.BlockSpec((d, tf), lambda i, j: (0, j + nj)),
            pl.BlockSpec((tf, d), lambda i, j: (j, 0)),
            pl.BlockSpec((1, d), lambda i, j: (0, 0)),
            pl.BlockSpec((1, d), lambda i, j: (0, 0)),
        ],
        out_specs=pl.BlockSpec((tm, d), lambda i, j: (i, 0)),
        out_shape=jax.ShapeDtypeStruct((m, d), F32),
        scratch_shapes=[pltpu.VMEM((tm, d), BF16)],
        compiler_params=_cparams(("parallel", "arbitrary"), FFN_VMEM_LIMIT),
        name="ffn",
    )(x2d, mods3, mods3, mods3, w_gu, w_gu, w_down, ln_g, ln_b)


PROJ_TM = 1024


def _rope_kernel(pos_ref, freq_ref, *out_refs, dilations):
    cos_ref, sin_ref = out_refs[0], out_refs[1]
    tm = cos_ref.shape[0]
    pos_t = pos_ref[0].astype(F32).T
    lane = lax.broadcasted_iota(jnp.int32, (LANES, DIL_HEAD_DIM), 1)
    for k in range(tm // LANES):
        ang = pos_t[:, k:k + 1] * freq_ref[...]
        s = jnp.sin(ang)
        cos_ref[k * LANES:(k + 1) * LANES, :] = jnp.cos(ang)
        sin_ref[k * LANES:(k + 1) * LANES, :] = jnp.where(lane < DIL_HEAD_DIM // 2, -s, s)
    for gi, r in enumerate(dilations[1:]):
        tl = tm // r
        for c in range(r):
            out_refs[2 + 2 * gi][c * tl:(c + 1) * tl, :] = cos_ref[pl.ds(c, tl, stride=r), :]
            out_refs[3 + 2 * gi][c * tl:(c + 1) * tl, :] = sin_ref[pl.ds(c, tl, stride=r), :]


def _rope_tables(pos3, freq2, dilations, tm=PROJ_TM):
    assert dilations[0] == 1
    m = pos3.shape[0] * tm
    n_out = 2 * len(dilations)
    return pl.pallas_call(
        functools.partial(_rope_kernel, dilations=dilations),
        grid=(m // tm,),
        in_specs=[
            pl.BlockSpec((1, tm // LANES, LANES), lambda i: (i, 0, 0)),
            pl.BlockSpec((1, DIL_HEAD_DIM), lambda i: (0, 0)),
        ],
        out_specs=[pl.BlockSpec((tm, DIL_HEAD_DIM), lambda i: (i, 0))] * n_out,
        out_shape=[jax.ShapeDtypeStruct((m, DIL_HEAD_DIM), F32)] * n_out,
        compiler_params=_cparams(("parallel",)),
        name="rope_tables",
    )(pos3, freq2)


def _proj_kernel(x_ref, sh_ref, sc_ref, w_ref, wlr_ref, o_ref, lr_ref, h_scr):
    @pl.when(pl.program_id(1) == 0)
    def _():
        h = (x_ref[...] * (1.0 + sc_ref[0]) + sh_ref[0]).astype(BF16)
        h_scr[...] = h
        lr_ref[...] = jnp.dot(h, wlr_ref[...], preferred_element_type=F32)

    o_ref[...] = jnp.dot(h_scr[...], w_ref[...], preferred_element_type=F32).astype(BF16)


def _proj(x2d, mods3, k_shift, k_scale, w_p, w_lr, seq, tm=PROJ_TM, tn=1024):
    m, d = x2d.shape
    n = w_p.shape[1]
    tiles_per_seq = seq // tm

    def mod_spec(k):
        return pl.BlockSpec((1, 1, d), lambda i, j: ((i // tiles_per_seq) * N_MOD + k, 0, 0))

    return pl.pallas_call(
        _proj_kernel,
        grid=(m // tm, n // tn),
        in_specs=[
            pl.BlockSpec((tm, d), lambda i, j: (i, 0)),
            mod_spec(k_shift), mod_spec(k_scale),
            pl.BlockSpec((d, tn), lambda i, j: (0, j)),
            pl.BlockSpec((d, LR_PAD), lambda i, j: (0, 0)),
        ],
        out_specs=[
            pl.BlockSpec((tm, tn), lambda i, j: (i, j)),
            pl.BlockSpec((tm, LR_PAD), lambda i, j: (i, 0)),
        ],
        out_shape=[
            jax.ShapeDtypeStruct((m, n), BF16),
            jax.ShapeDtypeStruct((m, LR_PAD), F32),
        ],
        scratch_shapes=[pltpu.VMEM((tm, d), BF16)],
        compiler_params=_cparams(("parallel", "arbitrary")),
        name="proj",
    )(x2d, mods3, mods3, w_p, w_lr)


def _proj_dil_kernel(x_ref, sh_ref, sc_ref, w_ref, cos_ref, sin_ref, o_ref, h_scr, slab_scr, *, r):
    j = pl.program_id(1)
    tm, d = x_ref.shape
    tl = tm // r

    @pl.when(j == 0)
    def _():
        if r == 1:
            h_scr[...] = (x_ref[...] * (1.0 + sc_ref[0]) + sh_ref[0]).astype(BF16)
        else:
            for k in range(d // LANES):
                sl = slice(k * LANES, (k + 1) * LANES)
                slab_scr[k] = x_ref[:, sl] * (1.0 + sc_ref[0, :, sl]) + sh_ref[0, :, sl]
            for c in range(r):
                for k in range(d // LANES):
                    sl = slice(k * LANES, (k + 1) * LANES)
                    h_scr[c * tl:(c + 1) * tl, sl] = slab_scr[k, pl.ds(c, tl, stride=r), :].astype(BF16)

    t = jnp.dot(h_scr[...], w_ref[...], preferred_element_type=F32)
    n_slab = t.shape[1] // DIL_HEAD_DIM

    @pl.when(j == 2)
    def _():
        for c in range(r):
            o_ref[0, c] = t[c * tl:(c + 1) * tl, :].astype(BF16)

    @pl.when(j < 2)
    def _():
        scale = jnp.where(j == 0, DIL_HEAD_DIM ** -0.5, 1.0).astype(F32)
        cos = cos_ref[...] * scale
        sin = sin_ref[...] * scale
        for s in range(n_slab):
            sl = slice(s * DIL_HEAD_DIM, (s + 1) * DIL_HEAD_DIM)
            ts = t[:, sl]
            rot = (ts * cos + pltpu.roll(ts, DIL_HEAD_DIM // 2, 1) * sin).astype(BF16)
            for c in range(r):
                o_ref[0, c, :, sl] = rot[c * tl:(c + 1) * tl, :]


def _proj_dil(x2d, mods3, k_shift, k_scale, w_g, cos_t, sin_t, bsz, seq, r, tm=PROJ_TM):
    m, d = x2d.shape
    tiles_per_seq = seq // tm
    tl = tm // r

    def mod_spec(k):
        return pl.BlockSpec((1, 1, d), lambda i, j: ((i // tiles_per_seq) * N_MOD + k, 0, 0))

    return pl.pallas_call(
        functools.partial(_proj_dil_kernel, r=r),
        grid=(m // tm, 3),
        in_specs=[
            pl.BlockSpec((tm, d), lambda i, j: (i, 0)),
            mod_spec(k_shift), mod_spec(k_scale),
            pl.BlockSpec((d, DIL_OUT), lambda i, j: (0, j)),
            pl.BlockSpec((tm, DIL_HEAD_DIM), lambda i, j: (i, 0)),
            pl.BlockSpec((tm, DIL_HEAD_DIM), lambda i, j: (i, 0)),
        ],
        out_specs=pl.BlockSpec((1, r, tl, DIL_OUT),
                               lambda i, j: (i // tiles_per_seq, 0, i % tiles_per_seq, j)),
        out_shape=jax.ShapeDtypeStruct((bsz, r, seq // r, DIL_QKV), BF16),
        scratch_shapes=[pltpu.VMEM((tm, d), BF16),
                        pltpu.VMEM((d // LANES, tm, LANES) if r > 1 else (1, 8, LANES), F32)],
        compiler_params=_cparams(("parallel", "arbitrary")),
        name=f"proj_dil_r{r}",
    )(x2d, mods3, mods3, w_g, cos_t, sin_t)


def _gla_kernel(q_ref, k_ref, v_ref, r_ref, lr_ref, wa_ref, ba_ref, ng_ref, o_ref,
                st_scr, b_scr, qd_scr, ke_scr, o_scr, *, n_chunks):
    c_len = GLA_CHUNK
    tc = q_ref.shape[1]
    hb = 4 * c_len

    @pl.when(pl.program_id(2) == 0)
    def _():
        st_scr[...] = jnp.zeros_like(st_scr)

    ti = lax.broadcasted_iota(jnp.int32, (hb, hb), 0)
    tj = lax.broadcasted_iota(jnp.int32, (hb, hb), 1)
    tri = jnp.logical_and(ti >= tj, ti // c_len == tj // c_len).astype(BF16)
    for s in range(tc // hb):
        rows = slice(s * hb, (s + 1) * hb)
        logits = jnp.dot(lr_ref[0, rows, :].astype(BF16), wa_ref[...],
                         preferred_element_type=F32) + ba_ref[...]
        log_a = (jnp.minimum(logits, 0.0) - jnp.log(1.0 + jnp.exp(-jnp.abs(logits)))) / GLA_TAU
        p1 = log_a.astype(BF16)
        r1 = log_a - p1.astype(F32)
        p2 = r1.astype(BF16)
        p3 = (r1 - p2.astype(F32)).astype(BF16)
        b_scr[rows, :] = (jnp.dot(tri, p1, preferred_element_type=F32)
                          + jnp.dot(tri, p2, preferred_element_type=F32)
                          + jnp.dot(tri, p3, preferred_element_type=F32))

    row = lax.broadcasted_iota(jnp.int32, (c_len, c_len), 0)
    col = lax.broadcasted_iota(jnp.int32, (c_len, c_len), 1)
    causal = row >= col
    nt = (((1,), (1,)), ((), ()))

    def chunk_rows(c):
        return slice(c * c_len, (c + 1) * c_len)

    def b_last(c):
        return b_scr[(c + 1) * c_len - 1:(c + 1) * c_len, :]

    for c in range(n_chunks):
        rows = chunk_rows(c)
        b = b_scr[rows, :]
        q = q_ref[0, rows, :].astype(F32)
        k = k_ref[0, rows, :].astype(F32)
        q_dec = (q * jnp.exp(b) * (GLA_DK ** -0.5)).astype(BF16)
        k_inv = (k * jnp.exp(-b)).astype(BF16)
        qd_scr[rows, :] = q_dec
        ke_scr[rows, :] = (k * jnp.exp(b_last(c) - b)).astype(BF16)
        a = lax.dot_general(q_dec, k_inv, nt, preferred_element_type=F32)
        a = jnp.where(causal, a, 0.0).astype(BF16)
        o_scr[rows, :] = jnp.dot(a, v_ref[0, rows, :], preferred_element_type=F32)

    for c in range(n_chunks):
        rows = chunk_rows(c)
        st = st_scr[...]
        o_scr[rows, :] += lax.dot_general(qd_scr[rows, :], st.astype(BF16), nt,
                                          preferred_element_type=F32)
        st_scr[...] = st * jnp.exp(b_last(c)) + lax.dot_general(
            v_ref[0, rows, :], ke_scr[rows, :], (((0,), (0,)), ((), ())),
            preferred_element_type=F32)

    o = o_scr[...]
    o = o * lax.rsqrt(jnp.mean(o * o, axis=-1, keepdims=True) + LN_EPS) * ng_ref[0]
    r = r_ref[0].astype(F32)
    o_ref[0] = (o * (r * _sigmoid(r))).astype(BF16)


def _gla(p3, lr3, w_a2, b_a, norm_g3, tc=512):
    bsz, seq, _ = p3.shape
    nq, nv, nr = OFF_GK // GLA_DK, OFF_GV // GLA_DV, OFF_GR // GLA_DV
    return pl.pallas_call(
        functools.partial(_gla_kernel, n_chunks=tc // GLA_CHUNK),
        grid=(bsz, GLA_HEADS, seq // tc),
        in_specs=[
            pl.BlockSpec((1, tc, GLA_DK), lambda b, h, t: (b, t, h)),
            pl.BlockSpec((1, tc, GLA_DK), lambda b, h, t: (b, t, nq + h)),
            pl.BlockSpec((1, tc, GLA_DV), lambda b, h, t: (b, t, nv + h)),
            pl.BlockSpec((1, tc, GLA_DV), lambda b, h, t: (b, t, nr + h)),
            pl.BlockSpec((1, tc, LR_PAD), lambda b, h, t: (b, t, 0)),
            pl.BlockSpec((LR_PAD, GLA_DK), lambda b, h, t: (0, h)),
            pl.BlockSpec((1, GLA_DK), lambda b, h, t: (0, h)),
            pl.BlockSpec((1, 1, GLA_DV), lambda b, h, t: (h, 0, 0)),
        ],
        out_specs=pl.BlockSpec((1, tc, GLA_DV), lambda b, h, t: (b, t, h)),
        out_shape=jax.ShapeDtypeStruct((bsz, seq, GLA_VW), BF16),
        scratch_shapes=[pltpu.VMEM((GLA_DV, GLA_DK), F32), pltpu.VMEM((tc, GLA_DK), F32),
                        pltpu.VMEM((tc, GLA_DK), BF16), pltpu.VMEM((tc, GLA_DK), BF16),
                        pltpu.VMEM((tc, GLA_DV), F32)],
        compiler_params=_cparams(("parallel", "parallel", "arbitrary")),
        name="gla",
    )(p3, p3, p3, p3, lr3, w_a2, b_a, norm_g3)


def _dil_kernel(q_ref, k_ref, v_ref, kp_ref, vp_ref, o_ref, lse_ref, s_scr, m_scr, p_scr, *, nblk):
    n = pl.program_id(2)
    blk = DIL_BLOCK
    qi = lax.broadcasted_iota(jnp.int32, (blk, 2 * blk), 0)
    kj = lax.broadcasted_iota(jnp.int32, (blk, 2 * blk), 1)
    band = jnp.logical_or(jnp.logical_and(kj < blk, kj >= qi),
                          jnp.logical_and(kj >= blk, kj - blk <= qi))
    band_first = jnp.logical_and(band, jnp.logical_or(kj >= blk, n > 0))
    nt = (((1,), (1,)), ((), ()))

    def rows(i):
        return slice(i * blk, (i + 1) * blk)

    def cols(h):
        return slice(h * DIL_HEAD_DIM, (h + 1) * DIL_HEAD_DIM)

    for i in range(nblk):
        for h in range(DIL_HEADS):
            q = q_ref[0, 0, rows(i), cols(h)]
            k_prev = kp_ref[0, 0, :, cols(h)] if i == 0 else k_ref[0, 0, rows(i - 1), cols(h)]
            s_p = lax.dot_general(q, k_prev, nt, preferred_element_type=F32)
            s_c = lax.dot_general(q, k_ref[0, 0, rows(i), cols(h)], nt, preferred_element_type=F32)
            s = jnp.concatenate([s_p, s_c], axis=1)
            s_scr[rows(i * DIL_HEADS + h), :] = jnp.where(band_first if i == 0 else band, s, NEG_BIG)

    s_all = s_scr[...]
    m_all = jnp.max(s_all, axis=-1, keepdims=True)
    p_scr[...] = jnp.exp(s_all - m_all).astype(BF16)
    m_scr[...] = jnp.broadcast_to(m_all, m_scr.shape)

    lane = lax.broadcasted_iota(jnp.int32, (blk, LANES), 1)
    ones = jnp.ones((2 * blk, DIL_HEAD_DIM), BF16)
    for i in range(nblk):
        lse_tile = jnp.zeros((blk, LANES), F32)
        for h in range(DIL_HEADS):
            u = i * DIL_HEADS + h
            v_prev = vp_ref[0, 0, :, cols(h)] if i == 0 else v_ref[0, 0, rows(i - 1), cols(h)]
            v_aug = jnp.concatenate(
                [jnp.concatenate([v_prev, v_ref[0, 0, rows(i), cols(h)]], axis=0), ones], axis=1)
            acc = jnp.dot(p_scr[rows(u), :], v_aug, preferred_element_type=F32)
            den = acc[:, DIL_HEAD_DIM:]
            o_ref[0, 0, rows(i), cols(h)] = (acc[:, :DIL_HEAD_DIM] / den).astype(BF16)
            lse_tile = jnp.where(lane == h, m_scr[rows(u), :] + jnp.log(den), lse_tile)
        lse_ref[0, 0, rows(i), :] = lse_tile


def _dil_group(qkv, nblk=4):
    bsz, r, length, _ = qkv.shape
    nb = length // DIL_BLOCK
    nblk = min(nblk, nb)
    assert nb % nblk == 0
    tq = nblk * DIL_BLOCK
    units = nblk * DIL_HEADS * DIL_BLOCK

    def cur_spec(u):
        return pl.BlockSpec((1, 1, tq, DIL_OUT), lambda b, c, n: (b, c, n, u))

    def prev_spec(u):
        return pl.BlockSpec((1, 1, DIL_BLOCK, DIL_OUT),
                            lambda b, c, n: (b, c, jnp.maximum(n * nblk - 1, 0), u))

    return pl.pallas_call(
        functools.partial(_dil_kernel, nblk=nblk),
        grid=(bsz, r, nb // nblk),
        in_specs=[cur_spec(0), cur_spec(1), cur_spec(2), prev_spec(1), prev_spec(2)],
        out_specs=[
            pl.BlockSpec((1, 1, tq, DIL_OUT), lambda b, c, n: (b, c, n, 0)),
            pl.BlockSpec((1, 1, tq, LANES), lambda b, c, n: (b, c, n, 0)),
        ],
        out_shape=[
            jax.ShapeDtypeStruct((bsz, r, length, DIL_OUT), BF16),
            jax.ShapeDtypeStruct((bsz, r, length, LANES), F32),
        ],
        scratch_shapes=[pltpu.VMEM((units, 2 * DIL_BLOCK), F32),
                        pltpu.VMEM((units, LANES), F32),
                        pltpu.VMEM((units, 2 * DIL_BLOCK), BF16)],
        compiler_params=_cparams(("parallel", "parallel", "parallel")),
        name=f"dilattn_r{r}",
    )(qkv, qkv, qkv, qkv, qkv)


def _merge_kernel(oa_ref, o1_ref, o2_ref, o3_ref, l1_ref, l2_ref, l3_ref, ga_ref, gb_ref, x_ref,
                  g2_ref, wa_ref, wb_ref, wo_ref, lng_ref, lnb_ref, out_ref, l_scr, w_scr, ob_scr,
                  *, dilations):
    y_a = jnp.dot(oa_ref[...], wa_ref[...], preferred_element_type=F32)
    o_refs = (o1_ref, o2_ref, o3_ref)
    l_refs = (l1_ref, l2_ref, l3_ref)
    tm = oa_ref.shape[0]

    def tok_rows(c, r):
        return pl.ds(c, tm // r, stride=r) if r > 1 else slice(None)

    for g, r in enumerate(dilations):
        for c in range(r):
            l_scr[g, tok_rows(c, r), :] = l_refs[g][0, c]
    l1, l2, l3 = l_scr[0], l_scr[1], l_scr[2]
    mx = jnp.maximum(jnp.maximum(l1, l2), l3)
    e1, e2, e3 = jnp.exp(l1 - mx), jnp.exp(l2 - mx), jnp.exp(l3 - mx)
    inv = 1.0 / (e1 + e2 + e3)
    w_scr[0], w_scr[1], w_scr[2] = e1 * inv, e2 * inv, e3 * inv
    for g, r in enumerate(dilations):
        for c in range(r):
            rows = tok_rows(c, r)
            wr = w_scr[g, rows, :]
            for h in range(DIL_HEADS):
                sl = slice(h * DIL_HEAD_DIM, (h + 1) * DIL_HEAD_DIM)
                part = wr[:, h:h + 1] * o_refs[g][0, c, :, sl].astype(F32)
                if g == 0:
                    ob_scr[h, rows, :] = part
                else:
                    ob_scr[h, rows, :] += part
    o_b = jnp.concatenate([ob_scr[h].astype(BF16) for h in range(DIL_HEADS)], axis=1)
    y_b = jnp.dot(o_b, wb_ref[...], preferred_element_type=F32)
    merged = _sigmoid(ga_ref[...].astype(F32)) * y_a + _sigmoid(gb_ref[...].astype(F32)) * y_b
    mix = jnp.dot(merged.astype(BF16), wo_ref[...], preferred_element_type=F32)
    y = DN_ALPHA * x_ref[...] + g2_ref[0] * mix
    out_ref[...] = _layer_norm(y, lng_ref[...], lnb_ref[...])


def _merge(oa2d, o_groups, lse_groups, p2d, x2d, mods3, k_gate, w_a, w_b, w_o, ln_g, ln_b, seq,
           dilations, tm=256):
    m, d = x2d.shape
    tiles_per_seq = seq // tm
    resident = pl.Buffered(1)
    n_g = len(dilations)

    def row_spec(width, col=0):
        return pl.BlockSpec((tm, width), lambda i: (i, col))

    def class_spec(r, width):
        return pl.BlockSpec((1, r, tm // r, width),
                            lambda i: (i // tiles_per_seq, 0, i % tiles_per_seq, 0))

    def const_spec(shape):
        return pl.BlockSpec(shape, lambda i: (0, 0), pipeline_mode=resident)

    return pl.pallas_call(
        functools.partial(_merge_kernel, dilations=dilations),
        grid=(m // tm,),
        in_specs=[
            row_spec(GLA_VW),
            *[class_spec(r, DIL_OUT) for r in dilations],
            *[class_spec(r, LANES) for r in dilations],
            row_spec(D_MODEL, OFF_GA // D_MODEL), row_spec(D_MODEL, OFF_GB // D_MODEL),
            row_spec(d),
            pl.BlockSpec((1, 1, d), lambda i: ((i // tiles_per_seq) * N_MOD + k_gate, 0, 0)),
            const_spec(w_a.shape), const_spec(w_b.shape), const_spec(w_o.shape),
            const_spec((1, d)), const_spec((1, d)),
        ],
        out_specs=row_spec(d),
        out_shape=jax.ShapeDtypeStruct((m, d), F32),
        scratch_shapes=[pltpu.VMEM((n_g, tm, LANES), F32), pltpu.VMEM((n_g, tm, LANES), F32),
                        pltpu.VMEM((DIL_HEADS, tm, DIL_HEAD_DIM), F32)],
        compiler_params=_cparams(("parallel",)),
        name="merge",
    )(oa2d, *o_groups, *lse_groups, p2d, p2d, x2d, mods3, w_a, w_b, w_o, ln_g, ln_b)


def _pack_w_in(w_in):
    o = 0
    segs = []
    for wdt in (GLA_QK, GLA_QK, GLA_VW, GLA_VW, GLA_GATE_RANK, DIL_W, DIL_W, DIL_W, D_MODEL, D_MODEL):
        segs.append(w_in[:, o:o + wdt])
        o += wdt
    gq, gk, gv, gr, glr, dq, dk, dv, ga, gb = segs
    w_p = jnp.concatenate([gq, gk, gv, gr, ga, gb], axis=1).astype(BF16)
    w_lr = jnp.pad(glr, ((0, 0), (0, LR_PAD - GLA_GATE_RANK))).astype(BF16)
    w_groups = []
    for g in range(DIL_GROUPS):
        sl = slice(g * DIL_OUT, (g + 1) * DIL_OUT)
        w_groups.append(jnp.concatenate([dq[:, sl], dk[:, sl], dv[:, sl]], axis=1).astype(BF16))
    return w_p, w_lr, w_groups


def kernel(x, c, positions, w_ada, b_ada, ln1_g, ln1_b, w_ffn1_gu, w_ffn1_down, w_in, w_alpha2,
           b_alpha, gla_norm_g, w_branch_a, w_branch_b, w_out, ln2_g, ln2_b, w_ffn2_gu, w_ffn2_down,
           ln3_g, ln3_b):
    bsz, seq, d = x.shape
    m = bsz * seq
    x2d = x.reshape(m, d)
    c_pad = jnp.pad(c, ((0, 8 - bsz % 8 if bsz % 8 else 0), (0, 0)))
    dilations = tuple(r for _, r in DIL_PATTERNS)

    half = DIL_HEAD_DIM // 2
    freq = ROPE_THETA ** (-jnp.arange(half, dtype=F32) / half)
    freq2 = jnp.concatenate([freq, freq]).reshape(1, DIL_HEAD_DIM)
    rope_t = _rope_tables(positions.reshape(m // PROJ_TM, PROJ_TM // LANES, LANES), freq2, dilations)

    for l in range(DEPTH):
        mods = _mods(c_pad, w_ada[l], b_ada[l].reshape(1, -1))[:bsz]
        mods3 = mods.reshape(bsz * N_MOD, 1, d)

        x1 = _ffn(x2d, mods3, 0, 1, 2, w_ffn1_gu[l].astype(BF16), w_ffn1_down[l].astype(BF16),
                  ln1_g[l].reshape(1, d), ln1_b[l].reshape(1, d), seq)

        w_p, w_lr, w_groups = _pack_w_in(w_in[l])
        p2d, lr2d = _proj(x1, mods3, 3, 4, w_p, w_lr, seq)
        p3 = p2d.reshape(bsz, seq, PROJ_W)

        w_a2 = jnp.pad(w_alpha2[l], ((0, LR_PAD - GLA_GATE_RANK), (0, 0))).astype(BF16)
        o_a = _gla(p3, lr2d.reshape(bsz, seq, LR_PAD), w_a2, b_alpha[l].reshape(1, GLA_QK),
                   gla_norm_g[l].reshape(GLA_HEADS, 1, GLA_DV))

        o_groups, lse_groups = [], []
        for g, r in enumerate(dilations):
            qkv = _proj_dil(x1, mods3, 3, 4, w_groups[g], rope_t[2 * g], rope_t[2 * g + 1],
                            bsz, seq, r)
            o_g, lse_g = _dil_group(qkv)
            o_groups.append(o_g)
            lse_groups.append(lse_g)

        x2 = _merge(o_a.reshape(m, GLA_VW), o_groups, lse_groups, p2d, x1, mods3, 5,
                    w_branch_a[l].astype(BF16), w_branch_b[l].astype(BF16), w_out[l].astype(BF16),
                    ln2_g[l].reshape(1, d), ln2_b[l].reshape(1, d), seq, dilations)

        x2d = _ffn(x2, mods3, 6, 7, 8, w_ffn2_gu[l].astype(BF16), w_ffn2_down[l].astype(BF16),
                   ln3_g[l].reshape(1, d), ln3_b[l].reshape(1, d), seq)
    return x2d.reshape(bsz, seq, d)
```

```python
import functools

import jax
import jax.numpy as jnp
from jax import lax
from jax.experimental import pallas as pl
from jax.experimental.pallas import tpu as pltpu

F32 = jnp.float32
BF16 = jnp.bfloat16

D_MODEL = 2048
DEPTH = 1
D_FF = 5632
N_MOD = 9
LN_EPS = 1e-5
DN_ALPHA = (2.0 * DEPTH) ** 0.25

GLA_HEADS = 4
GLA_DK = 256
GLA_DV = 512
GLA_GATE_RANK = 16
GLA_TAU = 16.0
GLA_CHUNK = 64
GLA_QK = GLA_HEADS * GLA_DK
GLA_VW = GLA_HEADS * GLA_DV

DIL_PATTERNS = ((128, 1), (512, 4), (2048, 16))
DIL_GROUPS = len(DIL_PATTERNS)
DIL_HEADS = 8
DIL_HEAD_DIM = 128
DIL_BLOCK = 128
DIL_W = DIL_GROUPS * DIL_HEADS * DIL_HEAD_DIM
DIL_OUT = DIL_HEADS * DIL_HEAD_DIM
ROPE_THETA = 10000.0

LANES = 128
LR_PAD = LANES

PROJ_W = 2 * GLA_QK + 2 * GLA_VW + 2 * D_MODEL
OFF_GQ = 0
OFF_GK = GLA_QK
OFF_GV = 2 * GLA_QK
OFF_GR = OFF_GV + GLA_VW
OFF_GA = OFF_GR + GLA_VW
OFF_GB = OFF_GA + D_MODEL
DIL_QKV = 3 * DIL_OUT

NEG_BIG = -1e30

VMEM_LIMIT = 56 * 1024 * 1024


def _cparams(sem, vmem=VMEM_LIMIT):
    return pltpu.CompilerParams(dimension_semantics=sem, vmem_limit_bytes=vmem)


def _sigmoid(x):
    return 1.0 / (1.0 + jnp.exp(-x))


def _layer_norm(y, g, b):
    mu = jnp.mean(y, axis=-1, keepdims=True)
    d = y - mu
    var = jnp.mean(d * d, axis=-1, keepdims=True)
    return d * lax.rsqrt(var + LN_EPS) * g + b


def _mods_kernel(c_ref, w_ref, b_ref, o_ref):
    c = c_ref[...]
    c_act = (c * _sigmoid(c)).astype(BF16)
    o_ref[...] = jnp.dot(c_act, w_ref[...].astype(BF16), preferred_element_type=F32) + b_ref[...]


def _mods(c_pad, w_ada, b_ada, tn=1024):
    rows, d = c_pad.shape
    n = w_ada.shape[1]
    return pl.pallas_call(
        _mods_kernel,
        grid=(n // tn,),
        in_specs=[
            pl.BlockSpec((rows, d), lambda j: (0, 0)),
            pl.BlockSpec((d, tn), lambda j: (0, j)),
            pl.BlockSpec((1, tn), lambda j: (0, j)),
        ],
        out_specs=pl.BlockSpec((rows, tn), lambda j: (0, j)),
        out_shape=jax.ShapeDtypeStruct((rows, n), F32),
        compiler_params=_cparams(("arbitrary",)),
        name="mods",
    )(c_pad, w_ada, b_ada)


def _ffn_kernel(x_ref, sh_ref, sc_ref, g_ref, wg_ref, wu_ref, wd_ref, lng_ref, lnb_ref,
                o_ref, h_scr):
    j = pl.program_id(1)

    @pl.when(j == 0)
    def _():
        h_scr[...] = (x_ref[...] * (1.0 + sc_ref[0]) + sh_ref[0]).astype(BF16)
        o_ref[...] = jnp.zeros_like(o_ref)

    h = h_scr[...]
    gate = jnp.dot(h, wg_ref[...], preferred_element_type=F32)
    up = jnp.dot(h, wu_ref[...], preferred_element_type=F32)
    act = (gate * _sigmoid(gate) * up).astype(BF16)
    o_ref[...] += jnp.dot(act, wd_ref[...], preferred_element_type=F32)

    @pl.when(j == pl.num_programs(1) - 1)
    def _():
        y = DN_ALPHA * x_ref[...] + 0.5 * g_ref[0] * o_ref[...]
        o_ref[...] = _layer_norm(y, lng_ref[...], lnb_ref[...])


def _ffn(x2d, mods3, k_shift, k_scale, k_gate, w_gu, w_down, ln_g, ln_b, seq, tm=512, tf=512):
    m, d = x2d.shape
    n_ff = w_down.shape[0]
    nj = n_ff // tf
    tiles_per_seq = seq // tm

    def mod_spec(k):
        return pl.BlockSpec((1, 1, d), lambda i, j: ((i // tiles_per_seq) * N_MOD + k, 0, 0))

    return pl.pallas_call(
        _ffn_kernel,
        grid=(m // tm, nj),
        in_specs=[
            pl.BlockSpec((tm, d), lambda i, j: (i, 0)),
            mod_spec(k_shift), mod_spec(k_scale), mod_spec(k_gate),
            pl.BlockSpec((d, tf), lambda i, j: (0, j)),
            pl.BlockSpec((d, tf), lambda i, j: (0, j + nj)),
            pl.BlockSpec((tf, d), lambda i, j: (j, 0)),
            pl.BlockSpec((1, d), lambda i, j: (0, 0)),
            pl.BlockSpec((1, d), lambda i, j: (0, 0)),
        ],
        out_specs=pl.BlockSpec((tm, d), lambda i, j: (i, 0)),
        out_shape=jax.ShapeDtypeStruct((m, d), F32),
        scratch_shapes=[pltpu.VMEM((tm, d), BF16)],
        compiler_params=_cparams(("parallel", "arbitrary")),
        name="ffn",
    )(x2d, mods3, mods3, mods3, w_gu, w_gu, w_down, ln_g, ln_b)


PROJ_TM = 1024


def _rope_kernel(pos_ref, freq_ref, *out_refs, dilations):
    cos_ref, sin_ref = out_refs[0], out_refs[1]
    tm = cos_ref.shape[0]
    pos_t = pos_ref[0].astype(F32).T
    lane = lax.broadcasted_iota(jnp.int32, (LANES, DIL_HEAD_DIM), 1)
    for k in range(tm // LANES):
        ang = pos_t[:, k:k + 1] * freq_ref[...]
        s = jnp.sin(ang)
        cos_ref[k * LANES:(k + 1) * LANES, :] = jnp.cos(ang)
        sin_ref[k * LANES:(k + 1) * LANES, :] = jnp.where(lane < DIL_HEAD_DIM // 2, -s, s)
    for gi, r in enumerate(dilations[1:]):
        tl = tm // r
        for c in range(r):
            out_refs[2 + 2 * gi][c * tl:(c + 1) * tl, :] = cos_ref[pl.ds(c, tl, stride=r), :]
            out_refs[3 + 2 * gi][c * tl:(c + 1) * tl, :] = sin_ref[pl.ds(c, tl, stride=r), :]


def _rope_tables(pos3, freq2, dilations, tm=PROJ_TM):
    assert dilations[0] == 1
    m = pos3.shape[0] * tm
    n_out = 2 * len(dilations)
    return pl.pallas_call(
        functools.partial(_rope_kernel, dilations=dilations),
        grid=(m // tm,),
        in_specs=[
            pl.BlockSpec((1, tm // LANES, LANES), lambda i: (i, 0, 0)),
            pl.BlockSpec((1, DIL_HEAD_DIM), lambda i: (0, 0)),
        ],
        out_specs=[pl.BlockSpec((tm, DIL_HEAD_DIM), lambda i: (i, 0))] * n_out,
        out_shape=[jax.ShapeDtypeStruct((m, DIL_HEAD_DIM), F32)] * n_out,
        compiler_params=_cparams(("parallel",)),
        name="rope_tables",
    )(pos3, freq2)


def _proj_kernel(x_ref, sh_ref, sc_ref, w_ref, wlr_ref, o_ref, lr_ref, h_scr):
    @pl.when(pl.program_id(1) == 0)
    def _():
        h = (x_ref[...] * (1.0 + sc_ref[0]) + sh_ref[0]).astype(BF16)
        h_scr[...] = h
        lr_ref[...] = jnp.dot(h, wlr_ref[...], preferred_element_type=F32)

    o_ref[...] = jnp.dot(h_scr[...], w_ref[...], preferred_element_type=F32).astype(BF16)


def _proj(x2d, mods3, k_shift, k_scale, w_p, w_lr, seq, tm=PROJ_TM, tn=1024):
    m, d = x2d.shape
    n = w_p.shape[1]
    tiles_per_seq = seq // tm

    def mod_spec(k):
        return pl.BlockSpec((1, 1, d), lambda i, j: ((i // tiles_per_seq) * N_MOD + k, 0, 0))

    return pl.pallas_call(
        _proj_kernel,
        grid=(m // tm, n // tn),
        in_specs=[
            pl.BlockSpec((tm, d), lambda i, j: (i, 0)),
            mod_spec(k_shift), mod_spec(k_scale),
            pl.BlockSpec((d, tn), lambda i, j: (0, j)),
            pl.BlockSpec((d, LR_PAD), lambda i, j: (0, 0)),
        ],
        out_specs=[
            pl.BlockSpec((tm, tn), lambda i, j: (i, j)),
            pl.BlockSpec((tm, LR_PAD), lambda i, j: (i, 0)),
        ],
        out_shape=[
            jax.ShapeDtypeStruct((m, n), BF16),
            jax.ShapeDtypeStruct((m, LR_PAD), F32),
        ],
        scratch_shapes=[pltpu.VMEM((tm, d), BF16)],
        compiler_params=_cparams(("parallel", "arbitrary")),
        name="proj",
    )(x2d, mods3, mods3, w_p, w_lr)


def _proj_dil_kernel(x_ref, sh_ref, sc_ref, w_ref, cos_ref, sin_ref, o_ref, h_scr, slab_scr, *, r):
    j = pl.program_id(1)
    tm, d = x_ref.shape
    tl = tm // r

    @pl.when(j == 0)
    def _():
        if r == 1:
            h_scr[...] = (x_ref[...] * (1.0 + sc_ref[0]) + sh_ref[0]).astype(BF16)
        else:
            for k in range(d // LANES):
                sl = slice(k * LANES, (k + 1) * LANES)
                slab_scr[k] = x_ref[:, sl] * (1.0 + sc_ref[0, :, sl]) + sh_ref[0, :, sl]
            for c in range(r):
                for k in range(d // LANES):
                    sl = slice(k * LANES, (k + 1) * LANES)
                    h_scr[c * tl:(c + 1) * tl, sl] = slab_scr[k, pl.ds(c, tl, stride=r), :].astype(BF16)

    t = jnp.dot(h_scr[...], w_ref[...], preferred_element_type=F32)
    n_slab = t.shape[1] // DIL_HEAD_DIM

    @pl.when(j == 2)
    def _():
        for c in range(r):
            o_ref[0, c] = t[c * tl:(c + 1) * tl, :].astype(BF16)

    @pl.when(j < 2)
    def _():
        scale = jnp.where(j == 0, DIL_HEAD_DIM ** -0.5, 1.0).astype(F32)
        cos = cos_ref[...] * scale
        sin = sin_ref[...] * scale
        for s in range(n_slab):
            sl = slice(s * DIL_HEAD_DIM, (s + 1) * DIL_HEAD_DIM)
            ts = t[:, sl]
            rot = (ts * cos + pltpu.roll(ts, DIL_HEAD_DIM // 2, 1) * sin).astype(BF16)
            for c in range(r):
                o_ref[0, c, :, sl] = rot[c * tl:(c + 1) * tl, :]


def _proj_dil(x2d, mods3, k_shift, k_scale, w_g, cos_t, sin_t, bsz, seq, r, tm=PROJ_TM):
    m, d = x2d.shape
    tiles_per_seq = seq // tm
    tl = tm // r

    def mod_spec(k):
        return pl.BlockSpec((1, 1, d), lambda i, j: ((i // tiles_per_seq) * N_MOD + k, 0, 0))

    return pl.pallas_call(
        functools.partial(_proj_dil_kernel, r=r),
        grid=(m // tm, 3),
        in_specs=[
            pl.BlockSpec((tm, d), lambda i, j: (i, 0)),
            mod_spec(k_shift), mod_spec(k_scale),
            pl.BlockSpec((d, DIL_OUT), lambda i, j: (0, j)),
            pl.BlockSpec((tm, DIL_HEAD_DIM), lambda i, j: (i, 0)),
            pl.BlockSpec((tm, DIL_HEAD_DIM), lambda i, j: (i, 0)),
        ],
        out_specs=pl.BlockSpec((1, r, tl, DIL_OUT),
                               lambda i, j: (i // tiles_per_seq, 0, i % tiles_per_seq, j)),
        out_shape=jax.ShapeDtypeStruct((bsz, r, seq // r, DIL_QKV), BF16),
        scratch_shapes=[pltpu.VMEM((tm, d), BF16),
                        pltpu.VMEM((d // LANES, tm, LANES) if r > 1 else (1, 8, LANES), F32)],
        compiler_params=_cparams(("parallel", "arbitrary")),
        name=f"proj_dil_r{r}",
    )(x2d, mods3, mods3, w_g, cos_t, sin_t)


def _gla_kernel(q_ref, k_ref, v_ref, r_ref, lr_ref, wa_ref, ba_ref, ng_ref, o_ref,
                st_scr, b_scr, qd_scr, qh_scr, ki_scr, ke_scr, kp_scr, kh_scr, a_scr, *, n_chunks):
    c_len = GLA_CHUNK
    tc = q_ref.shape[1]
    hb = 4 * c_len
    pw = 2 * c_len
    n_pairs = n_chunks // 2

    @pl.when(pl.program_id(2) == 0)
    def _():
        st_scr[...] = jnp.zeros_like(st_scr)

    ti = lax.broadcasted_iota(jnp.int32, (hb, hb), 0)
    tj = lax.broadcasted_iota(jnp.int32, (hb, hb), 1)
    tri = jnp.logical_and(ti >= tj, ti // c_len == tj // c_len).astype(BF16)
    for s in range(tc // hb):
        rows = slice(s * hb, (s + 1) * hb)
        logits = jnp.dot(lr_ref[0, rows, :].astype(BF16), wa_ref[...],
                         preferred_element_type=F32) + ba_ref[...]
        log_a = (jnp.minimum(logits, 0.0) - jnp.log(1.0 + jnp.exp(-jnp.abs(logits)))) / GLA_TAU
        p1 = log_a.astype(BF16)
        r1 = log_a - p1.astype(F32)
        p2 = r1.astype(BF16)
        p3 = (r1 - p2.astype(F32)).astype(BF16)
        b_scr[rows, :] = (jnp.dot(tri, p1, preferred_element_type=F32)
                          + jnp.dot(tri, p2, preferred_element_type=F32)
                          + jnp.dot(tri, p3, preferred_element_type=F32))

    nt = (((1,), (1,)), ((), ()))

    def chunk_rows(c):
        return slice(c * c_len, (c + 1) * c_len)

    def pair_rows(p):
        return slice(p * pw, (p + 1) * pw)

    b_last = [b_scr[(c + 1) * c_len - 1:(c + 1) * c_len, :] for c in range(n_chunks)]
    h_log = [jnp.zeros_like(b_last[0])]
    for c in range(n_chunks):
        h_log.append(h_log[-1] + b_last[c])

    for c in range(n_chunks):
        rows = chunk_rows(c)
        b = b_scr[rows, :]
        q = q_ref[0, rows, :].astype(F32)
        k = k_ref[0, rows, :].astype(F32)
        qd = q * jnp.exp(b) * (GLA_DK ** -0.5)
        ke = k * jnp.exp(b_last[c] - b)
        qd_scr[rows, :] = qd
        qh_scr[rows, :] = (qd * jnp.exp(h_log[c])).astype(BF16)
        ki_scr[rows, :] = (k * jnp.exp(-b)).astype(BF16)
        ke_scr[rows, :] = ke.astype(BF16)
        kp_scr[rows, :] = (ke * jnp.exp(b_last[c + 1]) if c % 2 == 0 else ke).astype(BF16)
        kh_scr[rows, :] = (ke * jnp.exp(h_log[n_chunks] - h_log[c + 1])).astype(BF16)

    pi = lax.broadcasted_iota(jnp.int32, (pw, pw), 0)
    pj = lax.broadcasted_iota(jnp.int32, (pw, pw), 1)
    diag = jnp.logical_and(pi >= pj, pi // c_len == pj // c_len)
    lower = jnp.logical_and(pi >= c_len, pj < c_len)
    for p in range(n_pairs):
        first = chunk_rows(2 * p)
        keys = jnp.concatenate([ki_scr[pair_rows(p), :], ke_scr[first, :], ke_scr[first, :]], axis=0)
        s = lax.dot_general(qd_scr[pair_rows(p), :].astype(BF16), keys, nt,
                            preferred_element_type=F32)
        s = jnp.where(diag, s[:, :pw], jnp.where(lower, s[:, pw:], 0.0))
        a_scr[pair_rows(p), pair_rows(p)] = s.astype(BF16)

    for p in range(n_pairs - 1):
        c0 = 2 * p + 2
        lhs = jnp.concatenate(
            [(qd_scr[chunk_rows(c), :] * jnp.exp(h_log[c] - h_log[c0])).astype(BF16)
             for c in range(c0, n_chunks)], axis=0)
        s = lax.dot_general(lhs, kp_scr[pair_rows(p), :], nt, preferred_element_type=F32)
        a_scr[c0 * c_len:, pair_rows(p)] = s.astype(BF16)

    half = 2 * pw
    for hf in range(tc // half):
        a_scr[hf * half:hf * half + pw, hf * half + pw:(hf + 1) * half] = jnp.zeros((pw, pw), BF16)

    st = st_scr[...]
    st_b = st.astype(BF16)
    for hf in range(tc // half):
        rows = slice(hf * half, (hf + 1) * half)
        keys = (hf + 1) * half
        o = jnp.dot(a_scr[rows, :keys], v_ref[0, :keys, :], preferred_element_type=F32)
        o += jnp.dot(qh_scr[rows, :], st_b, preferred_element_type=F32)
        o = o * lax.rsqrt(jnp.mean(o * o, axis=-1, keepdims=True) + LN_EPS) * ng_ref[0]
        r = r_ref[0, rows, :].astype(F32)
        o_ref[0, rows, :] = (o * (r * _sigmoid(r))).astype(BF16)

    d_col = jnp.broadcast_to(jnp.exp(h_log[n_chunks]), (8, GLA_DK)).T[:, :1]
    st_scr[...] = st * d_col + lax.dot_general(
        kh_scr[...], v_ref[0], (((0,), (0,)), ((), ())), preferred_element_type=F32)


def _gla(p3, lr3, w_a2, b_a, norm_g3, tc=512):
    bsz, seq, _ = p3.shape
    nq, nv, nr = OFF_GK // GLA_DK, OFF_GV // GLA_DV, OFF_GR // GLA_DV
    return pl.pallas_call(
        functools.partial(_gla_kernel, n_chunks=tc // GLA_CHUNK),
        grid=(bsz, GLA_HEADS, seq // tc),
        in_specs=[
            pl.BlockSpec((1, tc, GLA_DK), lambda b, h, t: (b, t, h)),
            pl.BlockSpec((1, tc, GLA_DK), lambda b, h, t: (b, t, nq + h)),
            pl.BlockSpec((1, tc, GLA_DV), lambda b, h, t: (b, t, nv + h)),
            pl.BlockSpec((1, tc, GLA_DV), lambda b, h, t: (b, t, nr + h)),
            pl.BlockSpec((1, tc, LR_PAD), lambda b, h, t: (b, t, 0)),
            pl.BlockSpec((LR_PAD, GLA_DK), lambda b, h, t: (0, h)),
            pl.BlockSpec((1, GLA_DK), lambda b, h, t: (0, h)),
            pl.BlockSpec((1, 1, GLA_DV), lambda b, h, t: (h, 0, 0)),
        ],
        out_specs=pl.BlockSpec((1, tc, GLA_DV), lambda b, h, t: (b, t, h)),
        out_shape=jax.ShapeDtypeStruct((bsz, seq, GLA_VW), BF16),
        scratch_shapes=[pltpu.VMEM((GLA_DK, GLA_DV), F32), pltpu.VMEM((tc, GLA_DK), F32),
                        pltpu.VMEM((tc, GLA_DK), F32)]
                       + [pltpu.VMEM((tc, GLA_DK), BF16)] * 5 + [pltpu.VMEM((tc, tc), BF16)],
        compiler_params=_cparams(("parallel", "parallel", "arbitrary")),
        name="gla",
    )(p3, p3, p3, p3, lr3, w_a2, b_a, norm_g3)


def _dil_kernel(q_ref, k_ref, v_ref, kp_ref, vp_ref, o_ref, lse_ref, s_scr, m_scr, p_scr, *, nblk):
    n = pl.program_id(2)
    blk = DIL_BLOCK
    qi = lax.broadcasted_iota(jnp.int32, (blk, 2 * blk), 0)
    kj = lax.broadcasted_iota(jnp.int32, (blk, 2 * blk), 1)
    band = jnp.logical_or(jnp.logical_and(kj < blk, kj >= qi),
                          jnp.logical_and(kj >= blk, kj - blk <= qi))
    band_first = jnp.logical_and(band, jnp.logical_or(kj >= blk, n > 0))
    nt = (((1,), (1,)), ((), ()))

    def rows(i):
        return slice(i * blk, (i + 1) * blk)

    def cols(h):
        return slice(h * DIL_HEAD_DIM, (h + 1) * DIL_HEAD_DIM)

    for i in range(nblk):
        for h in range(DIL_HEADS):
            q = q_ref[0, 0, rows(i), cols(h)]
            k_prev = kp_ref[0, 0, :, cols(h)] if i == 0 else k_ref[0, 0, rows(i - 1), cols(h)]
            s_p = lax.dot_general(q, k_prev, nt, preferred_element_type=F32)
            s_c = lax.dot_general(q, k_ref[0, 0, rows(i), cols(h)], nt, preferred_element_type=F32)
            s = jnp.concatenate([s_p, s_c], axis=1)
            s_scr[rows(i * DIL_HEADS + h), :] = jnp.where(band_first if i == 0 else band, s, NEG_BIG)

    s_all = s_scr[...]
    m_all = jnp.max(s_all, axis=-1, keepdims=True)
    p_scr[...] = jnp.exp(s_all - m_all).astype(BF16)
    m_scr[...] = jnp.broadcast_to(m_all, m_scr.shape)

    lane = lax.broadcasted_iota(jnp.int32, (blk, LANES), 1)
    ones = jnp.ones((2 * blk, DIL_HEAD_DIM), BF16)
    for i in range(nblk):
        lse_tile = jnp.zeros((blk, LANES), F32)
        for h in range(DIL_HEADS):
            u = i * DIL_HEADS + h
            v_prev = vp_ref[0, 0, :, cols(h)] if i == 0 else v_ref[0, 0, rows(i - 1), cols(h)]
            v_aug = jnp.concatenate(
                [jnp.concatenate([v_prev, v_ref[0, 0, rows(i), cols(h)]], axis=0), ones], axis=1)
            acc = jnp.dot(p_scr[rows(u), :], v_aug, preferred_element_type=F32)
            den = acc[:, DIL_HEAD_DIM:]
            o_ref[0, 0, rows(i), cols(h)] = (acc[:, :DIL_HEAD_DIM] / den).astype(BF16)
            lse_tile = jnp.where(lane == h, m_scr[rows(u), :] + jnp.log(den), lse_tile)
        lse_ref[0, 0, rows(i), :] = lse_tile


def _dil_group(qkv, nblk=4):
    bsz, r, length, _ = qkv.shape
    nb = length // DIL_BLOCK
    nblk = min(nblk, nb)
    assert nb % nblk == 0
    tq = nblk * DIL_BLOCK
    units = nblk * DIL_HEADS * DIL_BLOCK

    def cur_spec(u):
        return pl.BlockSpec((1, 1, tq, DIL_OUT), lambda b, c, n: (b, c, n, u))

    def prev_spec(u):
        return pl.BlockSpec((1, 1, DIL_BLOCK, DIL_OUT),
                            lambda b, c, n: (b, c, jnp.maximum(n * nblk - 1, 0), u))

    return pl.pallas_call(
        functools.partial(_dil_kernel, nblk=nblk),
        grid=(bsz, r, nb // nblk),
        in_specs=[cur_spec(0), cur_spec(1), cur_spec(2), prev_spec(1), prev_spec(2)],
        out_specs=[
            pl.BlockSpec((1, 1, tq, DIL_OUT), lambda b, c, n: (b, c, n, 0)),
            pl.BlockSpec((1, 1, tq, LANES), lambda b, c, n: (b, c, n, 0)),
        ],
        out_shape=[
            jax.ShapeDtypeStruct((bsz, r, length, DIL_OUT), BF16),
            jax.ShapeDtypeStruct((bsz, r, length, LANES), F32),
        ],
        scratch_shapes=[pltpu.VMEM((units, 2 * DIL_BLOCK), F32),
                        pltpu.VMEM((units, LANES), F32),
                        pltpu.VMEM((units, 2 * DIL_BLOCK), BF16)],
        compiler_params=_cparams(("parallel", "parallel", "parallel")),
        name=f"dilattn_r{r}",
    )(qkv, qkv, qkv, qkv, qkv)


def _merge_kernel(oa_ref, o1_ref, o2_ref, o3_ref, l1_ref, l2_ref, l3_ref, ga_ref, gb_ref, x_ref,
                  g2_ref, wa_ref, wb_ref, wo_ref, lng_ref, lnb_ref, out_ref, l_scr, w_scr, ob_scr,
                  *, dilations):
    y_a = jnp.dot(oa_ref[...], wa_ref[...], preferred_element_type=F32)
    o_refs = (o1_ref, o2_ref, o3_ref)
    l_refs = (l1_ref, l2_ref, l3_ref)
    tm = oa_ref.shape[0]

    def tok_rows(c, r):
        return pl.ds(c, tm // r, stride=r) if r > 1 else slice(None)

    for g, r in enumerate(dilations):
        for c in range(r):
            l_scr[g, tok_rows(c, r), :] = l_refs[g][0, c]
    l1, l2, l3 = l_scr[0], l_scr[1], l_scr[2]
    mx = jnp.maximum(jnp.maximum(l1, l2), l3)
    e1, e2, e3 = jnp.exp(l1 - mx), jnp.exp(l2 - mx), jnp.exp(l3 - mx)
    inv = 1.0 / (e1 + e2 + e3)
    w_scr[0], w_scr[1], w_scr[2] = e1 * inv, e2 * inv, e3 * inv
    for g, r in enumerate(dilations):
        for c in range(r):
            rows = tok_rows(c, r)
            wr = w_scr[g, rows, :]
            for h in range(DIL_HEADS):
                sl = slice(h * DIL_HEAD_DIM, (h + 1) * DIL_HEAD_DIM)
                part = wr[:, h:h + 1] * o_refs[g][0, c, :, sl].astype(F32)
                if g == 0:
                    ob_scr[h, rows, :] = part
                else:
                    ob_scr[h, rows, :] += part
    o_b = jnp.concatenate([ob_scr[h].astype(BF16) for h in range(DIL_HEADS)], axis=1)
    y_b = jnp.dot(o_b, wb_ref[...], preferred_element_type=F32)
    merged = _sigmoid(ga_ref[...].astype(F32)) * y_a + _sigmoid(gb_ref[...].astype(F32)) * y_b
    mix = jnp.dot(merged.astype(BF16), wo_ref[...], preferred_element_type=F32)
    y = DN_ALPHA * x_ref[...] + g2_ref[0] * mix
    out_ref[...] = _layer_norm(y, lng_ref[...], lnb_ref[...])


def _merge(oa2d, o_groups, lse_groups, p2d, x2d, mods3, k_gate, w_a, w_b, w_o, ln_g, ln_b, seq,
           dilations, tm=256):
    m, d = x2d.shape
    tiles_per_seq = seq // tm
    resident = pl.Buffered(1)
    n_g = len(dilations)

    def row_spec(width, col=0):
        return pl.BlockSpec((tm, width), lambda i: (i, col))

    def class_spec(r, width):
        return pl.BlockSpec((1, r, tm // r, width),
                            lambda i: (i // tiles_per_seq, 0, i % tiles_per_seq, 0))

    def const_spec(shape):
        return pl.BlockSpec(shape, lambda i: (0, 0), pipeline_mode=resident)

    return pl.pallas_call(
        functools.partial(_merge_kernel, dilations=dilations),
        grid=(m // tm,),
        in_specs=[
            row_spec(GLA_VW),
            *[class_spec(r, DIL_OUT) for r in dilations],
            *[class_spec(r, LANES) for r in dilations],
            row_spec(D_MODEL, OFF_GA // D_MODEL), row_spec(D_MODEL, OFF_GB // D_MODEL),
            row_spec(d),
            pl.BlockSpec((1, 1, d), lambda i: ((i // tiles_per_seq) * N_MOD + k_gate, 0, 0)),
            const_spec(w_a.shape), const_spec(w_b.shape), const_spec(w_o.shape),
            const_spec((1, d)), const_spec((1, d)),
        ],
        out_specs=row_spec(d),
        out_shape=jax.ShapeDtypeStruct((m, d), F32),
        scratch_shapes=[pltpu.VMEM((n_g, tm, LANES), F32), pltpu.VMEM((n_g, tm, LANES), F32),
                        pltpu.VMEM((DIL_HEADS, tm, DIL_HEAD_DIM), F32)],
        compiler_params=_cparams(("parallel",)),
        name="merge",
    )(oa2d, *o_groups, *lse_groups, p2d, p2d, x2d, mods3, w_a, w_b, w_o, ln_g, ln_b)


def _pack_w_in(w_in):
    o = 0
    segs = []
    for wdt in (GLA_QK, GLA_QK, GLA_VW, GLA_VW, GLA_GATE_RANK, DIL_W, DIL_W, DIL_W, D_MODEL, D_MODEL):
        segs.append(w_in[:, o:o + wdt])
        o += wdt
    gq, gk, gv, gr, glr, dq, dk, dv, ga, gb = segs
    w_p = jnp.concatenate([gq, gk, gv, gr, ga, gb], axis=1).astype(BF16)
    w_lr = jnp.pad(glr, ((0, 0), (0, LR_PAD - GLA_GATE_RANK))).astype(BF16)
    w_groups = []
    for g in range(DIL_GROUPS):
        sl = slice(g * DIL_OUT, (g + 1) * DIL_OUT)
        w_groups.append(jnp.concatenate([dq[:, sl], dk[:, sl], dv[:, sl]], axis=1).astype(BF16))
    return w_p, w_lr, w_groups


def kernel(x, c, positions, w_ada, b_ada, ln1_g, ln1_b, w_ffn1_gu, w_ffn1_down, w_in, w_alpha2,
           b_alpha, gla_norm_g, w_branch_a, w_branch_b, w_out, ln2_g, ln2_b, w_ffn2_gu, w_ffn2_down,
           ln3_g, ln3_b):
    bsz, seq, d = x.shape
    m = bsz * seq
    x2d = x.reshape(m, d)
    c_pad = jnp.pad(c, ((0, 8 - bsz % 8 if bsz % 8 else 0), (0, 0)))
    dilations = tuple(r for _, r in DIL_PATTERNS)

    half = DIL_HEAD_DIM // 2
    freq = ROPE_THETA ** (-jnp.arange(half, dtype=F32) / half)
    freq2 = jnp.concatenate([freq, freq]).reshape(1, DIL_HEAD_DIM)
    rope_t = _rope_tables(positions.reshape(m // PROJ_TM, PROJ_TM // LANES, LANES), freq2, dilations)

    for l in range(DEPTH):
        mods = _mods(c_pad, w_ada[l], b_ada[l].reshape(1, -1))[:bsz]
        mods3 = mods.reshape(bsz * N_MOD, 1, d)

        x1 = _ffn(x2d, mods3, 0, 1, 2, w_ffn1_gu[l].astype(BF16), w_ffn1_down[l].astype(BF16),
                  ln1_g[l].reshape(1, d), ln1_b[l].reshape(1, d), seq)

        w_p, w_lr, w_groups = _pack_w_in(w_in[l])
        p2d, lr2d = _proj(x1, mods3, 3, 4, w_p, w_lr, seq)
        p3 = p2d.reshape(bsz, seq, PROJ_W)

        w_a2 = jnp.pad(w_alpha2[l], ((0, LR_PAD - GLA_GATE_RANK), (0, 0))).astype(BF16)
        o_a = _gla(p3, lr2d.reshape(bsz, seq, LR_PAD), w_a2, b_alpha[l].reshape(1, GLA_QK),
                   gla_norm_g[l].reshape(GLA_HEADS, 1, GLA_DV))

        o_groups, lse_groups = [], []
        for g, r in enumerate(dilations):
            qkv = _proj_dil(x1, mods3, 3, 4, w_groups[g], rope_t[2 * g], rope_t[2 * g + 1],
                            bsz, seq, r)
            o_g, lse_g = _dil_group(qkv)
            o_groups.append(o_g)
            lse_groups.append(lse_g)

        x2 = _merge(o_a.reshape(m, GLA_VW), o_groups, lse_groups, p2d, x1, mods3, 5,
                    w_branch_a[l].astype(BF16), w_branch_b[l].astype(BF16), w_out[l].astype(BF16),
                    ln2_g[l].reshape(1, d), ln2_b[l].reshape(1, d), seq, dilations)

        x2d = _ffn(x2, mods3, 6, 7, 8, w_ffn2_gu[l].astype(BF16), w_ffn2_down[l].astype(BF16),
                   ln3_g[l].reshape(1, d), ln3_b[l].reshape(1, d), seq)
    return x2d.reshape(bsz, seq, d)
```

```python
import functools

import jax
import jax.numpy as jnp
from jax import lax
from jax.experimental import pallas as pl
from jax.experimental.pallas import tpu as pltpu

F32 = jnp.float32
BF16 = jnp.bfloat16

D_MODEL = 2048
DEPTH = 1
D_FF = 5632
N_MOD = 9
LN_EPS = 1e-5
DN_ALPHA = (2.0 * DEPTH) ** 0.25

GLA_HEADS = 4
GLA_DK = 256
GLA_DV = 512
GLA_GATE_RANK = 16
GLA_TAU = 16.0
GLA_CHUNK = 64
GLA_QK = GLA_HEADS * GLA_DK
GLA_VW = GLA_HEADS * GLA_DV

DIL_PATTERNS = ((128, 1), (512, 4), (2048, 16))
DIL_GROUPS = len(DIL_PATTERNS)
DIL_HEADS = 8
DIL_HEAD_DIM = 128
DIL_BLOCK = 128
DIL_W = DIL_GROUPS * DIL_HEADS * DIL_HEAD_DIM
DIL_OUT = DIL_HEADS * DIL_HEAD_DIM
ROPE_THETA = 10000.0

LANES = 128
LR_PAD = LANES

PROJ_W = 2 * GLA_QK + 2 * GLA_VW + 2 * D_MODEL
OFF_GQ = 0
OFF_GK = GLA_QK
OFF_GV = 2 * GLA_QK
OFF_GR = OFF_GV + GLA_VW
OFF_GA = OFF_GR + GLA_VW
OFF_GB = OFF_GA + D_MODEL
DIL_QKV = 3 * DIL_OUT
OFF_LR = PROJ_W + DIL_GROUPS * DIL_QKV

NEG_BIG = -1e30

VMEM_LIMIT = 56 * 1024 * 1024


def _cparams(sem, vmem=VMEM_LIMIT):
    return pltpu.CompilerParams(dimension_semantics=sem, vmem_limit_bytes=vmem)


def _sigmoid(x):
    return 1.0 / (1.0 + jnp.exp(-x))


def _layer_norm(y, g, b):
    mu = jnp.mean(y, axis=-1, keepdims=True)
    d = y - mu
    var = jnp.mean(d * d, axis=-1, keepdims=True)
    return d * lax.rsqrt(var + LN_EPS) * g + b


def _mods_kernel(c_ref, w_ref, b_ref, o_ref):
    c = c_ref[...]
    c_act = (c * _sigmoid(c)).astype(BF16)
    o_ref[...] = jnp.dot(c_act, w_ref[...].astype(BF16), preferred_element_type=F32) + b_ref[...]


def _mods(c_pad, w_ada, b_ada, tn=1024):
    rows, d = c_pad.shape
    n = w_ada.shape[1]
    return pl.pallas_call(
        _mods_kernel,
        grid=(n // tn,),
        in_specs=[
            pl.BlockSpec((rows, d), lambda j: (0, 0)),
            pl.BlockSpec((d, tn), lambda j: (0, j)),
            pl.BlockSpec((1, tn), lambda j: (0, j)),
        ],
        out_specs=pl.BlockSpec((rows, tn), lambda j: (0, j)),
        out_shape=jax.ShapeDtypeStruct((rows, n), F32),
        compiler_params=_cparams(("arbitrary",)),
        name="mods",
    )(c_pad, w_ada, b_ada)


FFN_EPILOGUE_ROWS = 128


def _ffn_kernel(x_ref, sh_ref, sc_ref, g_ref, wg_ref, wu_ref, wd_ref, lng_ref, lnb_ref,
                o_ref, h_scr):
    j = pl.program_id(1)

    last = pl.num_programs(1) - 1

    @pl.when(j == 0)
    def _():
        h_scr[...] = (x_ref[...] * (1.0 + sc_ref[0]) + sh_ref[0]).astype(BF16)

    h = h_scr[...]
    gate = jnp.dot(h, wg_ref[...], preferred_element_type=F32)
    up = jnp.dot(h, wu_ref[...], preferred_element_type=F32)
    act = (gate * _sigmoid(gate) * up).astype(BF16)

    @pl.when(j == 0)
    def _():
        o_ref[...] = jnp.dot(act, wd_ref[...], preferred_element_type=F32)

    @pl.when(jnp.logical_and(j > 0, j < last))
    def _():
        o_ref[...] += jnp.dot(act, wd_ref[...], preferred_element_type=F32)

    @pl.when(j == last)
    def _():
        tm = o_ref.shape[0]
        slab = min(tm, FFN_EPILOGUE_ROWS)
        for s in range(tm // slab):
            rs = slice(s * slab, (s + 1) * slab)
            acc = o_ref[rs, :] + jnp.dot(act[rs, :], wd_ref[...], preferred_element_type=F32)
            y = DN_ALPHA * x_ref[rs, :] + 0.5 * g_ref[0] * acc
            o_ref[rs, :] = _layer_norm(y, lng_ref[...], lnb_ref[...])


def _ffn(x2d, mods3, k_shift, k_scale, k_gate, w_gu, w_down, ln_g, ln_b, seq, tm=512, tf=512):
    m, d = x2d.shape
    n_ff = w_down.shape[0]
    nj = n_ff // tf
    tiles_per_seq = seq // tm

    def mod_spec(k):
        return pl.BlockSpec((1, 1, d), lambda i, j: ((i // tiles_per_seq) * N_MOD + k, 0, 0))

    return pl.pallas_call(
        _ffn_kernel,
        grid=(m // tm, nj),
        in_specs=[
            pl.BlockSpec((tm, d), lambda i, j: (i, 0)),
            mod_spec(k_shift), mod_spec(k_scale), mod_spec(k_gate),
            pl.BlockSpec((d, tf), lambda i, j: (0, j)),
            pl.BlockSpec((d, tf), lambda i, j: (0, j + nj)),
            pl.BlockSpec((tf, d), lambda i, j: (j, 0)),
            pl.BlockSpec((1, d), lambda i, j: (0, 0)),
            pl.BlockSpec((1, d), lambda i, j: (0, 0)),
        ],
        out_specs=pl.BlockSpec((tm, d), lambda i, j: (i, 0)),
        out_shape=jax.ShapeDtypeStruct((m, d), F32),
        scratch_shapes=[pltpu.VMEM((tm, d), BF16)],
        compiler_params=_cparams(("parallel", "arbitrary")),
        name="ffn",
    )(x2d, mods3, mods3, mods3, w_gu, w_gu, w_down, ln_g, ln_b)


PROJ_TM = 1024


def _rope_kernel(pos_ref, freq_ref, *out_refs, dilations):
    cos_ref, sin_ref = out_refs[0], out_refs[1]
    tm = cos_ref.shape[0]
    pos_t = pos_ref[0].astype(F32).T
    lane = lax.broadcasted_iota(jnp.int32, (LANES, DIL_HEAD_DIM), 1)
    for k in range(tm // LANES):
        ang = pos_t[:, k:k + 1] * freq_ref[...]
        s = jnp.sin(ang)
        cos_ref[k * LANES:(k + 1) * LANES, :] = jnp.cos(ang)
        sin_ref[k * LANES:(k + 1) * LANES, :] = jnp.where(lane < DIL_HEAD_DIM // 2, -s, s)
    for gi, r in enumerate(dilations[1:]):
        tl = tm // r
        for c in range(r):
            out_refs[2 + 2 * gi][c * tl:(c + 1) * tl, :] = cos_ref[pl.ds(c, tl, stride=r), :]
            out_refs[3 + 2 * gi][c * tl:(c + 1) * tl, :] = sin_ref[pl.ds(c, tl, stride=r), :]


def _rope_tables(pos3, freq2, dilations, tm=PROJ_TM):
    assert dilations[0] == 1
    m = pos3.shape[0] * tm
    n_out = 2 * len(dilations)
    return pl.pallas_call(
        functools.partial(_rope_kernel, dilations=dilations),
        grid=(m // tm,),
        in_specs=[
            pl.BlockSpec((1, tm // LANES, LANES), lambda i: (i, 0, 0)),
            pl.BlockSpec((1, DIL_HEAD_DIM), lambda i: (0, 0)),
        ],
        out_specs=[pl.BlockSpec((tm, DIL_HEAD_DIM), lambda i: (i, 0))] * n_out,
        out_shape=[jax.ShapeDtypeStruct((m, DIL_HEAD_DIM), F32)] * n_out,
        compiler_params=_cparams(("parallel",)),
        name="rope_tables",
    )(pos3, freq2)


def _proj_kernel(x_ref, sh_ref, sc_ref, w_ref, wlr_ref, o_ref, lr_ref, h_scr):
    @pl.when(pl.program_id(1) == 0)
    def _():
        h = (x_ref[...] * (1.0 + sc_ref[0]) + sh_ref[0]).astype(BF16)
        h_scr[...] = h
        lr_ref[...] = jnp.dot(h, wlr_ref[...], preferred_element_type=F32)

    o_ref[...] = jnp.dot(h_scr[...], w_ref[...], preferred_element_type=F32).astype(BF16)


def _proj(x2d, mods3, k_shift, k_scale, w_all, seq, tm=PROJ_TM, tn=1024):
    m, d = x2d.shape
    n = PROJ_W
    tiles_per_seq = seq // tm

    def mod_spec(k):
        return pl.BlockSpec((1, 1, d), lambda i, j: ((i // tiles_per_seq) * N_MOD + k, 0, 0))

    return pl.pallas_call(
        _proj_kernel,
        grid=(m // tm, n // tn),
        in_specs=[
            pl.BlockSpec((tm, d), lambda i, j: (i, 0)),
            mod_spec(k_shift), mod_spec(k_scale),
            pl.BlockSpec((d, tn), lambda i, j: (0, j)),
            pl.BlockSpec((d, LR_PAD), lambda i, j: (0, OFF_LR // LR_PAD)),
        ],
        out_specs=[
            pl.BlockSpec((tm, tn), lambda i, j: (i, j)),
            pl.BlockSpec((tm, LR_PAD), lambda i, j: (i, 0)),
        ],
        out_shape=[
            jax.ShapeDtypeStruct((m, n), BF16),
            jax.ShapeDtypeStruct((m, LR_PAD), F32),
        ],
        scratch_shapes=[pltpu.VMEM((tm, d), BF16)],
        compiler_params=_cparams(("parallel", "arbitrary")),
        name="proj",
    )(x2d, mods3, mods3, w_all, w_all)


def _proj_dil_kernel(x_ref, sh_ref, sc_ref, w_ref, cos_ref, sin_ref, o_ref, h_scr, slab_scr, *, r):
    j = pl.program_id(1)
    tm, d = x_ref.shape
    tl = tm // r

    @pl.when(j == 0)
    def _():
        if r == 1:
            h_scr[...] = (x_ref[...] * (1.0 + sc_ref[0]) + sh_ref[0]).astype(BF16)
        else:
            for k in range(d // LANES):
                sl = slice(k * LANES, (k + 1) * LANES)
                slab_scr[k] = x_ref[:, sl] * (1.0 + sc_ref[0, :, sl]) + sh_ref[0, :, sl]
            for c in range(r):
                for k in range(d // LANES):
                    sl = slice(k * LANES, (k + 1) * LANES)
                    h_scr[c * tl:(c + 1) * tl, sl] = slab_scr[k, pl.ds(c, tl, stride=r), :].astype(BF16)

    t = jnp.dot(h_scr[...], w_ref[...], preferred_element_type=F32)
    n_slab = t.shape[1] // DIL_HEAD_DIM

    @pl.when(j == 2)
    def _():
        for c in range(r):
            o_ref[0, c] = t[c * tl:(c + 1) * tl, :].astype(BF16)

    @pl.when(j < 2)
    def _():
        scale = jnp.where(j == 0, DIL_HEAD_DIM ** -0.5, 1.0).astype(F32)
        cos = cos_ref[...] * scale
        sin = sin_ref[...] * scale
        for s in range(n_slab):
            sl = slice(s * DIL_HEAD_DIM, (s + 1) * DIL_HEAD_DIM)
            ts = t[:, sl]
            rot = (ts * cos + pltpu.roll(ts, DIL_HEAD_DIM // 2, 1) * sin).astype(BF16)
            for c in range(r):
                o_ref[0, c, :, sl] = rot[c * tl:(c + 1) * tl, :]


def _proj_dil(x2d, mods3, k_shift, k_scale, w_all, g, cos_t, sin_t, bsz, seq, r, tm=PROJ_TM):
    m, d = x2d.shape
    tiles_per_seq = seq // tm
    tl = tm // r
    col0 = (PROJ_W + g * DIL_QKV) // DIL_OUT

    def mod_spec(k):
        return pl.BlockSpec((1, 1, d), lambda i, j: ((i // tiles_per_seq) * N_MOD + k, 0, 0))

    return pl.pallas_call(
        functools.partial(_proj_dil_kernel, r=r),
        grid=(m // tm, 3),
        in_specs=[
            pl.BlockSpec((tm, d), lambda i, j: (i, 0)),
            mod_spec(k_shift), mod_spec(k_scale),
            pl.BlockSpec((d, DIL_OUT), lambda i, j: (0, col0 + j)),
            pl.BlockSpec((tm, DIL_HEAD_DIM), lambda i, j: (i, 0)),
            pl.BlockSpec((tm, DIL_HEAD_DIM), lambda i, j: (i, 0)),
        ],
        out_specs=pl.BlockSpec((1, r, tl, DIL_OUT),
                               lambda i, j: (i // tiles_per_seq, 0, i % tiles_per_seq, j)),
        out_shape=jax.ShapeDtypeStruct((bsz, r, seq // r, DIL_QKV), BF16),
        scratch_shapes=[pltpu.VMEM((tm, d), BF16),
                        pltpu.VMEM((d // LANES, tm, LANES) if r > 1 else (1, 8, LANES), F32)],
        compiler_params=_cparams(("parallel", "arbitrary")),
        name=f"proj_dil_r{r}",
    )(x2d, mods3, mods3, w_all, cos_t, sin_t)


def _gla_kernel(q_ref, k_ref, v_ref, r_ref, lr_ref, wa_ref, ba_ref, ng_ref, o_ref,
                st_scr, b_scr, qd_scr, qh_scr, ki_scr, ke_scr, kp_scr, kh_scr, a_scr, *, n_chunks):
    c_len = GLA_CHUNK
    tc = q_ref.shape[1]
    hb = 4 * c_len
    pw = 2 * c_len
    n_pairs = n_chunks // 2

    @pl.when(pl.program_id(2) == 0)
    def _():
        st_scr[...] = jnp.zeros_like(st_scr)

    ti = lax.broadcasted_iota(jnp.int32, (hb, hb), 0)
    tj = lax.broadcasted_iota(jnp.int32, (hb, hb), 1)
    tri = jnp.logical_and(ti >= tj, ti // c_len == tj // c_len).astype(BF16)
    for s in range(tc // hb):
        rows = slice(s * hb, (s + 1) * hb)
        logits = jnp.dot(lr_ref[0, rows, :].astype(BF16), wa_ref[...],
                         preferred_element_type=F32) + ba_ref[...]
        log_a = (jnp.minimum(logits, 0.0) - jnp.log(1.0 + jnp.exp(-jnp.abs(logits)))) / GLA_TAU
        p1 = log_a.astype(BF16)
        r1 = log_a - p1.astype(F32)
        p2 = r1.astype(BF16)
        p3 = (r1 - p2.astype(F32)).astype(BF16)
        b_scr[rows, :] = (jnp.dot(tri, p1, preferred_element_type=F32)
                          + jnp.dot(tri, p2, preferred_element_type=F32)
                          + jnp.dot(tri, p3, preferred_element_type=F32))

    nt = (((1,), (1,)), ((), ()))

    def chunk_rows(c):
        return slice(c * c_len, (c + 1) * c_len)

    def pair_rows(p):
        return slice(p * pw, (p + 1) * pw)

    b_last = [b_scr[(c + 1) * c_len - 1:(c + 1) * c_len, :] for c in range(n_chunks)]
    h_log = [jnp.zeros_like(b_last[0])]
    for c in range(n_chunks):
        h_log.append(h_log[-1] + b_last[c])

    for c in range(n_chunks):
        rows = chunk_rows(c)
        b = b_scr[rows, :]
        q = q_ref[0, rows, :].astype(F32)
        k = k_ref[0, rows, :].astype(F32)
        qd = q * jnp.exp(b) * (GLA_DK ** -0.5)
        ke = k * jnp.exp(b_last[c] - b)
        qd_scr[rows, :] = qd
        qh_scr[rows, :] = (qd * jnp.exp(h_log[c])).astype(BF16)
        ki_scr[rows, :] = (k * jnp.exp(-b)).astype(BF16)
        ke_scr[rows, :] = ke.astype(BF16)
        kp_scr[rows, :] = (ke * jnp.exp(b_last[c + 1]) if c % 2 == 0 else ke).astype(BF16)
        kh_scr[rows, :] = (ke * jnp.exp(h_log[n_chunks] - h_log[c + 1])).astype(BF16)

    pi = lax.broadcasted_iota(jnp.int32, (pw, pw), 0)
    pj = lax.broadcasted_iota(jnp.int32, (pw, pw), 1)
    diag = jnp.logical_and(pi >= pj, pi // c_len == pj // c_len)
    lower = jnp.logical_and(pi >= c_len, pj < c_len)
    for p in range(n_pairs):
        first = chunk_rows(2 * p)
        keys = jnp.concatenate([ki_scr[pair_rows(p), :], ke_scr[first, :], ke_scr[first, :]], axis=0)
        s = lax.dot_general(qd_scr[pair_rows(p), :].astype(BF16), keys, nt,
                            preferred_element_type=F32)
        s = jnp.where(diag, s[:, :pw], jnp.where(lower, s[:, pw:], 0.0))
        a_scr[pair_rows(p), pair_rows(p)] = s.astype(BF16)

    for p in range(n_pairs - 1):
        c0 = 2 * p + 2
        lhs = jnp.concatenate(
            [(qd_scr[chunk_rows(c), :] * jnp.exp(h_log[c] - h_log[c0])).astype(BF16)
             for c in range(c0, n_chunks)], axis=0)
        s = lax.dot_general(lhs, kp_scr[pair_rows(p), :], nt, preferred_element_type=F32)
        a_scr[c0 * c_len:, pair_rows(p)] = s.astype(BF16)

    half = 2 * pw
    for hf in range(tc // half):
        a_scr[hf * half:hf * half + pw, hf * half + pw:(hf + 1) * half] = jnp.zeros((pw, pw), BF16)

    st = st_scr[...]
    st_b = st.astype(BF16)
    for hf in range(tc // half):
        rows = slice(hf * half, (hf + 1) * half)
        keys = (hf + 1) * half
        o = jnp.dot(a_scr[rows, :keys], v_ref[0, :keys, :], preferred_element_type=F32)
        o += jnp.dot(qh_scr[rows, :], st_b, preferred_element_type=F32)
        o = o * lax.rsqrt(jnp.mean(o * o, axis=-1, keepdims=True) + LN_EPS) * ng_ref[0]
        r = r_ref[0, rows, :].astype(F32)
        o_ref[0, rows, :] = (o * (r * _sigmoid(r))).astype(BF16)

    d_col = jnp.broadcast_to(jnp.exp(h_log[n_chunks]), (8, GLA_DK)).T[:, :1]
    st_scr[...] = st * d_col + lax.dot_general(
        kh_scr[...], v_ref[0], (((0,), (0,)), ((), ())), preferred_element_type=F32)


def _gla(p3, lr3, w_a2, b_a, norm_g3, tc=512):
    bsz, seq, _ = p3.shape
    nq, nv, nr = OFF_GK // GLA_DK, OFF_GV // GLA_DV, OFF_GR // GLA_DV
    return pl.pallas_call(
        functools.partial(_gla_kernel, n_chunks=tc // GLA_CHUNK),
        grid=(bsz, GLA_HEADS, seq // tc),
        in_specs=[
            pl.BlockSpec((1, tc, GLA_DK), lambda b, h, t: (b, t, h)),
            pl.BlockSpec((1, tc, GLA_DK), lambda b, h, t: (b, t, nq + h)),
            pl.BlockSpec((1, tc, GLA_DV), lambda b, h, t: (b, t, nv + h)),
            pl.BlockSpec((1, tc, GLA_DV), lambda b, h, t: (b, t, nr + h)),
            pl.BlockSpec((1, tc, LR_PAD), lambda b, h, t: (b, t, 0)),
            pl.BlockSpec((LR_PAD, GLA_DK), lambda b, h, t: (0, h)),
            pl.BlockSpec((1, GLA_DK), lambda b, h, t: (0, h)),
            pl.BlockSpec((1, 1, GLA_DV), lambda b, h, t: (h, 0, 0)),
        ],
        out_specs=pl.BlockSpec((1, tc, GLA_DV), lambda b, h, t: (b, t, h)),
        out_shape=jax.ShapeDtypeStruct((bsz, seq, GLA_VW), BF16),
        scratch_shapes=[pltpu.VMEM((GLA_DK, GLA_DV), F32), pltpu.VMEM((tc, GLA_DK), F32),
                        pltpu.VMEM((tc, GLA_DK), F32)]
                       + [pltpu.VMEM((tc, GLA_DK), BF16)] * 5 + [pltpu.VMEM((tc, tc), BF16)],
        compiler_params=_cparams(("parallel", "parallel", "arbitrary")),
        name="gla",
    )(p3, p3, p3, p3, lr3, w_a2, b_a, norm_g3)


def _dil_kernel(q_ref, k_ref, v_ref, kp_ref, vp_ref, o_ref, lse_ref, s_scr, m_scr, p_scr, *, nblk):
    n = pl.program_id(2)
    blk = DIL_BLOCK
    qi = lax.broadcasted_iota(jnp.int32, (blk, 2 * blk), 0)
    kj = lax.broadcasted_iota(jnp.int32, (blk, 2 * blk), 1)
    band = jnp.logical_or(jnp.logical_and(kj < blk, kj >= qi),
                          jnp.logical_and(kj >= blk, kj - blk <= qi))
    band_first = jnp.logical_and(band, jnp.logical_or(kj >= blk, n > 0))
    nt = (((1,), (1,)), ((), ()))

    def rows(i):
        return slice(i * blk, (i + 1) * blk)

    def cols(h):
        return slice(h * DIL_HEAD_DIM, (h + 1) * DIL_HEAD_DIM)

    for i in range(nblk):
        for h in range(DIL_HEADS):
            q = q_ref[0, 0, rows(i), cols(h)]
            k_prev = kp_ref[0, 0, :, cols(h)] if i == 0 else k_ref[0, 0, rows(i - 1), cols(h)]
            s_p = lax.dot_general(q, k_prev, nt, preferred_element_type=F32)
            s_c = lax.dot_general(q, k_ref[0, 0, rows(i), cols(h)], nt, preferred_element_type=F32)
            s = jnp.concatenate([s_p, s_c], axis=1)
            s_scr[rows(i * DIL_HEADS + h), :] = jnp.where(band_first if i == 0 else band, s, NEG_BIG)

    s_all = s_scr[...]
    m_all = jnp.max(s_all, axis=-1, keepdims=True)
    p_scr[...] = jnp.exp(s_all - m_all).astype(BF16)
    m_scr[...] = jnp.broadcast_to(m_all, m_scr.shape)

    lane = lax.broadcasted_iota(jnp.int32, (blk, LANES), 1)
    ones = jnp.ones((2 * blk, DIL_HEAD_DIM), BF16)
    for i in range(nblk):
        lse_tile = jnp.zeros((blk, LANES), F32)
        for h in range(DIL_HEADS):
            u = i * DIL_HEADS + h
            v_prev = vp_ref[0, 0, :, cols(h)] if i == 0 else v_ref[0, 0, rows(i - 1), cols(h)]
            v_aug = jnp.concatenate(
                [jnp.concatenate([v_prev, v_ref[0, 0, rows(i), cols(h)]], axis=0), ones], axis=1)
            acc = jnp.dot(p_scr[rows(u), :], v_aug, preferred_element_type=F32)
            den = acc[:, DIL_HEAD_DIM:]
            o_ref[0, 0, rows(i), cols(h)] = (acc[:, :DIL_HEAD_DIM] / den).astype(BF16)
            lse_tile = jnp.where(lane == h, m_scr[rows(u), :] + jnp.log(den), lse_tile)
        lse_ref[0, 0, rows(i), :] = lse_tile


def _dil_group(qkv, nblk=4):
    bsz, r, length, _ = qkv.shape
    nb = length // DIL_BLOCK
    nblk = min(nblk, nb)
    assert nb % nblk == 0
    tq = nblk * DIL_BLOCK
    units = nblk * DIL_HEADS * DIL_BLOCK

    def cur_spec(u):
        return pl.BlockSpec((1, 1, tq, DIL_OUT), lambda b, c, n: (b, c, n, u))

    def prev_spec(u):
        return pl.BlockSpec((1, 1, DIL_BLOCK, DIL_OUT),
                            lambda b, c, n: (b, c, jnp.maximum(n * nblk - 1, 0), u))

    return pl.pallas_call(
        functools.partial(_dil_kernel, nblk=nblk),
        grid=(bsz, r, nb // nblk),
        in_specs=[cur_spec(0), cur_spec(1), cur_spec(2), prev_spec(1), prev_spec(2)],
        out_specs=[
            pl.BlockSpec((1, 1, tq, DIL_OUT), lambda b, c, n: (b, c, n, 0)),
            pl.BlockSpec((1, 1, tq, LANES), lambda b, c, n: (b, c, n, 0)),
        ],
        out_shape=[
            jax.ShapeDtypeStruct((bsz, r, length, DIL_OUT), BF16),
            jax.ShapeDtypeStruct((bsz, r, length, LANES), F32),
        ],
        scratch_shapes=[pltpu.VMEM((units, 2 * DIL_BLOCK), F32),
                        pltpu.VMEM((units, LANES), F32),
                        pltpu.VMEM((units, 2 * DIL_BLOCK), BF16)],
        compiler_params=_cparams(("parallel", "parallel", "parallel")),
        name=f"dilattn_r{r}",
    )(qkv, qkv, qkv, qkv, qkv)


def _merge_kernel(oa_ref, o1_ref, o2_ref, o3_ref, l1_ref, l2_ref, l3_ref, ga_ref, gb_ref, x_ref,
                  g2_ref, wa_ref, wb_ref, wo_ref, lng_ref, lnb_ref, out_ref, l_scr, w_scr, ob_scr,
                  *, dilations):
    y_a = jnp.dot(oa_ref[...], wa_ref[...], preferred_element_type=F32)
    o_refs = (o1_ref, o2_ref, o3_ref)
    l_refs = (l1_ref, l2_ref, l3_ref)
    tm = oa_ref.shape[0]

    def tok_rows(c, r):
        return pl.ds(c, tm // r, stride=r) if r > 1 else slice(None)

    for g, r in enumerate(dilations):
        for c in range(r):
            l_scr[g, tok_rows(c, r), :] = l_refs[g][0, c]
    l1, l2, l3 = l_scr[0], l_scr[1], l_scr[2]
    mx = jnp.maximum(jnp.maximum(l1, l2), l3)
    e1, e2, e3 = jnp.exp(l1 - mx), jnp.exp(l2 - mx), jnp.exp(l3 - mx)
    inv = 1.0 / (e1 + e2 + e3)
    w_scr[0], w_scr[1], w_scr[2] = e1 * inv, e2 * inv, e3 * inv
    for g, r in enumerate(dilations):
        for c in range(r):
            rows = tok_rows(c, r)
            wr = w_scr[g, rows, :]
            for h in range(DIL_HEADS):
                sl = slice(h * DIL_HEAD_DIM, (h + 1) * DIL_HEAD_DIM)
                part = wr[:, h:h + 1] * o_refs[g][0, c, :, sl].astype(F32)
                if g == 0:
                    ob_scr[h, rows, :] = part
                else:
                    ob_scr[h, rows, :] += part
    o_b = jnp.concatenate([ob_scr[h].astype(BF16) for h in range(DIL_HEADS)], axis=1)
    y_b = jnp.dot(o_b, wb_ref[...], preferred_element_type=F32)
    merged = _sigmoid(ga_ref[...].astype(F32)) * y_a + _sigmoid(gb_ref[...].astype(F32)) * y_b
    mix = jnp.dot(merged.astype(BF16), wo_ref[...], preferred_element_type=F32)
    y = DN_ALPHA * x_ref[...] + g2_ref[0] * mix
    out_ref[...] = _layer_norm(y, lng_ref[...], lnb_ref[...])


def _merge(oa2d, o_groups, lse_groups, p2d, x2d, mods3, k_gate, w_a, w_b, w_o, ln_g, ln_b, seq,
           dilations, tm=256):
    m, d = x2d.shape
    tiles_per_seq = seq // tm
    resident = pl.Buffered(1)
    n_g = len(dilations)

    def row_spec(width, col=0):
        return pl.BlockSpec((tm, width), lambda i: (i, col))

    def class_spec(r, width):
        return pl.BlockSpec((1, r, tm // r, width),
                            lambda i: (i // tiles_per_seq, 0, i % tiles_per_seq, 0))

    def const_spec(shape):
        return pl.BlockSpec(shape, lambda i: (0, 0), pipeline_mode=resident)

    return pl.pallas_call(
        functools.partial(_merge_kernel, dilations=dilations),
        grid=(m // tm,),
        in_specs=[
            row_spec(GLA_VW),
            *[class_spec(r, DIL_OUT) for r in dilations],
            *[class_spec(r, LANES) for r in dilations],
            row_spec(D_MODEL, OFF_GA // D_MODEL), row_spec(D_MODEL, OFF_GB // D_MODEL),
            row_spec(d),
            pl.BlockSpec((1, 1, d), lambda i: ((i // tiles_per_seq) * N_MOD + k_gate, 0, 0)),
            const_spec(w_a.shape), const_spec(w_b.shape), const_spec(w_o.shape),
            const_spec((1, d)), const_spec((1, d)),
        ],
        out_specs=row_spec(d),
        out_shape=jax.ShapeDtypeStruct((m, d), F32),
        scratch_shapes=[pltpu.VMEM((n_g, tm, LANES), F32), pltpu.VMEM((n_g, tm, LANES), F32),
                        pltpu.VMEM((DIL_HEADS, tm, DIL_HEAD_DIM), F32)],
        compiler_params=_cparams(("parallel",)),
        name="merge",
    )(oa2d, *o_groups, *lse_groups, p2d, p2d, x2d, mods3, w_a, w_b, w_o, ln_g, ln_b)


def _pack_w_in(w_in):
    o = 0
    segs = []
    for wdt in (GLA_QK, GLA_QK, GLA_VW, GLA_VW, GLA_GATE_RANK, DIL_W, DIL_W, DIL_W, D_MODEL, D_MODEL):
        segs.append(w_in[:, o:o + wdt].astype(BF16))
        o += wdt
    gq, gk, gv, gr, glr, dq, dk, dv, ga, gb = segs
    cols = [gq, gk, gv, gr, ga, gb]
    for g in range(DIL_GROUPS):
        sl = slice(g * DIL_OUT, (g + 1) * DIL_OUT)
        cols += [dq[:, sl], dk[:, sl], dv[:, sl]]
    cols.append(jnp.pad(glr, ((0, 0), (0, LR_PAD - GLA_GATE_RANK))))
    return jnp.concatenate(cols, axis=1)


def kernel(x, c, positions, w_ada, b_ada, ln1_g, ln1_b, w_ffn1_gu, w_ffn1_down, w_in, w_alpha2,
           b_alpha, gla_norm_g, w_branch_a, w_branch_b, w_out, ln2_g, ln2_b, w_ffn2_gu, w_ffn2_down,
           ln3_g, ln3_b):
    bsz, seq, d = x.shape
    m = bsz * seq
    x2d = x.reshape(m, d)
    c_pad = jnp.pad(c, ((0, 8 - bsz % 8 if bsz % 8 else 0), (0, 0)))
    dilations = tuple(r for _, r in DIL_PATTERNS)

    half = DIL_HEAD_DIM // 2
    freq = ROPE_THETA ** (-jnp.arange(half, dtype=F32) / half)
    freq2 = jnp.concatenate([freq, freq]).reshape(1, DIL_HEAD_DIM)
    rope_t = _rope_tables(positions.reshape(m // PROJ_TM, PROJ_TM // LANES, LANES), freq2, dilations)

    for l in range(DEPTH):
        mods = _mods(c_pad, w_ada[l], b_ada[l].reshape(1, -1))[:bsz]
        mods3 = mods.reshape(bsz * N_MOD, 1, d)

        x1 = _ffn(x2d, mods3, 0, 1, 2, w_ffn1_gu[l].astype(BF16), w_ffn1_down[l].astype(BF16),
                  ln1_g[l].reshape(1, d), ln1_b[l].reshape(1, d), seq)

        w_all = _pack_w_in(w_in[l])
        p2d, lr2d = _proj(x1, mods3, 3, 4, w_all, seq)
        p3 = p2d.reshape(bsz, seq, PROJ_W)

        w_a2 = jnp.pad(w_alpha2[l], ((0, LR_PAD - GLA_GATE_RANK), (0, 0))).astype(BF16)
        o_a = _gla(p3, lr2d.reshape(bsz, seq, LR_PAD), w_a2, b_alpha[l].reshape(1, GLA_QK),
                   gla_norm_g[l].reshape(GLA_HEADS, 1, GLA_DV))

        o_groups, lse_groups = [], []
        for g, r in enumerate(dilations):
            qkv = _proj_dil(x1, mods3, 3, 4, w_all, g, rope_t[2 * g], rope_t[2 * g + 1],
                            bsz, seq, r)
            o_g, lse_g = _dil_group(qkv)
            o_groups.append(o_g)
            lse_groups.append(lse_g)

        x2 = _merge(o_a.reshape(m, GLA_VW), o_groups, lse_groups, p2d, x1, mods3, 5,
                    w_branch_a[l].astype(BF16), w_branch_b[l].astype(BF16), w_out[l].astype(BF16),
                    ln2_g[l].reshape(1, d), ln2_b[l].reshape(1, d), seq, dilations)

        x2d = _ffn(x2, mods3, 6, 7, 8, w_ffn2_gu[l].astype(BF16), w_ffn2_down[l].astype(BF16),
                   ln3_g[l].reshape(1, d), ln3_b[l].reshape(1, d), seq)
    return x2d.reshape(bsz, seq, d)
```

```python
import functools

import jax
import jax.numpy as jnp
from jax import lax
from jax.experimental import pallas as pl
from jax.experimental.pallas import tpu as pltpu

F32 = jnp.float32
BF16 = jnp.bfloat16

D_MODEL = 2048
DEPTH = 1
D_FF = 5632
N_MOD = 9
LN_EPS = 1e-5
DN_ALPHA = (2.0 * DEPTH) ** 0.25

GLA_HEADS = 4
GLA_DK = 256
GLA_DV = 512
GLA_GATE_RANK = 16
GLA_TAU = 16.0
GLA_CHUNK = 64
GLA_QK = GLA_HEADS * GLA_DK
GLA_VW = GLA_HEADS * GLA_DV

DIL_PATTERNS = ((128, 1), (512, 4), (2048, 16))
DIL_GROUPS = len(DIL_PATTERNS)
DIL_HEADS = 8
DIL_HEAD_DIM = 128
DIL_BLOCK = 128
DIL_W = DIL_GROUPS * DIL_HEADS * DIL_HEAD_DIM
DIL_OUT = DIL_HEADS * DIL_HEAD_DIM
ROPE_THETA = 10000.0

LANES = 128
LR_PAD = LANES

PROJ_W = 2 * GLA_QK + 2 * GLA_VW + 2 * D_MODEL
OFF_GQ = 0
OFF_GK = GLA_QK
OFF_GV = 2 * GLA_QK
OFF_GR = OFF_GV + GLA_VW
OFF_GA = OFF_GR + GLA_VW
OFF_GB = OFF_GA + D_MODEL
DIL_QKV = 3 * DIL_OUT
OFF_LR = PROJ_W + DIL_GROUPS * DIL_QKV

NEG_BIG = -1e30

VMEM_LIMIT = 56 * 1024 * 1024


def _cparams(sem, vmem=VMEM_LIMIT):
    return pltpu.CompilerParams(dimension_semantics=sem, vmem_limit_bytes=vmem)


def _sigmoid(x):
    return 1.0 / (1.0 + jnp.exp(-x))


def _layer_norm(y, g, b):
    mu = jnp.mean(y, axis=-1, keepdims=True)
    d = y - mu
    var = jnp.mean(d * d, axis=-1, keepdims=True)
    return d * lax.rsqrt(var + LN_EPS) * g + b


def _mods_kernel(c_ref, w_ref, b_ref, o_ref):
    c = c_ref[...]
    c_act = (c * _sigmoid(c)).astype(BF16)
    o_ref[...] = jnp.dot(c_act, w_ref[...].astype(BF16), preferred_element_type=F32) + b_ref[...]


def _mods(c_pad, w_ada, b_ada, tn=1024):
    rows, d = c_pad.shape
    n = w_ada.shape[1]
    return pl.pallas_call(
        _mods_kernel,
        grid=(n // tn,),
        in_specs=[
            pl.BlockSpec((rows, d), lambda j: (0, 0)),
            pl.BlockSpec((d, tn), lambda j: (0, j)),
            pl.BlockSpec((1, tn), lambda j: (0, j)),
        ],
        out_specs=pl.BlockSpec((rows, tn), lambda j: (0, j)),
        out_shape=jax.ShapeDtypeStruct((rows, n), F32),
        compiler_params=_cparams(("arbitrary",)),
        name="mods",
    )(c_pad, w_ada, b_ada)


def _ffn_kernel(x_ref, sh_ref, sc_ref, g_ref, wg_ref, wu_ref, wd_ref, lng_ref, lnb_ref,
                o_ref, h_scr):
    j = pl.program_id(1)

    @pl.when(j == 0)
    def _():
        h_scr[...] = (x_ref[...] * (1.0 + sc_ref[0]) + sh_ref[0]).astype(BF16)
        o_ref[...] = jnp.zeros_like(o_ref)

    h = h_scr[...]
    gate = jnp.dot(h, wg_ref[...], preferred_element_type=F32)
    up = jnp.dot(h, wu_ref[...], preferred_element_type=F32)
    act = (gate * _sigmoid(gate) * up).astype(BF16)
    o_ref[...] += jnp.dot(act, wd_ref[...], preferred_element_type=F32)

    @pl.when(j == pl.num_programs(1) - 1)
    def _():
        y = DN_ALPHA * x_ref[...] + 0.5 * g_ref[0] * o_ref[...]
        o_ref[...] = _layer_norm(y, lng_ref[...], lnb_ref[...])


def _ffn(x2d, mods3, k_shift, k_scale, k_gate, w_gu, w_down, ln_g, ln_b, seq, tm=512, tf=512):
    m, d = x2d.shape
    n_ff = w_down.shape[0]
    nj = n_ff // tf
    tiles_per_seq = seq // tm

    def mod_spec(k):
        return pl.BlockSpec((1, 1, d), lambda i, j: ((i // tiles_per_seq) * N_MOD + k, 0, 0))

    return pl.pallas_call(
        _ffn_kernel,
        grid=(m // tm, nj),
        in_specs=[
            pl.BlockSpec((tm, d), lambda i, j: (i, 0)),
            mod_spec(k_shift), mod_spec(k_scale), mod_spec(k_gate),
            pl.BlockSpec((d, tf), lambda i, j: (0, j)),
            pl.BlockSpec((d, tf), lambda i, j: (0, j + nj)),
            pl.BlockSpec((tf, d), lambda i, j: (j, 0)),
            pl.BlockSpec((1, d), lambda i, j: (0, 0)),
            pl.BlockSpec((1, d), lambda i, j: (0, 0)),
        ],
        out_specs=pl.BlockSpec((tm, d), lambda i, j: (i, 0)),
        out_shape=jax.ShapeDtypeStruct((m, d), F32),
        scratch_shapes=[pltpu.VMEM((tm, d), BF16)],
        compiler_params=_cparams(("parallel", "arbitrary")),
        name="ffn",
    )(x2d, mods3, mods3, mods3, w_gu, w_gu, w_down, ln_g, ln_b)


PROJ_TM = 1024


def _rope_kernel(pos_ref, freq_ref, *out_refs, dilations):
    cos_ref, sin_ref = out_refs[0], out_refs[1]
    tm = cos_ref.shape[0]
    pos_t = pos_ref[0].astype(F32).T
    lane = lax.broadcasted_iota(jnp.int32, (LANES, DIL_HEAD_DIM), 1)
    for k in range(tm // LANES):
        ang = pos_t[:, k:k + 1] * freq_ref[...]
        s = jnp.sin(ang)
        cos_ref[k * LANES:(k + 1) * LANES, :] = jnp.cos(ang)
        sin_ref[k * LANES:(k + 1) * LANES, :] = jnp.where(lane < DIL_HEAD_DIM // 2, -s, s)
    for gi, r in enumerate(dilations[1:]):
        tl = tm // r
        for c in range(r):
            out_refs[2 + 2 * gi][c * tl:(c + 1) * tl, :] = cos_ref[pl.ds(c, tl, stride=r), :]
            out_refs[3 + 2 * gi][c * tl:(c + 1) * tl, :] = sin_ref[pl.ds(c, tl, stride=r), :]


def _rope_tables(pos3, freq2, dilations, tm=PROJ_TM):
    assert dilations[0] == 1
    m = pos3.shape[0] * tm
    n_out = 2 * len(dilations)
    return pl.pallas_call(
        functools.partial(_rope_kernel, dilations=dilations),
        grid=(m // tm,),
        in_specs=[
            pl.BlockSpec((1, tm // LANES, LANES), lambda i: (i, 0, 0)),
            pl.BlockSpec((1, DIL_HEAD_DIM), lambda i: (0, 0)),
        ],
        out_specs=[pl.BlockSpec((tm, DIL_HEAD_DIM), lambda i: (i, 0))] * n_out,
        out_shape=[jax.ShapeDtypeStruct((m, DIL_HEAD_DIM), F32)] * n_out,
        compiler_params=_cparams(("parallel",)),
        name="rope_tables",
    )(pos3, freq2)


def _proj_kernel(x_ref, sh_ref, sc_ref, w_ref, wlr_ref, o_ref, lr_ref, h_scr):
    @pl.when(pl.program_id(1) == 0)
    def _():
        h = (x_ref[...] * (1.0 + sc_ref[0]) + sh_ref[0]).astype(BF16)
        h_scr[...] = h
        lr_ref[...] = jnp.dot(h, wlr_ref[...], preferred_element_type=F32)

    o_ref[...] = jnp.dot(h_scr[...], w_ref[...], preferred_element_type=F32).astype(BF16)


def _proj(x2d, mods3, k_shift, k_scale, w_all, seq, tm=PROJ_TM, tn=2048):
    m, d = x2d.shape
    n = PROJ_W
    tiles_per_seq = seq // tm

    def mod_spec(k):
        return pl.BlockSpec((1, 1, d), lambda i, j: ((i // tiles_per_seq) * N_MOD + k, 0, 0))

    return pl.pallas_call(
        _proj_kernel,
        grid=(m // tm, n // tn),
        in_specs=[
            pl.BlockSpec((tm, d), lambda i, j: (i, 0)),
            mod_spec(k_shift), mod_spec(k_scale),
            pl.BlockSpec((d, tn), lambda i, j: (0, j)),
            pl.BlockSpec((d, LR_PAD), lambda i, j: (0, OFF_LR // LR_PAD)),
        ],
        out_specs=[
            pl.BlockSpec((tm, tn), lambda i, j: (i, j)),
            pl.BlockSpec((tm, LR_PAD), lambda i, j: (i, 0)),
        ],
        out_shape=[
            jax.ShapeDtypeStruct((m, n), BF16),
            jax.ShapeDtypeStruct((m, LR_PAD), F32),
        ],
        scratch_shapes=[pltpu.VMEM((tm, d), BF16)],
        compiler_params=_cparams(("parallel", "arbitrary")),
        name="proj",
    )(x2d, mods3, mods3, w_all, w_all)


def _proj_dil_kernel(x_ref, sh_ref, sc_ref, w_ref, cos_ref, sin_ref, o_ref, h_scr, slab_scr, *, r):
    j = pl.program_id(1)
    tm, d = x_ref.shape
    tl = tm // r

    @pl.when(j == 0)
    def _():
        if r == 1:
            h_scr[...] = (x_ref[...] * (1.0 + sc_ref[0]) + sh_ref[0]).astype(BF16)
        else:
            for k in range(d // LANES):
                sl = slice(k * LANES, (k + 1) * LANES)
                slab_scr[k] = x_ref[:, sl] * (1.0 + sc_ref[0, :, sl]) + sh_ref[0, :, sl]
            for c in range(r):
                for k in range(d // LANES):
                    sl = slice(k * LANES, (k + 1) * LANES)
                    h_scr[c * tl:(c + 1) * tl, sl] = slab_scr[k, pl.ds(c, tl, stride=r), :].astype(BF16)

    _proj_dil_tile(j, h_scr, w_ref, cos_ref, sin_ref, o_ref, r)


def _proj_dil_tile(j, h_scr, w_ref, cos_ref, sin_ref, o_ref, r):
    tl = h_scr.shape[0] // r
    t = jnp.dot(h_scr[...], w_ref[...], preferred_element_type=F32)
    is_v = j == 2
    scale = jnp.where(j == 0, DIL_HEAD_DIM ** -0.5, 1.0).astype(F32)
    a = jnp.where(is_v, 1.0, cos_ref[...] * scale)
    b = jnp.where(is_v, 0.0, sin_ref[...] * scale)
    for s in range(t.shape[1] // DIL_HEAD_DIM):
        sl = slice(s * DIL_HEAD_DIM, (s + 1) * DIL_HEAD_DIM)
        ts = t[:, sl]
        rot = (ts * a + pltpu.roll(ts, DIL_HEAD_DIM // 2, 1) * b).astype(BF16)
        for c in range(r):
            o_ref[0, c, :, sl] = rot[c * tl:(c + 1) * tl, :]


def _proj_dil_dma_kernel(x_hbm, sh_ref, sc_ref, w_ref, cos_ref, sin_ref, o_ref, h_scr, x_buf, sem,
                         *, r):
    i = pl.program_id(0)
    j = pl.program_id(1)
    tl = h_scr.shape[0] // r
    slot = i % 2

    def tile_copies(tile, buf_slot):
        return [pltpu.make_async_copy(x_hbm.at[pl.ds(tile * tl, tl), c], x_buf.at[buf_slot, c],
                                      sem.at[buf_slot]) for c in range(r)]

    @pl.when(j == 0)
    def _():
        @pl.when(i == 0)
        def _():
            for cp in tile_copies(0, 0):
                cp.start()

        @pl.when(i + 1 < pl.num_programs(0))
        def _():
            for cp in tile_copies(i + 1, 1 - slot):
                cp.start()

        for cp in tile_copies(i, slot):
            cp.wait()
        for c in range(r):
            h_scr[c * tl:(c + 1) * tl, :] = (x_buf[slot, c] * (1.0 + sc_ref[0]) + sh_ref[0]).astype(BF16)

    _proj_dil_tile(j, h_scr, w_ref, cos_ref, sin_ref, o_ref, r)


SUBLANES_F32 = 8


def _proj_dil(x2d, mods3, k_shift, k_scale, w_all, g, cos_t, sin_t, bsz, seq, r, tm=PROJ_TM):
    m, d = x2d.shape
    tiles_per_seq = seq // tm
    tl = tm // r
    col0 = (PROJ_W + g * DIL_QKV) // DIL_OUT
    dma_gather = r % SUBLANES_F32 == 0

    def mod_spec(k):
        return pl.BlockSpec((1, 1, d), lambda i, j: ((i // tiles_per_seq) * N_MOD + k, 0, 0))

    if dma_gather:
        body = functools.partial(_proj_dil_dma_kernel, r=r)
        x_arg = x2d.reshape(m // r, r, d)
        x_spec = pl.BlockSpec(memory_space=pl.ANY)
        scratch = [pltpu.VMEM((tm, d), BF16), pltpu.VMEM((2, r, tl, d), F32),
                   pltpu.SemaphoreType.DMA((2,))]
        semantics = ("arbitrary", "arbitrary")
    else:
        body = functools.partial(_proj_dil_kernel, r=r)
        x_arg = x2d
        x_spec = pl.BlockSpec((tm, d), lambda i, j: (i, 0))
        scratch = [pltpu.VMEM((tm, d), BF16),
                   pltpu.VMEM((d // LANES, tm, LANES) if r > 1 else (1, 8, LANES), F32)]
        semantics = ("parallel", "arbitrary")

    return pl.pallas_call(
        body,
        grid=(m // tm, 3),
        in_specs=[
            x_spec,
            mod_spec(k_shift), mod_spec(k_scale),
            pl.BlockSpec((d, DIL_OUT), lambda i, j: (0, col0 + j)),
            pl.BlockSpec((tm, DIL_HEAD_DIM), lambda i, j: (i, 0)),
            pl.BlockSpec((tm, DIL_HEAD_DIM), lambda i, j: (i, 0)),
        ],
        out_specs=pl.BlockSpec((1, r, tl, DIL_OUT),
                               lambda i, j: (i // tiles_per_seq, 0, i % tiles_per_seq, j)),
        out_shape=jax.ShapeDtypeStruct((bsz, r, seq // r, DIL_QKV), BF16),
        scratch_shapes=scratch,
        compiler_params=_cparams(semantics),
        name=f"proj_dil_r{r}",
    )(x_arg, mods3, mods3, w_all, cos_t, sin_t)


def _gla_kernel(q_ref, k_ref, v_ref, r_ref, lr_ref, wa_ref, ba_ref, ng_ref, o_ref,
                st_scr, b_scr, qd_scr, qh_scr, ki_scr, ke_scr, kp_scr, kh_scr, a_scr, *, n_chunks):
    c_len = GLA_CHUNK
    tc = q_ref.shape[1]
    hb = 4 * c_len
    pw = 2 * c_len
    n_pairs = n_chunks // 2

    @pl.when(pl.program_id(2) == 0)
    def _():
        st_scr[...] = jnp.zeros_like(st_scr)

    ti = lax.broadcasted_iota(jnp.int32, (hb, hb), 0)
    tj = lax.broadcasted_iota(jnp.int32, (hb, hb), 1)
    tri = jnp.logical_and(ti >= tj, ti // c_len == tj // c_len).astype(BF16)
    for s in range(tc // hb):
        rows = slice(s * hb, (s + 1) * hb)
        logits = jnp.dot(lr_ref[0, rows, :].astype(BF16), wa_ref[...],
                         preferred_element_type=F32) + ba_ref[...]
        log_a = (jnp.minimum(logits, 0.0) - jnp.log(1.0 + jnp.exp(-jnp.abs(logits)))) / GLA_TAU
        p1 = log_a.astype(BF16)
        r1 = log_a - p1.astype(F32)
        p2 = r1.astype(BF16)
        p3 = (r1 - p2.astype(F32)).astype(BF16)
        b_scr[rows, :] = (jnp.dot(tri, p1, preferred_element_type=F32)
                          + jnp.dot(tri, p2, preferred_element_type=F32)
                          + jnp.dot(tri, p3, preferred_element_type=F32))

    nt = (((1,), (1,)), ((), ()))

    def chunk_rows(c):
        return slice(c * c_len, (c + 1) * c_len)

    def pair_rows(p):
        return slice(p * pw, (p + 1) * pw)

    b_last = [b_scr[(c + 1) * c_len - 1:(c + 1) * c_len, :] for c in range(n_chunks)]
    h_log = [jnp.zeros_like(b_last[0])]
    for c in range(n_chunks):
        h_log.append(h_log[-1] + b_last[c])

    for c in range(n_chunks):
        rows = chunk_rows(c)
        b = b_scr[rows, :]
        q = q_ref[0, rows, :].astype(F32)
        k = k_ref[0, rows, :].astype(F32)
        qd = q * jnp.exp(b) * (GLA_DK ** -0.5)
        ke = k * jnp.exp(b_last[c] - b)
        qd_scr[rows, :] = qd
        qh_scr[rows, :] = (qd * jnp.exp(h_log[c])).astype(BF16)
        ki_scr[rows, :] = (k * jnp.exp(-b)).astype(BF16)
        ke_scr[rows, :] = ke.astype(BF16)
        kp_scr[rows, :] = (ke * jnp.exp(b_last[c + 1]) if c % 2 == 0 else ke).astype(BF16)
        kh_scr[rows, :] = (ke * jnp.exp(h_log[n_chunks] - h_log[c + 1])).astype(BF16)

    pi = lax.broadcasted_iota(jnp.int32, (pw, pw), 0)
    pj = lax.broadcasted_iota(jnp.int32, (pw, pw), 1)
    diag = jnp.logical_and(pi >= pj, pi // c_len == pj // c_len)
    lower = jnp.logical_and(pi >= c_len, pj < c_len)
    for p in range(n_pairs):
        first = chunk_rows(2 * p)
        keys = jnp.concatenate([ki_scr[pair_rows(p), :], ke_scr[first, :], ke_scr[first, :]], axis=0)
        s = lax.dot_general(qd_scr[pair_rows(p), :].astype(BF16), keys, nt,
                            preferred_element_type=F32)
        s = jnp.where(diag, s[:, :pw], jnp.where(lower, s[:, pw:], 0.0))
        a_scr[pair_rows(p), pair_rows(p)] = s.astype(BF16)

    for p in range(n_pairs - 1):
        c0 = 2 * p + 2
        lhs = jnp.concatenate(
            [(qd_scr[chunk_rows(c), :] * jnp.exp(h_log[c] - h_log[c0])).astype(BF16)
             for c in range(c0, n_chunks)], axis=0)
        s = lax.dot_general(lhs, kp_scr[pair_rows(p), :], nt, preferred_element_type=F32)
        a_scr[c0 * c_len:, pair_rows(p)] = s.astype(BF16)

    half = 2 * pw
    for hf in range(tc // half):
        a_scr[hf * half:hf * half + pw, hf * half + pw:(hf + 1) * half] = jnp.zeros((pw, pw), BF16)

    st = st_scr[...]
    st_b = st.astype(BF16)
    for hf in range(tc // half):
        rows = slice(hf * half, (hf + 1) * half)
        keys = (hf + 1) * half
        o = jnp.dot(a_scr[rows, :keys], v_ref[0, :keys, :], preferred_element_type=F32)
        o += jnp.dot(qh_scr[rows, :], st_b, preferred_element_type=F32)
        o = o * lax.rsqrt(jnp.mean(o * o, axis=-1, keepdims=True) + LN_EPS) * ng_ref[0]
        r = r_ref[0, rows, :].astype(F32)
        o_ref[0, rows, :] = (o * (r * _sigmoid(r))).astype(BF16)

    d_col = jnp.broadcast_to(jnp.exp(h_log[n_chunks]), (8, GLA_DK)).T[:, :1]
    st_scr[...] = st * d_col + lax.dot_general(
        kh_scr[...], v_ref[0], (((0,), (0,)), ((), ())), preferred_element_type=F32)


def _gla(p3, lr3, w_a2, b_a, norm_g3, tc=512):
    bsz, seq, _ = p3.shape
    nq, nv, nr = OFF_GK // GLA_DK, OFF_GV // GLA_DV, OFF_GR // GLA_DV
    return pl.pallas_call(
        functools.partial(_gla_kernel, n_chunks=tc // GLA_CHUNK),
        grid=(bsz, GLA_HEADS, seq // tc),
        in_specs=[
            pl.BlockSpec((1, tc, GLA_DK), lambda b, h, t: (b, t, h)),
            pl.BlockSpec((1, tc, GLA_DK), lambda b, h, t: (b, t, nq + h)),
            pl.BlockSpec((1, tc, GLA_DV), lambda b, h, t: (b, t, nv + h)),
            pl.BlockSpec((1, tc, GLA_DV), lambda b, h, t: (b, t, nr + h)),
            pl.BlockSpec((1, tc, LR_PAD), lambda b, h, t: (b, t, 0)),
            pl.BlockSpec((LR_PAD, GLA_DK), lambda b, h, t: (0, h)),
            pl.BlockSpec((1, GLA_DK), lambda b, h, t: (0, h)),
            pl.BlockSpec((1, 1, GLA_DV), lambda b, h, t: (h, 0, 0)),
        ],
        out_specs=pl.BlockSpec((1, tc, GLA_DV), lambda b, h, t: (b, t, h)),
        out_shape=jax.ShapeDtypeStruct((bsz, seq, GLA_VW), BF16),
        scratch_shapes=[pltpu.VMEM((GLA_DK, GLA_DV), F32), pltpu.VMEM((tc, GLA_DK), F32),
                        pltpu.VMEM((tc, GLA_DK), F32)]
                       + [pltpu.VMEM((tc, GLA_DK), BF16)] * 5 + [pltpu.VMEM((tc, tc), BF16)],
        compiler_params=_cparams(("parallel", "parallel", "arbitrary")),
        name="gla",
    )(p3, p3, p3, p3, lr3, w_a2, b_a, norm_g3)


def _dil_kernel(q_ref, k_ref, v_ref, kp_ref, vp_ref, o_ref, lse_ref, s_scr, m_scr, p_scr, *, nblk):
    n = pl.program_id(2)
    blk = DIL_BLOCK
    qi = lax.broadcasted_iota(jnp.int32, (blk, 2 * blk), 0)
    kj = lax.broadcasted_iota(jnp.int32, (blk, 2 * blk), 1)
    band = jnp.logical_or(jnp.logical_and(kj < blk, kj >= qi),
                          jnp.logical_and(kj >= blk, kj - blk <= qi))
    band_first = jnp.logical_and(band, jnp.logical_or(kj >= blk, n > 0))
    nt = (((1,), (1,)), ((), ()))

    def rows(i):
        return slice(i * blk, (i + 1) * blk)

    def cols(h):
        return slice(h * DIL_HEAD_DIM, (h + 1) * DIL_HEAD_DIM)

    for i in range(nblk):
        for h in range(DIL_HEADS):
            q = q_ref[0, 0, rows(i), cols(h)]
            k_prev = kp_ref[0, 0, :, cols(h)] if i == 0 else k_ref[0, 0, rows(i - 1), cols(h)]
            s_p = lax.dot_general(q, k_prev, nt, preferred_element_type=F32)
            s_c = lax.dot_general(q, k_ref[0, 0, rows(i), cols(h)], nt, preferred_element_type=F32)
            s = jnp.concatenate([s_p, s_c], axis=1)
            s_scr[rows(i * DIL_HEADS + h), :] = jnp.where(band_first if i == 0 else band, s, NEG_BIG)

    s_all = s_scr[...]
    m_all = jnp.max(s_all, axis=-1, keepdims=True)
    p_scr[...] = jnp.exp(s_all - m_all).astype(BF16)
    m_scr[...] = jnp.broadcast_to(m_all, m_scr.shape)

    lane = lax.broadcasted_iota(jnp.int32, (blk, LANES), 1)
    ones = jnp.ones((2 * blk, DIL_HEAD_DIM), BF16)
    for i in range(nblk):
        lse_tile = jnp.zeros((blk, LANES), F32)
        for h in range(DIL_HEADS):
            u = i * DIL_HEADS + h
            v_prev = vp_ref[0, 0, :, cols(h)] if i == 0 else v_ref[0, 0, rows(i - 1), cols(h)]
            v_aug = jnp.concatenate(
                [jnp.concatenate([v_prev, v_ref[0, 0, rows(i), cols(h)]], axis=0), ones], axis=1)
            acc = jnp.dot(p_scr[rows(u), :], v_aug, preferred_element_type=F32)
            den = acc[:, DIL_HEAD_DIM:]
            o_ref[0, 0, rows(i), cols(h)] = (acc[:, :DIL_HEAD_DIM] / den).astype(BF16)
            lse_tile = jnp.where(lane == h, m_scr[rows(u), :] + jnp.log(den), lse_tile)
        lse_ref[0, 0, rows(i), :] = lse_tile


def _dil_group(qkv, nblk=4):
    bsz, r, length, _ = qkv.shape
    nb = length // DIL_BLOCK
    nblk = min(nblk, nb)
    assert nb % nblk == 0
    tq = nblk * DIL_BLOCK
    units = nblk * DIL_HEADS * DIL_BLOCK

    def cur_spec(u):
        return pl.BlockSpec((1, 1, tq, DIL_OUT), lambda b, c, n: (b, c, n, u))

    def prev_spec(u):
        return pl.BlockSpec((1, 1, DIL_BLOCK, DIL_OUT),
                            lambda b, c, n: (b, c, jnp.maximum(n * nblk - 1, 0), u))

    return pl.pallas_call(
        functools.partial(_dil_kernel, nblk=nblk),
        grid=(bsz, r, nb // nblk),
        in_specs=[cur_spec(0), cur_spec(1), cur_spec(2), prev_spec(1), prev_spec(2)],
        out_specs=[
            pl.BlockSpec((1, 1, tq, DIL_OUT), lambda b, c, n: (b, c, n, 0)),
            pl.BlockSpec((1, 1, tq, LANES), lambda b, c, n: (b, c, n, 0)),
        ],
        out_shape=[
            jax.ShapeDtypeStruct((bsz, r, length, DIL_OUT), BF16),
            jax.ShapeDtypeStruct((bsz, r, length, LANES), F32),
        ],
        scratch_shapes=[pltpu.VMEM((units, 2 * DIL_BLOCK), F32),
                        pltpu.VMEM((units, LANES), F32),
                        pltpu.VMEM((units, 2 * DIL_BLOCK), BF16)],
        compiler_params=_cparams(("parallel", "parallel", "parallel")),
        name=f"dilattn_r{r}",
    )(qkv, qkv, qkv, qkv, qkv)


def _merge_kernel(oa_ref, o1_ref, o2_ref, o3_ref, l1_ref, l2_ref, l3_ref, ga_ref, gb_ref, x_ref,
                  g2_ref, wa_ref, wb_ref, wo_ref, lng_ref, lnb_ref, out_ref, l_scr, w_scr, ob_scr,
                  *, dilations):
    y_a = jnp.dot(oa_ref[...], wa_ref[...], preferred_element_type=F32)
    o_refs = (o1_ref, o2_ref, o3_ref)
    l_refs = (l1_ref, l2_ref, l3_ref)
    tm = oa_ref.shape[0]

    def tok_rows(c, r):
        return pl.ds(c, tm // r, stride=r) if r > 1 else slice(None)

    for g, r in enumerate(dilations):
        for c in range(r):
            l_scr[g, tok_rows(c, r), :] = l_refs[g][0, c]
    l1, l2, l3 = l_scr[0], l_scr[1], l_scr[2]
    mx = jnp.maximum(jnp.maximum(l1, l2), l3)
    e1, e2, e3 = jnp.exp(l1 - mx), jnp.exp(l2 - mx), jnp.exp(l3 - mx)
    inv = 1.0 / (e1 + e2 + e3)
    w_scr[0], w_scr[1], w_scr[2] = e1 * inv, e2 * inv, e3 * inv
    for g, r in enumerate(dilations):
        for c in range(r):
            rows = tok_rows(c, r)
            wr = w_scr[g, rows, :]
            for h in range(DIL_HEADS):
                sl = slice(h * DIL_HEAD_DIM, (h + 1) * DIL_HEAD_DIM)
                part = wr[:, h:h + 1] * o_refs[g][0, c, :, sl].astype(F32)
                if g == 0:
                    ob_scr[h, rows, :] = part
                else:
                    ob_scr[h, rows, :] += part
    o_b = jnp.concatenate([ob_scr[h].astype(BF16) for h in range(DIL_HEADS)], axis=1)
    y_b = jnp.dot(o_b, wb_ref[...], preferred_element_type=F32)
    merged = _sigmoid(ga_ref[...].astype(F32)) * y_a + _sigmoid(gb_ref[...].astype(F32)) * y_b
    mix = jnp.dot(merged.astype(BF16), wo_ref[...], preferred_element_type=F32)
    y = DN_ALPHA * x_ref[...] + g2_ref[0] * mix
    out_ref[...] = _layer_norm(y, lng_ref[...], lnb_ref[...])


def _merge(oa2d, o_groups, lse_groups, p2d, x2d, mods3, k_gate, w_a, w_b, w_o, ln_g, ln_b, seq,
           dilations, tm=256):
    m, d = x2d.shape
    tiles_per_seq = seq // tm
    resident = pl.Buffered(1)
    n_g = len(dilations)

    def row_spec(width, col=0):
        return pl.BlockSpec((tm, width), lambda i: (i, col))

    def class_spec(r, width):
        return pl.BlockSpec((1, r, tm // r, width),
                            lambda i: (i // tiles_per_seq, 0, i % tiles_per_seq, 0))

    def const_spec(shape):
        return pl.BlockSpec(shape, lambda i: (0, 0), pipeline_mode=resident)

    return pl.pallas_call(
        functools.partial(_merge_kernel, dilations=dilations),
        grid=(m // tm,),
        in_specs=[
            row_spec(GLA_VW),
            *[class_spec(r, DIL_OUT) for r in dilations],
            *[class_spec(r, LANES) for r in dilations],
            row_spec(D_MODEL, OFF_GA // D_MODEL), row_spec(D_MODEL, OFF_GB // D_MODEL),
            row_spec(d),
            pl.BlockSpec((1, 1, d), lambda i: ((i // tiles_per_seq) * N_MOD + k_gate, 0, 0)),
            const_spec(w_a.shape), const_spec(w_b.shape), const_spec(w_o.shape),
            const_spec((1, d)), const_spec((1, d)),
        ],
        out_specs=row_spec(d),
        out_shape=jax.ShapeDtypeStruct((m, d), F32),
        scratch_shapes=[pltpu.VMEM((n_g, tm, LANES), F32), pltpu.VMEM((n_g, tm, LANES), F32),
                        pltpu.VMEM((DIL_HEADS, tm, DIL_HEAD_DIM), F32)],
        compiler_params=_cparams(("parallel",)),
        name="merge",
    )(oa2d, *o_groups, *lse_groups, p2d, p2d, x2d, mods3, w_a, w_b, w_o, ln_g, ln_b)


def _pack_w_in(w_in):
    o = 0
    segs = []
    for wdt in (GLA_QK, GLA_QK, GLA_VW, GLA_VW, GLA_GATE_RANK, DIL_W, DIL_W, DIL_W, D_MODEL, D_MODEL):
        segs.append(w_in[:, o:o + wdt].astype(BF16))
        o += wdt
    gq, gk, gv, gr, glr, dq, dk, dv, ga, gb = segs
    cols = [gq, gk, gv, gr, ga, gb]
    for g in range(DIL_GROUPS):
        sl = slice(g * DIL_OUT, (g + 1) * DIL_OUT)
        cols += [dq[:, sl], dk[:, sl], dv[:, sl]]
    cols.append(jnp.pad(glr, ((0, 0), (0, LR_PAD - GLA_GATE_RANK))))
    return jnp.concatenate(cols, axis=1)


def kernel(x, c, positions, w_ada, b_ada, ln1_g, ln1_b, w_ffn1_gu, w_ffn1_down, w_in, w_alpha2,
           b_alpha, gla_norm_g, w_branch_a, w_branch_b, w_out, ln2_g, ln2_b, w_ffn2_gu, w_ffn2_down,
           ln3_g, ln3_b):
    bsz, seq, d = x.shape
    m = bsz * seq
    x2d = x.reshape(m, d)
    c_pad = jnp.pad(c, ((0, 8 - bsz % 8 if bsz % 8 else 0), (0, 0)))
    dilations = tuple(r for _, r in DIL_PATTERNS)

    half = DIL_HEAD_DIM // 2
    freq = ROPE_THETA ** (-jnp.arange(half, dtype=F32) / half)
    freq2 = jnp.concatenate([freq, freq]).reshape(1, DIL_HEAD_DIM)
    rope_t = _rope_tables(positions.reshape(m // PROJ_TM, PROJ_TM // LANES, LANES), freq2, dilations)

    for l in range(DEPTH):
        mods = _mods(c_pad, w_ada[l], b_ada[l].reshape(1, -1))[:bsz]
        mods3 = mods.reshape(bsz * N_MOD, 1, d)

        x1 = _ffn(x2d, mods3, 0, 1, 2, w_ffn1_gu[l].astype(BF16), w_ffn1_down[l].astype(BF16),
                  ln1_g[l].reshape(1, d), ln1_b[l].reshape(1, d), seq)

        w_all = _pack_w_in(w_in[l])
        p2d, lr2d = _proj(x1, mods3, 3, 4, w_all, seq)
        p3 = p2d.reshape(bsz, seq, PROJ_W)

        w_a2 = jnp.pad(w_alpha2[l], ((0, LR_PAD - GLA_GATE_RANK), (0, 0))).astype(BF16)
        o_a = _gla(p3, lr2d.reshape(bsz, seq, LR_PAD), w_a2, b_alpha[l].reshape(1, GLA_QK),
                   gla_norm_g[l].reshape(GLA_HEADS, 1, GLA_DV))

        o_groups, lse_groups = [], []
        for g, r in enumerate(dilations):
            qkv = _proj_dil(x1, mods3, 3, 4, w_all, g, rope_t[2 * g], rope_t[2 * g + 1],
                            bsz, seq, r)
            o_g, lse_g = _dil_group(qkv)
            o_groups.append(o_g)
            lse_groups.append(lse_g)

        x2 = _merge(o_a.reshape(m, GLA_VW), o_groups, lse_groups, p2d, x1, mods3, 5,
                    w_branch_a[l].astype(BF16), w_branch_b[l].astype(BF16), w_out[l].astype(BF16),
                    ln2_g[l].reshape(1, d), ln2_b[l].reshape(1, d), seq, dilations)

        x2d = _ffn(x2, mods3, 6, 7, 8, w_ffn2_gu[l].astype(BF16), w_ffn2_down[l].astype(BF16),
                   ln3_g[l].reshape(1, d), ln3_b[l].reshape(1, d), seq)
    return x2d.reshape(bsz, seq, d)
```

```python
import functools

import jax
import jax.numpy as jnp
from jax import lax
from jax.experimental import pallas as pl
from jax.experimental.pallas import tpu as pltpu

F32 = jnp.float32
BF16 = jnp.bfloat16

D_MODEL = 2048
DEPTH = 1
D_FF = 5632
N_MOD = 9
LN_EPS = 1e-5
DN_ALPHA = (2.0 * DEPTH) ** 0.25

GLA_HEADS = 4
GLA_DK = 256
GLA_DV = 512
GLA_GATE_RANK = 16
GLA_TAU = 16.0
GLA_CHUNK = 64
GLA_QK = GLA_HEADS * GLA_DK
GLA_VW = GLA_HEADS * GLA_DV

DIL_PATTERNS = ((128, 1), (512, 4), (2048, 16))
DIL_GROUPS = len(DIL_PATTERNS)
DIL_HEADS = 8
DIL_HEAD_DIM = 128
DIL_BLOCK = 128
DIL_W = DIL_GROUPS * DIL_HEADS * DIL_HEAD_DIM
DIL_OUT = DIL_HEADS * DIL_HEAD_DIM
ROPE_THETA = 10000.0

LANES = 128
LR_PAD = LANES

PROJ_W = 2 * GLA_QK + 2 * GLA_VW + 2 * D_MODEL
OFF_GQ = 0
OFF_GK = GLA_QK
OFF_GV = 2 * GLA_QK
OFF_GR = OFF_GV + GLA_VW
OFF_GA = OFF_GR + GLA_VW
OFF_GB = OFF_GA + D_MODEL
DIL_QKV = 3 * DIL_OUT
OFF_LR = PROJ_W + DIL_GROUPS * DIL_QKV

NEG_BIG = -1e30
LOG2_E = 1.4426950408889634

VMEM_LIMIT = 56 * 1024 * 1024


def _cparams(sem, vmem=VMEM_LIMIT):
    return pltpu.CompilerParams(dimension_semantics=sem, vmem_limit_bytes=vmem)


def _sigmoid(x):
    return 1.0 / (1.0 + jnp.exp(-x))


def _layer_norm(y, g, b):
    mu = jnp.mean(y, axis=-1, keepdims=True)
    d = y - mu
    var = jnp.mean(d * d, axis=-1, keepdims=True)
    return d * lax.rsqrt(var + LN_EPS) * g + b


def _mods_kernel(c_ref, w_ref, b_ref, o_ref):
    c = c_ref[...]
    c_act = (c * _sigmoid(c)).astype(BF16)
    o_ref[...] = jnp.dot(c_act, w_ref[...].astype(BF16), preferred_element_type=F32) + b_ref[...]


def _mods(c_pad, w_ada, b_ada, tn=1024):
    rows, d = c_pad.shape
    n = w_ada.shape[1]
    return pl.pallas_call(
        _mods_kernel,
        grid=(n // tn,),
        in_specs=[
            pl.BlockSpec((rows, d), lambda j: (0, 0)),
            pl.BlockSpec((d, tn), lambda j: (0, j)),
            pl.BlockSpec((1, tn), lambda j: (0, j)),
        ],
        out_specs=pl.BlockSpec((rows, tn), lambda j: (0, j)),
        out_shape=jax.ShapeDtypeStruct((rows, n), F32),
        compiler_params=_cparams(("arbitrary",)),
        name="mods",
    )(c_pad, w_ada, b_ada)


def _ffn_kernel(x_ref, sh_ref, sc_ref, g_ref, wg_ref, wu_ref, wd_ref, lng_ref, lnb_ref,
                o_ref, h_scr):
    j = pl.program_id(1)

    @pl.when(j == 0)
    def _():
        h_scr[...] = (x_ref[...] * (1.0 + sc_ref[0]) + sh_ref[0]).astype(BF16)
        o_ref[...] = jnp.zeros_like(o_ref)

    h = h_scr[...]
    gate = jnp.dot(h, wg_ref[...], preferred_element_type=F32)
    up = jnp.dot(h, wu_ref[...], preferred_element_type=F32)
    act = (gate * _sigmoid(gate) * up).astype(BF16)
    o_ref[...] += jnp.dot(act, wd_ref[...].astype(BF16), preferred_element_type=F32)

    @pl.when(j == pl.num_programs(1) - 1)
    def _():
        y = DN_ALPHA * x_ref[...] + 0.5 * g_ref[0] * o_ref[...]
        o_ref[...] = _layer_norm(y, lng_ref[...], lnb_ref[...])


def _ffn(x2d, mods3, k_shift, k_scale, k_gate, w_gu, w_down, ln_g, ln_b, seq, tm=512, tf=512):
    m, d = x2d.shape
    n_ff = w_down.shape[0]
    nj = n_ff // tf
    tiles_per_seq = seq // tm

    def mod_spec(k):
        return pl.BlockSpec((1, 1, d), lambda i, j: ((i // tiles_per_seq) * N_MOD + k, 0, 0))

    return pl.pallas_call(
        _ffn_kernel,
        grid=(m // tm, nj),
        in_specs=[
            pl.BlockSpec((tm, d), lambda i, j: (i, 0)),
            mod_spec(k_shift), mod_spec(k_scale), mod_spec(k_gate),
            pl.BlockSpec((d, tf), lambda i, j: (0, j)),
            pl.BlockSpec((d, tf), lambda i, j: (0, j + nj)),
            pl.BlockSpec((tf, d), lambda i, j: (j, 0)),
            pl.BlockSpec((1, d), lambda i, j: (0, 0)),
            pl.BlockSpec((1, d), lambda i, j: (0, 0)),
        ],
        out_specs=pl.BlockSpec((tm, d), lambda i, j: (i, 0)),
        out_shape=jax.ShapeDtypeStruct((m, d), F32),
        scratch_shapes=[pltpu.VMEM((tm, d), BF16)],
        compiler_params=_cparams(("parallel", "arbitrary")),
        name="ffn",
    )(x2d, mods3, mods3, mods3, w_gu, w_gu, w_down, ln_g, ln_b)


PROJ_TM = 1024


def _rope_kernel(pos_ref, freq_ref, *out_refs, dilations):
    cos_ref, sin_ref = out_refs[0], out_refs[1]
    tm = cos_ref.shape[0]
    pos_t = pos_ref[0].astype(F32).T
    lane = lax.broadcasted_iota(jnp.int32, (LANES, DIL_HEAD_DIM), 1)
    for k in range(tm // LANES):
        ang = pos_t[:, k:k + 1] * freq_ref[...]
        s = jnp.sin(ang)
        cos_ref[k * LANES:(k + 1) * LANES, :] = jnp.cos(ang)
        sin_ref[k * LANES:(k + 1) * LANES, :] = jnp.where(lane < DIL_HEAD_DIM // 2, -s, s)
    for gi, r in enumerate(dilations[1:]):
        tl = tm // r
        for c in range(r):
            out_refs[2 + 2 * gi][c * tl:(c + 1) * tl, :] = cos_ref[pl.ds(c, tl, stride=r), :]
            out_refs[3 + 2 * gi][c * tl:(c + 1) * tl, :] = sin_ref[pl.ds(c, tl, stride=r), :]


def _rope_tables(pos3, freq2, dilations, tm=PROJ_TM):
    assert dilations[0] == 1
    m = pos3.shape[0] * tm
    n_out = 2 * len(dilations)
    return pl.pallas_call(
        functools.partial(_rope_kernel, dilations=dilations),
        grid=(m // tm,),
        in_specs=[
            pl.BlockSpec((1, tm // LANES, LANES), lambda i: (i, 0, 0)),
            pl.BlockSpec((1, DIL_HEAD_DIM), lambda i: (0, 0)),
        ],
        out_specs=[pl.BlockSpec((tm, DIL_HEAD_DIM), lambda i: (i, 0))] * n_out,
        out_shape=[jax.ShapeDtypeStruct((m, DIL_HEAD_DIM), F32)] * n_out,
        compiler_params=_cparams(("parallel",)),
        name="rope_tables",
    )(pos3, freq2)


def _proj_kernel(x_ref, sh_ref, sc_ref, w_ref, wlr_ref, o_ref, lr_ref, h_scr):
    @pl.when(pl.program_id(1) == 0)
    def _():
        h = (x_ref[...] * (1.0 + sc_ref[0]) + sh_ref[0]).astype(BF16)
        h_scr[...] = h
        lr_ref[...] = jnp.dot(h, wlr_ref[...], preferred_element_type=F32)

    o_ref[...] = jnp.dot(h_scr[...], w_ref[...], preferred_element_type=F32).astype(BF16)


def _proj(x2d, mods3, k_shift, k_scale, w_all, seq, tm=PROJ_TM, tn=2048):
    m, d = x2d.shape
    n = PROJ_W
    tiles_per_seq = seq // tm

    def mod_spec(k):
        return pl.BlockSpec((1, 1, d), lambda i, j: ((i // tiles_per_seq) * N_MOD + k, 0, 0))

    return pl.pallas_call(
        _proj_kernel,
        grid=(m // tm, n // tn),
        in_specs=[
            pl.BlockSpec((tm, d), lambda i, j: (i, 0)),
            mod_spec(k_shift), mod_spec(k_scale),
            pl.BlockSpec((d, tn), lambda i, j: (0, j)),
            pl.BlockSpec((d, LR_PAD), lambda i, j: (0, OFF_LR // LR_PAD)),
        ],
        out_specs=[
            pl.BlockSpec((tm, tn), lambda i, j: (i, j)),
            pl.BlockSpec((tm, LR_PAD), lambda i, j: (i, 0)),
        ],
        out_shape=[
            jax.ShapeDtypeStruct((m, n), BF16),
            jax.ShapeDtypeStruct((m, LR_PAD), F32),
        ],
        scratch_shapes=[pltpu.VMEM((tm, d), BF16)],
        compiler_params=_cparams(("parallel", "arbitrary")),
        name="proj",
    )(x2d, mods3, mods3, w_all, w_all)


def _proj_dil_kernel(x_ref, sh_ref, sc_ref, w_ref, cos_ref, sin_ref, o_ref, h_scr, slab_scr, *, r):
    j = pl.program_id(1)
    tm, d = x_ref.shape
    tl = tm // r

    @pl.when(j == 0)
    def _():
        if r == 1:
            h_scr[...] = (x_ref[...] * (1.0 + sc_ref[0]) + sh_ref[0]).astype(BF16)
        else:
            for k in range(d // LANES):
                sl = slice(k * LANES, (k + 1) * LANES)
                slab_scr[k] = x_ref[:, sl] * (1.0 + sc_ref[0, :, sl]) + sh_ref[0, :, sl]
            for c in range(r):
                for k in range(d // LANES):
                    sl = slice(k * LANES, (k + 1) * LANES)
                    h_scr[c * tl:(c + 1) * tl, sl] = slab_scr[k, pl.ds(c, tl, stride=r), :].astype(BF16)

    _proj_dil_tile(j, h_scr, w_ref, cos_ref, sin_ref, o_ref, r)


def _proj_dil_tile(j, h_scr, w_ref, cos_ref, sin_ref, o_ref, r):
    tl = h_scr.shape[0] // r
    t = jnp.dot(h_scr[...], w_ref[...], preferred_element_type=F32)
    is_v = j == 2
    scale = jnp.where(j == 0, LOG2_E * DIL_HEAD_DIM ** -0.5, 1.0).astype(F32)
    a = jnp.where(is_v, 1.0, cos_ref[...] * scale)
    b = jnp.where(is_v, 0.0, sin_ref[...] * scale)
    for s in range(t.shape[1] // DIL_HEAD_DIM):
        sl = slice(s * DIL_HEAD_DIM, (s + 1) * DIL_HEAD_DIM)
        ts = t[:, sl]
        rot = (ts * a + pltpu.roll(ts, DIL_HEAD_DIM // 2, 1) * b).astype(BF16)
        for c in range(r):
            o_ref[0, c, :, sl] = rot[c * tl:(c + 1) * tl, :]


def _proj_dil_dma_kernel(x_hbm, sh_ref, sc_ref, w_ref, cos_ref, sin_ref, o_ref, h_scr, x_buf, sem,
                         *, r):
    i = pl.program_id(0)
    j = pl.program_id(1)
    tl = h_scr.shape[0] // r
    slot = i % 2

    def tile_copies(tile, buf_slot):
        return [pltpu.make_async_copy(x_hbm.at[pl.ds(tile * tl, tl), c], x_buf.at[buf_slot, c],
                                      sem.at[buf_slot]) for c in range(r)]

    @pl.when(j == 0)
    def _():
        @pl.when(i == 0)
        def _():
            for cp in tile_copies(0, 0):
                cp.start()

        @pl.when(i + 1 < pl.num_programs(0))
        def _():
            for cp in tile_copies(i + 1, 1 - slot):
                cp.start()

        for cp in tile_copies(i, slot):
            cp.wait()
        for c in range(r):
            h_scr[c * tl:(c + 1) * tl, :] = (x_buf[slot, c] * (1.0 + sc_ref[0]) + sh_ref[0]).astype(BF16)

    _proj_dil_tile(j, h_scr, w_ref, cos_ref, sin_ref, o_ref, r)


SUBLANES_F32 = 8


def _proj_dil(x2d, mods3, k_shift, k_scale, w_all, g, cos_t, sin_t, bsz, seq, r, tm=PROJ_TM):
    m, d = x2d.shape
    tiles_per_seq = seq // tm
    tl = tm // r
    col0 = (PROJ_W + g * DIL_QKV) // DIL_OUT
    dma_gather = r % SUBLANES_F32 == 0

    def mod_spec(k):
        return pl.BlockSpec((1, 1, d), lambda i, j: ((i // tiles_per_seq) * N_MOD + k, 0, 0))

    if dma_gather:
        body = functools.partial(_proj_dil_dma_kernel, r=r)
        x_arg = x2d.reshape(m // r, r, d)
        x_spec = pl.BlockSpec(memory_space=pl.ANY)
        scratch = [pltpu.VMEM((tm, d), BF16), pltpu.VMEM((2, r, tl, d), F32),
                   pltpu.SemaphoreType.DMA((2,))]
        semantics = ("arbitrary", "arbitrary")
    else:
        body = functools.partial(_proj_dil_kernel, r=r)
        x_arg = x2d
        x_spec = pl.BlockSpec((tm, d), lambda i, j: (i, 0))
        scratch = [pltpu.VMEM((tm, d), BF16),
                   pltpu.VMEM((d // LANES, tm, LANES) if r > 1 else (1, 8, LANES), F32)]
        semantics = ("parallel", "arbitrary")

    return pl.pallas_call(
        body,
        grid=(m // tm, 3),
        in_specs=[
            x_spec,
            mod_spec(k_shift), mod_spec(k_scale),
            pl.BlockSpec((d, DIL_OUT), lambda i, j: (0, col0 + j)),
            pl.BlockSpec((tm, DIL_HEAD_DIM), lambda i, j: (i, 0)),
            pl.BlockSpec((tm, DIL_HEAD_DIM), lambda i, j: (i, 0)),
        ],
        out_specs=pl.BlockSpec((1, r, tl, DIL_OUT),
                               lambda i, j: (i // tiles_per_seq, 0, i % tiles_per_seq, j)),
        out_shape=jax.ShapeDtypeStruct((bsz, r, seq // r, DIL_QKV), BF16),
        scratch_shapes=scratch,
        compiler_params=_cparams(semantics),
        name=f"proj_dil_r{r}",
    )(x_arg, mods3, mods3, w_all, cos_t, sin_t)


def _gla_kernel(q_ref, k_ref, v_ref, r_ref, lr_ref, wa_ref, ba_ref, ng_ref, o_ref,
                st_scr, b_scr, qd_scr, qh_scr, ki_scr, ke_scr, kp_scr, kh_scr, a_scr, *, n_chunks):
    c_len = GLA_CHUNK
    tc = q_ref.shape[1]
    hb = 4 * c_len
    pw = 2 * c_len
    n_pairs = n_chunks // 2

    @pl.when(pl.program_id(2) == 0)
    def _():
        st_scr[...] = jnp.zeros_like(st_scr)

    ti = lax.broadcasted_iota(jnp.int32, (hb, hb), 0)
    tj = lax.broadcasted_iota(jnp.int32, (hb, hb), 1)
    tri = jnp.logical_and(ti >= tj, ti // c_len == tj // c_len).astype(BF16)
    for s in range(tc // hb):
        rows = slice(s * hb, (s + 1) * hb)
        logits = jnp.dot(lr_ref[0, rows, :].astype(BF16), wa_ref[...],
                         preferred_element_type=F32) + ba_ref[...]
        log_a = (jnp.minimum(logits, 0.0) - jnp.log(1.0 + jnp.exp(-jnp.abs(logits)))) * (LOG2_E / GLA_TAU)
        p1 = log_a.astype(BF16)
        r1 = log_a - p1.astype(F32)
        p2 = r1.astype(BF16)
        p3 = (r1 - p2.astype(F32)).astype(BF16)
        b_scr[rows, :] = (jnp.dot(tri, p1, preferred_element_type=F32)
                          + jnp.dot(tri, p2, preferred_element_type=F32)
                          + jnp.dot(tri, p3, preferred_element_type=F32))

    nt = (((1,), (1,)), ((), ()))

    def chunk_rows(c):
        return slice(c * c_len, (c + 1) * c_len)

    def pair_rows(p):
        return slice(p * pw, (p + 1) * pw)

    b_last = [b_scr[(c + 1) * c_len - 1:(c + 1) * c_len, :] for c in range(n_chunks)]
    h_log = [jnp.zeros_like(b_last[0])]
    for c in range(n_chunks):
        h_log.append(h_log[-1] + b_last[c])

    for c in range(n_chunks):
        rows = chunk_rows(c)
        b = b_scr[rows, :]
        q = q_ref[0, rows, :].astype(F32)
        k = k_ref[0, rows, :].astype(F32)
        qd = q * jnp.exp2(b) * (GLA_DK ** -0.5)
        ke = k * jnp.exp2(b_last[c] - b)
        qd_scr[rows, :] = qd
        qh_scr[rows, :] = (qd * jnp.exp2(h_log[c])).astype(BF16)
        ki_scr[rows, :] = (k * jnp.exp2(-b)).astype(BF16)
        ke_scr[rows, :] = ke.astype(BF16)
        kp_scr[rows, :] = (ke * jnp.exp2(b_last[c + 1]) if c % 2 == 0 else ke).astype(BF16)
        kh_scr[rows, :] = (ke * jnp.exp2(h_log[n_chunks] - h_log[c + 1])).astype(BF16)

    pi = lax.broadcasted_iota(jnp.int32, (pw, pw), 0)
    pj = lax.broadcasted_iota(jnp.int32, (pw, pw), 1)
    diag = jnp.logical_and(pi >= pj, pi // c_len == pj // c_len)
    lower = jnp.logical_and(pi >= c_len, pj < c_len)
    for p in range(n_pairs):
        first = chunk_rows(2 * p)
        keys = jnp.concatenate([ki_scr[pair_rows(p), :], ke_scr[first, :], ke_scr[first, :]], axis=0)
        s = lax.dot_general(qd_scr[pair_rows(p), :].astype(BF16), keys, nt,
                            preferred_element_type=F32)
        s = jnp.where(diag, s[:, :pw], jnp.where(lower, s[:, pw:], 0.0))
        a_scr[pair_rows(p), pair_rows(p)] = s.astype(BF16)

    for p in range(n_pairs - 1):
        c0 = 2 * p + 2
        lhs = jnp.concatenate(
            [(qd_scr[chunk_rows(c), :] * jnp.exp2(h_log[c] - h_log[c0])).astype(BF16)
             for c in range(c0, n_chunks)], axis=0)
        s = lax.dot_general(lhs, kp_scr[pair_rows(p), :], nt, preferred_element_type=F32)
        a_scr[c0 * c_len:, pair_rows(p)] = s.astype(BF16)

    half = 2 * pw
    for hf in range(tc // half):
        a_scr[hf * half:hf * half + pw, hf * half + pw:(hf + 1) * half] = jnp.zeros((pw, pw), BF16)

    st = st_scr[...]
    st_b = st.astype(BF16)
    for hf in range(tc // half):
        rows = slice(hf * half, (hf + 1) * half)
        keys = (hf + 1) * half
        o = jnp.dot(a_scr[rows, :keys], v_ref[0, :keys, :], preferred_element_type=F32)
        o += jnp.dot(qh_scr[rows, :], st_b, preferred_element_type=F32)
        o = o * lax.rsqrt(jnp.mean(o * o, axis=-1, keepdims=True) + LN_EPS) * ng_ref[0]
        r = r_ref[0, rows, :].astype(F32)
        o_ref[0, rows, :] = (o * (r * _sigmoid(r))).astype(BF16)

    d_col = jnp.broadcast_to(jnp.exp2(h_log[n_chunks]), (8, GLA_DK)).T[:, :1]
    st_scr[...] = st * d_col + lax.dot_general(
        kh_scr[...], v_ref[0], (((0,), (0,)), ((), ())), preferred_element_type=F32)


def _gla(p3, lr3, w_a2, b_a, norm_g3, tc=512):
    bsz, seq, _ = p3.shape
    nq, nv, nr = OFF_GK // GLA_DK, OFF_GV // GLA_DV, OFF_GR // GLA_DV
    return pl.pallas_call(
        functools.partial(_gla_kernel, n_chunks=tc // GLA_CHUNK),
        grid=(bsz, GLA_HEADS, seq // tc),
        in_specs=[
            pl.BlockSpec((1, tc, GLA_DK), lambda b, h, t: (b, t, h)),
            pl.BlockSpec((1, tc, GLA_DK), lambda b, h, t: (b, t, nq + h)),
            pl.BlockSpec((1, tc, GLA_DV), lambda b, h, t: (b, t, nv + h)),
            pl.BlockSpec((1, tc, GLA_DV), lambda b, h, t: (b, t, nr + h)),
            pl.BlockSpec((1, tc, LR_PAD), lambda b, h, t: (b, t, 0)),
            pl.BlockSpec((LR_PAD, GLA_DK), lambda b, h, t: (0, h)),
            pl.BlockSpec((1, GLA_DK), lambda b, h, t: (0, h)),
            pl.BlockSpec((1, 1, GLA_DV), lambda b, h, t: (h, 0, 0)),
        ],
        out_specs=pl.BlockSpec((1, tc, GLA_DV), lambda b, h, t: (b, t, h)),
        out_shape=jax.ShapeDtypeStruct((bsz, seq, GLA_VW), BF16),
        scratch_shapes=[pltpu.VMEM((GLA_DK, GLA_DV), F32), pltpu.VMEM((tc, GLA_DK), F32),
                        pltpu.VMEM((tc, GLA_DK), F32)]
                       + [pltpu.VMEM((tc, GLA_DK), BF16)] * 5 + [pltpu.VMEM((tc, tc), BF16)],
        compiler_params=_cparams(("parallel", "parallel", "arbitrary")),
        name="gla",
    )(p3, p3, p3, p3, lr3, w_a2, b_a, norm_g3)


def _dil_kernel(q_ref, k_ref, v_ref, kp_ref, vp_ref, o_ref, lse_ref, s_scr, m_scr, p_scr, *, nblk):
    n = pl.program_id(2)
    blk = DIL_BLOCK
    qi = lax.broadcasted_iota(jnp.int32, (blk, 2 * blk), 0)
    kj = lax.broadcasted_iota(jnp.int32, (blk, 2 * blk), 1)
    band = jnp.logical_or(jnp.logical_and(kj < blk, kj >= qi),
                          jnp.logical_and(kj >= blk, kj - blk <= qi))
    band_first = jnp.logical_and(band, jnp.logical_or(kj >= blk, n > 0))
    nt = (((1,), (1,)), ((), ()))

    def rows(i):
        return slice(i * blk, (i + 1) * blk)

    def cols(h):
        return slice(h * DIL_HEAD_DIM, (h + 1) * DIL_HEAD_DIM)

    for i in range(nblk):
        for h in range(DIL_HEADS):
            q = q_ref[0, 0, rows(i), cols(h)]
            k_prev = kp_ref[0, 0, :, cols(h)] if i == 0 else k_ref[0, 0, rows(i - 1), cols(h)]
            s_p = lax.dot_general(q, k_prev, nt, preferred_element_type=F32)
            s_c = lax.dot_general(q, k_ref[0, 0, rows(i), cols(h)], nt, preferred_element_type=F32)
            s = jnp.concatenate([s_p, s_c], axis=1)
            s_scr[rows(i * DIL_HEADS + h), :] = jnp.where(band_first if i == 0 else band, s, NEG_BIG)

    s_all = s_scr[...]
    m_all = jnp.max(s_all, axis=-1, keepdims=True)
    p_scr[...] = jnp.exp2(s_all - m_all).astype(BF16)
    m_scr[...] = jnp.broadcast_to(m_all, m_scr.shape)

    lane = lax.broadcasted_iota(jnp.int32, (blk, LANES), 1)
    ones = jnp.ones((2 * blk, DIL_HEAD_DIM), BF16)
    for i in range(nblk):
        lse_tile = jnp.zeros((blk, LANES), F32)
        for h in range(DIL_HEADS):
            u = i * DIL_HEADS + h
            v_prev = vp_ref[0, 0, :, cols(h)] if i == 0 else v_ref[0, 0, rows(i - 1), cols(h)]
            v_aug = jnp.concatenate(
                [jnp.concatenate([v_prev, v_ref[0, 0, rows(i), cols(h)]], axis=0), ones], axis=1)
            acc = jnp.dot(p_scr[rows(u), :], v_aug, preferred_element_type=F32)
            den = acc[:, DIL_HEAD_DIM:]
            o_ref[0, 0, rows(i), cols(h)] = (acc[:, :DIL_HEAD_DIM] / den).astype(BF16)
            lse_tile = jnp.where(lane == h, m_scr[rows(u), :] + jnp.log2(den), lse_tile)
        lse_ref[0, 0, rows(i), :] = lse_tile


def _dil_group(qkv, nblk=4):
    bsz, r, length, _ = qkv.shape
    nb = length // DIL_BLOCK
    nblk = min(nblk, nb)
    assert nb % nblk == 0
    tq = nblk * DIL_BLOCK
    units = nblk * DIL_HEADS * DIL_BLOCK

    def cur_spec(u):
        return pl.BlockSpec((1, 1, tq, DIL_OUT), lambda b, c, n: (b, c, n, u))

    def prev_spec(u):
        return pl.BlockSpec((1, 1, DIL_BLOCK, DIL_OUT),
                            lambda b, c, n: (b, c, jnp.maximum(n * nblk - 1, 0), u))

    return pl.pallas_call(
        functools.partial(_dil_kernel, nblk=nblk),
        grid=(bsz, r, nb // nblk),
        in_specs=[cur_spec(0), cur_spec(1), cur_spec(2), prev_spec(1), prev_spec(2)],
        out_specs=[
            pl.BlockSpec((1, 1, tq, DIL_OUT), lambda b, c, n: (b, c, n, 0)),
            pl.BlockSpec((1, 1, tq, LANES), lambda b, c, n: (b, c, n, 0)),
        ],
        out_shape=[
            jax.ShapeDtypeStruct((bsz, r, length, DIL_OUT), BF16),
            jax.ShapeDtypeStruct((bsz, r, length, LANES), F32),
        ],
        scratch_shapes=[pltpu.VMEM((units, 2 * DIL_BLOCK), F32),
                        pltpu.VMEM((units, LANES), F32),
                        pltpu.VMEM((units, 2 * DIL_BLOCK), BF16)],
        compiler_params=_cparams(("parallel", "parallel", "parallel")),
        name=f"dilattn_r{r}",
    )(qkv, qkv, qkv, qkv, qkv)


def _merge_kernel(oa_ref, o1_ref, o2_ref, o3_ref, l1_ref, l2_ref, l3_ref, ga_ref, gb_ref, x_ref,
                  g2_ref, wa_ref, wb_ref, wo_ref, lng_ref, lnb_ref, out_ref, l_scr, w_scr, ob_scr,
                  *, dilations):
    y_a = jnp.dot(oa_ref[...], wa_ref[...], preferred_element_type=F32)
    o_refs = (o1_ref, o2_ref, o3_ref)
    l_refs = (l1_ref, l2_ref, l3_ref)
    tm = oa_ref.shape[0]

    def tok_rows(c, r):
        return pl.ds(c, tm // r, stride=r) if r > 1 else slice(None)

    for g, r in enumerate(dilations):
        for c in range(r):
            l_scr[g, tok_rows(c, r), :] = l_refs[g][0, c]
    l1, l2, l3 = l_scr[0], l_scr[1], l_scr[2]
    mx = jnp.maximum(jnp.maximum(l1, l2), l3)
    e1, e2, e3 = jnp.exp2(l1 - mx), jnp.exp2(l2 - mx), jnp.exp2(l3 - mx)
    inv = 1.0 / (e1 + e2 + e3)
    w_scr[0], w_scr[1], w_scr[2] = e1 * inv, e2 * inv, e3 * inv
    for g, r in enumerate(dilations):
        for c in range(r):
            rows = tok_rows(c, r)
            wr = w_scr[g, rows, :]
            for h in range(DIL_HEADS):
                sl = slice(h * DIL_HEAD_DIM, (h + 1) * DIL_HEAD_DIM)
                part = wr[:, h:h + 1] * o_refs[g][0, c, :, sl].astype(F32)
                if g == 0:
                    ob_scr[h, rows, :] = part
                else:
                    ob_scr[h, rows, :] += part
    o_b = jnp.concatenate([ob_scr[h].astype(BF16) for h in range(DIL_HEADS)], axis=1)
    y_b = jnp.dot(o_b, wb_ref[...], preferred_element_type=F32)
    merged = _sigmoid(ga_ref[...].astype(F32)) * y_a + _sigmoid(gb_ref[...].astype(F32)) * y_b
    mix = jnp.dot(merged.astype(BF16), wo_ref[...], preferred_element_type=F32)
    y = DN_ALPHA * x_ref[...] + g2_ref[0] * mix
    out_ref[...] = _layer_norm(y, lng_ref[...], lnb_ref[...])


def _merge(oa2d, o_groups, lse_groups, p2d, x2d, mods3, k_gate, w_a, w_b, w_o, ln_g, ln_b, seq,
           dilations, tm=256):
    m, d = x2d.shape
    tiles_per_seq = seq // tm
    resident = pl.Buffered(1)
    n_g = len(dilations)

    def row_spec(width, col=0):
        return pl.BlockSpec((tm, width), lambda i: (i, col))

    def class_spec(r, width):
        return pl.BlockSpec((1, r, tm // r, width),
                            lambda i: (i // tiles_per_seq, 0, i % tiles_per_seq, 0))

    def const_spec(shape):
        return pl.BlockSpec(shape, lambda i: (0, 0), pipeline_mode=resident)

    return pl.pallas_call(
        functools.partial(_merge_kernel, dilations=dilations),
        grid=(m // tm,),
        in_specs=[
            row_spec(GLA_VW),
            *[class_spec(r, DIL_OUT) for r in dilations],
            *[class_spec(r, LANES) for r in dilations],
            row_spec(D_MODEL, OFF_GA // D_MODEL), row_spec(D_MODEL, OFF_GB // D_MODEL),
            row_spec(d),
            pl.BlockSpec((1, 1, d), lambda i: ((i // tiles_per_seq) * N_MOD + k_gate, 0, 0)),
            const_spec(w_a.shape), const_spec(w_b.shape), const_spec(w_o.shape),
            const_spec((1, d)), const_spec((1, d)),
        ],
        out_specs=row_spec(d),
        out_shape=jax.ShapeDtypeStruct((m, d), F32),
        scratch_shapes=[pltpu.VMEM((n_g, tm, LANES), F32), pltpu.VMEM((n_g, tm, LANES), F32),
                        pltpu.VMEM((DIL_HEADS, tm, DIL_HEAD_DIM), F32)],
        compiler_params=_cparams(("parallel",)),
        name="merge",
    )(oa2d, *o_groups, *lse_groups, p2d, p2d, x2d, mods3, w_a, w_b, w_o, ln_g, ln_b)


def _pack_w_in(w_in):
    o = 0
    segs = []
    for wdt in (GLA_QK, GLA_QK, GLA_VW, GLA_VW, GLA_GATE_RANK, DIL_W, DIL_W, DIL_W, D_MODEL, D_MODEL):
        segs.append(w_in[:, o:o + wdt].astype(BF16))
        o += wdt
    gq, gk, gv, gr, glr, dq, dk, dv, ga, gb = segs
    cols = [gq, gk, gv, gr, ga, gb]
    for g in range(DIL_GROUPS):
        sl = slice(g * DIL_OUT, (g + 1) * DIL_OUT)
        cols += [dq[:, sl], dk[:, sl], dv[:, sl]]
    cols.append(jnp.pad(glr, ((0, 0), (0, LR_PAD - GLA_GATE_RANK))))
    return jnp.concatenate(cols, axis=1)


def kernel(x, c, positions, w_ada, b_ada, ln1_g, ln1_b, w_ffn1_gu, w_ffn1_down, w_in, w_alpha2,
           b_alpha, gla_norm_g, w_branch_a, w_branch_b, w_out, ln2_g, ln2_b, w_ffn2_gu, w_ffn2_down,
           ln3_g, ln3_b):
    bsz, seq, d = x.shape
    m = bsz * seq
    x2d = x.reshape(m, d)
    c_pad = jnp.pad(c, ((0, 8 - bsz % 8 if bsz % 8 else 0), (0, 0)))
    dilations = tuple(r for _, r in DIL_PATTERNS)

    half = DIL_HEAD_DIM // 2
    freq = ROPE_THETA ** (-jnp.arange(half, dtype=F32) / half)
    freq2 = jnp.concatenate([freq, freq]).reshape(1, DIL_HEAD_DIM)
    rope_t = _rope_tables(positions.reshape(m // PROJ_TM, PROJ_TM // LANES, LANES), freq2, dilations)

    for l in range(DEPTH):
        mods = _mods(c_pad, w_ada[l], b_ada[l].reshape(1, -1))[:bsz]
        mods3 = mods.reshape(bsz * N_MOD, 1, d)

        x1 = _ffn(x2d, mods3, 0, 1, 2, w_ffn1_gu[l].astype(BF16), w_ffn1_down[l],
                  ln1_g[l].reshape(1, d), ln1_b[l].reshape(1, d), seq)

        w_all = _pack_w_in(w_in[l])
        p2d, lr2d = _proj(x1, mods3, 3, 4, w_all, seq)
        p3 = p2d.reshape(bsz, seq, PROJ_W)

        w_a2 = jnp.pad(w_alpha2[l], ((0, LR_PAD - GLA_GATE_RANK), (0, 0))).astype(BF16)
        o_a = _gla(p3, lr2d.reshape(bsz, seq, LR_PAD), w_a2, b_alpha[l].reshape(1, GLA_QK),
                   gla_norm_g[l].reshape(GLA_HEADS, 1, GLA_DV))

        o_groups, lse_groups = [], []
        for g, r in enumerate(dilations):
            qkv = _proj_dil(x1, mods3, 3, 4, w_all, g, rope_t[2 * g], rope_t[2 * g + 1],
                            bsz, seq, r)
            o_g, lse_g = _dil_group(qkv)
            o_groups.append(o_g)
            lse_groups.append(lse_g)

        x2 = _merge(o_a.reshape(m, GLA_VW), o_groups, lse_groups, p2d, x1, mods3, 5,
                    w_branch_a[l].astype(BF16), w_branch_b[l].astype(BF16), w_out[l].astype(BF16),
                    ln2_g[l].reshape(1, d), ln2_b[l].reshape(1, d), seq, dilations)

        x2d = _ffn(x2, mods3, 6, 7, 8, w_ffn2_gu[l].astype(BF16), w_ffn2_down[l],
                   ln3_g[l].reshape(1, d), ln3_b[l].reshape(1, d), seq)
    return x2d.reshape(bsz, seq, d)
```

```python
import functools

import jax
import jax.numpy as jnp
from jax import lax
from jax.experimental import pallas as pl
from jax.experimental.pallas import tpu as pltpu

F32 = jnp.float32
BF16 = jnp.bfloat16

D_MODEL = 2048
DEPTH = 1
D_FF = 5632
N_MOD = 9
LN_EPS = 1e-5
DN_ALPHA = (2.0 * DEPTH) ** 0.25

GLA_HEADS = 4
GLA_DK = 256
GLA_DV = 512
GLA_GATE_RANK = 16
GLA_TAU = 16.0
GLA_CHUNK = 64
GLA_QK = GLA_HEADS * GLA_DK
GLA_VW = GLA_HEADS * GLA_DV

DIL_PATTERNS = ((128, 1), (512, 4), (2048, 16))
DIL_GROUPS = len(DIL_PATTERNS)
DIL_HEADS = 8
DIL_HEAD_DIM = 128
DIL_BLOCK = 128
DIL_W = DIL_GROUPS * DIL_HEADS * DIL_HEAD_DIM
DIL_OUT = DIL_HEADS * DIL_HEAD_DIM
ROPE_THETA = 10000.0

LANES = 128
LR_PAD = LANES

PROJ_W = 2 * GLA_QK + 2 * GLA_VW + 2 * D_MODEL
OFF_GQ = 0
OFF_GK = GLA_QK
OFF_GV = 2 * GLA_QK
OFF_GR = OFF_GV + GLA_VW
OFF_GA = OFF_GR + GLA_VW
OFF_GB = OFF_GA + D_MODEL
DIL_QKV = 3 * DIL_OUT
OFF_LR = PROJ_W + DIL_GROUPS * DIL_QKV

NEG_BIG = -1e30

VMEM_LIMIT = 56 * 1024 * 1024


def _cparams(sem, vmem=VMEM_LIMIT):
    return pltpu.CompilerParams(dimension_semantics=sem, vmem_limit_bytes=vmem)


def _sigmoid(x):
    return 1.0 / (1.0 + jnp.exp(-x))


def _layer_norm(y, g, b):
    mu = jnp.mean(y, axis=-1, keepdims=True)
    d = y - mu
    var = jnp.mean(d * d, axis=-1, keepdims=True)
    return d * lax.rsqrt(var + LN_EPS) * g + b


def _mods_kernel(c_ref, w_ref, b_ref, o_ref):
    c = c_ref[...]
    c_act = (c * _sigmoid(c)).astype(BF16)
    o_ref[...] = jnp.dot(c_act, w_ref[...].astype(BF16), preferred_element_type=F32) + b_ref[...]


def _mods(c_pad, w_ada, b_ada, tn=1024):
    rows, d = c_pad.shape
    n = w_ada.shape[1]
    return pl.pallas_call(
        _mods_kernel,
        grid=(n // tn,),
        in_specs=[
            pl.BlockSpec((rows, d), lambda j: (0, 0)),
            pl.BlockSpec((d, tn), lambda j: (0, j)),
            pl.BlockSpec((1, tn), lambda j: (0, j)),
        ],
        out_specs=pl.BlockSpec((rows, tn), lambda j: (0, j)),
        out_shape=jax.ShapeDtypeStruct((rows, n), F32),
        compiler_params=_cparams(("arbitrary",)),
        name="mods",
    )(c_pad, w_ada, b_ada)


FFN_ROW_CHUNK = 512
FFN_VMEM_LIMIT = 60 * 1024 * 1024


def _ffn_kernel(x_ref, sh_ref, sc_ref, g_ref, wg_ref, wu_ref, wd_ref, lng_ref, lnb_ref,
                o_ref, h_scr):
    j = pl.program_id(1)

    @pl.when(j == 0)
    def _():
        h_scr[...] = (x_ref[...] * (1.0 + sc_ref[0]) + sh_ref[0]).astype(BF16)
        o_ref[...] = jnp.zeros_like(o_ref)

    for r0 in range(0, o_ref.shape[0], FFN_ROW_CHUNK):
        rs = slice(r0, r0 + FFN_ROW_CHUNK)
        h = h_scr[rs, :]
        gate = jnp.dot(h, wg_ref[...], preferred_element_type=F32)
        up = jnp.dot(h, wu_ref[...], preferred_element_type=F32)
        act = (gate * _sigmoid(gate) * up).astype(BF16)
        o_ref[rs, :] += jnp.dot(act, wd_ref[...], preferred_element_type=F32)

    @pl.when(j == pl.num_programs(1) - 1)
    def _():
        y = DN_ALPHA * x_ref[...] + 0.5 * g_ref[0] * o_ref[...]
        o_ref[...] = _layer_norm(y, lng_ref[...], lnb_ref[...])


def _ffn(x2d, mods3, k_shift, k_scale, k_gate, w_gu, w_down, ln_g, ln_b, seq, tm=1024, tf=512):
    m, d = x2d.shape
    n_ff = w_down.shape[0]
    nj = n_ff // tf
    tiles_per_seq = seq // tm

    def mod_spec(k):
        return pl.BlockSpec((1, 1, d), lambda i, j: ((i // tiles_per_seq) * N_MOD + k, 0, 0))

    return pl.pallas_call(
        _ffn_kernel,
        grid=(m // tm, nj),
        in_specs=[
            pl.BlockSpec((tm, d), lambda i, j: (i, 0)),
            mod_spec(k_shift), mod_spec(k_scale), mod_spec(k_gate),
            pl.BlockSpec((d, tf), lambda i, j: (0, j)),
            pl.BlockSpec((d, tf), lambda i, j: (0, j + nj)),
            pl.BlockSpec((tf, d), lambda i, j: (j, 0)),
            pl.BlockSpec((1, d), lambda i, j: (0, 0)),
            pl.BlockSpec((1, d), lambda i, j: (0, 0)),
        ],
        out_specs=pl.BlockSpec((tm, d), lambda i, j: (i, 0)),
        out_shape=jax.ShapeDtypeStruct((m, d), F32),
        scratch_shapes=[pltpu.VMEM((tm, d), BF16)],
        compiler_params=_cparams(("parallel", "arbitrary"), FFN_VMEM_LIMIT),
        name="ffn",
    )(x2d, mods3, mods3, mods3, w_gu, w_gu, w_down, ln_g, ln_b)


PROJ_TM = 1024


def _rope_kernel(pos_ref, freq_ref, *out_refs, dilations):
    cos_ref, sin_ref = out_refs[0], out_refs[1]
    tm = cos_ref.shape[0]
    pos_t = pos_ref[0].astype(F32).T
    lane = lax.broadcasted_iota(jnp.int32, (LANES, DIL_HEAD_DIM), 1)
    for k in range(tm // LANES):
        ang = pos_t[:, k:k + 1] * freq_ref[...]
        s = jnp.sin(ang)
        cos_ref[k * LANES:(k + 1) * LANES, :] = jnp.cos(ang)
        sin_ref[k * LANES:(k + 1) * LANES, :] = jnp.where(lane < DIL_HEAD_DIM // 2, -s, s)
    for gi, r in enumerate(dilations[1:]):
        tl = tm // r
        for c in range(r):
            out_refs[2 + 2 * gi][c * tl:(c + 1) * tl, :] = cos_ref[pl.ds(c, tl, stride=r), :]
            out_refs[3 + 2 * gi][c * tl:(c + 1) * tl, :] = sin_ref[pl.ds(c, tl, stride=r), :]


def _rope_tables(pos3, freq2, dilations, tm=PROJ_TM):
    assert dilations[0] == 1
    m = pos3.shape[0] * tm
    n_out = 2 * len(dilations)
    return pl.pallas_call(
        functools.partial(_rope_kernel, dilations=dilations),
        grid=(m // tm,),
        in_specs=[
            pl.BlockSpec((1, tm // LANES, LANES), lambda i: (i, 0, 0)),
            pl.BlockSpec((1, DIL_HEAD_DIM), lambda i: (0, 0)),
        ],
        out_specs=[pl.BlockSpec((tm, DIL_HEAD_DIM), lambda i: (i, 0))] * n_out,
        out_shape=[jax.ShapeDtypeStruct((m, DIL_HEAD_DIM), F32)] * n_out,
        compiler_params=_cparams(("parallel",)),
        name="rope_tables",
    )(pos3, freq2)


def _proj_kernel(x_ref, sh_ref, sc_ref, w_ref, wlr_ref, o_ref, lr_ref, h_scr):
    @pl.when(pl.program_id(1) == 0)
    def _():
        h = (x_ref[...] * (1.0 + sc_ref[0]) + sh_ref[0]).astype(BF16)
        h_scr[...] = h
        lr_ref[...] = jnp.dot(h, wlr_ref[...], preferred_element_type=F32)

    o_ref[...] = jnp.dot(h_scr[...], w_ref[...], preferred_element_type=F32).astype(BF16)


def _proj(x2d, mods3, k_shift, k_scale, w_all, seq, tm=PROJ_TM, tn=2048):
    m, d = x2d.shape
    n = PROJ_W
    tiles_per_seq = seq // tm

    def mod_spec(k):
        return pl.BlockSpec((1, 1, d), lambda i, j: ((i // tiles_per_seq) * N_MOD + k, 0, 0))

    return pl.pallas_call(
        _proj_kernel,
        grid=(m // tm, n // tn),
        in_specs=[
            pl.BlockSpec((tm, d), lambda i, j: (i, 0)),
            mod_spec(k_shift), mod_spec(k_scale),
            pl.BlockSpec((d, tn), lambda i, j: (0, j)),
            pl.BlockSpec((d, LR_PAD), lambda i, j: (0, OFF_LR // LR_PAD)),
        ],
        out_specs=[
            pl.BlockSpec((tm, tn), lambda i, j: (i, j)),
            pl.BlockSpec((tm, LR_PAD), lambda i, j: (i, 0)),
        ],
        out_shape=[
            jax.ShapeDtypeStruct((m, n), BF16),
            jax.ShapeDtypeStruct((m, LR_PAD), F32),
        ],
        scratch_shapes=[pltpu.VMEM((tm, d), BF16)],
        compiler_params=_cparams(("parallel", "arbitrary")),
        name="proj",
    )(x2d, mods3, mods3, w_all, w_all)


def _proj_dil_kernel(x_ref, sh_ref, sc_ref, w_ref, cos_ref, sin_ref, o_ref, h_scr, slab_scr, *, r):
    j = pl.program_id(1)
    tm, d = x_ref.shape
    tl = tm // r

    @pl.when(j == 0)
    def _():
        if r == 1:
            h_scr[...] = (x_ref[...] * (1.0 + sc_ref[0]) + sh_ref[0]).astype(BF16)
        else:
            for k in range(d // LANES):
                sl = slice(k * LANES, (k + 1) * LANES)
                slab_scr[k] = x_ref[:, sl] * (1.0 + sc_ref[0, :, sl]) + sh_ref[0, :, sl]
            for c in range(r):
                for k in range(d // LANES):
                    sl = slice(k * LANES, (k + 1) * LANES)
                    h_scr[c * tl:(c + 1) * tl, sl] = slab_scr[k, pl.ds(c, tl, stride=r), :].astype(BF16)

    _proj_dil_tile(j, h_scr, w_ref, cos_ref, sin_ref, o_ref, r)


def _proj_dil_tile(j, h_scr, w_ref, cos_ref, sin_ref, o_ref, r):
    tl = h_scr.shape[0] // r
    t = jnp.dot(h_scr[...], w_ref[...], preferred_element_type=F32)
    is_v = j == 2
    scale = jnp.where(j == 0, DIL_HEAD_DIM ** -0.5, 1.0).astype(F32)
    a = jnp.where(is_v, 1.0, cos_ref[...] * scale)
    b = jnp.where(is_v, 0.0, sin_ref[...] * scale)
    for s in range(t.shape[1] // DIL_HEAD_DIM):
        sl = slice(s * DIL_HEAD_DIM, (s + 1) * DIL_HEAD_DIM)
        ts = t[:, sl]
        rot = (ts * a + pltpu.roll(ts, DIL_HEAD_DIM // 2, 1) * b).astype(BF16)
        for c in range(r):
            o_ref[0, c, :, sl] = rot[c * tl:(c + 1) * tl, :]


def _proj_dil_dma_kernel(x_hbm, sh_ref, sc_ref, w_ref, cos_ref, sin_ref, o_ref, h_scr, x_buf, sem,
                         *, r):
    i = pl.program_id(0)
    j = pl.program_id(1)
    tl = h_scr.shape[0] // r
    slot = i % 2

    def tile_copies(tile, buf_slot):
        return [pltpu.make_async_copy(x_hbm.at[pl.ds(tile * tl, tl), c], x_buf.at[buf_slot, c],
                                      sem.at[buf_slot]) for c in range(r)]

    @pl.when(j == 0)
    def _():
        @pl.when(i == 0)
        def _():
            for cp in tile_copies(0, 0):
                cp.start()

        @pl.when(i + 1 < pl.num_programs(0))
        def _():
            for cp in tile_copies(i + 1, 1 - slot):
                cp.start()

        for cp in tile_copies(i, slot):
            cp.wait()
        for c in range(r):
            h_scr[c * tl:(c + 1) * tl, :] = (x_buf[slot, c] * (1.0 + sc_ref[0]) + sh_ref[0]).astype(BF16)

    _proj_dil_tile(j, h_scr, w_ref, cos_ref, sin_ref, o_ref, r)


SUBLANES_F32 = 8


def _proj_dil(x2d, mods3, k_shift, k_scale, w_all, g, cos_t, sin_t, bsz, seq, r, tm=PROJ_TM):
    m, d = x2d.shape
    tiles_per_seq = seq // tm
    tl = tm // r
    col0 = (PROJ_W + g * DIL_QKV) // DIL_OUT
    dma_gather = r % SUBLANES_F32 == 0

    def mod_spec(k):
        return pl.BlockSpec((1, 1, d), lambda i, j: ((i // tiles_per_seq) * N_MOD + k, 0, 0))

    if dma_gather:
        body = functools.partial(_proj_dil_dma_kernel, r=r)
        x_arg = x2d.reshape(m // r, r, d)
        x_spec = pl.BlockSpec(memory_space=pl.ANY)
        scratch = [pltpu.VMEM((tm, d), BF16), pltpu.VMEM((2, r, tl, d), F32),
                   pltpu.SemaphoreType.DMA((2,))]
        semantics = ("arbitrary", "arbitrary")
    else:
        body = functools.partial(_proj_dil_kernel, r=r)
        x_arg = x2d
        x_spec = pl.BlockSpec((tm, d), lambda i, j: (i, 0))
        scratch = [pltpu.VMEM((tm, d), BF16),
                   pltpu.VMEM((d // LANES, tm, LANES) if r > 1 else (1, 8, LANES), F32)]
        semantics = ("parallel", "arbitrary")

    return pl.pallas_call(
        body,
        grid=(m // tm, 3),
        in_specs=[
            x_spec,
            mod_spec(k_shift), mod_spec(k_scale),
            pl.BlockSpec((d, DIL_OUT), lambda i, j: (0, col0 + j)),
            pl.BlockSpec((tm, DIL_HEAD_DIM), lambda i, j: (i, 0)),
            pl.BlockSpec((tm, DIL_HEAD_DIM), lambda i, j: (i, 0)),
        ],
        out_specs=pl.BlockSpec((1, r, tl, DIL_OUT),
                               lambda i, j: (i // tiles_per_seq, 0, i % tiles_per_seq, j)),
        out_shape=jax.ShapeDtypeStruct((bsz, r, seq // r, DIL_QKV), BF16),
        scratch_shapes=scratch,
        compiler_params=_cparams(semantics),
        name=f"proj_dil_r{r}",
    )(x_arg, mods3, mods3, w_all, cos_t, sin_t)


def _gla_kernel(q_ref, k_ref, v_ref, r_ref, lr_ref, wa_ref, ba_ref, ng_ref, o_ref,
                st_scr, b_scr, qd_scr, qh_scr, ki_scr, ke_scr, kp_scr, kh_scr, a_scr, *, n_chunks):
    c_len = GLA_CHUNK
    tc = q_ref.shape[1]
    hb = 4 * c_len
    pw = 2 * c_len
    n_pairs = n_chunks // 2

    @pl.when(pl.program_id(2) == 0)
    def _():
        st_scr[...] = jnp.zeros_like(st_scr)

    ti = lax.broadcasted_iota(jnp.int32, (hb, hb), 0)
    tj = lax.broadcasted_iota(jnp.int32, (hb, hb), 1)
    tri = jnp.logical_and(ti >= tj, ti // c_len == tj // c_len).astype(BF16)
    for s in range(tc // hb):
        rows = slice(s * hb, (s + 1) * hb)
        logits = jnp.dot(lr_ref[0, rows, :].astype(BF16), wa_ref[...],
                         preferred_element_type=F32) + ba_ref[...]
        log_a = (jnp.minimum(logits, 0.0) - jnp.log(1.0 + jnp.exp(-jnp.abs(logits)))) / GLA_TAU
        p1 = log_a.astype(BF16)
        r1 = log_a - p1.astype(F32)
        p2 = r1.astype(BF16)
        p3 = (r1 - p2.astype(F32)).astype(BF16)
        b_scr[rows, :] = (jnp.dot(tri, p1, preferred_element_type=F32)
                          + jnp.dot(tri, p2, preferred_element_type=F32)
                          + jnp.dot(tri, p3, preferred_element_type=F32))

    nt = (((1,), (1,)), ((), ()))

    def chunk_rows(c):
        return slice(c * c_len, (c + 1) * c_len)

    def pair_rows(p):
        return slice(p * pw, (p + 1) * pw)

    b_last = [b_scr[(c + 1) * c_len - 1:(c + 1) * c_len, :] for c in range(n_chunks)]
    h_log = [jnp.zeros_like(b_last[0])]
    for c in range(n_chunks):
        h_log.append(h_log[-1] + b_last[c])

    for c in range(n_chunks):
        rows = chunk_rows(c)
        b = b_scr[rows, :]
        q = q_ref[0, rows, :].astype(F32)
        k = k_ref[0, rows, :].astype(F32)
        qd = q * jnp.exp(b) * (GLA_DK ** -0.5)
        ke = k * jnp.exp(b_last[c] - b)
        qd_scr[rows, :] = qd
        qh_scr[rows, :] = (qd * jnp.exp(h_log[c])).astype(BF16)
        ki_scr[rows, :] = (k * jnp.exp(-b)).astype(BF16)
        ke_scr[rows, :] = ke.astype(BF16)
        kp_scr[rows, :] = (ke * jnp.exp(b_last[c + 1]) if c % 2 == 0 else ke).astype(BF16)
        kh_scr[rows, :] = (ke * jnp.exp(h_log[n_chunks] - h_log[c + 1])).astype(BF16)

    pi = lax.broadcasted_iota(jnp.int32, (pw, pw), 0)
    pj = lax.broadcasted_iota(jnp.int32, (pw, pw), 1)
    diag = jnp.logical_and(pi >= pj, pi // c_len == pj // c_len)
    lower = jnp.logical_and(pi >= c_len, pj < c_len)
    for p in range(n_pairs):
        first = chunk_rows(2 * p)
        keys = jnp.concatenate([ki_scr[pair_rows(p), :], ke_scr[first, :], ke_scr[first, :]], axis=0)
        s = lax.dot_general(qd_scr[pair_rows(p), :].astype(BF16), keys, nt,
                            preferred_element_type=F32)
        s = jnp.where(diag, s[:, :pw], jnp.where(lower, s[:, pw:], 0.0))
        a_scr[pair_rows(p), pair_rows(p)] = s.astype(BF16)

    for p in range(n_pairs - 1):
        c0 = 2 * p + 2
        lhs = jnp.concatenate(
            [(qd_scr[chunk_rows(c), :] * jnp.exp(h_log[c] - h_log[c0])).astype(BF16)
             for c in range(c0, n_chunks)], axis=0)
        s = lax.dot_general(lhs, kp_scr[pair_rows(p), :], nt, preferred_element_type=F32)
        a_scr[c0 * c_len:, pair_rows(p)] = s.astype(BF16)

    half = 2 * pw
    for hf in range(tc // half):
        a_scr[hf * half:hf * half + pw, hf * half + pw:(hf + 1) * half] = jnp.zeros((pw, pw), BF16)

    st = st_scr[...]
    st_b = st.astype(BF16)
    for hf in range(tc // half):
        rows = slice(hf * half, (hf + 1) * half)
        keys = (hf + 1) * half
        o = jnp.dot(a_scr[rows, :keys], v_ref[0, :keys, :], preferred_element_type=F32)
        o += jnp.dot(qh_scr[rows, :], st_b, preferred_element_type=F32)
        o = o * lax.rsqrt(jnp.mean(o * o, axis=-1, keepdims=True) + LN_EPS) * ng_ref[0]
        r = r_ref[0, rows, :].astype(F32)
        o_ref[0, rows, :] = (o * (r * _sigmoid(r))).astype(BF16)

    d_col = jnp.broadcast_to(jnp.exp(h_log[n_chunks]), (8, GLA_DK)).T[:, :1]
    st_scr[...] = st * d_col + lax.dot_general(
        kh_scr[...], v_ref[0], (((0,), (0,)), ((), ())), preferred_element_type=F32)


def _gla(p3, lr3, w_a2, b_a, norm_g3, tc=512):
    bsz, seq, _ = p3.shape
    nq, nv, nr = OFF_GK // GLA_DK, OFF_GV // GLA_DV, OFF_GR // GLA_DV
    return pl.pallas_call(
        functools.partial(_gla_kernel, n_chunks=tc // GLA_CHUNK),
        grid=(bsz, GLA_HEADS, seq // tc),
        in_specs=[
            pl.BlockSpec((1, tc, GLA_DK), lambda b, h, t: (b, t, h)),
            pl.BlockSpec((1, tc, GLA_DK), lambda b, h, t: (b, t, nq + h)),
            pl.BlockSpec((1, tc, GLA_DV), lambda b, h, t: (b, t, nv + h)),
            pl.BlockSpec((1, tc, GLA_DV), lambda b, h, t: (b, t, nr + h)),
            pl.BlockSpec((1, tc, LR_PAD), lambda b, h, t: (b, t, 0)),
            pl.BlockSpec((LR_PAD, GLA_DK), lambda b, h, t: (0, h)),
            pl.BlockSpec((1, GLA_DK), lambda b, h, t: (0, h)),
            pl.BlockSpec((1, 1, GLA_DV), lambda b, h, t: (h, 0, 0)),
        ],
        out_specs=pl.BlockSpec((1, tc, GLA_DV), lambda b, h, t: (b, t, h)),
        out_shape=jax.ShapeDtypeStruct((bsz, seq, GLA_VW), BF16),
        scratch_shapes=[pltpu.VMEM((GLA_DK, GLA_DV), F32), pltpu.VMEM((tc, GLA_DK), F32),
                        pltpu.VMEM((tc, GLA_DK), F32)]
                       + [pltpu.VMEM((tc, GLA_DK), BF16)] * 5 + [pltpu.VMEM((tc, tc), BF16)],
        compiler_params=_cparams(("parallel", "parallel", "arbitrary")),
        name="gla",
    )(p3, p3, p3, p3, lr3, w_a2, b_a, norm_g3)


def _dil_kernel(q_ref, k_ref, v_ref, kp_ref, vp_ref, o_ref, lse_ref, s_scr, m_scr, p_scr, *, nblk):
    n = pl.program_id(2)
    blk = DIL_BLOCK
    qi = lax.broadcasted_iota(jnp.int32, (blk, 2 * blk), 0)
    kj = lax.broadcasted_iota(jnp.int32, (blk, 2 * blk), 1)
    band = jnp.logical_or(jnp.logical_and(kj < blk, kj >= qi),
                          jnp.logical_and(kj >= blk, kj - blk <= qi))
    band_first = jnp.logical_and(band, jnp.logical_or(kj >= blk, n > 0))
    nt = (((1,), (1,)), ((), ()))

    def rows(i):
        return slice(i * blk, (i + 1) * blk)

    def cols(h):
        return slice(h * DIL_HEAD_DIM, (h + 1) * DIL_HEAD_DIM)

    for i in range(nblk):
        for h in range(DIL_HEADS):
            q = q_ref[0, 0, rows(i), cols(h)]
            k_prev = kp_ref[0, 0, :, cols(h)] if i == 0 else k_ref[0, 0, rows(i - 1), cols(h)]
            s_p = lax.dot_general(q, k_prev, nt, preferred_element_type=F32)
            s_c = lax.dot_general(q, k_ref[0, 0, rows(i), cols(h)], nt, preferred_element_type=F32)
            s = jnp.concatenate([s_p, s_c], axis=1)
            s_scr[rows(i * DIL_HEADS + h), :] = jnp.where(band_first if i == 0 else band, s, NEG_BIG)

    s_all = s_scr[...]
    m_all = jnp.max(s_all, axis=-1, keepdims=True)
    p_scr[...] = jnp.exp(s_all - m_all).astype(BF16)
    m_scr[...] = jnp.broadcast_to(m_all, m_scr.shape)

    lane = lax.broadcasted_iota(jnp.int32, (blk, LANES), 1)
    ones = jnp.ones((2 * blk, DIL_HEAD_DIM), BF16)
    for i in range(nblk):
        lse_tile = jnp.zeros((blk, LANES), F32)
        for h in range(DIL_HEADS):
            u = i * DIL_HEADS + h
            v_prev = vp_ref[0, 0, :, cols(h)] if i == 0 else v_ref[0, 0, rows(i - 1), cols(h)]
            v_aug = jnp.concatenate(
                [jnp.concatenate([v_prev, v_ref[0, 0, rows(i), cols(h)]], axis=0), ones], axis=1)
            acc = jnp.dot(p_scr[rows(u), :], v_aug, preferred_element_type=F32)
            den = acc[:, DIL_HEAD_DIM:]
            o_ref[0, 0, rows(i), cols(h)] = (acc[:, :DIL_HEAD_DIM] / den).astype(BF16)
            lse_tile = jnp.where(lane == h, m_scr[rows(u), :] + jnp.log(den), lse_tile)
        lse_ref[0, 0, rows(i), :] = lse_tile


def _dil_group(qkv, nblk=4):
    bsz, r, length, _ = qkv.shape
    nb = length // DIL_BLOCK
    nblk = min(nblk, nb)
    assert nb % nblk == 0
    tq = nblk * DIL_BLOCK
    units = nblk * DIL_HEADS * DIL_BLOCK

    def cur_spec(u):
        return pl.BlockSpec((1, 1, tq, DIL_OUT), lambda b, c, n: (b, c, n, u))

    def prev_spec(u):
        return pl.BlockSpec((1, 1, DIL_BLOCK, DIL_OUT),
                            lambda b, c, n: (b, c, jnp.maximum(n * nblk - 1, 0), u))

    return pl.pallas_call(
        functools.partial(_dil_kernel, nblk=nblk),
        grid=(bsz, r, nb // nblk),
        in_specs=[cur_spec(0), cur_spec(1), cur_spec(2), prev_spec(1), prev_spec(2)],
        out_specs=[
            pl.BlockSpec((1, 1, tq, DIL_OUT), lambda b, c, n: (b, c, n, 0)),
            pl.BlockSpec((1, 1, tq, LANES), lambda b, c, n: (b, c, n, 0)),
        ],
        out_shape=[
            jax.ShapeDtypeStruct((bsz, r, length, DIL_OUT), BF16),
            jax.ShapeDtypeStruct((bsz, r, length, LANES), F32),
        ],
        scratch_shapes=[pltpu.VMEM((units, 2 * DIL_BLOCK), F32),
                        pltpu.VMEM((units, LANES), F32),
                        pltpu.VMEM((units, 2 * DIL_BLOCK), BF16)],
        compiler_params=_cparams(("parallel", "parallel", "parallel")),
        name=f"dilattn_r{r}",
    )(qkv, qkv, qkv, qkv, qkv)


def _merge_kernel(oa_ref, o1_ref, o2_ref, o3_ref, l1_ref, l2_ref, l3_ref, ga_ref, gb_ref, x_ref,
                  g2_ref, wa_ref, wb_ref, wo_ref, lng_ref, lnb_ref, out_ref, l_scr, w_scr, ob_scr,
                  *, dilations):
    y_a = jnp.dot(oa_ref[...], wa_ref[...], preferred_element_type=F32)
    o_refs = (o1_ref, o2_ref, o3_ref)
    l_refs = (l1_ref, l2_ref, l3_ref)
    tm = oa_ref.shape[0]

    def tok_rows(c, r):
        return pl.ds(c, tm // r, stride=r) if r > 1 else slice(None)

    for g, r in enumerate(dilations):
        for c in range(r):
            l_scr[g, tok_rows(c, r), :] = l_refs[g][0, c]
    l1, l2, l3 = l_scr[0], l_scr[1], l_scr[2]
    mx = jnp.maximum(jnp.maximum(l1, l2), l3)
    e1, e2, e3 = jnp.exp(l1 - mx), jnp.exp(l2 - mx), jnp.exp(l3 - mx)
    inv = 1.0 / (e1 + e2 + e3)
    w_scr[0], w_scr[1], w_scr[2] = e1 * inv, e2 * inv, e3 * inv
    for g, r in enumerate(dilations):
        for c in range(r):
            rows = tok_rows(c, r)
            wr = w_scr[g, rows, :]
            for h in range(DIL_HEADS):
                sl = slice(h * DIL_HEAD_DIM, (h + 1) * DIL_HEAD_DIM)
                part = wr[:, h:h + 1] * o_refs[g][0, c, :, sl].astype(F32)
                if g == 0:
                    ob_scr[h, rows, :] = part
                else:
                    ob_scr[h, rows, :] += part
    o_b = jnp.concatenate([ob_scr[h].astype(BF16) for h in range(DIL_HEADS)], axis=1)
    y_b = jnp.dot(o_b, wb_ref[...], preferred_element_type=F32)
    merged = _sigmoid(ga_ref[...].astype(F32)) * y_a + _sigmoid(gb_ref[...].astype(F32)) * y_b
    mix = jnp.dot(merged.astype(BF16), wo_ref[...], preferred_element_type=F32)
    y = DN_ALPHA * x_ref[...] + g2_ref[0] * mix
    out_ref[...] = _layer_norm(y, lng_ref[...], lnb_ref[...])


def _merge(oa2d, o_groups, lse_groups, p2d, x2d, mods3, k_gate, w_a, w_b, w_o, ln_g, ln_b, seq,
           dilations, tm=256):
    m, d = x2d.shape
    tiles_per_seq = seq // tm
    resident = pl.Buffered(1)
    n_g = len(dilations)

    def row_spec(width, col=0):
        return pl.BlockSpec((tm, width), lambda i: (i, col))

    def class_spec(r, width):
        return pl.BlockSpec((1, r, tm // r, width),
                            lambda i: (i // tiles_per_seq, 0, i % tiles_per_seq, 0))

    def const_spec(shape):
        return pl.BlockSpec(shape, lambda i: (0, 0), pipeline_mode=resident)

    return pl.pallas_call(
        functools.partial(_merge_kernel, dilations=dilations),
        grid=(m // tm,),
        in_specs=[
            row_spec(GLA_VW),
            *[class_spec(r, DIL_OUT) for r in dilations],
            *[class_spec(r, LANES) for r in dilations],
            row_spec(D_MODEL, OFF_GA // D_MODEL), row_spec(D_MODEL, OFF_GB // D_MODEL),
            row_spec(d),
            pl.BlockSpec((1, 1, d), lambda i: ((i // tiles_per_seq) * N_MOD + k_gate, 0, 0)),
            const_spec(w_a.shape), const_spec(w_b.shape), const_spec(w_o.shape),
            const_spec((1, d)), const_spec((1, d)),
        ],
        out_specs=row_spec(d),
        out_shape=jax.ShapeDtypeStruct((m, d), F32),
        scratch_shapes=[pltpu.VMEM((n_g, tm, LANES), F32), pltpu.VMEM((n_g, tm, LANES), F32),
                        pltpu.VMEM((DIL_HEADS, tm, DIL_HEAD_DIM), F32)],
        compiler_params=_cparams(("parallel",)),
        name="merge",
    )(oa2d, *o_groups, *lse_groups, p2d, p2d, x2d, mods3, w_a, w_b, w_o, ln_g, ln_b)


def _pack_w_in(w_in):
    o = 0
    segs = []
    for wdt in (GLA_QK, GLA_QK, GLA_VW, GLA_VW, GLA_GATE_RANK, DIL_W, DIL_W, DIL_W, D_MODEL, D_MODEL):
        segs.append(w_in[:, o:o + wdt].astype(BF16))
        o += wdt
    gq, gk, gv, gr, glr, dq, dk, dv, ga, gb = segs
    cols = [gq, gk, gv, gr, ga, gb]
    for g in range(DIL_GROUPS):
        sl = slice(g * DIL_OUT, (g + 1) * DIL_OUT)
        cols += [dq[:, sl], dk[:, sl], dv[:, sl]]
    cols.append(jnp.pad(glr, ((0, 0), (0, LR_PAD - GLA_GATE_RANK))))
    return jnp.concatenate(cols, axis=1)


def kernel(x, c, positions, w_ada, b_ada, ln1_g, ln1_b, w_ffn1_gu, w_ffn1_down, w_in, w_alpha2,
           b_alpha, gla_norm_g, w_branch_a, w_branch_b, w_out, ln2_g, ln2_b, w_ffn2_gu, w_ffn2_down,
           ln3_g, ln3_b):
    bsz, seq, d = x.shape
    m = bsz * seq
    x2d = x.reshape(m, d)
    c_pad = jnp.pad(c, ((0, 8 - bsz % 8 if bsz % 8 else 0), (0, 0)))
    dilations = tuple(r for _, r in DIL_PATTERNS)

    half = DIL_HEAD_DIM // 2
    freq = ROPE_THETA ** (-jnp.arange(half, dtype=F32) / half)
    freq2 = jnp.concatenate([freq, freq]).reshape(1, DIL_HEAD_DIM)
    rope_t = _rope_tables(positions.reshape(m // PROJ_TM, PROJ_TM // LANES, LANES), freq2, dilations)

    for l in range(DEPTH):
        mods = _mods(c_pad, w_ada[l], b_ada[l].reshape(1, -1))[:bsz]
        mods3 = mods.reshape(bsz * N_MOD, 1, d)

        x1 = _ffn(x2d, mods3, 0, 1, 2, w_ffn1_gu[l].astype(BF16), w_ffn1_down[l].astype(BF16),
                  ln1_g[l].reshape(1, d), ln1_b[l].reshape(1, d), seq)

        w_all = _pack_w_in(w_in[l])
        p2d, lr2d = _proj(x1, mods3, 3, 4, w_all, seq)
        p3 = p2d.reshape(bsz, seq, PROJ_W)

        w_a2 = jnp.pad(w_alpha2[l], ((0, LR_PAD - GLA_GATE_RANK), (0, 0))).astype(BF16)
        o_a = _gla(p3, lr2d.reshape(bsz, seq, LR_PAD), w_a2, b_alpha[l].reshape(1, GLA_QK),
                   gla_norm_g[l].reshape(GLA_HEADS, 1, GLA_DV))

        o_groups, lse_groups = [], []
        for g, r in enumerate(dilations):
            qkv = _proj_dil(x1, mods3, 3, 4, w_all, g, rope_t[2 * g], rope_t[2 * g + 1],
                            bsz, seq, r)
            o_g, lse_g = _dil_group(qkv)
            o_groups.append(o_g)
            lse_groups.append(lse_g)

        x2 = _merge(o_a.reshape(m, GLA_VW), o_groups, lse_groups, p2d, x1, mods3, 5,
                    w_branch_a[l].astype(BF16), w_branch_b[l].astype(BF16), w_out[l].astype(BF16),
                    ln2_g[l].reshape(1, d), ln2_b[l].reshape(1, d), seq, dilations)

        x2d = _ffn(x2, mods3, 6, 7, 8, w_ffn2_gu[l].astype(BF16), w_ffn2_down[l].astype(BF16),
                   ln3_g[l].reshape(1, d), ln3_b[l].reshape(1, d), seq)
    return x2d.reshape(bsz, seq, d)
```

```python
import functools

import jax
import jax.numpy as jnp
from jax import lax
from jax.experimental import pallas as pl
from jax.experimental.pallas import tpu as pltpu

F32 = jnp.float32
BF16 = jnp.bfloat16

D_MODEL = 2048
DEPTH = 1
D_FF = 5632
N_MOD = 9
LN_EPS = 1e-5
DN_ALPHA = (2.0 * DEPTH) ** 0.25

GLA_HEADS = 4
GLA_DK = 256
GLA_DV = 512
GLA_GATE_RANK = 16
GLA_TAU = 16.0
GLA_CHUNK = 64
GLA_QK = GLA_HEADS * GLA_DK
GLA_VW = GLA_HEADS * GLA_DV

DIL_PATTERNS = ((128, 1), (512, 4), (2048, 16))
DIL_GROUPS = len(DIL_PATTERNS)
DIL_HEADS = 8
DIL_HEAD_DIM = 128
DIL_BLOCK = 128
DIL_W = DIL_GROUPS * DIL_HEADS * DIL_HEAD_DIM
DIL_OUT = DIL_HEADS * DIL_HEAD_DIM
ROPE_THETA = 10000.0

LANES = 128
LR_PAD = LANES

PROJ_W = 2 * GLA_QK + 2 * GLA_VW + 2 * D_MODEL
OFF_GQ = 0
OFF_GK = GLA_QK
OFF_GV = 2 * GLA_QK
OFF_GR = OFF_GV + GLA_VW
OFF_GA = OFF_GR + GLA_VW
OFF_GB = OFF_GA + D_MODEL
DIL_QKV = 3 * DIL_OUT
OFF_DIL = -(-PROJ_W // DIL_QKV) * DIL_QKV
OFF_LR = OFF_DIL + DIL_GROUPS * DIL_QKV

NEG_BIG = -1e30

VMEM_LIMIT = 56 * 1024 * 1024
VMEM_LIMIT_BIG = 60 * 1024 * 1024


def _cparams(sem, vmem=VMEM_LIMIT):
    return pltpu.CompilerParams(dimension_semantics=sem, vmem_limit_bytes=vmem)


def _sigmoid(x):
    return 1.0 / (1.0 + jnp.exp(-x))


def _layer_norm(y, g, b):
    mu = jnp.mean(y, axis=-1, keepdims=True)
    d = y - mu
    var = jnp.mean(d * d, axis=-1, keepdims=True)
    return d * lax.rsqrt(var + LN_EPS) * g + b


def _mods_kernel(c_ref, w_ref, b_ref, o_ref):
    c = c_ref[...]
    c_act = (c * _sigmoid(c)).astype(BF16)
    o_ref[...] = jnp.dot(c_act, w_ref[...].astype(BF16), preferred_element_type=F32) + b_ref[...]


def _mods(c_pad, w_ada, b_ada, tn=1024):
    rows, d = c_pad.shape
    n = w_ada.shape[1]
    return pl.pallas_call(
        _mods_kernel,
        grid=(n // tn,),
        in_specs=[
            pl.BlockSpec((rows, d), lambda j: (0, 0)),
            pl.BlockSpec((d, tn), lambda j: (0, j)),
            pl.BlockSpec((1, tn), lambda j: (0, j)),
        ],
        out_specs=pl.BlockSpec((rows, tn), lambda j: (0, j)),
        out_shape=jax.ShapeDtypeStruct((rows, n), F32),
        compiler_params=_cparams(("arbitrary",)),
        name="mods",
    )(c_pad, w_ada, b_ada)


FFN_ROW_CHUNK = 512


def _ffn_kernel(x_ref, sh_ref, sc_ref, g_ref, wg_ref, wu_ref, wd_ref, lng_ref, lnb_ref,
                o_ref, h_scr):
    j = pl.program_id(1)

    @pl.when(j == 0)
    def _():
        h_scr[...] = (x_ref[...] * (1.0 + sc_ref[0]) + sh_ref[0]).astype(BF16)
        o_ref[...] = jnp.zeros_like(o_ref)

    for r0 in range(0, o_ref.shape[0], FFN_ROW_CHUNK):
        rs = slice(r0, r0 + FFN_ROW_CHUNK)
        h = h_scr[rs, :]
        gate = jnp.dot(h, wg_ref[...], preferred_element_type=F32)
        up = jnp.dot(h, wu_ref[...], preferred_element_type=F32)
        act = (gate * _sigmoid(gate) * up).astype(BF16)
        o_ref[rs, :] += jnp.dot(act, wd_ref[...], preferred_element_type=F32)

    @pl.when(j == pl.num_programs(1) - 1)
    def _():
        y = DN_ALPHA * x_ref[...] + 0.5 * g_ref[0] * o_ref[...]
        o_ref[...] = _layer_norm(y, lng_ref[...], lnb_ref[...])


def _ffn(x2d, mods3, k_shift, k_scale, k_gate, w_gu, w_down, ln_g, ln_b, seq, tm=1024, tf=512):
    m, d = x2d.shape
    n_ff = w_down.shape[0]
    nj = n_ff // tf
    tiles_per_seq = seq // tm

    def mod_spec(k):
        return pl.BlockSpec((1, 1, d), lambda i, j: ((i // tiles_per_seq) * N_MOD + k, 0, 0))

    return pl.pallas_call(
        _ffn_kernel,
        grid=(m // tm, nj),
        in_specs=[
            pl.BlockSpec((tm, d), lambda i, j: (i, 0)),
            mod_spec(k_shift), mod_spec(k_scale), mod_spec(k_gate),
            pl.BlockSpec((d, tf), lambda i, j: (0, j)),
            pl.BlockSpec((d, tf), lambda i, j: (0, j + nj)),
            pl.BlockSpec((tf, d), lambda i, j: (j, 0)),
            pl.BlockSpec((1, d), lambda i, j: (0, 0)),
            pl.BlockSpec((1, d), lambda i, j: (0, 0)),
        ],
        out_specs=pl.BlockSpec((tm, d), lambda i, j: (i, 0)),
        out_shape=jax.ShapeDtypeStruct((m, d), F32),
        scratch_shapes=[pltpu.VMEM((tm, d), BF16)],
        compiler_params=_cparams(("parallel", "arbitrary"), VMEM_LIMIT_BIG),
        name="ffn",
    )(x2d, mods3, mods3, mods3, w_gu, w_gu, w_down, ln_g, ln_b)


PROJ_TM = 1024


def _rope_kernel(pos_ref, freq_ref, *out_refs, dilations):
    cos_ref, sin_ref = out_refs[0], out_refs[1]
    tm = cos_ref.shape[0]
    pos_t = pos_ref[0].astype(F32).T
    lane = lax.broadcasted_iota(jnp.int32, (LANES, DIL_HEAD_DIM), 1)
    for k in range(tm // LANES):
        ang = pos_t[:, k:k + 1] * freq_ref[...]
        s = jnp.sin(ang)
        cos_ref[k * LANES:(k + 1) * LANES, :] = jnp.cos(ang)
        sin_ref[k * LANES:(k + 1) * LANES, :] = jnp.where(lane < DIL_HEAD_DIM // 2, -s, s)
    for gi, r in enumerate(dilations[1:]):
        tl = tm // r
        for c in range(r):
            out_refs[2 + 2 * gi][c * tl:(c + 1) * tl, :] = cos_ref[pl.ds(c, tl, stride=r), :]
            out_refs[3 + 2 * gi][c * tl:(c + 1) * tl, :] = sin_ref[pl.ds(c, tl, stride=r), :]


def _rope_tables(pos3, freq2, dilations, tm=PROJ_TM):
    assert dilations[0] == 1
    m = pos3.shape[0] * tm
    n_out = 2 * len(dilations)
    return pl.pallas_call(
        functools.partial(_rope_kernel, dilations=dilations),
        grid=(m // tm,),
        in_specs=[
            pl.BlockSpec((1, tm // LANES, LANES), lambda i: (i, 0, 0)),
            pl.BlockSpec((1, DIL_HEAD_DIM), lambda i: (0, 0)),
        ],
        out_specs=[pl.BlockSpec((tm, DIL_HEAD_DIM), lambda i: (i, 0))] * n_out,
        out_shape=[jax.ShapeDtypeStruct((m, DIL_HEAD_DIM), F32)] * n_out,
        compiler_params=_cparams(("parallel",)),
        name="rope_tables",
    )(pos3, freq2)


def _proj_kernel(x_ref, sh_ref, sc_ref, w_ref, wlr_ref, o_ref, lr_ref, h_scr):
    @pl.when(pl.program_id(1) == 0)
    def _():
        h = (x_ref[...] * (1.0 + sc_ref[0]) + sh_ref[0]).astype(BF16)
        h_scr[...] = h
        lr_ref[...] = jnp.dot(h, wlr_ref[...], preferred_element_type=F32)

    o_ref[...] = jnp.dot(h_scr[...], w_ref[...], preferred_element_type=F32).astype(BF16)


def _proj(x2d, mods3, k_shift, k_scale, w_all, seq, tm=PROJ_TM, tn=2048):
    m, d = x2d.shape
    n = PROJ_W
    tiles_per_seq = seq // tm

    def mod_spec(k):
        return pl.BlockSpec((1, 1, d), lambda i, j: ((i // tiles_per_seq) * N_MOD + k, 0, 0))

    return pl.pallas_call(
        _proj_kernel,
        grid=(m // tm, n // tn),
        in_specs=[
            pl.BlockSpec((tm, d), lambda i, j: (i, 0)),
            mod_spec(k_shift), mod_spec(k_scale),
            pl.BlockSpec((d, tn), lambda i, j: (0, j)),
            pl.BlockSpec((d, LR_PAD), lambda i, j: (0, OFF_LR // LR_PAD)),
        ],
        out_specs=[
            pl.BlockSpec((tm, tn), lambda i, j: (i, j)),
            pl.BlockSpec((tm, LR_PAD), lambda i, j: (i, 0)),
        ],
        out_shape=[
            jax.ShapeDtypeStruct((m, n), BF16),
            jax.ShapeDtypeStruct((m, LR_PAD), F32),
        ],
        scratch_shapes=[pltpu.VMEM((tm, d), BF16)],
        compiler_params=_cparams(("parallel", "arbitrary")),
        name="proj",
    )(x2d, mods3, mods3, w_all, w_all)


def _proj_dil_kernel(x_ref, sh_ref, sc_ref, w_ref, cos_ref, sin_ref, o_ref, h_scr, slab_scr, *, r):
    tm, d = x_ref.shape
    tl = tm // r
    if r == 1:
        h_scr[...] = (x_ref[...] * (1.0 + sc_ref[0]) + sh_ref[0]).astype(BF16)
    else:
        for k in range(d // LANES):
            sl = slice(k * LANES, (k + 1) * LANES)
            slab_scr[k] = x_ref[:, sl] * (1.0 + sc_ref[0, :, sl]) + sh_ref[0, :, sl]
        for c in range(r):
            for k in range(d // LANES):
                sl = slice(k * LANES, (k + 1) * LANES)
                h_scr[c * tl:(c + 1) * tl, sl] = slab_scr[k, pl.ds(c, tl, stride=r), :].astype(BF16)
    _proj_dil_qkv(h_scr, w_ref, cos_ref, sin_ref, o_ref, r)


def _proj_dil_qkv(h_scr, w_ref, cos_ref, sin_ref, o_ref, r):
    tl = h_scr.shape[0] // r
    for part, scale in enumerate((DIL_HEAD_DIM ** -0.5, 1.0, None)):
        cols = slice(part * DIL_OUT, (part + 1) * DIL_OUT)
        t = jnp.dot(h_scr[...], w_ref[:, cols], preferred_element_type=F32)
        if scale is None:
            for c in range(r):
                o_ref[0, c, :, cols] = t[c * tl:(c + 1) * tl, :].astype(BF16)
            continue
        cos = cos_ref[...] * scale
        sin = sin_ref[...] * scale
        for s in range(DIL_HEADS):
            ts = t[:, s * DIL_HEAD_DIM:(s + 1) * DIL_HEAD_DIM]
            rot = (ts * cos + pltpu.roll(ts, DIL_HEAD_DIM // 2, 1) * sin).astype(BF16)
            lo = part * DIL_OUT + s * DIL_HEAD_DIM
            for c in range(r):
                o_ref[0, c, :, lo:lo + DIL_HEAD_DIM] = rot[c * tl:(c + 1) * tl, :]


def _proj_dil_dma_kernel(x_hbm, sh_ref, sc_ref, w_ref, cos_ref, sin_ref, o_ref, h_scr, x_buf, sem,
                         *, r):
    i = pl.program_id(0)
    tl = h_scr.shape[0] // r
    slot = i % 2

    def tile_copies(tile, buf_slot):
        return [pltpu.make_async_copy(x_hbm.at[pl.ds(tile * tl, tl), c], x_buf.at[buf_slot, c],
                                      sem.at[buf_slot]) for c in range(r)]

    @pl.when(i == 0)
    def _():
        for cp in tile_copies(0, 0):
            cp.start()

    @pl.when(i + 1 < pl.num_programs(0))
    def _():
        for cp in tile_copies(i + 1, 1 - slot):
            cp.start()

    for cp in tile_copies(i, slot):
        cp.wait()
    for c in range(r):
        h_scr[c * tl:(c + 1) * tl, :] = (x_buf[slot, c] * (1.0 + sc_ref[0]) + sh_ref[0]).astype(BF16)
    _proj_dil_qkv(h_scr, w_ref, cos_ref, sin_ref, o_ref, r)


SUBLANES_F32 = 8


def _proj_dil(x2d, mods3, k_shift, k_scale, w_all, g, cos_t, sin_t, bsz, seq, r, tm=PROJ_TM):
    m, d = x2d.shape
    tiles_per_seq = seq // tm
    tl = tm // r
    dma_gather = r % SUBLANES_F32 == 0

    def mod_spec(k):
        return pl.BlockSpec((1, 1, d), lambda i: ((i // tiles_per_seq) * N_MOD + k, 0, 0))

    if dma_gather:
        body = functools.partial(_proj_dil_dma_kernel, r=r)
        x_arg = x2d.reshape(m // r, r, d)
        x_spec = pl.BlockSpec(memory_space=pl.ANY)
        scratch = [pltpu.VMEM((tm, d), BF16), pltpu.VMEM((2, r, tl, d), F32),
                   pltpu.SemaphoreType.DMA((2,))]
        semantics = ("arbitrary",)
    else:
        body = functools.partial(_proj_dil_kernel, r=r)
        x_arg = x2d
        x_spec = pl.BlockSpec((tm, d), lambda i: (i, 0))
        scratch = [pltpu.VMEM((tm, d), BF16),
                   pltpu.VMEM((d // LANES, tm, LANES) if r > 1 else (1, 8, LANES), F32)]
        semantics = ("parallel",)

    return pl.pallas_call(
        body,
        grid=(m // tm,),
        in_specs=[
            x_spec,
            mod_spec(k_shift), mod_spec(k_scale),
            pl.BlockSpec((d, DIL_QKV), lambda i: (0, OFF_DIL // DIL_QKV + g),
                         pipeline_mode=pl.Buffered(1)),
            pl.BlockSpec((tm, DIL_HEAD_DIM), lambda i: (i, 0)),
            pl.BlockSpec((tm, DIL_HEAD_DIM), lambda i: (i, 0)),
        ],
        out_specs=pl.BlockSpec((1, r, tl, DIL_QKV),
                               lambda i: (i // tiles_per_seq, 0, i % tiles_per_seq, 0)),
        out_shape=jax.ShapeDtypeStruct((bsz, r, seq // r, DIL_QKV), BF16),
        scratch_shapes=scratch,
        compiler_params=_cparams(semantics, VMEM_LIMIT_BIG),
        name=f"proj_dil_r{r}",
    )(x_arg, mods3, mods3, w_all, cos_t, sin_t)


def _gla_kernel(q_ref, k_ref, v_ref, r_ref, lr_ref, wa_ref, ba_ref, ng_ref, o_ref,
                st_scr, b_scr, qd_scr, qh_scr, ki_scr, ke_scr, kp_scr, kh_scr, a_scr, *, n_chunks):
    c_len = GLA_CHUNK
    tc = q_ref.shape[1]
    hb = 4 * c_len
    pw = 2 * c_len
    n_pairs = n_chunks // 2

    @pl.when(pl.program_id(2) == 0)
    def _():
        st_scr[...] = jnp.zeros_like(st_scr)

    ti = lax.broadcasted_iota(jnp.int32, (hb, hb), 0)
    tj = lax.broadcasted_iota(jnp.int32, (hb, hb), 1)
    tri = jnp.logical_and(ti >= tj, ti // c_len == tj // c_len).astype(BF16)
    for s in range(tc // hb):
        rows = slice(s * hb, (s + 1) * hb)
        logits = jnp.dot(lr_ref[0, rows, :].astype(BF16), wa_ref[...],
                         preferred_element_type=F32) + ba_ref[...]
        log_a = (jnp.minimum(logits, 0.0) - jnp.log(1.0 + jnp.exp(-jnp.abs(logits)))) / GLA_TAU
        p1 = log_a.astype(BF16)
        r1 = log_a - p1.astype(F32)
        p2 = r1.astype(BF16)
        p3 = (r1 - p2.astype(F32)).astype(BF16)
        b_scr[rows, :] = (jnp.dot(tri, p1, preferred_element_type=F32)
                          + jnp.dot(tri, p2, preferred_element_type=F32)
                          + jnp.dot(tri, p3, preferred_element_type=F32))

    nt = (((1,), (1,)), ((), ()))

    def chunk_rows(c):
        return slice(c * c_len, (c + 1) * c_len)

    def pair_rows(p):
        return slice(p * pw, (p + 1) * pw)

    b_last = [b_scr[(c + 1) * c_len - 1:(c + 1) * c_len, :] for c in range(n_chunks)]
    h_log = [jnp.zeros_like(b_last[0])]
    for c in range(n_chunks):
        h_log.append(h_log[-1] + b_last[c])

    for c in range(n_chunks):
        rows = chunk_rows(c)
        b = b_scr[rows, :]
        q = q_ref[0, rows, :].astype(F32)
        k = k_ref[0, rows, :].astype(F32)
        qd = q * jnp.exp(b) * (GLA_DK ** -0.5)
        ke = k * jnp.exp(b_last[c] - b)
        qd_scr[rows, :] = qd
        qh_scr[rows, :] = (qd * jnp.exp(h_log[c])).astype(BF16)
        ki_scr[rows, :] = (k * jnp.exp(-b)).astype(BF16)
        ke_scr[rows, :] = ke.astype(BF16)
        kp_scr[rows, :] = (ke * jnp.exp(b_last[c + 1]) if c % 2 == 0 else ke).astype(BF16)
        kh_scr[rows, :] = (ke * jnp.exp(h_log[n_chunks] - h_log[c + 1])).astype(BF16)

    pi = lax.broadcasted_iota(jnp.int32, (pw, pw), 0)
    pj = lax.broadcasted_iota(jnp.int32, (pw, pw), 1)
    diag = jnp.logical_and(pi >= pj, pi // c_len == pj // c_len)
    lower = jnp.logical_and(pi >= c_len, pj < c_len)
    for p in range(n_pairs):
        first = chunk_rows(2 * p)
        keys = jnp.concatenate([ki_scr[pair_rows(p), :], ke_scr[first, :], ke_scr[first, :]], axis=0)
        s = lax.dot_general(qd_scr[pair_rows(p), :].astype(BF16), keys, nt,
                            preferred_element_type=F32)
        s = jnp.where(diag, s[:, :pw], jnp.where(lower, s[:, pw:], 0.0))
        a_scr[pair_rows(p), pair_rows(p)] = s.astype(BF16)

    for p in range(n_pairs - 1):
        c0 = 2 * p + 2
        lhs = jnp.concatenate(
            [(qd_scr[chunk_rows(c), :] * jnp.exp(h_log[c] - h_log[c0])).astype(BF16)
             for c in range(c0, n_chunks)], axis=0)
        s = lax.dot_general(lhs, kp_scr[pair_rows(p), :], nt, preferred_element_type=F32)
        a_scr[c0 * c_len:, pair_rows(p)] = s.astype(BF16)

    half = 2 * pw
    for hf in range(tc // half):
        a_scr[hf * half:hf * half + pw, hf * half + pw:(hf + 1) * half] = jnp.zeros((pw, pw), BF16)

    st = st_scr[...]
    st_b = st.astype(BF16)
    for hf in range(tc // half):
        rows = slice(hf * half, (hf + 1) * half)
        keys = (hf + 1) * half
        o = jnp.dot(a_scr[rows, :keys], v_ref[0, :keys, :], preferred_element_type=F32)
        o += jnp.dot(qh_scr[rows, :], st_b, preferred_element_type=F32)
        o = o * lax.rsqrt(jnp.mean(o * o, axis=-1, keepdims=True) + LN_EPS) * ng_ref[0]
        r = r_ref[0, rows, :].astype(F32)
        o_ref[0, rows, :] = (o * (r * _sigmoid(r))).astype(BF16)

    d_col = jnp.broadcast_to(jnp.exp(h_log[n_chunks]), (8, GLA_DK)).T[:, :1]
    st_scr[...] = st * d_col + lax.dot_general(
        kh_scr[...], v_ref[0], (((0,), (0,)), ((), ())), preferred_element_type=F32)


def _gla(p3, lr3, w_a2, b_a, norm_g3, tc=512):
    bsz, seq, _ = p3.shape
    nq, nv, nr = OFF_GK // GLA_DK, OFF_GV // GLA_DV, OFF_GR // GLA_DV
    return pl.pallas_call(
        functools.partial(_gla_kernel, n_chunks=tc // GLA_CHUNK),
        grid=(bsz, GLA_HEADS, seq // tc),
        in_specs=[
            pl.BlockSpec((1, tc, GLA_DK), lambda b, h, t: (b, t, h)),
            pl.BlockSpec((1, tc, GLA_DK), lambda b, h, t: (b, t, nq + h)),
            pl.BlockSpec((1, tc, GLA_DV), lambda b, h, t: (b, t, nv + h)),
            pl.BlockSpec((1, tc, GLA_DV), lambda b, h, t: (b, t, nr + h)),
            pl.BlockSpec((1, tc, LR_PAD), lambda b, h, t: (b, t, 0)),
            pl.BlockSpec((LR_PAD, GLA_DK), lambda b, h, t: (0, h)),
            pl.BlockSpec((1, GLA_DK), lambda b, h, t: (0, h)),
            pl.BlockSpec((1, 1, GLA_DV), lambda b, h, t: (h, 0, 0)),
        ],
        out_specs=pl.BlockSpec((1, tc, GLA_DV), lambda b, h, t: (b, t, h)),
        out_shape=jax.ShapeDtypeStruct((bsz, seq, GLA_VW), BF16),
        scratch_shapes=[pltpu.VMEM((GLA_DK, GLA_DV), F32), pltpu.VMEM((tc, GLA_DK), F32),
                        pltpu.VMEM((tc, GLA_DK), F32)]
                       + [pltpu.VMEM((tc, GLA_DK), BF16)] * 5 + [pltpu.VMEM((tc, tc), BF16)],
        compiler_params=_cparams(("parallel", "parallel", "arbitrary")),
        name="gla",
    )(p3, p3, p3, p3, lr3, w_a2, b_a, norm_g3)


def _dil_kernel(q_ref, k_ref, v_ref, kp_ref, vp_ref, o_ref, lse_ref, s_scr, m_scr, p_scr, *, nblk):
    n = pl.program_id(2)
    blk = DIL_BLOCK
    qi = lax.broadcasted_iota(jnp.int32, (blk, 2 * blk), 0)
    kj = lax.broadcasted_iota(jnp.int32, (blk, 2 * blk), 1)
    band = jnp.logical_or(jnp.logical_and(kj < blk, kj >= qi),
                          jnp.logical_and(kj >= blk, kj - blk <= qi))
    band_first = jnp.logical_and(band, jnp.logical_or(kj >= blk, n > 0))
    nt = (((1,), (1,)), ((), ()))

    def rows(i):
        return slice(i * blk, (i + 1) * blk)

    def cols(h):
        return slice(h * DIL_HEAD_DIM, (h + 1) * DIL_HEAD_DIM)

    for i in range(nblk):
        for h in range(DIL_HEADS):
            q = q_ref[0, 0, rows(i), cols(h)]
            k_prev = kp_ref[0, 0, :, cols(h)] if i == 0 else k_ref[0, 0, rows(i - 1), cols(h)]
            s_p = lax.dot_general(q, k_prev, nt, preferred_element_type=F32)
            s_c = lax.dot_general(q, k_ref[0, 0, rows(i), cols(h)], nt, preferred_element_type=F32)
            s = jnp.concatenate([s_p, s_c], axis=1)
            s_scr[rows(i * DIL_HEADS + h), :] = jnp.where(band_first if i == 0 else band, s, NEG_BIG)

    s_all = s_scr[...]
    m_all = jnp.max(s_all, axis=-1, keepdims=True)
    p_scr[...] = jnp.exp(s_all - m_all).astype(BF16)
    m_scr[...] = jnp.broadcast_to(m_all, m_scr.shape)

    lane = lax.broadcasted_iota(jnp.int32, (blk, LANES), 1)
    ones = jnp.ones((2 * blk, DIL_HEAD_DIM), BF16)
    for i in range(nblk):
        lse_tile = jnp.zeros((blk, LANES), F32)
        for h in range(DIL_HEADS):
            u = i * DIL_HEADS + h
            v_prev = vp_ref[0, 0, :, cols(h)] if i == 0 else v_ref[0, 0, rows(i - 1), cols(h)]
            v_aug = jnp.concatenate(
                [jnp.concatenate([v_prev, v_ref[0, 0, rows(i), cols(h)]], axis=0), ones], axis=1)
            acc = jnp.dot(p_scr[rows(u), :], v_aug, preferred_element_type=F32)
            den = acc[:, DIL_HEAD_DIM:]
            o_ref[0, 0, rows(i), cols(h)] = (acc[:, :DIL_HEAD_DIM] / den).astype(BF16)
            lse_tile = jnp.where(lane == h, m_scr[rows(u), :] + jnp.log(den), lse_tile)
        lse_ref[0, 0, rows(i), :] = lse_tile


def _dil_group(qkv, nblk=4):
    bsz, r, length, _ = qkv.shape
    nb = length // DIL_BLOCK
    nblk = min(nblk, nb)
    assert nb % nblk == 0
    tq = nblk * DIL_BLOCK
    units = nblk * DIL_HEADS * DIL_BLOCK

    def cur_spec(u):
        return pl.BlockSpec((1, 1, tq, DIL_OUT), lambda b, c, n: (b, c, n, u))

    def prev_spec(u):
        return pl.BlockSpec((1, 1, DIL_BLOCK, DIL_OUT),
                            lambda b, c, n: (b, c, jnp.maximum(n * nblk - 1, 0), u))

    return pl.pallas_call(
        functools.partial(_dil_kernel, nblk=nblk),
        grid=(bsz, r, nb // nblk),
        in_specs=[cur_spec(0), cur_spec(1), cur_spec(2), prev_spec(1), prev_spec(2)],
        out_specs=[
            pl.BlockSpec((1, 1, tq, DIL_OUT), lambda b, c, n: (b, c, n, 0)),
            pl.BlockSpec((1, 1, tq, LANES), lambda b, c, n: (b, c, n, 0)),
        ],
        out_shape=[
            jax.ShapeDtypeStruct((bsz, r, length, DIL_OUT), BF16),
            jax.ShapeDtypeStruct((bsz, r, length, LANES), F32),
        ],
        scratch_shapes=[pltpu.VMEM((units, 2 * DIL_BLOCK), F32),
                        pltpu.VMEM((units, LANES), F32),
                        pltpu.VMEM((units, 2 * DIL_BLOCK), BF16)],
        compiler_params=_cparams(("parallel", "parallel", "parallel")),
        name=f"dilattn_r{r}",
    )(qkv, qkv, qkv, qkv, qkv)


def _merge_kernel(oa_ref, o1_ref, o2_ref, o3_ref, l1_ref, l2_ref, l3_ref, ga_ref, gb_ref, x_ref,
                  g2_ref, wa_ref, wb_ref, wo_ref, lng_ref, lnb_ref, out_ref, l_scr, w_scr, ob_scr,
                  *, dilations):
    y_a = jnp.dot(oa_ref[...], wa_ref[...], preferred_element_type=F32)
    o_refs = (o1_ref, o2_ref, o3_ref)
    l_refs = (l1_ref, l2_ref, l3_ref)
    tm = oa_ref.shape[0]

    def tok_rows(c, r):
        return pl.ds(c, tm // r, stride=r) if r > 1 else slice(None)

    for g, r in enumerate(dilations):
        for c in range(r):
            l_scr[g, tok_rows(c, r), :] = l_refs[g][0, c]
    l1, l2, l3 = l_scr[0], l_scr[1], l_scr[2]
    mx = jnp.maximum(jnp.maximum(l1, l2), l3)
    e1, e2, e3 = jnp.exp(l1 - mx), jnp.exp(l2 - mx), jnp.exp(l3 - mx)
    inv = 1.0 / (e1 + e2 + e3)
    w_scr[0], w_scr[1], w_scr[2] = e1 * inv, e2 * inv, e3 * inv
    for g, r in enumerate(dilations):
        for c in range(r):
            rows = tok_rows(c, r)
            wr = w_scr[g, rows, :]
            for h in range(DIL_HEADS):
                sl = slice(h * DIL_HEAD_DIM, (h + 1) * DIL_HEAD_DIM)
                part = wr[:, h:h + 1] * o_refs[g][0, c, :, sl].astype(F32)
                if g == 0:
                    ob_scr[h, rows, :] = part
                else:
                    ob_scr[h, rows, :] += part
    o_b = jnp.concatenate([ob_scr[h].astype(BF16) for h in range(DIL_HEADS)], axis=1)
    y_b = jnp.dot(o_b, wb_ref[...], preferred_element_type=F32)
    merged = _sigmoid(ga_ref[...].astype(F32)) * y_a + _sigmoid(gb_ref[...].astype(F32)) * y_b
    mix = jnp.dot(merged.astype(BF16), wo_ref[...], preferred_element_type=F32)
    y = DN_ALPHA * x_ref[...] + g2_ref[0] * mix
    out_ref[...] = _layer_norm(y, lng_ref[...], lnb_ref[...])


def _merge(oa2d, o_groups, lse_groups, p2d, x2d, mods3, k_gate, w_a, w_b, w_o, ln_g, ln_b, seq,
           dilations, tm=256):
    m, d = x2d.shape
    tiles_per_seq = seq // tm
    resident = pl.Buffered(1)
    n_g = len(dilations)

    def row_spec(width, col=0):
        return pl.BlockSpec((tm, width), lambda i: (i, col))

    def class_spec(r, width):
        return pl.BlockSpec((1, r, tm // r, width),
                            lambda i: (i // tiles_per_seq, 0, i % tiles_per_seq, 0))

    def const_spec(shape):
        return pl.BlockSpec(shape, lambda i: (0, 0), pipeline_mode=resident)

    return pl.pallas_call(
        functools.partial(_merge_kernel, dilations=dilations),
        grid=(m // tm,),
        in_specs=[
            row_spec(GLA_VW),
            *[class_spec(r, DIL_OUT) for r in dilations],
            *[class_spec(r, LANES) for r in dilations],
            row_spec(D_MODEL, OFF_GA // D_MODEL), row_spec(D_MODEL, OFF_GB // D_MODEL),
            row_spec(d),
            pl.BlockSpec((1, 1, d), lambda i: ((i // tiles_per_seq) * N_MOD + k_gate, 0, 0)),
            const_spec(w_a.shape), const_spec(w_b.shape), const_spec(w_o.shape),
            const_spec((1, d)), const_spec((1, d)),
        ],
        out_specs=row_spec(d),
        out_shape=jax.ShapeDtypeStruct((m, d), F32),
        scratch_shapes=[pltpu.VMEM((n_g, tm, LANES), F32), pltpu.VMEM((n_g, tm, LANES), F32),
                        pltpu.VMEM((DIL_HEADS, tm, DIL_HEAD_DIM), F32)],
        compiler_params=_cparams(("parallel",)),
        name="merge",
    )(oa2d, *o_groups, *lse_groups, p2d, p2d, x2d, mods3, w_a, w_b, w_o, ln_g, ln_b)


def _pack_w_in(w_in):
    o = 0
    segs = []
    for wdt in (GLA_QK, GLA_QK, GLA_VW, GLA_VW, GLA_GATE_RANK, DIL_W, DIL_W, DIL_W, D_MODEL, D_MODEL):
        segs.append(w_in[:, o:o + wdt].astype(BF16))
        o += wdt
    gq, gk, gv, gr, glr, dq, dk, dv, ga, gb = segs
    cols = [gq, gk, gv, gr, ga, gb, jnp.zeros((w_in.shape[0], OFF_DIL - PROJ_W), BF16)]
    for g in range(DIL_GROUPS):
        sl = slice(g * DIL_OUT, (g + 1) * DIL_OUT)
        cols += [dq[:, sl], dk[:, sl], dv[:, sl]]
    cols.append(jnp.pad(glr, ((0, 0), (0, LR_PAD - GLA_GATE_RANK))))
    return jnp.concatenate(cols, axis=1)


def kernel(x, c, positions, w_ada, b_ada, ln1_g, ln1_b, w_ffn1_gu, w_ffn1_down, w_in, w_alpha2,
           b_alpha, gla_norm_g, w_branch_a, w_branch_b, w_out, ln2_g, ln2_b, w_ffn2_gu, w_ffn2_down,
           ln3_g, ln3_b):
    bsz, seq, d = x.shape
    m = bsz * seq
    x2d = x.reshape(m, d)
    c_pad = jnp.pad(c, ((0, 8 - bsz % 8 if bsz % 8 else 0), (0, 0)))
    dilations = tuple(r for _, r in DIL_PATTERNS)

    half = DIL_HEAD_DIM // 2
    freq = ROPE_THETA ** (-jnp.arange(half, dtype=F32) / half)
    freq2 = jnp.concatenate([freq, freq]).reshape(1, DIL_HEAD_DIM)
    rope_t = _rope_tables(positions.reshape(m // PROJ_TM, PROJ_TM // LANES, LANES), freq2, dilations)

    for l in range(DEPTH):
        mods = _mods(c_pad, w_ada[l], b_ada[l].reshape(1, -1))[:bsz]
        mods3 = mods.reshape(bsz * N_MOD, 1, d)

        x1 = _ffn(x2d, mods3, 0, 1, 2, w_ffn1_gu[l].astype(BF16), w_ffn1_down[l].astype(BF16),
                  ln1_g[l].reshape(1, d), ln1_b[l].reshape(1, d), seq)

        w_all = _pack_w_in(w_in[l])
        p2d, lr2d = _proj(x1, mods3, 3, 4, w_all, seq)
        p3 = p2d.reshape(bsz, seq, PROJ_W)

        w_a2 = jnp.pad(w_alpha2[l], ((0, LR_PAD - GLA_GATE_RANK), (0, 0))).astype(BF16)
        o_a = _gla(p3, lr2d.reshape(bsz, seq, LR_PAD), w_a2, b_alpha[l].reshape(1, GLA_QK),
                   gla_norm_g[l].reshape(GLA_HEADS, 1, GLA_DV))

        o_groups, lse_groups = [], []
        for g, r in enumerate(dilations):
            qkv = _proj_dil(x1, mods3, 3, 4, w_all, g, rope_t[2 * g], rope_t[2 * g + 1],
                            bsz, seq, r)
            o_g, lse_g = _dil_group(qkv)
            o_groups.append(o_g)
            lse_groups.append(lse_g)

        x2 = _merge(o_a.reshape(m, GLA_VW), o_groups, lse_groups, p2d, x1, mods3, 5,
                    w_branch_a[l].astype(BF16), w_branch_b[l].astype(BF16), w_out[l].astype(BF16),
                    ln2_g[l].reshape(1, d), ln2_b[l].reshape(1, d), seq, dilations)

        x2d = _ffn(x2, mods3, 6, 7, 8, w_ffn2_gu[l].astype(BF16), w_ffn2_down[l].astype(BF16),
                   ln3_g[l].reshape(1, d), ln3_b[l].reshape(1, d), seq)
    return x2d.reshape(bsz, seq, d)
```

```python
import functools

import jax
import jax.numpy as jnp
from jax import lax
from jax.experimental import pallas as pl
from jax.experimental.pallas import tpu as pltpu

F32 = jnp.float32
BF16 = jnp.bfloat16

D_MODEL = 2048
DEPTH = 1
D_FF = 5632
N_MOD = 9
LN_EPS = 1e-5
DN_ALPHA = (2.0 * DEPTH) ** 0.25

GLA_HEADS = 4
GLA_DK = 256
GLA_DV = 512
GLA_GATE_RANK = 16
GLA_TAU = 16.0
GLA_CHUNK = 64
GLA_QK = GLA_HEADS * GLA_DK
GLA_VW = GLA_HEADS * GLA_DV

DIL_PATTERNS = ((128, 1), (512, 4), (2048, 16))
DIL_GROUPS = len(DIL_PATTERNS)
DIL_HEADS = 8
DIL_HEAD_DIM = 128
DIL_BLOCK = 128
DIL_W = DIL_GROUPS * DIL_HEADS * DIL_HEAD_DIM
DIL_OUT = DIL_HEADS * DIL_HEAD_DIM
ROPE_THETA = 10000.0

LANES = 128
LR_PAD = LANES

PROJ_W = 2 * GLA_QK + 2 * GLA_VW + 2 * D_MODEL
OFF_GQ = 0
OFF_GK = GLA_QK
OFF_GV = 2 * GLA_QK
OFF_GR = OFF_GV + GLA_VW
OFF_GA = OFF_GR + GLA_VW
OFF_GB = OFF_GA + D_MODEL
DIL_QKV = 3 * DIL_OUT
OFF_DIL = -(-PROJ_W // DIL_QKV) * DIL_QKV
OFF_LR = OFF_DIL + DIL_GROUPS * DIL_QKV

NEG_BIG = -1e30

VMEM_LIMIT = 56 * 1024 * 1024
VMEM_LIMIT_BIG = 60 * 1024 * 1024


def _cparams(sem, vmem=VMEM_LIMIT):
    return pltpu.CompilerParams(dimension_semantics=sem, vmem_limit_bytes=vmem)


def _sigmoid(x):
    return 1.0 / (1.0 + jnp.exp(-x))


def _layer_norm(y, g, b):
    mu = jnp.mean(y, axis=-1, keepdims=True)
    d = y - mu
    var = jnp.mean(d * d, axis=-1, keepdims=True)
    return d * lax.rsqrt(var + LN_EPS) * g + b


def _mods_kernel(c_ref, w_ref, b_ref, o_ref):
    c = c_ref[...]
    c_act = (c * _sigmoid(c)).astype(BF16)
    o_ref[...] = jnp.dot(c_act, w_ref[...].astype(BF16), preferred_element_type=F32) + b_ref[...]


def _mods(c_pad, w_ada, b_ada, tn=1024):
    rows, d = c_pad.shape
    n = w_ada.shape[1]
    return pl.pallas_call(
        _mods_kernel,
        grid=(n // tn,),
        in_specs=[
            pl.BlockSpec((rows, d), lambda j: (0, 0)),
            pl.BlockSpec((d, tn), lambda j: (0, j)),
            pl.BlockSpec((1, tn), lambda j: (0, j)),
        ],
        out_specs=pl.BlockSpec((rows, tn), lambda j: (0, j)),
        out_shape=jax.ShapeDtypeStruct((rows, n), F32),
        compiler_params=_cparams(("arbitrary",)),
        name="mods",
    )(c_pad, w_ada, b_ada)


FFN_TF = 512
FFN_WEIGHT_SLOTS = 3


def _ffn_kernel(x_ref, sh_ref, sc_ref, g_ref, wgu_hbm, wd_hbm, lng_ref, lnb_ref, o_ref,
                wg_buf, wu_buf, wd_buf, sem, *, nj):
    i = pl.program_id(0)
    last_tile = pl.num_programs(0) - 1
    tf = wg_buf.shape[2]

    def chunk_copies(j):
        slot = j % FFN_WEIGHT_SLOTS
        return (
            pltpu.make_async_copy(wgu_hbm.at[:, pl.ds(j * tf, tf)], wg_buf.at[slot], sem.at[0, slot]),
            pltpu.make_async_copy(wgu_hbm.at[:, pl.ds((nj + j) * tf, tf)], wu_buf.at[slot],
                                  sem.at[1, slot]),
            pltpu.make_async_copy(wd_hbm.at[pl.ds(j * tf, tf), :], wd_buf.at[slot], sem.at[2, slot]),
        )

    @pl.when(i == 0)
    def _():
        for cp in chunk_copies(0):
            cp.start()

    h = (x_ref[...] * (1.0 + sc_ref[0]) + sh_ref[0]).astype(BF16)
    for j in range(nj):
        slot = j % FFN_WEIGHT_SLOTS
        for cp in chunk_copies((j + 1) % nj):
            cp.start()
        for cp in chunk_copies(j):
            cp.wait()
        gate = jnp.dot(h, wg_buf[slot], preferred_element_type=F32)
        up = jnp.dot(h, wu_buf[slot], preferred_element_type=F32)
        act = (gate * _sigmoid(gate) * up).astype(BF16)
        down = jnp.dot(act, wd_buf[slot], preferred_element_type=F32)
        if j == 0:
            o_ref[...] = down
        else:
            o_ref[...] += down

    y = DN_ALPHA * x_ref[...] + 0.5 * g_ref[0] * o_ref[...]
    o_ref[...] = _layer_norm(y, lng_ref[...], lnb_ref[...])

    @pl.when(i == last_tile)
    def _():
        for cp in chunk_copies(0):
            cp.wait()


def _ffn(x2d, mods3, k_shift, k_scale, k_gate, w_gu, w_down, ln_g, ln_b, seq, tm=512, tf=FFN_TF):
    m, d = x2d.shape
    n_ff = w_down.shape[0]
    nj = n_ff // tf
    tiles_per_seq = seq // tm
    assert nj % FFN_WEIGHT_SLOTS != 1 or nj == 1

    def mod_spec(k):
        return pl.BlockSpec((1, 1, d), lambda i: ((i // tiles_per_seq) * N_MOD + k, 0, 0))

    return pl.pallas_call(
        functools.partial(_ffn_kernel, nj=nj),
        grid=(m // tm,),
        in_specs=[
            pl.BlockSpec((tm, d), lambda i: (i, 0)),
            mod_spec(k_shift), mod_spec(k_scale), mod_spec(k_gate),
            pl.BlockSpec(memory_space=pl.ANY),
            pl.BlockSpec(memory_space=pl.ANY),
            pl.BlockSpec((1, d), lambda i: (0, 0)),
            pl.BlockSpec((1, d), lambda i: (0, 0)),
        ],
        out_specs=pl.BlockSpec((tm, d), lambda i: (i, 0)),
        out_shape=jax.ShapeDtypeStruct((m, d), F32),
        scratch_shapes=[pltpu.VMEM((FFN_WEIGHT_SLOTS, d, tf), BF16),
                        pltpu.VMEM((FFN_WEIGHT_SLOTS, d, tf), BF16),
                        pltpu.VMEM((FFN_WEIGHT_SLOTS, tf, d), BF16),
                        pltpu.SemaphoreType.DMA((3, FFN_WEIGHT_SLOTS))],
        compiler_params=_cparams(("arbitrary",)),
        name="ffn",
    )(x2d, mods3, mods3, mods3, w_gu, w_down, ln_g, ln_b)


PROJ_TM = 1024


def _rope_kernel(pos_ref, freq_ref, *out_refs, dilations):
    cos_ref, sin_ref = out_refs[0], out_refs[1]
    tm = cos_ref.shape[0]
    pos_t = pos_ref[0].astype(F32).T
    lane = lax.broadcasted_iota(jnp.int32, (LANES, DIL_HEAD_DIM), 1)
    for k in range(tm // LANES):
        ang = pos_t[:, k:k + 1] * freq_ref[...]
        s = jnp.sin(ang)
        cos_ref[k * LANES:(k + 1) * LANES, :] = jnp.cos(ang)
        sin_ref[k * LANES:(k + 1) * LANES, :] = jnp.where(lane < DIL_HEAD_DIM // 2, -s, s)
    for gi, r in enumerate(dilations[1:]):
        tl = tm // r
        for c in range(r):
            out_refs[2 + 2 * gi][c * tl:(c + 1) * tl, :] = cos_ref[pl.ds(c, tl, stride=r), :]
            out_refs[3 + 2 * gi][c * tl:(c + 1) * tl, :] = sin_ref[pl.ds(c, tl, stride=r), :]


def _rope_tables(pos3, freq2, dilations, tm=PROJ_TM):
    assert dilations[0] == 1
    m = pos3.shape[0] * tm
    n_out = 2 * len(dilations)
    return pl.pallas_call(
        functools.partial(_rope_kernel, dilations=dilations),
        grid=(m // tm,),
        in_specs=[
            pl.BlockSpec((1, tm // LANES, LANES), lambda i: (i, 0, 0)),
            pl.BlockSpec((1, DIL_HEAD_DIM), lambda i: (0, 0)),
        ],
        out_specs=[pl.BlockSpec((tm, DIL_HEAD_DIM), lambda i: (i, 0))] * n_out,
        out_shape=[jax.ShapeDtypeStruct((m, DIL_HEAD_DIM), F32)] * n_out,
        compiler_params=_cparams(("parallel",)),
        name="rope_tables",
    )(pos3, freq2)


def _proj_kernel(x_ref, sh_ref, sc_ref, w_ref, wlr_ref, o_ref, lr_ref, h_scr):
    @pl.when(pl.program_id(1) == 0)
    def _():
        h = (x_ref[...] * (1.0 + sc_ref[0]) + sh_ref[0]).astype(BF16)
        h_scr[...] = h
        lr_ref[...] = jnp.dot(h, wlr_ref[...], preferred_element_type=F32)

    o_ref[...] = jnp.dot(h_scr[...], w_ref[...], preferred_element_type=F32).astype(BF16)


def _proj(x2d, mods3, k_shift, k_scale, w_all, seq, tm=PROJ_TM, tn=2048):
    m, d = x2d.shape
    n = PROJ_W
    tiles_per_seq = seq // tm

    def mod_spec(k):
        return pl.BlockSpec((1, 1, d), lambda i, j: ((i // tiles_per_seq) * N_MOD + k, 0, 0))

    return pl.pallas_call(
        _proj_kernel,
        grid=(m // tm, n // tn),
        in_specs=[
            pl.BlockSpec((tm, d), lambda i, j: (i, 0)),
            mod_spec(k_shift), mod_spec(k_scale),
            pl.BlockSpec((d, tn), lambda i, j: (0, j)),
            pl.BlockSpec((d, LR_PAD), lambda i, j: (0, OFF_LR // LR_PAD)),
        ],
        out_specs=[
            pl.BlockSpec((tm, tn), lambda i, j: (i, j)),
            pl.BlockSpec((tm, LR_PAD), lambda i, j: (i, 0)),
        ],
        out_shape=[
            jax.ShapeDtypeStruct((m, n), BF16),
            jax.ShapeDtypeStruct((m, LR_PAD), F32),
        ],
        scratch_shapes=[pltpu.VMEM((tm, d), BF16)],
        compiler_params=_cparams(("parallel", "arbitrary")),
        name="proj",
    )(x2d, mods3, mods3, w_all, w_all)


def _proj_dil_kernel(x_ref, sh_ref, sc_ref, w_ref, cos_ref, sin_ref, o_ref, h_scr, slab_scr, *, r):
    tm, d = x_ref.shape
    tl = tm // r
    if r == 1:
        h_scr[...] = (x_ref[...] * (1.0 + sc_ref[0]) + sh_ref[0]).astype(BF16)
    else:
        for k in range(d // LANES):
            sl = slice(k * LANES, (k + 1) * LANES)
            slab_scr[k] = x_ref[:, sl] * (1.0 + sc_ref[0, :, sl]) + sh_ref[0, :, sl]
        for c in range(r):
            for k in range(d // LANES):
                sl = slice(k * LANES, (k + 1) * LANES)
                h_scr[c * tl:(c + 1) * tl, sl] = slab_scr[k, pl.ds(c, tl, stride=r), :].astype(BF16)
    _proj_dil_qkv(h_scr, w_ref, cos_ref, sin_ref, o_ref, r)


def _proj_dil_qkv(h_scr, w_ref, cos_ref, sin_ref, o_ref, r):
    tl = h_scr.shape[0] // r
    for part, scale in enumerate((DIL_HEAD_DIM ** -0.5, 1.0, None)):
        cols = slice(part * DIL_OUT, (part + 1) * DIL_OUT)
        t = jnp.dot(h_scr[...], w_ref[:, cols], preferred_element_type=F32)
        if scale is None:
            for c in range(r):
                o_ref[0, c, :, cols] = t[c * tl:(c + 1) * tl, :].astype(BF16)
            continue
        cos = cos_ref[...] * scale
        sin = sin_ref[...] * scale
        for s in range(DIL_HEADS):
            ts = t[:, s * DIL_HEAD_DIM:(s + 1) * DIL_HEAD_DIM]
            rot = (ts * cos + pltpu.roll(ts, DIL_HEAD_DIM // 2, 1) * sin).astype(BF16)
            lo = part * DIL_OUT + s * DIL_HEAD_DIM
            for c in range(r):
                o_ref[0, c, :, lo:lo + DIL_HEAD_DIM] = rot[c * tl:(c + 1) * tl, :]


def _proj_dil_dma_kernel(x_hbm, sh_ref, sc_ref, w_ref, cos_ref, sin_ref, o_ref, h_scr, x_buf, sem,
                         *, r):
    i = pl.program_id(0)
    tl = h_scr.shape[0] // r
    slot = i % 2

    def tile_copies(tile, buf_slot):
        return [pltpu.make_async_copy(x_hbm.at[pl.ds(tile * tl, tl), c], x_buf.at[buf_slot, c],
                                      sem.at[buf_slot]) for c in range(r)]

    @pl.when(i == 0)
    def _():
        for cp in tile_copies(0, 0):
            cp.start()

    @pl.when(i + 1 < pl.num_programs(0))
    def _():
        for cp in tile_copies(i + 1, 1 - slot):
            cp.start()

    for cp in tile_copies(i, slot):
        cp.wait()
    for c in range(r):
        h_scr[c * tl:(c + 1) * tl, :] = (x_buf[slot, c] * (1.0 + sc_ref[0]) + sh_ref[0]).astype(BF16)
    _proj_dil_qkv(h_scr, w_ref, cos_ref, sin_ref, o_ref, r)


SUBLANES_F32 = 8


def _proj_dil(x2d, mods3, k_shift, k_scale, w_all, g, cos_t, sin_t, bsz, seq, r, tm=PROJ_TM):
    m, d = x2d.shape
    tiles_per_seq = seq // tm
    tl = tm // r
    dma_gather = r % SUBLANES_F32 == 0

    def mod_spec(k):
        return pl.BlockSpec((1, 1, d), lambda i: ((i // tiles_per_seq) * N_MOD + k, 0, 0))

    if dma_gather:
        body = functools.partial(_proj_dil_dma_kernel, r=r)
        x_arg = x2d.reshape(m // r, r, d)
        x_spec = pl.BlockSpec(memory_space=pl.ANY)
        scratch = [pltpu.VMEM((tm, d), BF16), pltpu.VMEM((2, r, tl, d), F32),
                   pltpu.SemaphoreType.DMA((2,))]
        semantics = ("arbitrary",)
    else:
        body = functools.partial(_proj_dil_kernel, r=r)
        x_arg = x2d
        x_spec = pl.BlockSpec((tm, d), lambda i: (i, 0))
        scratch = [pltpu.VMEM((tm, d), BF16),
                   pltpu.VMEM((d // LANES, tm, LANES) if r > 1 else (1, 8, LANES), F32)]
        semantics = ("parallel",)

    return pl.pallas_call(
        body,
        grid=(m // tm,),
        in_specs=[
            x_spec,
            mod_spec(k_shift), mod_spec(k_scale),
            pl.BlockSpec((d, DIL_QKV), lambda i: (0, OFF_DIL // DIL_QKV + g),
                         pipeline_mode=pl.Buffered(1)),
            pl.BlockSpec((tm, DIL_HEAD_DIM), lambda i: (i, 0)),
            pl.BlockSpec((tm, DIL_HEAD_DIM), lambda i: (i, 0)),
        ],
        out_specs=pl.BlockSpec((1, r, tl, DIL_QKV),
                               lambda i: (i // tiles_per_seq, 0, i % tiles_per_seq, 0)),
        out_shape=jax.ShapeDtypeStruct((bsz, r, seq // r, DIL_QKV), BF16),
        scratch_shapes=scratch,
        compiler_params=_cparams(semantics, VMEM_LIMIT_BIG),
        name=f"proj_dil_r{r}",
    )(x_arg, mods3, mods3, w_all, cos_t, sin_t)


def _gla_kernel(q_ref, k_ref, v_ref, r_ref, lr_ref, wa_ref, ba_ref, ng_ref, o_ref,
                st_scr, b_scr, qd_scr, qh_scr, ki_scr, ke_scr, kp_scr, kh_scr, a_scr, *, n_chunks):
    c_len = GLA_CHUNK
    tc = q_ref.shape[1]
    hb = 4 * c_len
    pw = 2 * c_len
    n_pairs = n_chunks // 2

    @pl.when(pl.program_id(2) == 0)
    def _():
        st_scr[...] = jnp.zeros_like(st_scr)

    ti = lax.broadcasted_iota(jnp.int32, (hb, hb), 0)
    tj = lax.broadcasted_iota(jnp.int32, (hb, hb), 1)
    tri = jnp.logical_and(ti >= tj, ti // c_len == tj // c_len).astype(BF16)
    for s in range(tc // hb):
        rows = slice(s * hb, (s + 1) * hb)
        logits = jnp.dot(lr_ref[0, rows, :].astype(BF16), wa_ref[...],
                         preferred_element_type=F32) + ba_ref[...]
        log_a = (jnp.minimum(logits, 0.0) - jnp.log(1.0 + jnp.exp(-jnp.abs(logits)))) / GLA_TAU
        p1 = log_a.astype(BF16)
        r1 = log_a - p1.astype(F32)
        p2 = r1.astype(BF16)
        p3 = (r1 - p2.astype(F32)).astype(BF16)
        b_scr[rows, :] = (jnp.dot(tri, p1, preferred_element_type=F32)
                          + jnp.dot(tri, p2, preferred_element_type=F32)
                          + jnp.dot(tri, p3, preferred_element_type=F32))

    nt = (((1,), (1,)), ((), ()))

    def chunk_rows(c):
        return slice(c * c_len, (c + 1) * c_len)

    def pair_rows(p):
        return slice(p * pw, (p + 1) * pw)

    b_last = [b_scr[(c + 1) * c_len - 1:(c + 1) * c_len, :] for c in range(n_chunks)]
    h_log = [jnp.zeros_like(b_last[0])]
    for c in range(n_chunks):
        h_log.append(h_log[-1] + b_last[c])

    for c in range(n_chunks):
        rows = chunk_rows(c)
        b = b_scr[rows, :]
        q = q_ref[0, rows, :].astype(F32)
        k = k_ref[0, rows, :].astype(F32)
        qd = q * jnp.exp(b) * (GLA_DK ** -0.5)
        ke = k * jnp.exp(b_last[c] - b)
        qd_scr[rows, :] = qd
        qh_scr[rows, :] = (qd * jnp.exp(h_log[c])).astype(BF16)
        ki_scr[rows, :] = (k * jnp.exp(-b)).astype(BF16)
        ke_scr[rows, :] = ke.astype(BF16)
        kp_scr[rows, :] = (ke * jnp.exp(b_last[c + 1]) if c % 2 == 0 else ke).astype(BF16)
        kh_scr[rows, :] = (ke * jnp.exp(h_log[n_chunks] - h_log[c + 1])).astype(BF16)

    pi = lax.broadcasted_iota(jnp.int32, (pw, pw), 0)
    pj = lax.broadcasted_iota(jnp.int32, (pw, pw), 1)
    diag = jnp.logical_and(pi >= pj, pi // c_len == pj // c_len)
    lower = jnp.logical_and(pi >= c_len, pj < c_len)
    for p in range(n_pairs):
        first = chunk_rows(2 * p)
        keys = jnp.concatenate([ki_scr[pair_rows(p), :], ke_scr[first, :], ke_scr[first, :]], axis=0)
        s = lax.dot_general(qd_scr[pair_rows(p), :].astype(BF16), keys, nt,
                            preferred_element_type=F32)
        s = jnp.where(diag, s[:, :pw], jnp.where(lower, s[:, pw:], 0.0))
        a_scr[pair_rows(p), pair_rows(p)] = s.astype(BF16)

    for p in range(n_pairs - 1):
        c0 = 2 * p + 2
        lhs = jnp.concatenate(
            [(qd_scr[chunk_rows(c), :] * jnp.exp(h_log[c] - h_log[c0])).astype(BF16)
             for c in range(c0, n_chunks)], axis=0)
        s = lax.dot_general(lhs, kp_scr[pair_rows(p), :], nt, preferred_element_type=F32)
        a_scr[c0 * c_len:, pair_rows(p)] = s.astype(BF16)

    half = 2 * pw
    for hf in range(tc // half):
        a_scr[hf * half:hf * half + pw, hf * half + pw:(hf + 1) * half] = jnp.zeros((pw, pw), BF16)

    st = st_scr[...]
    st_b = st.astype(BF16)
    for hf in range(tc // half):
        rows = slice(hf * half, (hf + 1) * half)
        keys = (hf + 1) * half
        o = jnp.dot(a_scr[rows, :keys], v_ref[0, :keys, :], preferred_element_type=F32)
        o += jnp.dot(qh_scr[rows, :], st_b, preferred_element_type=F32)
        o = o * lax.rsqrt(jnp.mean(o * o, axis=-1, keepdims=True) + LN_EPS) * ng_ref[0]
        r = r_ref[0, rows, :].astype(F32)
        o_ref[0, rows, :] = (o * (r * _sigmoid(r))).astype(BF16)

    d_col = jnp.broadcast_to(jnp.exp(h_log[n_chunks]), (8, GLA_DK)).T[:, :1]
    st_scr[...] = st * d_col + lax.dot_general(
        kh_scr[...], v_ref[0], (((0,), (0,)), ((), ())), preferred_element_type=F32)


def _gla(p3, lr3, w_a2, b_a, norm_g3, tc=512):
    bsz, seq, _ = p3.shape
    nq, nv, nr = OFF_GK // GLA_DK, OFF_GV // GLA_DV, OFF_GR // GLA_DV
    return pl.pallas_call(
        functools.partial(_gla_kernel, n_chunks=tc // GLA_CHUNK),
        grid=(bsz, GLA_HEADS, seq // tc),
        in_specs=[
            pl.BlockSpec((1, tc, GLA_DK), lambda b, h, t: (b, t, h)),
            pl.BlockSpec((1, tc, GLA_DK), lambda b, h, t: (b, t, nq + h)),
            pl.BlockSpec((1, tc, GLA_DV), lambda b, h, t: (b, t, nv + h)),
            pl.BlockSpec((1, tc, GLA_DV), lambda b, h, t: (b, t, nr + h)),
            pl.BlockSpec((1, tc, LR_PAD), lambda b, h, t: (b, t, 0)),
            pl.BlockSpec((LR_PAD, GLA_DK), lambda b, h, t: (0, h)),
            pl.BlockSpec((1, GLA_DK), lambda b, h, t: (0, h)),
            pl.BlockSpec((1, 1, GLA_DV), lambda b, h, t: (h, 0, 0)),
        ],
        out_specs=pl.BlockSpec((1, tc, GLA_DV), lambda b, h, t: (b, t, h)),
        out_shape=jax.ShapeDtypeStruct((bsz, seq, GLA_VW), BF16),
        scratch_shapes=[pltpu.VMEM((GLA_DK, GLA_DV), F32), pltpu.VMEM((tc, GLA_DK), F32),
                        pltpu.VMEM((tc, GLA_DK), F32)]
                       + [pltpu.VMEM((tc, GLA_DK), BF16)] * 5 + [pltpu.VMEM((tc, tc), BF16)],
        compiler_params=_cparams(("parallel", "parallel", "arbitrary")),
        name="gla",
    )(p3, p3, p3, p3, lr3, w_a2, b_a, norm_g3)


def _dil_kernel(q_ref, k_ref, v_ref, kp_ref, vp_ref, o_ref, lse_ref, s_scr, m_scr, p_scr, *, nblk):
    n = pl.program_id(2)
    blk = DIL_BLOCK
    qi = lax.broadcasted_iota(jnp.int32, (blk, 2 * blk), 0)
    kj = lax.broadcasted_iota(jnp.int32, (blk, 2 * blk), 1)
    band = jnp.logical_or(jnp.logical_and(kj < blk, kj >= qi),
                          jnp.logical_and(kj >= blk, kj - blk <= qi))
    band_first = jnp.logical_and(band, jnp.logical_or(kj >= blk, n > 0))
    nt = (((1,), (1,)), ((), ()))

    def rows(i):
        return slice(i * blk, (i + 1) * blk)

    def cols(h):
        return slice(h * DIL_HEAD_DIM, (h + 1) * DIL_HEAD_DIM)

    for i in range(nblk):
        for h in range(DIL_HEADS):
            q = q_ref[0, 0, rows(i), cols(h)]
            k_prev = kp_ref[0, 0, :, cols(h)] if i == 0 else k_ref[0, 0, rows(i - 1), cols(h)]
            s_p = lax.dot_general(q, k_prev, nt, preferred_element_type=F32)
            s_c = lax.dot_general(q, k_ref[0, 0, rows(i), cols(h)], nt, preferred_element_type=F32)
            s = jnp.concatenate([s_p, s_c], axis=1)
            s_scr[rows(i * DIL_HEADS + h), :] = jnp.where(band_first if i == 0 else band, s, NEG_BIG)

    s_all = s_scr[...]
    m_all = jnp.max(s_all, axis=-1, keepdims=True)
    p_scr[...] = jnp.exp(s_all - m_all).astype(BF16)
    m_scr[...] = jnp.broadcast_to(m_all, m_scr.shape)

    lane = lax.broadcasted_iota(jnp.int32, (blk, LANES), 1)
    ones = jnp.ones((2 * blk, DIL_HEAD_DIM), BF16)
    for i in range(nblk):
        lse_tile = jnp.zeros((blk, LANES), F32)
        for h in range(DIL_HEADS):
            u = i * DIL_HEADS + h
            v_prev = vp_ref[0, 0, :, cols(h)] if i == 0 else v_ref[0, 0, rows(i - 1), cols(h)]
            v_aug = jnp.concatenate(
                [jnp.concatenate([v_prev, v_ref[0, 0, rows(i), cols(h)]], axis=0), ones], axis=1)
            acc = jnp.dot(p_scr[rows(u), :], v_aug, preferred_element_type=F32)
            den = acc[:, DIL_HEAD_DIM:]
            o_ref[0, 0, rows(i), cols(h)] = (acc[:, :DIL_HEAD_DIM] / den).astype(BF16)
            lse_tile = jnp.where(lane == h, m_scr[rows(u), :] + jnp.log(den), lse_tile)
        lse_ref[0, 0, rows(i), :] = lse_tile


def _dil_group(qkv, nblk=4):
    bsz, r, length, _ = qkv.shape
    nb = length // DIL_BLOCK
    nblk = min(nblk, nb)
    assert nb % nblk == 0
    tq = nblk * DIL_BLOCK
    units = nblk * DIL_HEADS * DIL_BLOCK

    def cur_spec(u):
        return pl.BlockSpec((1, 1, tq, DIL_OUT), lambda b, c, n: (b, c, n, u))

    def prev_spec(u):
        return pl.BlockSpec((1, 1, DIL_BLOCK, DIL_OUT),
                            lambda b, c, n: (b, c, jnp.maximum(n * nblk - 1, 0), u))

    return pl.pallas_call(
        functools.partial(_dil_kernel, nblk=nblk),
        grid=(bsz, r, nb // nblk),
        in_specs=[cur_spec(0), cur_spec(1), cur_spec(2), prev_spec(1), prev_spec(2)],
        out_specs=[
            pl.BlockSpec((1, 1, tq, DIL_OUT), lambda b, c, n: (b, c, n, 0)),
            pl.BlockSpec((1, 1, tq, LANES), lambda b, c, n: (b, c, n, 0)),
        ],
        out_shape=[
            jax.ShapeDtypeStruct((bsz, r, length, DIL_OUT), BF16),
            jax.ShapeDtypeStruct((bsz, r, length, LANES), F32),
        ],
        scratch_shapes=[pltpu.VMEM((units, 2 * DIL_BLOCK), F32),
                        pltpu.VMEM((units, LANES), F32),
                        pltpu.VMEM((units, 2 * DIL_BLOCK), BF16)],
        compiler_params=_cparams(("parallel", "parallel", "parallel")),
        name=f"dilattn_r{r}",
    )(qkv, qkv, qkv, qkv, qkv)


def _merge_kernel(oa_ref, o1_ref, o2_ref, o3_ref, l1_ref, l2_ref, l3_ref, ga_ref, gb_ref, x_ref,
                  g2_ref, wa_ref, wb_ref, wo_ref, lng_ref, lnb_ref, out_ref, l_scr, w_scr, ob_scr,
                  *, dilations):
    y_a = jnp.dot(oa_ref[...], wa_ref[...], preferred_element_type=F32)
    o_refs = (o1_ref, o2_ref, o3_ref)
    l_refs = (l1_ref, l2_ref, l3_ref)
    tm = oa_ref.shape[0]

    def tok_rows(c, r):
        return pl.ds(c, tm // r, stride=r) if r > 1 else slice(None)

    for g, r in enumerate(dilations):
        for c in range(r):
            l_scr[g, tok_rows(c, r), :] = l_refs[g][0, c]
    l1, l2, l3 = l_scr[0], l_scr[1], l_scr[2]
    mx = jnp.maximum(jnp.maximum(l1, l2), l3)
    e1, e2, e3 = jnp.exp(l1 - mx), jnp.exp(l2 - mx), jnp.exp(l3 - mx)
    inv = 1.0 / (e1 + e2 + e3)
    w_scr[0], w_scr[1], w_scr[2] = e1 * inv, e2 * inv, e3 * inv
    for g, r in enumerate(dilations):
        for c in range(r):
            rows = tok_rows(c, r)
            wr = w_scr[g, rows, :]
            for h in range(DIL_HEADS):
                sl = slice(h * DIL_HEAD_DIM, (h + 1) * DIL_HEAD_DIM)
                part = wr[:, h:h + 1] * o_refs[g][0, c, :, sl].astype(F32)
                if g == 0:
                    ob_scr[h, rows, :] = part
                else:
                    ob_scr[h, rows, :] += part
    o_b = jnp.concatenate([ob_scr[h].astype(BF16) for h in range(DIL_HEADS)], axis=1)
    y_b = jnp.dot(o_b, wb_ref[...], preferred_element_type=F32)
    merged = _sigmoid(ga_ref[...].astype(F32)) * y_a + _sigmoid(gb_ref[...].astype(F32)) * y_b
    mix = jnp.dot(merged.astype(BF16), wo_ref[...], preferred_element_type=F32)
    y = DN_ALPHA * x_ref[...] + g2_ref[0] * mix
    out_ref[...] = _layer_norm(y, lng_ref[...], lnb_ref[...])


def _merge(oa2d, o_groups, lse_groups, p2d, x2d, mods3, k_gate, w_a, w_b, w_o, ln_g, ln_b, seq,
           dilations, tm=256):
    m, d = x2d.shape
    tiles_per_seq = seq // tm
    resident = pl.Buffered(1)
    n_g = len(dilations)

    def row_spec(width, col=0):
        return pl.BlockSpec((tm, width), lambda i: (i, col))

    def class_spec(r, width):
        return pl.BlockSpec((1, r, tm // r, width),
                            lambda i: (i // tiles_per_seq, 0, i % tiles_per_seq, 0))

    def const_spec(shape):
        return pl.BlockSpec(shape, lambda i: (0, 0), pipeline_mode=resident)

    return pl.pallas_call(
        functools.partial(_merge_kernel, dilations=dilations),
        grid=(m // tm,),
        in_specs=[
            row_spec(GLA_VW),
            *[class_spec(r, DIL_OUT) for r in dilations],
            *[class_spec(r, LANES) for r in dilations],
            row_spec(D_MODEL, OFF_GA // D_MODEL), row_spec(D_MODEL, OFF_GB // D_MODEL),
            row_spec(d),
            pl.BlockSpec((1, 1, d), lambda i: ((i // tiles_per_seq) * N_MOD + k_gate, 0, 0)),
            const_spec(w_a.shape), const_spec(w_b.shape), const_spec(w_o.shape),
            const_spec((1, d)), const_spec((1, d)),
        ],
        out_specs=row_spec(d),
        out_shape=jax.ShapeDtypeStruct((m, d), F32),
        scratch_shapes=[pltpu.VMEM((n_g, tm, LANES), F32), pltpu.VMEM((n_g, tm, LANES), F32),
                        pltpu.VMEM((DIL_HEADS, tm, DIL_HEAD_DIM), F32)],
        compiler_params=_cparams(("parallel",)),
        name="merge",
    )(oa2d, *o_groups, *lse_groups, p2d, p2d, x2d, mods3, w_a, w_b, w_o, ln_g, ln_b)


def _pack_w_in(w_in):
    o = 0
    segs = []
    for wdt in (GLA_QK, GLA_QK, GLA_VW, GLA_VW, GLA_GATE_RANK, DIL_W, DIL_W, DIL_W, D_MODEL, D_MODEL):
        segs.append(w_in[:, o:o + wdt].astype(BF16))
        o += wdt
    gq, gk, gv, gr, glr, dq, dk, dv, ga, gb = segs
    cols = [gq, gk, gv, gr, ga, gb, jnp.zeros((w_in.shape[0], OFF_DIL - PROJ_W), BF16)]
    for g in range(DIL_GROUPS):
        sl = slice(g * DIL_OUT, (g + 1) * DIL_OUT)
        cols += [dq[:, sl], dk[:, sl], dv[:, sl]]
    cols.append(jnp.pad(glr, ((0, 0), (0, LR_PAD - GLA_GATE_RANK))))
    return jnp.concatenate(cols, axis=1)


def kernel(x, c, positions, w_ada, b_ada, ln1_g, ln1_b, w_ffn1_gu, w_ffn1_down, w_in, w_alpha2,
           b_alpha, gla_norm_g, w_branch_a, w_branch_b, w_out, ln2_g, ln2_b, w_ffn2_gu, w_ffn2_down,
           ln3_g, ln3_b):
    bsz, seq, d = x.shape
    m = bsz * seq
    x2d = x.reshape(m, d)
    c_pad = jnp.pad(c, ((0, 8 - bsz % 8 if bsz % 8 else 0), (0, 0)))
    dilations = tuple(r for _, r in DIL_PATTERNS)

    half = DIL_HEAD_DIM // 2
    freq = ROPE_THETA ** (-jnp.arange(half, dtype=F32) / half)
    freq2 = jnp.concatenate([freq, freq]).reshape(1, DIL_HEAD_DIM)
    rope_t = _rope_tables(positions.reshape(m // PROJ_TM, PROJ_TM // LANES, LANES), freq2, dilations)

    for l in range(DEPTH):
        mods = _mods(c_pad, w_ada[l], b_ada[l].reshape(1, -1))[:bsz]
        mods3 = mods.reshape(bsz * N_MOD, 1, d)

        x1 = _ffn(x2d, mods3, 0, 1, 2, w_ffn1_gu[l].astype(BF16), w_ffn1_down[l].astype(BF16),
                  ln1_g[l].reshape(1, d), ln1_b[l].reshape(1, d), seq)

        w_all = _pack_w_in(w_in[l])
        p2d, lr2d = _proj(x1, mods3, 3, 4, w_all, seq)
        p3 = p2d.reshape(bsz, seq, PROJ_W)

        w_a2 = jnp.pad(w_alpha2[l], ((0, LR_PAD - GLA_GATE_RANK), (0, 0))).astype(BF16)
        o_a = _gla(p3, lr2d.reshape(bsz, seq, LR_PAD), w_a2, b_alpha[l].reshape(1, GLA_QK),
                   gla_norm_g[l].reshape(GLA_HEADS, 1, GLA_DV))

        o_groups, lse_groups = [], []
        for g, r in enumerate(dilations):
            qkv = _proj_dil(x1, mods3, 3, 4, w_all, g, rope_t[2 * g], rope_t[2 * g + 1],
                            bsz, seq, r)
            o_g, lse_g = _dil_group(qkv)
            o_groups.append(o_g)
            lse_groups.append(lse_g)

        x2 = _merge(o_a.reshape(m, GLA_VW), o_groups, lse_groups, p2d, x1, mods3, 5,
                    w_branch_a[l].astype(BF16), w_branch_b[l].astype(BF16), w_out[l].astype(BF16),
                    ln2_g[l].reshape(1, d), ln2_b[l].reshape(1, d), seq, dilations)

        x2d = _ffn(x2, mods3, 6, 7, 8, w_ffn2_gu[l].astype(BF16), w_ffn2_down[l].astype(BF16),
                   ln3_g[l].reshape(1, d), ln3_b[l].reshape(1, d), seq)
    return x2d.reshape(bsz, seq, d)
```

```python
import functools

import jax
import jax.numpy as jnp
from jax import lax
from jax.experimental import pallas as pl
from jax.experimental.pallas import tpu as pltpu

F32 = jnp.float32
BF16 = jnp.bfloat16

D_MODEL = 2048
DEPTH = 1
D_FF = 5632
N_MOD = 9
LN_EPS = 1e-5
DN_ALPHA = (2.0 * DEPTH) ** 0.25

GLA_HEADS = 4
GLA_DK = 256
GLA_DV = 512
GLA_GATE_RANK = 16
GLA_TAU = 16.0
GLA_CHUNK = 64
GLA_QK = GLA_HEADS * GLA_DK
GLA_VW = GLA_HEADS * GLA_DV

DIL_PATTERNS = ((128, 1), (512, 4), (2048, 16))
DIL_GROUPS = len(DIL_PATTERNS)
DIL_HEADS = 8
DIL_HEAD_DIM = 128
DIL_BLOCK = 128
DIL_W = DIL_GROUPS * DIL_HEADS * DIL_HEAD_DIM
DIL_OUT = DIL_HEADS * DIL_HEAD_DIM
ROPE_THETA = 10000.0

LANES = 128
LR_PAD = LANES

PROJ_W = 2 * GLA_QK + 2 * GLA_VW + 2 * D_MODEL
OFF_GQ = 0
OFF_GK = GLA_QK
OFF_GV = 2 * GLA_QK
OFF_GR = OFF_GV + GLA_VW
OFF_GA = OFF_GR + GLA_VW
OFF_GB = OFF_GA + D_MODEL
DIL_QKV = 3 * DIL_OUT
OFF_DIL = -(-PROJ_W // DIL_QKV) * DIL_QKV
OFF_LR = OFF_DIL + DIL_GROUPS * DIL_QKV

NEG_BIG = -1e30

VMEM_LIMIT = 56 * 1024 * 1024
VMEM_LIMIT_BIG = 60 * 1024 * 1024


def _cparams(sem, vmem=VMEM_LIMIT):
    return pltpu.CompilerParams(dimension_semantics=sem, vmem_limit_bytes=vmem)


def _sigmoid(x):
    return 1.0 / (1.0 + jnp.exp(-x))


def _layer_norm(y, g, b):
    mu = jnp.mean(y, axis=-1, keepdims=True)
    d = y - mu
    var = jnp.mean(d * d, axis=-1, keepdims=True)
    return d * lax.rsqrt(var + LN_EPS) * g + b


def _mods_kernel(c_ref, w_ref, b_ref, o_ref):
    c = c_ref[...]
    c_act = (c * _sigmoid(c)).astype(BF16)
    o_ref[...] = jnp.dot(c_act, w_ref[...].astype(BF16), preferred_element_type=F32) + b_ref[...]


def _mods(c_pad, w_ada, b_ada, tn=1024):
    rows, d = c_pad.shape
    n = w_ada.shape[1]
    return pl.pallas_call(
        _mods_kernel,
        grid=(n // tn,),
        in_specs=[
            pl.BlockSpec((rows, d), lambda j: (0, 0)),
            pl.BlockSpec((d, tn), lambda j: (0, j)),
            pl.BlockSpec((1, tn), lambda j: (0, j)),
        ],
        out_specs=pl.BlockSpec((rows, tn), lambda j: (0, j)),
        out_shape=jax.ShapeDtypeStruct((rows, n), F32),
        compiler_params=_cparams(("arbitrary",)),
        name="mods",
    )(c_pad, w_ada, b_ada)


FFN_TF = 512
FFN_WEIGHT_SLOTS = 4
FFN_PREFETCH = 2


def _ffn_kernel(x_ref, sh_ref, sc_ref, g_ref, wgu_hbm, wd_hbm, lng_ref, lnb_ref, o_ref,
                wg_buf, wu_buf, wd_buf, sem, *, nj):
    i = pl.program_id(0)
    last_tile = pl.num_programs(0) - 1
    tf = wg_buf.shape[2]

    def chunk_copies(j):
        slot = j % FFN_WEIGHT_SLOTS
        return (
            pltpu.make_async_copy(wgu_hbm.at[:, pl.ds(j * tf, tf)], wg_buf.at[slot], sem.at[0, slot]),
            pltpu.make_async_copy(wgu_hbm.at[:, pl.ds((nj + j) * tf, tf)], wu_buf.at[slot],
                                  sem.at[1, slot]),
            pltpu.make_async_copy(wd_hbm.at[pl.ds(j * tf, tf), :], wd_buf.at[slot], sem.at[2, slot]),
        )

    @pl.when(i == 0)
    def _():
        for j in range(FFN_PREFETCH):
            for cp in chunk_copies(j):
                cp.start()

    h = (x_ref[...] * (1.0 + sc_ref[0]) + sh_ref[0]).astype(BF16)
    for j in range(nj):
        slot = j % FFN_WEIGHT_SLOTS
        for cp in chunk_copies((j + FFN_PREFETCH) % nj):
            cp.start()
        for cp in chunk_copies(j):
            cp.wait()
        gate = jnp.dot(h, wg_buf[slot], preferred_element_type=F32)
        up = jnp.dot(h, wu_buf[slot], preferred_element_type=F32)
        act = (gate * _sigmoid(gate) * up).astype(BF16)
        down = jnp.dot(act, wd_buf[slot], preferred_element_type=F32)
        if j == 0:
            o_ref[...] = down
        else:
            o_ref[...] += down

    y = DN_ALPHA * x_ref[...] + 0.5 * g_ref[0] * o_ref[...]
    o_ref[...] = _layer_norm(y, lng_ref[...], lnb_ref[...])

    @pl.when(i == last_tile)
    def _():
        for j in range(FFN_PREFETCH):
            for cp in chunk_copies(j):
                cp.wait()


def _ffn(x2d, mods3, k_shift, k_scale, k_gate, w_gu, w_down, ln_g, ln_b, seq, tm=512, tf=FFN_TF):
    m, d = x2d.shape
    n_ff = w_down.shape[0]
    nj = n_ff // tf
    tiles_per_seq = seq // tm
    for w in range(FFN_PREFETCH):
        live = {c % FFN_WEIGHT_SLOTS for c in range(nj - FFN_PREFETCH + w, nj)} | set(range(w))
        assert w not in live

    def mod_spec(k):
        return pl.BlockSpec((1, 1, d), lambda i: ((i // tiles_per_seq) * N_MOD + k, 0, 0))

    return pl.pallas_call(
        functools.partial(_ffn_kernel, nj=nj),
        grid=(m // tm,),
        in_specs=[
            pl.BlockSpec((tm, d), lambda i: (i, 0)),
            mod_spec(k_shift), mod_spec(k_scale), mod_spec(k_gate),
            pl.BlockSpec(memory_space=pl.ANY),
            pl.BlockSpec(memory_space=pl.ANY),
            pl.BlockSpec((1, d), lambda i: (0, 0)),
            pl.BlockSpec((1, d), lambda i: (0, 0)),
        ],
        out_specs=pl.BlockSpec((tm, d), lambda i: (i, 0)),
        out_shape=jax.ShapeDtypeStruct((m, d), F32),
        scratch_shapes=[pltpu.VMEM((FFN_WEIGHT_SLOTS, d, tf), BF16),
                        pltpu.VMEM((FFN_WEIGHT_SLOTS, d, tf), BF16),
                        pltpu.VMEM((FFN_WEIGHT_SLOTS, tf, d), BF16),
                        pltpu.SemaphoreType.DMA((3, FFN_WEIGHT_SLOTS))],
        compiler_params=_cparams(("arbitrary",)),
        name="ffn",
    )(x2d, mods3, mods3, mods3, w_gu, w_down, ln_g, ln_b)


PROJ_TM = 1024


def _rope_kernel(pos_ref, freq_ref, *out_refs, dilations):
    cos_ref, sin_ref = out_refs[0], out_refs[1]
    tm = cos_ref.shape[0]
    pos_t = pos_ref[0].astype(F32).T
    lane = lax.broadcasted_iota(jnp.int32, (LANES, DIL_HEAD_DIM), 1)
    for k in range(tm // LANES):
        ang = pos_t[:, k:k + 1] * freq_ref[...]
        s = jnp.sin(ang)
        cos_ref[k * LANES:(k + 1) * LANES, :] = jnp.cos(ang)
        sin_ref[k * LANES:(k + 1) * LANES, :] = jnp.where(lane < DIL_HEAD_DIM // 2, -s, s)
    for gi, r in enumerate(dilations[1:]):
        tl = tm // r
        for c in range(r):
            out_refs[2 + 2 * gi][c * tl:(c + 1) * tl, :] = cos_ref[pl.ds(c, tl, stride=r), :]
            out_refs[3 + 2 * gi][c * tl:(c + 1) * tl, :] = sin_ref[pl.ds(c, tl, stride=r), :]


def _rope_tables(pos3, freq2, dilations, tm=PROJ_TM):
    assert dilations[0] == 1
    m = pos3.shape[0] * tm
    n_out = 2 * len(dilations)
    return pl.pallas_call(
        functools.partial(_rope_kernel, dilations=dilations),
        grid=(m // tm,),
        in_specs=[
            pl.BlockSpec((1, tm // LANES, LANES), lambda i: (i, 0, 0)),
            pl.BlockSpec((1, DIL_HEAD_DIM), lambda i: (0, 0)),
        ],
        out_specs=[pl.BlockSpec((tm, DIL_HEAD_DIM), lambda i: (i, 0))] * n_out,
        out_shape=[jax.ShapeDtypeStruct((m, DIL_HEAD_DIM), F32)] * n_out,
        compiler_params=_cparams(("parallel",)),
        name="rope_tables",
    )(pos3, freq2)


def _proj_kernel(x_ref, sh_ref, sc_ref, w_ref, wlr_ref, o_ref, lr_ref, h_scr):
    @pl.when(pl.program_id(1) == 0)
    def _():
        h = (x_ref[...] * (1.0 + sc_ref[0]) + sh_ref[0]).astype(BF16)
        h_scr[...] = h
        lr_ref[...] = jnp.dot(h, wlr_ref[...], preferred_element_type=F32)

    o_ref[...] = jnp.dot(h_scr[...], w_ref[...], preferred_element_type=F32).astype(BF16)


def _proj(x2d, mods3, k_shift, k_scale, w_all, seq, tm=PROJ_TM, tn=2048):
    m, d = x2d.shape
    n = PROJ_W
    tiles_per_seq = seq // tm

    def mod_spec(k):
        return pl.BlockSpec((1, 1, d), lambda i, j: ((i // tiles_per_seq) * N_MOD + k, 0, 0))

    return pl.pallas_call(
        _proj_kernel,
        grid=(m // tm, n // tn),
        in_specs=[
            pl.BlockSpec((tm, d), lambda i, j: (i, 0)),
            mod_spec(k_shift), mod_spec(k_scale),
            pl.BlockSpec((d, tn), lambda i, j: (0, j)),
            pl.BlockSpec((d, LR_PAD), lambda i, j: (0, OFF_LR // LR_PAD)),
        ],
        out_specs=[
            pl.BlockSpec((tm, tn), lambda i, j: (i, j)),
            pl.BlockSpec((tm, LR_PAD), lambda i, j: (i, 0)),
        ],
        out_shape=[
            jax.ShapeDtypeStruct((m, n), BF16),
            jax.ShapeDtypeStruct((m, LR_PAD), F32),
        ],
        scratch_shapes=[pltpu.VMEM((tm, d), BF16)],
        compiler_params=_cparams(("parallel", "arbitrary")),
        name="proj",
    )(x2d, mods3, mods3, w_all, w_all)


def _proj_dil_kernel(x_ref, sh_ref, sc_ref, w_ref, cos_ref, sin_ref, o_ref, h_scr, slab_scr, *, r):
    tm, d = x_ref.shape
    tl = tm // r
    if r == 1:
        h_scr[...] = (x_ref[...] * (1.0 + sc_ref[0]) + sh_ref[0]).astype(BF16)
    else:
        for k in range(d // LANES):
            sl = slice(k * LANES, (k + 1) * LANES)
            slab_scr[k] = x_ref[:, sl] * (1.0 + sc_ref[0, :, sl]) + sh_ref[0, :, sl]
        for c in range(r):
            for k in range(d // LANES):
                sl = slice(k * LANES, (k + 1) * LANES)
                h_scr[c * tl:(c + 1) * tl, sl] = slab_scr[k, pl.ds(c, tl, stride=r), :].astype(BF16)
    _proj_dil_qkv(h_scr, w_ref, cos_ref, sin_ref, o_ref, r)


def _proj_dil_qkv(h_scr, w_ref, cos_ref, sin_ref, o_ref, r):
    tl = h_scr.shape[0] // r
    for part, scale in enumerate((DIL_HEAD_DIM ** -0.5, 1.0, None)):
        cols = slice(part * DIL_OUT, (part + 1) * DIL_OUT)
        t = jnp.dot(h_scr[...], w_ref[:, cols], preferred_element_type=F32)
        if scale is None:
            for c in range(r):
                o_ref[0, c, :, cols] = t[c * tl:(c + 1) * tl, :].astype(BF16)
            continue
        cos = cos_ref[...] * scale
        sin = sin_ref[...] * scale
        for s in range(DIL_HEADS):
            ts = t[:, s * DIL_HEAD_DIM:(s + 1) * DIL_HEAD_DIM]
            rot = (ts * cos + pltpu.roll(ts, DIL_HEAD_DIM // 2, 1) * sin).astype(BF16)
            lo = part * DIL_OUT + s * DIL_HEAD_DIM
            for c in range(r):
                o_ref[0, c, :, lo:lo + DIL_HEAD_DIM] = rot[c * tl:(c + 1) * tl, :]


def _proj_dil_dma_kernel(x_hbm, sh_ref, sc_ref, w_ref, cos_ref, sin_ref, o_ref, h_scr, x_buf, sem,
                         *, r):
    i = pl.program_id(0)
    tl = h_scr.shape[0] // r
    slot = i % 2

    def tile_copies(tile, buf_slot):
        return [pltpu.make_async_copy(x_hbm.at[pl.ds(tile * tl, tl), c], x_buf.at[buf_slot, c],
                                      sem.at[buf_slot]) for c in range(r)]

    @pl.when(i == 0)
    def _():
        for cp in tile_copies(0, 0):
            cp.start()

    @pl.when(i + 1 < pl.num_programs(0))
    def _():
        for cp in tile_copies(i + 1, 1 - slot):
            cp.start()

    for cp in tile_copies(i, slot):
        cp.wait()
    for c in range(r):
        h_scr[c * tl:(c + 1) * tl, :] = (x_buf[slot, c] * (1.0 + sc_ref[0]) + sh_ref[0]).astype(BF16)
    _proj_dil_qkv(h_scr, w_ref, cos_ref, sin_ref, o_ref, r)


SUBLANES_F32 = 8


def _proj_dil(x2d, mods3, k_shift, k_scale, w_all, g, cos_t, sin_t, bsz, seq, r, tm=PROJ_TM):
    m, d = x2d.shape
    tiles_per_seq = seq // tm
    tl = tm // r
    dma_gather = r % SUBLANES_F32 == 0

    def mod_spec(k):
        return pl.BlockSpec((1, 1, d), lambda i: ((i // tiles_per_seq) * N_MOD + k, 0, 0))

    if dma_gather:
        body = functools.partial(_proj_dil_dma_kernel, r=r)
        x_arg = x2d.reshape(m // r, r, d)
        x_spec = pl.BlockSpec(memory_space=pl.ANY)
        scratch = [pltpu.VMEM((tm, d), BF16), pltpu.VMEM((2, r, tl, d), F32),
                   pltpu.SemaphoreType.DMA((2,))]
        semantics = ("arbitrary",)
    else:
        body = functools.partial(_proj_dil_kernel, r=r)
        x_arg = x2d
        x_spec = pl.BlockSpec((tm, d), lambda i: (i, 0))
        scratch = [pltpu.VMEM((tm, d), BF16),
                   pltpu.VMEM((d // LANES, tm, LANES) if r > 1 else (1, 8, LANES), F32)]
        semantics = ("parallel",)

    return pl.pallas_call(
        body,
        grid=(m // tm,),
        in_specs=[
            x_spec,
            mod_spec(k_shift), mod_spec(k_scale),
            pl.BlockSpec((d, DIL_QKV), lambda i: (0, OFF_DIL // DIL_QKV + g),
                         pipeline_mode=pl.Buffered(1)),
            pl.BlockSpec((tm, DIL_HEAD_DIM), lambda i: (i, 0)),
            pl.BlockSpec((tm, DIL_HEAD_DIM), lambda i: (i, 0)),
        ],
        out_specs=pl.BlockSpec((1, r, tl, DIL_QKV),
                               lambda i: (i // tiles_per_seq, 0, i % tiles_per_seq, 0)),
        out_shape=jax.ShapeDtypeStruct((bsz, r, seq // r, DIL_QKV), BF16),
        scratch_shapes=scratch,
        compiler_params=_cparams(semantics, VMEM_LIMIT_BIG),
        name=f"proj_dil_r{r}",
    )(x_arg, mods3, mods3, w_all, cos_t, sin_t)


def _gla_kernel(q_ref, k_ref, v_ref, r_ref, lr_ref, wa_ref, ba_ref, ng_ref, o_ref,
                st_scr, b_scr, qd_scr, qh_scr, ki_scr, ke_scr, kp_scr, kh_scr, a_scr, *, n_chunks):
    c_len = GLA_CHUNK
    tc = q_ref.shape[1]
    hb = 4 * c_len
    pw = 2 * c_len
    n_pairs = n_chunks // 2

    @pl.when(pl.program_id(2) == 0)
    def _():
        st_scr[...] = jnp.zeros_like(st_scr)

    ti = lax.broadcasted_iota(jnp.int32, (hb, hb), 0)
    tj = lax.broadcasted_iota(jnp.int32, (hb, hb), 1)
    tri = jnp.logical_and(ti >= tj, ti // c_len == tj // c_len).astype(BF16)
    for s in range(tc // hb):
        rows = slice(s * hb, (s + 1) * hb)
        logits = jnp.dot(lr_ref[0, rows, :].astype(BF16), wa_ref[...],
                         preferred_element_type=F32) + ba_ref[...]
        log_a = (jnp.minimum(logits, 0.0) - jnp.log(1.0 + jnp.exp(-jnp.abs(logits)))) / GLA_TAU
        p1 = log_a.astype(BF16)
        r1 = log_a - p1.astype(F32)
        p2 = r1.astype(BF16)
        p3 = (r1 - p2.astype(F32)).astype(BF16)
        b_scr[rows, :] = (jnp.dot(tri, p1, preferred_element_type=F32)
                          + jnp.dot(tri, p2, preferred_element_type=F32)
                          + jnp.dot(tri, p3, preferred_element_type=F32))

    nt = (((1,), (1,)), ((), ()))

    def chunk_rows(c):
        return slice(c * c_len, (c + 1) * c_len)

    def pair_rows(p):
        return slice(p * pw, (p + 1) * pw)

    b_last = [b_scr[(c + 1) * c_len - 1:(c + 1) * c_len, :] for c in range(n_chunks)]
    h_log = [jnp.zeros_like(b_last[0])]
    for c in range(n_chunks):
        h_log.append(h_log[-1] + b_last[c])

    for c in range(n_chunks):
        rows = chunk_rows(c)
        b = b_scr[rows, :]
        q = q_ref[0, rows, :].astype(F32)
        k = k_ref[0, rows, :].astype(F32)
        qd = q * jnp.exp(b) * (GLA_DK ** -0.5)
        ke = k * jnp.exp(b_last[c] - b)
        qd_scr[rows, :] = qd
        qh_scr[rows, :] = (qd * jnp.exp(h_log[c])).astype(BF16)
        ki_scr[rows, :] = (k * jnp.exp(-b)).astype(BF16)
        ke_scr[rows, :] = ke.astype(BF16)
        kp_scr[rows, :] = (ke * jnp.exp(b_last[c + 1]) if c % 2 == 0 else ke).astype(BF16)
        kh_scr[rows, :] = (ke * jnp.exp(h_log[n_chunks] - h_log[c + 1])).astype(BF16)

    pi = lax.broadcasted_iota(jnp.int32, (pw, pw), 0)
    pj = lax.broadcasted_iota(jnp.int32, (pw, pw), 1)
    diag = jnp.logical_and(pi >= pj, pi // c_len == pj // c_len)
    lower = jnp.logical_and(pi >= c_len, pj < c_len)
    for p in range(n_pairs):
        first = chunk_rows(2 * p)
        keys = jnp.concatenate([ki_scr[pair_rows(p), :], ke_scr[first, :], ke_scr[first, :]], axis=0)
        s = lax.dot_general(qd_scr[pair_rows(p), :].astype(BF16), keys, nt,
                            preferred_element_type=F32)
        s = jnp.where(diag, s[:, :pw], jnp.where(lower, s[:, pw:], 0.0))
        a_scr[pair_rows(p), pair_rows(p)] = s.astype(BF16)

    for p in range(n_pairs - 1):
        c0 = 2 * p + 2
        lhs = jnp.concatenate(
            [(qd_scr[chunk_rows(c), :] * jnp.exp(h_log[c] - h_log[c0])).astype(BF16)
             for c in range(c0, n_chunks)], axis=0)
        s = lax.dot_general(lhs, kp_scr[pair_rows(p), :], nt, preferred_element_type=F32)
        a_scr[c0 * c_len:, pair_rows(p)] = s.astype(BF16)

    half = 2 * pw
    for hf in range(tc // half):
        a_scr[hf * half:hf * half + pw, hf * half + pw:(hf + 1) * half] = jnp.zeros((pw, pw), BF16)

    st = st_scr[...]
    st_b = st.astype(BF16)
    for hf in range(tc // half):
        rows = slice(hf * half, (hf + 1) * half)
        keys = (hf + 1) * half
        o = jnp.dot(a_scr[rows, :keys], v_ref[0, :keys, :], preferred_element_type=F32)
        o += jnp.dot(qh_scr[rows, :], st_b, preferred_element_type=F32)
        o = o * lax.rsqrt(jnp.mean(o * o, axis=-1, keepdims=True) + LN_EPS) * ng_ref[0]
        r = r_ref[0, rows, :].astype(F32)
        o_ref[0, rows, :] = (o * (r * _sigmoid(r))).astype(BF16)

    d_col = jnp.broadcast_to(jnp.exp(h_log[n_chunks]), (8, GLA_DK)).T[:, :1]
    st_scr[...] = st * d_col + lax.dot_general(
        kh_scr[...], v_ref[0], (((0,), (0,)), ((), ())), preferred_element_type=F32)


def _gla(p3, lr3, w_a2, b_a, norm_g3, tc=512):
    bsz, seq, _ = p3.shape
    nq, nv, nr = OFF_GK // GLA_DK, OFF_GV // GLA_DV, OFF_GR // GLA_DV
    return pl.pallas_call(
        functools.partial(_gla_kernel, n_chunks=tc // GLA_CHUNK),
        grid=(bsz, GLA_HEADS, seq // tc),
        in_specs=[
            pl.BlockSpec((1, tc, GLA_DK), lambda b, h, t: (b, t, h)),
            pl.BlockSpec((1, tc, GLA_DK), lambda b, h, t: (b, t, nq + h)),
            pl.BlockSpec((1, tc, GLA_DV), lambda b, h, t: (b, t, nv + h)),
            pl.BlockSpec((1, tc, GLA_DV), lambda b, h, t: (b, t, nr + h)),
            pl.BlockSpec((1, tc, LR_PAD), lambda b, h, t: (b, t, 0)),
            pl.BlockSpec((LR_PAD, GLA_DK), lambda b, h, t: (0, h)),
            pl.BlockSpec((1, GLA_DK), lambda b, h, t: (0, h)),
            pl.BlockSpec((1, 1, GLA_DV), lambda b, h, t: (h, 0, 0)),
        ],
        out_specs=pl.BlockSpec((1, tc, GLA_DV), lambda b, h, t: (b, t, h)),
        out_shape=jax.ShapeDtypeStruct((bsz, seq, GLA_VW), BF16),
        scratch_shapes=[pltpu.VMEM((GLA_DK, GLA_DV), F32), pltpu.VMEM((tc, GLA_DK), F32),
                        pltpu.VMEM((tc, GLA_DK), F32)]
                       + [pltpu.VMEM((tc, GLA_DK), BF16)] * 5 + [pltpu.VMEM((tc, tc), BF16)],
        compiler_params=_cparams(("parallel", "parallel", "arbitrary")),
        name="gla",
    )(p3, p3, p3, p3, lr3, w_a2, b_a, norm_g3)


def _dil_kernel(q_ref, k_ref, v_ref, kp_ref, vp_ref, o_ref, lse_ref, s_scr, m_scr, p_scr, *, nblk):
    n = pl.program_id(2)
    blk = DIL_BLOCK
    qi = lax.broadcasted_iota(jnp.int32, (blk, 2 * blk), 0)
    kj = lax.broadcasted_iota(jnp.int32, (blk, 2 * blk), 1)
    band = jnp.logical_or(jnp.logical_and(kj < blk, kj >= qi),
                          jnp.logical_and(kj >= blk, kj - blk <= qi))
    band_first = jnp.logical_and(band, jnp.logical_or(kj >= blk, n > 0))
    nt = (((1,), (1,)), ((), ()))

    def rows(i):
        return slice(i * blk, (i + 1) * blk)

    def cols(h):
        return slice(h * DIL_HEAD_DIM, (h + 1) * DIL_HEAD_DIM)

    for i in range(nblk):
        for h in range(DIL_HEADS):
            q = q_ref[0, 0, rows(i), cols(h)]
            k_prev = kp_ref[0, 0, :, cols(h)] if i == 0 else k_ref[0, 0, rows(i - 1), cols(h)]
            s_p = lax.dot_general(q, k_prev, nt, preferred_element_type=F32)
            s_c = lax.dot_general(q, k_ref[0, 0, rows(i), cols(h)], nt, preferred_element_type=F32)
            s = jnp.concatenate([s_p, s_c], axis=1)
            s_scr[rows(i * DIL_HEADS + h), :] = jnp.where(band_first if i == 0 else band, s, NEG_BIG)

    s_all = s_scr[...]
    m_all = jnp.max(s_all, axis=-1, keepdims=True)
    p_scr[...] = jnp.exp(s_all - m_all).astype(BF16)
    m_scr[...] = jnp.broadcast_to(m_all, m_scr.shape)

    lane = lax.broadcasted_iota(jnp.int32, (blk, LANES), 1)
    ones = jnp.ones((2 * blk, DIL_HEAD_DIM), BF16)
    for i in range(nblk):
        lse_tile = jnp.zeros((blk, LANES), F32)
        for h in range(DIL_HEADS):
            u = i * DIL_HEADS + h
            v_prev = vp_ref[0, 0, :, cols(h)] if i == 0 else v_ref[0, 0, rows(i - 1), cols(h)]
            v_aug = jnp.concatenate(
                [jnp.concatenate([v_prev, v_ref[0, 0, rows(i), cols(h)]], axis=0), ones], axis=1)
            acc = jnp.dot(p_scr[rows(u), :], v_aug, preferred_element_type=F32)
            den = acc[:, DIL_HEAD_DIM:]
            o_ref[0, 0, rows(i), cols(h)] = (acc[:, :DIL_HEAD_DIM] / den).astype(BF16)
            lse_tile = jnp.where(lane == h, m_scr[rows(u), :] + jnp.log(den), lse_tile)
        lse_ref[0, 0, rows(i), :] = lse_tile


def _dil_group(qkv, nblk=4):
    bsz, r, length, _ = qkv.shape
    nb = length // DIL_BLOCK
    nblk = min(nblk, nb)
    assert nb % nblk == 0
    tq = nblk * DIL_BLOCK
    units = nblk * DIL_HEADS * DIL_BLOCK

    def cur_spec(u):
        return pl.BlockSpec((1, 1, tq, DIL_OUT), lambda b, c, n: (b, c, n, u))

    def prev_spec(u):
        return pl.BlockSpec((1, 1, DIL_BLOCK, DIL_OUT),
                            lambda b, c, n: (b, c, jnp.maximum(n * nblk - 1, 0), u))

    return pl.pallas_call(
        functools.partial(_dil_kernel, nblk=nblk),
        grid=(bsz, r, nb // nblk),
        in_specs=[cur_spec(0), cur_spec(1), cur_spec(2), prev_spec(1), prev_spec(2)],
        out_specs=[
            pl.BlockSpec((1, 1, tq, DIL_OUT), lambda b, c, n: (b, c, n, 0)),
            pl.BlockSpec((1, 1, tq, LANES), lambda b, c, n: (b, c, n, 0)),
        ],
        out_shape=[
            jax.ShapeDtypeStruct((bsz, r, length, DIL_OUT), BF16),
            jax.ShapeDtypeStruct((bsz, r, length, LANES), F32),
        ],
        scratch_shapes=[pltpu.VMEM((units, 2 * DIL_BLOCK), F32),
                        pltpu.VMEM((units, LANES), F32),
                        pltpu.VMEM((units, 2 * DIL_BLOCK), BF16)],
        compiler_params=_cparams(("parallel", "parallel", "parallel")),
        name=f"dilattn_r{r}",
    )(qkv, qkv, qkv, qkv, qkv)


def _merge_kernel(oa_ref, o1_ref, o2_ref, o3_ref, l1_ref, l2_ref, l3_ref, ga_ref, gb_ref, x_ref,
                  g2_ref, wa_ref, wb_ref, wo_ref, lng_ref, lnb_ref, out_ref, l_scr, w_scr, ob_scr,
                  *, dilations):
    y_a = jnp.dot(oa_ref[...], wa_ref[...], preferred_element_type=F32)
    o_refs = (o1_ref, o2_ref, o3_ref)
    l_refs = (l1_ref, l2_ref, l3_ref)
    tm = oa_ref.shape[0]

    def tok_rows(c, r):
        return pl.ds(c, tm // r, stride=r) if r > 1 else slice(None)

    for g, r in enumerate(dilations):
        for c in range(r):
            l_scr[g, tok_rows(c, r), :] = l_refs[g][0, c]
    l1, l2, l3 = l_scr[0], l_scr[1], l_scr[2]
    mx = jnp.maximum(jnp.maximum(l1, l2), l3)
    e1, e2, e3 = jnp.exp(l1 - mx), jnp.exp(l2 - mx), jnp.exp(l3 - mx)
    inv = 1.0 / (e1 + e2 + e3)
    w_scr[0], w_scr[1], w_scr[2] = e1 * inv, e2 * inv, e3 * inv
    for g, r in enumerate(dilations):
        for c in range(r):
            rows = tok_rows(c, r)
            wr = w_scr[g, rows, :]
            for h in range(DIL_HEADS):
                sl = slice(h * DIL_HEAD_DIM, (h + 1) * DIL_HEAD_DIM)
                part = wr[:, h:h + 1] * o_refs[g][0, c, :, sl].astype(F32)
                if g == 0:
                    ob_scr[h, rows, :] = part
                else:
                    ob_scr[h, rows, :] += part
    o_b = jnp.concatenate([ob_scr[h].astype(BF16) for h in range(DIL_HEADS)], axis=1)
    y_b = jnp.dot(o_b, wb_ref[...], preferred_element_type=F32)
    merged = _sigmoid(ga_ref[...].astype(F32)) * y_a + _sigmoid(gb_ref[...].astype(F32)) * y_b
    mix = jnp.dot(merged.astype(BF16), wo_ref[...], preferred_element_type=F32)
    y = DN_ALPHA * x_ref[...] + g2_ref[0] * mix
    out_ref[...] = _layer_norm(y, lng_ref[...], lnb_ref[...])


def _merge(oa2d, o_groups, lse_groups, p2d, x2d, mods3, k_gate, w_a, w_b, w_o, ln_g, ln_b, seq,
           dilations, tm=256):
    m, d = x2d.shape
    tiles_per_seq = seq // tm
    resident = pl.Buffered(1)
    n_g = len(dilations)

    def row_spec(width, col=0):
        return pl.BlockSpec((tm, width), lambda i: (i, col))

    def class_spec(r, width):
        return pl.BlockSpec((1, r, tm // r, width),
                            lambda i: (i // tiles_per_seq, 0, i % tiles_per_seq, 0))

    def const_spec(shape):
        return pl.BlockSpec(shape, lambda i: (0, 0), pipeline_mode=resident)

    return pl.pallas_call(
        functools.partial(_merge_kernel, dilations=dilations),
        grid=(m // tm,),
        in_specs=[
            row_spec(GLA_VW),
            *[class_spec(r, DIL_OUT) for r in dilations],
            *[class_spec(r, LANES) for r in dilations],
            row_spec(D_MODEL, OFF_GA // D_MODEL), row_spec(D_MODEL, OFF_GB // D_MODEL),
            row_spec(d),
            pl.BlockSpec((1, 1, d), lambda i: ((i // tiles_per_seq) * N_MOD + k_gate, 0, 0)),
            const_spec(w_a.shape), const_spec(w_b.shape), const_spec(w_o.shape),
            const_spec((1, d)), const_spec((1, d)),
        ],
        out_specs=row_spec(d),
        out_shape=jax.ShapeDtypeStruct((m, d), F32),
        scratch_shapes=[pltpu.VMEM((n_g, tm, LANES), F32), pltpu.VMEM((n_g, tm, LANES), F32),
                        pltpu.VMEM((DIL_HEADS, tm, DIL_HEAD_DIM), F32)],
        compiler_params=_cparams(("parallel",)),
        name="merge",
    )(oa2d, *o_groups, *lse_groups, p2d, p2d, x2d, mods3, w_a, w_b, w_o, ln_g, ln_b)


def _pack_w_in(w_in):
    o = 0
    segs = []
    for wdt in (GLA_QK, GLA_QK, GLA_VW, GLA_VW, GLA_GATE_RANK, DIL_W, DIL_W, DIL_W, D_MODEL, D_MODEL):
        segs.append(w_in[:, o:o + wdt].astype(BF16))
        o += wdt
    gq, gk, gv, gr, glr, dq, dk, dv, ga, gb = segs
    cols = [gq, gk, gv, gr, ga, gb, jnp.zeros((w_in.shape[0], OFF_DIL - PROJ_W), BF16)]
    for g in range(DIL_GROUPS):
        sl = slice(g * DIL_OUT, (g + 1) * DIL_OUT)
        cols += [dq[:, sl], dk[:, sl], dv[:, sl]]
    cols.append(jnp.pad(glr, ((0, 0), (0, LR_PAD - GLA_GATE_RANK))))
    return jnp.concatenate(cols, axis=1)


def kernel(x, c, positions, w_ada, b_ada, ln1_g, ln1_b, w_ffn1_gu, w_ffn1_down, w_in, w_alpha2,
           b_alpha, gla_norm_g, w_branch_a, w_branch_b, w_out, ln2_g, ln2_b, w_ffn2_gu, w_ffn2_down,
           ln3_g, ln3_b):
    bsz, seq, d = x.shape
    m = bsz * seq
    x2d = x.reshape(m, d)
    c_pad = jnp.pad(c, ((0, 8 - bsz % 8 if bsz % 8 else 0), (0, 0)))
    dilations = tuple(r for _, r in DIL_PATTERNS)

    half = DIL_HEAD_DIM // 2
    freq = ROPE_THETA ** (-jnp.arange(half, dtype=F32) / half)
    freq2 = jnp.concatenate([freq, freq]).reshape(1, DIL_HEAD_DIM)
    rope_t = _rope_tables(positions.reshape(m // PROJ_TM, PROJ_TM // LANES, LANES), freq2, dilations)

    for l in range(DEPTH):
        mods = _mods(c_pad, w_ada[l], b_ada[l].reshape(1, -1))[:bsz]
        mods3 = mods.reshape(bsz * N_MOD, 1, d)

        x1 = _ffn(x2d, mods3, 0, 1, 2, w_ffn1_gu[l].astype(BF16), w_ffn1_down[l].astype(BF16),
                  ln1_g[l].reshape(1, d), ln1_b[l].reshape(1, d), seq)

        w_all = _pack_w_in(w_in[l])
        p2d, lr2d = _proj(x1, mods3, 3, 4, w_all, seq)
        p3 = p2d.reshape(bsz, seq, PROJ_W)

        w_a2 = jnp.pad(w_alpha2[l], ((0, LR_PAD - GLA_GATE_RANK), (0, 0))).astype(BF16)
        o_a = _gla(p3, lr2d.reshape(bsz, seq, LR_PAD), w_a2, b_alpha[l].reshape(1, GLA_QK),
                   gla_norm_g[l].reshape(GLA_HEADS, 1, GLA_DV))

        o_groups, lse_groups = [], []
        for g, r in enumerate(dilations):
            qkv = _proj_dil(x1, mods3, 3, 4, w_all, g, rope_t[2 * g], rope_t[2 * g + 1],
                            bsz, seq, r)
            o_g, lse_g = _dil_group(qkv)
            o_groups.append(o_g)
            lse_groups.append(lse_g)

        x2 = _merge(o_a.reshape(m, GLA_VW), o_groups, lse_groups, p2d, x1, mods3, 5,
                    w_branch_a[l].astype(BF16), w_branch_b[l].astype(BF16), w_out[l].astype(BF16),
                    ln2_g[l].reshape(1, d), ln2_b[l].reshape(1, d), seq, dilations)

        x2d = _ffn(x2, mods3, 6, 7, 8, w_ffn2_gu[l].astype(BF16), w_ffn2_down[l].astype(BF16),
                   ln3_g[l].reshape(1, d), ln3_b[l].reshape(1, d), seq)
    return x2d.reshape(bsz, seq, d)
```

```python
import functools

import jax
import jax.numpy as jnp
from jax import lax
from jax.experimental import pallas as pl
from jax.experimental.pallas import tpu as pltpu

F32 = jnp.float32
BF16 = jnp.bfloat16

D_MODEL = 2048
DEPTH = 1
D_FF = 5632
N_MOD = 9
LN_EPS = 1e-5
DN_ALPHA = (2.0 * DEPTH) ** 0.25

GLA_HEADS = 4
GLA_DK = 256
GLA_DV = 512
GLA_GATE_RANK = 16
GLA_TAU = 16.0
GLA_CHUNK = 64
GLA_QK = GLA_HEADS * GLA_DK
GLA_VW = GLA_HEADS * GLA_DV

DIL_PATTERNS = ((128, 1), (512, 4), (2048, 16))
DIL_GROUPS = len(DIL_PATTERNS)
DIL_HEADS = 8
DIL_HEAD_DIM = 128
DIL_BLOCK = 128
DIL_W = DIL_GROUPS * DIL_HEADS * DIL_HEAD_DIM
DIL_OUT = DIL_HEADS * DIL_HEAD_DIM
ROPE_THETA = 10000.0

LANES = 128
LR_PAD = LANES

PROJ_W = 2 * GLA_QK + 2 * GLA_VW + 2 * D_MODEL
OFF_GQ = 0
OFF_GK = GLA_QK
OFF_GV = 2 * GLA_QK
OFF_GR = OFF_GV + GLA_VW
OFF_GA = OFF_GR + GLA_VW
OFF_GB = OFF_GA + D_MODEL
DIL_QKV = 3 * DIL_OUT
OFF_DIL = -(-PROJ_W // DIL_QKV) * DIL_QKV
OFF_LR = OFF_DIL + DIL_GROUPS * DIL_QKV

NEG_BIG = -1e30

VMEM_LIMIT = 56 * 1024 * 1024
VMEM_LIMIT_BIG = 60 * 1024 * 1024


def _cparams(sem, vmem=VMEM_LIMIT):
    return pltpu.CompilerParams(dimension_semantics=sem, vmem_limit_bytes=vmem)


def _sigmoid(x):
    return 1.0 / (1.0 + jnp.exp(-x))


def _layer_norm(y, g, b):
    mu = jnp.mean(y, axis=-1, keepdims=True)
    d = y - mu
    var = jnp.mean(d * d, axis=-1, keepdims=True)
    return d * lax.rsqrt(var + LN_EPS) * g + b


def _mods_kernel(c_ref, w_ref, b_ref, o_ref):
    c = c_ref[...]
    c_act = (c * _sigmoid(c)).astype(BF16)
    o_ref[...] = jnp.dot(c_act, w_ref[...].astype(BF16), preferred_element_type=F32) + b_ref[...]


def _mods(c_pad, w_ada, b_ada, tn=1024):
    rows, d = c_pad.shape
    n = w_ada.shape[1]
    return pl.pallas_call(
        _mods_kernel,
        grid=(n // tn,),
        in_specs=[
            pl.BlockSpec((rows, d), lambda j: (0, 0)),
            pl.BlockSpec((d, tn), lambda j: (0, j)),
            pl.BlockSpec((1, tn), lambda j: (0, j)),
        ],
        out_specs=pl.BlockSpec((rows, tn), lambda j: (0, j)),
        out_shape=jax.ShapeDtypeStruct((rows, n), F32),
        compiler_params=_cparams(("arbitrary",)),
        name="mods",
    )(c_pad, w_ada, b_ada)


FFN_ROW_CHUNK = 512


def _ffn_kernel(x_ref, sh_ref, sc_ref, g_ref, wg_ref, wu_ref, wd_ref, lng_ref, lnb_ref,
                o_ref, h_scr):
    j = pl.program_id(1)

    @pl.when(j == 0)
    def _():
        h_scr[...] = (x_ref[...] * (1.0 + sc_ref[0]) + sh_ref[0]).astype(BF16)
        o_ref[...] = jnp.zeros_like(o_ref)

    for r0 in range(0, o_ref.shape[0], FFN_ROW_CHUNK):
        rs = slice(r0, r0 + FFN_ROW_CHUNK)
        h = h_scr[rs, :]
        gate = jnp.dot(h, wg_ref[...], preferred_element_type=F32)
        up = jnp.dot(h, wu_ref[...], preferred_element_type=F32)
        act = (gate * _sigmoid(gate) * up).astype(BF16)
        o_ref[rs, :] += jnp.dot(act, wd_ref[...], preferred_element_type=F32)

    @pl.when(j == pl.num_programs(1) - 1)
    def _():
        y = DN_ALPHA * x_ref[...] + 0.5 * g_ref[0] * o_ref[...]
        o_ref[...] = _layer_norm(y, lng_ref[...], lnb_ref[...])


def _ffn(x2d, mods3, k_shift, k_scale, k_gate, w_gu, w_down, ln_g, ln_b, seq, tm=1024, tf=512):
    m, d = x2d.shape
    n_ff = w_down.shape[0]
    nj = n_ff // tf
    tiles_per_seq = seq // tm

    def mod_spec(k):
        return pl.BlockSpec((1, 1, d), lambda i, j: ((i // tiles_per_seq) * N_MOD + k, 0, 0))

    return pl.pallas_call(
        _ffn_kernel,
        grid=(m // tm, nj),
        in_specs=[
            pl.BlockSpec((tm, d), lambda i, j: (i, 0)),
            mod_spec(k_shift), mod_spec(k_scale), mod_spec(k_gate),
            pl.BlockSpec((d, tf), lambda i, j: (0, j)),
            pl.BlockSpec((d, tf), lambda i, j: (0, j + nj)),
            pl.BlockSpec((tf, d), lambda i, j: (j, 0)),
            pl.BlockSpec((1, d), lambda i, j: (0, 0)),
            pl.BlockSpec((1, d), lambda i, j: (0, 0)),
        ],
        out_specs=pl.BlockSpec((tm, d), lambda i, j: (i, 0)),
        out_shape=jax.ShapeDtypeStruct((m, d), F32),
        scratch_shapes=[pltpu.VMEM((tm, d), BF16)],
        compiler_params=_cparams(("parallel", "arbitrary"), VMEM_LIMIT_BIG),
        name="ffn",
    )(x2d, mods3, mods3, mods3, w_gu, w_gu, w_down, ln_g, ln_b)


PROJ_TM = 1024


def _rope_kernel(pos_ref, freq_ref, *out_refs, dilations):
    cos_ref, sin_ref = out_refs[0], out_refs[1]
    tm = cos_ref.shape[0]
    pos_t = pos_ref[0].astype(F32).T
    lane = lax.broadcasted_iota(jnp.int32, (LANES, DIL_HEAD_DIM), 1)
    for k in range(tm // LANES):
        ang = pos_t[:, k:k + 1] * freq_ref[...]
        s = jnp.sin(ang)
        cos_ref[k * LANES:(k + 1) * LANES, :] = jnp.cos(ang)
        sin_ref[k * LANES:(k + 1) * LANES, :] = jnp.where(lane < DIL_HEAD_DIM // 2, -s, s)
    for gi, r in enumerate(dilations[1:]):
        tl = tm // r
        for c in range(r):
            out_refs[2 + 2 * gi][c * tl:(c + 1) * tl, :] = cos_ref[pl.ds(c, tl, stride=r), :]
            out_refs[3 + 2 * gi][c * tl:(c + 1) * tl, :] = sin_ref[pl.ds(c, tl, stride=r), :]


def _rope_tables(pos3, freq2, dilations, tm=PROJ_TM):
    assert dilations[0] == 1
    m = pos3.shape[0] * tm
    n_out = 2 * len(dilations)
    return pl.pallas_call(
        functools.partial(_rope_kernel, dilations=dilations),
        grid=(m // tm,),
        in_specs=[
            pl.BlockSpec((1, tm // LANES, LANES), lambda i: (i, 0, 0)),
            pl.BlockSpec((1, DIL_HEAD_DIM), lambda i: (0, 0)),
        ],
        out_specs=[pl.BlockSpec((tm, DIL_HEAD_DIM), lambda i: (i, 0))] * n_out,
        out_shape=[jax.ShapeDtypeStruct((m, DIL_HEAD_DIM), F32)] * n_out,
        compiler_params=_cparams(("parallel",)),
        name="rope_tables",
    )(pos3, freq2)


def _proj_kernel(x_ref, sh_ref, sc_ref, w_ref, wlr_ref, o_ref, lr_ref, h_scr):
    @pl.when(pl.program_id(1) == 0)
    def _():
        h = (x_ref[...] * (1.0 + sc_ref[0]) + sh_ref[0]).astype(BF16)
        h_scr[...] = h
        lr_ref[...] = jnp.dot(h, wlr_ref[...], preferred_element_type=F32)

    o_ref[...] = jnp.dot(h_scr[...], w_ref[...], preferred_element_type=F32).astype(BF16)


def _proj(x2d, mods3, k_shift, k_scale, w_all, seq, tm=PROJ_TM, tn=2048):
    m, d = x2d.shape
    n = PROJ_W
    tiles_per_seq = seq // tm

    def mod_spec(k):
        return pl.BlockSpec((1, 1, d), lambda i, j: ((i // tiles_per_seq) * N_MOD + k, 0, 0))

    return pl.pallas_call(
        _proj_kernel,
        grid=(m // tm, n // tn),
        in_specs=[
            pl.BlockSpec((tm, d), lambda i, j: (i, 0)),
            mod_spec(k_shift), mod_spec(k_scale),
            pl.BlockSpec((d, tn), lambda i, j: (0, j)),
            pl.BlockSpec((d, LR_PAD), lambda i, j: (0, OFF_LR // LR_PAD)),
        ],
        out_specs=[
            pl.BlockSpec((tm, tn), lambda i, j: (i, j)),
            pl.BlockSpec((tm, LR_PAD), lambda i, j: (i, 0)),
        ],
        out_shape=[
            jax.ShapeDtypeStruct((m, n), BF16),
            jax.ShapeDtypeStruct((m, LR_PAD), F32),
        ],
        scratch_shapes=[pltpu.VMEM((tm, d), BF16)],
        compiler_params=_cparams(("parallel", "arbitrary")),
        name="proj",
    )(x2d, mods3, mods3, w_all, w_all)


def _proj_dil_kernel(x_ref, sh_ref, sc_ref, w_ref, cos_ref, sin_ref, o_ref, h_scr, slab_scr, *, r):
    tm, d = x_ref.shape
    tl = tm // r
    if r == 1:
        h_scr[...] = (x_ref[...] * (1.0 + sc_ref[0]) + sh_ref[0]).astype(BF16)
    else:
        for k in range(d // LANES):
            sl = slice(k * LANES, (k + 1) * LANES)
            slab_scr[k] = x_ref[:, sl] * (1.0 + sc_ref[0, :, sl]) + sh_ref[0, :, sl]
        for c in range(r):
            for k in range(d // LANES):
                sl = slice(k * LANES, (k + 1) * LANES)
                h_scr[c * tl:(c + 1) * tl, sl] = slab_scr[k, pl.ds(c, tl, stride=r), :].astype(BF16)
    _proj_dil_qkv(h_scr, w_ref, cos_ref, sin_ref, o_ref, r)


def _proj_dil_qkv(h_scr, w_ref, cos_ref, sin_ref, o_ref, r):
    tl = h_scr.shape[0] // r
    for part, scale in enumerate((DIL_HEAD_DIM ** -0.5, 1.0, None)):
        cols = slice(part * DIL_OUT, (part + 1) * DIL_OUT)
        t = jnp.dot(h_scr[...], w_ref[:, cols], preferred_element_type=F32)
        if scale is None:
            for c in range(r):
                o_ref[0, c, :, cols] = t[c * tl:(c + 1) * tl, :].astype(BF16)
            continue
        cos = cos_ref[...] * scale
        sin = sin_ref[...] * scale
        for s in range(DIL_HEADS):
            ts = t[:, s * DIL_HEAD_DIM:(s + 1) * DIL_HEAD_DIM]
            rot = (ts * cos + pltpu.roll(ts, DIL_HEAD_DIM // 2, 1) * sin).astype(BF16)
            lo = part * DIL_OUT + s * DIL_HEAD_DIM
            for c in range(r):
                o_ref[0, c, :, lo:lo + DIL_HEAD_DIM] = rot[c * tl:(c + 1) * tl, :]


def _proj_dil_dma_kernel(x_hbm, sh_ref, sc_ref, w_ref, cos_ref, sin_ref, o_ref, h_scr, x_buf, sem,
                         *, r):
    i = pl.program_id(0)
    tl = h_scr.shape[0] // r
    slot = i % 2

    def tile_copies(tile, buf_slot):
        return [pltpu.make_async_copy(x_hbm.at[pl.ds(tile * tl, tl), c], x_buf.at[buf_slot, c],
                                      sem.at[buf_slot]) for c in range(r)]

    @pl.when(i == 0)
    def _():
        for cp in tile_copies(0, 0):
            cp.start()

    @pl.when(i + 1 < pl.num_programs(0))
    def _():
        for cp in tile_copies(i + 1, 1 - slot):
            cp.start()

    for cp in tile_copies(i, slot):
        cp.wait()
    for c in range(r):
        h_scr[c * tl:(c + 1) * tl, :] = (x_buf[slot, c] * (1.0 + sc_ref[0]) + sh_ref[0]).astype(BF16)
    _proj_dil_qkv(h_scr, w_ref, cos_ref, sin_ref, o_ref, r)


SUBLANES_F32 = 8


def _proj_dil(x2d, mods3, k_shift, k_scale, w_all, g, cos_t, sin_t, bsz, seq, r, tm=PROJ_TM):
    m, d = x2d.shape
    tiles_per_seq = seq // tm
    tl = tm // r
    dma_gather = r % SUBLANES_F32 == 0

    def mod_spec(k):
        return pl.BlockSpec((1, 1, d), lambda i: ((i // tiles_per_seq) * N_MOD + k, 0, 0))

    if dma_gather:
        body = functools.partial(_proj_dil_dma_kernel, r=r)
        x_arg = x2d.reshape(m // r, r, d)
        x_spec = pl.BlockSpec(memory_space=pl.ANY)
        scratch = [pltpu.VMEM((tm, d), BF16), pltpu.VMEM((2, r, tl, d), F32),
                   pltpu.SemaphoreType.DMA((2,))]
        semantics = ("arbitrary",)
    else:
        body = functools.partial(_proj_dil_kernel, r=r)
        x_arg = x2d
        x_spec = pl.BlockSpec((tm, d), lambda i: (i, 0))
        scratch = [pltpu.VMEM((tm, d), BF16),
                   pltpu.VMEM((d // LANES, tm, LANES) if r > 1 else (1, 8, LANES), F32)]
        semantics = ("parallel",)

    return pl.pallas_call(
        body,
        grid=(m // tm,),
        in_specs=[
            x_spec,
            mod_spec(k_shift), mod_spec(k_scale),
            pl.BlockSpec((d, DIL_QKV), lambda i: (0, OFF_DIL // DIL_QKV + g),
                         pipeline_mode=pl.Buffered(1)),
            pl.BlockSpec((tm, DIL_HEAD_DIM), lambda i: (i, 0)),
            pl.BlockSpec((tm, DIL_HEAD_DIM), lambda i: (i, 0)),
        ],
        out_specs=pl.BlockSpec((1, r, tl, DIL_QKV),
                               lambda i: (i // tiles_per_seq, 0, i % tiles_per_seq, 0)),
        out_shape=jax.ShapeDtypeStruct((bsz, r, seq // r, DIL_QKV), BF16),
        scratch_shapes=scratch,
        compiler_params=_cparams(semantics, VMEM_LIMIT_BIG),
        name=f"proj_dil_r{r}",
    )(x_arg, mods3, mods3, w_all, cos_t, sin_t)


def _gla_kernel(q_ref, k_ref, v_ref, r_ref, lr_ref, wa_ref, ba_ref, ng_ref, o_ref,
                st_scr, b_scr, qd_scr, qh_scr, ki_scr, ke_scr, kp_scr, kh_scr, a_scr, *, n_chunks):
    c_len = GLA_CHUNK
    tc = q_ref.shape[1]
    hb = 4 * c_len
    pw = 2 * c_len
    n_pairs = n_chunks // 2

    @pl.when(pl.program_id(2) == 0)
    def _():
        st_scr[...] = jnp.zeros_like(st_scr)

    ti = lax.broadcasted_iota(jnp.int32, (hb, hb), 0)
    tj = lax.broadcasted_iota(jnp.int32, (hb, hb), 1)
    tri = jnp.logical_and(ti >= tj, ti // c_len == tj // c_len).astype(BF16)
    for s in range(tc // hb):
        rows = slice(s * hb, (s + 1) * hb)
        logits = jnp.dot(lr_ref[0, rows, :].astype(BF16), wa_ref[...],
                         preferred_element_type=F32) + ba_ref[...]
        log_a = (jnp.minimum(logits, 0.0) - jnp.log(1.0 + jnp.exp(-jnp.abs(logits)))) / GLA_TAU
        p1 = log_a.astype(BF16)
        r1 = log_a - p1.astype(F32)
        p2 = r1.astype(BF16)
        p3 = (r1 - p2.astype(F32)).astype(BF16)
        b_scr[rows, :] = (jnp.dot(tri, p1, preferred_element_type=F32)
                          + jnp.dot(tri, p2, preferred_element_type=F32)
                          + jnp.dot(tri, p3, preferred_element_type=F32))

    nt = (((1,), (1,)), ((), ()))

    def chunk_rows(c):
        return slice(c * c_len, (c + 1) * c_len)

    def pair_rows(p):
        return slice(p * pw, (p + 1) * pw)

    b_last = [b_scr[(c + 1) * c_len - 1:(c + 1) * c_len, :] for c in range(n_chunks)]
    h_log = [jnp.zeros_like(b_last[0])]
    for c in range(n_chunks):
        h_log.append(h_log[-1] + b_last[c])

    for c in range(n_chunks):
        rows = chunk_rows(c)
        b = b_scr[rows, :]
        q = q_ref[0, rows, :].astype(F32)
        k = k_ref[0, rows, :].astype(F32)
        qd = q * jnp.exp(b) * (GLA_DK ** -0.5)
        ke = k * jnp.exp(b_last[c] - b)
        qd_scr[rows, :] = qd
        qh_scr[rows, :] = (qd * jnp.exp(h_log[c])).astype(BF16)
        ki_scr[rows, :] = (k * jnp.exp(-b)).astype(BF16)
        ke_scr[rows, :] = ke.astype(BF16)
        kp_scr[rows, :] = (ke * jnp.exp(b_last[c + 1]) if c % 2 == 0 else ke).astype(BF16)
        kh_scr[rows, :] = (ke * jnp.exp(h_log[n_chunks] - h_log[c + 1])).astype(BF16)

    pi = lax.broadcasted_iota(jnp.int32, (pw, pw), 0)
    pj = lax.broadcasted_iota(jnp.int32, (pw, pw), 1)
    diag = jnp.logical_and(pi >= pj, pi // c_len == pj // c_len)
    lower = jnp.logical_and(pi >= c_len, pj < c_len)
    for p in range(n_pairs):
        first = chunk_rows(2 * p)
        keys = jnp.concatenate([ki_scr[pair_rows(p), :], ke_scr[first, :], ke_scr[first, :]], axis=0)
        s = lax.dot_general(qd_scr[pair_rows(p), :].astype(BF16), keys, nt,
                            preferred_element_type=F32)
        s = jnp.where(diag, s[:, :pw], jnp.where(lower, s[:, pw:], 0.0))
        a_scr[pair_rows(p), pair_rows(p)] = s.astype(BF16)

    for p in range(n_pairs - 1):
        c0 = 2 * p + 2
        lhs = jnp.concatenate(
            [(qd_scr[chunk_rows(c), :] * jnp.exp(h_log[c] - h_log[c0])).astype(BF16)
             for c in range(c0, n_chunks)], axis=0)
        s = lax.dot_general(lhs, kp_scr[pair_rows(p), :], nt, preferred_element_type=F32)
        a_scr[c0 * c_len:, pair_rows(p)] = s.astype(BF16)

    half = 2 * pw
    for hf in range(tc // half):
        a_scr[hf * half:hf * half + pw, hf * half + pw:(hf + 1) * half] = jnp.zeros((pw, pw), BF16)

    st = st_scr[...]
    st_b = st.astype(BF16)
    for hf in range(tc // half):
        rows = slice(hf * half, (hf + 1) * half)
        keys = (hf + 1) * half
        o = jnp.dot(a_scr[rows, :keys], v_ref[0, :keys, :], preferred_element_type=F32)
        o += jnp.dot(qh_scr[rows, :], st_b, preferred_element_type=F32)
        o = o * lax.rsqrt(jnp.mean(o * o, axis=-1, keepdims=True) + LN_EPS) * ng_ref[0]
        r = r_ref[0, rows, :].astype(F32)
        o_ref[0, rows, :] = (o * (r * _sigmoid(r))).astype(BF16)

    d_col = jnp.broadcast_to(jnp.exp(h_log[n_chunks]), (8, GLA_DK)).T[:, :1]
    st_scr[...] = st * d_col + lax.dot_general(
        kh_scr[...], v_ref[0], (((0,), (0,)), ((), ())), preferred_element_type=F32)


def _gla(p3, lr3, w_a2, b_a, norm_g3, tc=512):
    bsz, seq, _ = p3.shape
    nq, nv, nr = OFF_GK // GLA_DK, OFF_GV // GLA_DV, OFF_GR // GLA_DV
    return pl.pallas_call(
        functools.partial(_gla_kernel, n_chunks=tc // GLA_CHUNK),
        grid=(bsz, GLA_HEADS, seq // tc),
        in_specs=[
            pl.BlockSpec((1, tc, GLA_DK), lambda b, h, t: (b, t, h)),
            pl.BlockSpec((1, tc, GLA_DK), lambda b, h, t: (b, t, nq + h)),
            pl.BlockSpec((1, tc, GLA_DV), lambda b, h, t: (b, t, nv + h)),
            pl.BlockSpec((1, tc, GLA_DV), lambda b, h, t: (b, t, nr + h)),
            pl.BlockSpec((1, tc, LR_PAD), lambda b, h, t: (b, t, 0)),
            pl.BlockSpec((LR_PAD, GLA_DK), lambda b, h, t: (0, h)),
            pl.BlockSpec((1, GLA_DK), lambda b, h, t: (0, h)),
            pl.BlockSpec((1, 1, GLA_DV), lambda b, h, t: (h, 0, 0)),
        ],
        out_specs=pl.BlockSpec((1, tc, GLA_DV), lambda b, h, t: (b, t, h)),
        out_shape=jax.ShapeDtypeStruct((bsz, seq, GLA_VW), BF16),
        scratch_shapes=[pltpu.VMEM((GLA_DK, GLA_DV), F32), pltpu.VMEM((tc, GLA_DK), F32),
                        pltpu.VMEM((tc, GLA_DK), F32)]
                       + [pltpu.VMEM((tc, GLA_DK), BF16)] * 5 + [pltpu.VMEM((tc, tc), BF16)],
        compiler_params=_cparams(("parallel", "parallel", "arbitrary")),
        name="gla",
    )(p3, p3, p3, p3, lr3, w_a2, b_a, norm_g3)


def _dil_kernel(q_ref, k_ref, v_ref, kp_ref, vp_ref, o_ref, lse_ref, s_scr, m_scr, p_scr, *, nblk):
    n = pl.program_id(2)
    blk = DIL_BLOCK
    qi = lax.broadcasted_iota(jnp.int32, (blk, 2 * blk), 0)
    kj = lax.broadcasted_iota(jnp.int32, (blk, 2 * blk), 1)
    band = jnp.logical_or(jnp.logical_and(kj < blk, kj >= qi),
                          jnp.logical_and(kj >= blk, kj - blk <= qi))
    band_first = jnp.logical_and(band, jnp.logical_or(kj >= blk, n > 0))
    nt = (((1,), (1,)), ((), ()))

    def rows(i):
        return slice(i * blk, (i + 1) * blk)

    def cols(h):
        return slice(h * DIL_HEAD_DIM, (h + 1) * DIL_HEAD_DIM)

    for i in range(nblk):
        for h in range(DIL_HEADS):
            q = q_ref[0, 0, rows(i), cols(h)]
            k_prev = kp_ref[0, 0, :, cols(h)] if i == 0 else k_ref[0, 0, rows(i - 1), cols(h)]
            s_p = lax.dot_general(q, k_prev, nt, preferred_element_type=F32)
            s_c = lax.dot_general(q, k_ref[0, 0, rows(i), cols(h)], nt, preferred_element_type=F32)
            s = jnp.concatenate([s_p, s_c], axis=1)
            s_scr[rows(i * DIL_HEADS + h), :] = jnp.where(band_first if i == 0 else band, s, NEG_BIG)

    s_all = s_scr[...]
    m_all = jnp.max(s_all, axis=-1, keepdims=True)
    p_scr[...] = jnp.exp(s_all - m_all).astype(BF16)
    m_scr[...] = jnp.broadcast_to(m_all, m_scr.shape)

    lane = lax.broadcasted_iota(jnp.int32, (blk, LANES), 1)
    ones = jnp.ones((2 * blk, DIL_HEAD_DIM), BF16)
    for i in range(nblk):
        lse_tile = jnp.zeros((blk, LANES), F32)
        for h in range(DIL_HEADS):
            u = i * DIL_HEADS + h
            v_prev = vp_ref[0, 0, :, cols(h)] if i == 0 else v_ref[0, 0, rows(i - 1), cols(h)]
            v_aug = jnp.concatenate(
                [jnp.concatenate([v_prev, v_ref[0, 0, rows(i), cols(h)]], axis=0), ones], axis=1)
            acc = jnp.dot(p_scr[rows(u), :], v_aug, preferred_element_type=F32)
            den = acc[:, DIL_HEAD_DIM:]
            o_ref[0, 0, rows(i), cols(h)] = (acc[:, :DIL_HEAD_DIM] / den).astype(BF16)
            lse_tile = jnp.where(lane == h, m_scr[rows(u), :] + jnp.log(den), lse_tile)
        lse_ref[0, 0, rows(i), :] = lse_tile


def _dil_group(qkv, nblk=4):
    bsz, r, length, _ = qkv.shape
    nb = length // DIL_BLOCK
    nblk = min(nblk, nb)
    assert nb % nblk == 0
    tq = nblk * DIL_BLOCK
    units = nblk * DIL_HEADS * DIL_BLOCK

    def cur_spec(u):
        return pl.BlockSpec((1, 1, tq, DIL_OUT), lambda b, c, n: (b, c, n, u))

    def prev_spec(u):
        return pl.BlockSpec((1, 1, DIL_BLOCK, DIL_OUT),
                            lambda b, c, n: (b, c, jnp.maximum(n * nblk - 1, 0), u))

    return pl.pallas_call(
        functools.partial(_dil_kernel, nblk=nblk),
        grid=(bsz, r, nb // nblk),
        in_specs=[cur_spec(0), cur_spec(1), cur_spec(2), prev_spec(1), prev_spec(2)],
        out_specs=[
            pl.BlockSpec((1, 1, tq, DIL_OUT), lambda b, c, n: (b, c, n, 0)),
            pl.BlockSpec((1, 1, tq, LANES), lambda b, c, n: (b, c, n, 0)),
        ],
        out_shape=[
            jax.ShapeDtypeStruct((bsz, r, length, DIL_OUT), BF16),
            jax.ShapeDtypeStruct((bsz, r, length, LANES), F32),
        ],
        scratch_shapes=[pltpu.VMEM((units, 2 * DIL_BLOCK), F32),
                        pltpu.VMEM((units, LANES), F32),
                        pltpu.VMEM((units, 2 * DIL_BLOCK), BF16)],
        compiler_params=_cparams(("parallel", "parallel", "parallel")),
        name=f"dilattn_r{r}",
    )(qkv, qkv, qkv, qkv, qkv)


def _merge_kernel(oa_ref, o1_ref, o2_ref, o3_ref, l1_ref, l2_ref, l3_ref, ga_ref, gb_ref, x_ref,
                  g2_ref, wa_ref, wb_ref, wo_ref, lng_ref, lnb_ref, out_ref, l_scr, w_scr, ob_scr,
                  *, dilations):
    y_a = jnp.dot(oa_ref[...], wa_ref[...], preferred_element_type=F32)
    o_refs = (o1_ref, o2_ref, o3_ref)
    l_refs = (l1_ref, l2_ref, l3_ref)
    tm = oa_ref.shape[0]

    def tok_rows(c, r):
        return pl.ds(c, tm // r, stride=r) if r > 1 else slice(None)

    for g, r in enumerate(dilations):
        for c in range(r):
            l_scr[g, tok_rows(c, r), :] = l_refs[g][0, c]
    l1, l2, l3 = l_scr[0], l_scr[1], l_scr[2]
    mx = jnp.maximum(jnp.maximum(l1, l2), l3)
    e1, e2, e3 = jnp.exp(l1 - mx), jnp.exp(l2 - mx), jnp.exp(l3 - mx)
    inv = 1.0 / (e1 + e2 + e3)
    w_scr[0], w_scr[1], w_scr[2] = e1 * inv, e2 * inv, e3 * inv
    for g, r in enumerate(dilations):
        for c in range(r):
            rows = tok_rows(c, r)
            wr = w_scr[g, rows, :]
            for h in range(DIL_HEADS):
                sl = slice(h * DIL_HEAD_DIM, (h + 1) * DIL_HEAD_DIM)
                part = wr[:, h:h + 1] * o_refs[g][0, c, :, sl].astype(F32)
                if g == 0:
                    ob_scr[h, rows, :] = part
                else:
                    ob_scr[h, rows, :] += part
    o_b = jnp.concatenate([ob_scr[h].astype(BF16) for h in range(DIL_HEADS)], axis=1)
    y_b = jnp.dot(o_b, wb_ref[...], preferred_element_type=F32)
    merged = _sigmoid(ga_ref[...].astype(F32)) * y_a + _sigmoid(gb_ref[...].astype(F32)) * y_b
    mix = jnp.dot(merged.astype(BF16), wo_ref[...], preferred_element_type=F32)
    y = DN_ALPHA * x_ref[...] + g2_ref[0] * mix
    out_ref[...] = _layer_norm(y, lng_ref[...], lnb_ref[...])


def _merge(oa2d, o_groups, lse_groups, p2d, x2d, mods3, k_gate, w_a, w_b, w_o, ln_g, ln_b, seq,
           dilations, tm=256):
    m, d = x2d.shape
    tiles_per_seq = seq // tm
    resident = pl.Buffered(1)
    n_g = len(dilations)

    def row_spec(width, col=0):
        return pl.BlockSpec((tm, width), lambda i: (i, col))

    def class_spec(r, width):
        return pl.BlockSpec((1, r, tm // r, width),
                            lambda i: (i // tiles_per_seq, 0, i % tiles_per_seq, 0))

    def const_spec(shape):
        return pl.BlockSpec(shape, lambda i: (0, 0), pipeline_mode=resident)

    return pl.pallas_call(
        functools.partial(_merge_kernel, dilations=dilations),
        grid=(m // tm,),
        in_specs=[
            row_spec(GLA_VW),
            *[class_spec(r, DIL_OUT) for r in dilations],
            *[class_spec(r, LANES) for r in dilations],
            row_spec(D_MODEL, OFF_GA // D_MODEL), row_spec(D_MODEL, OFF_GB // D_MODEL),
            row_spec(d),
            pl.BlockSpec((1, 1, d), lambda i: ((i // tiles_per_seq) * N_MOD + k_gate, 0, 0)),
            const_spec(w_a.shape), const_spec(w_b.shape), const_spec(w_o.shape),
            const_spec((1, d)), const_spec((1, d)),
        ],
        out_specs=row_spec(d),
        out_shape=jax.ShapeDtypeStruct((m, d), F32),
        scratch_shapes=[pltpu.VMEM((n_g, tm, LANES), F32), pltpu.VMEM((n_g, tm, LANES), F32),
                        pltpu.VMEM((DIL_HEADS, tm, DIL_HEAD_DIM), F32)],
        compiler_params=_cparams(("parallel",)),
        name="merge",
    )(oa2d, *o_groups, *lse_groups, p2d, p2d, x2d, mods3, w_a, w_b, w_o, ln_g, ln_b)


PACK_W = 1024
PACK_COPY, PACK_SHIFTED, PACK_ZERO, PACK_LOW_RANK = range(4)


def _pack_plan():
    src, o = {}, 0
    for name, wdt in zip(("gq", "gk", "gv", "gr", "glr", "dq", "dk", "dv", "ga", "gb"),
                         (GLA_QK, GLA_QK, GLA_VW, GLA_VW, GLA_GATE_RANK, DIL_W, DIL_W, DIL_W,
                          D_MODEL, D_MODEL)):
        src[name] = (o, wdt)
        o += wdt
    plan = []

    def copy(first, width):
        for c in range(first, first + width, PACK_W):
            plan.append((c, PACK_COPY if c % PACK_W == 0 else PACK_SHIFTED))

    for name in ("gq", "gk", "gv", "gr", "ga", "gb"):
        copy(*src[name])
    plan += [(0, PACK_ZERO)] * ((OFF_DIL - PROJ_W) // PACK_W)
    for g in range(DIL_GROUPS):
        for name in ("dq", "dk", "dv"):
            copy(src[name][0] + g * DIL_OUT, DIL_OUT)
    plan.append((src["glr"][0], PACK_LOW_RANK))
    return plan


def _pack_kernel(plan_ref, a_ref, b_ref, o_ref):
    kind = plan_ref[2, pl.program_id(0)]
    rows = a_ref.shape[0]
    lane = lax.broadcasted_iota(jnp.int32, (rows, LANES), 1)
    keep = LANES - GLA_GATE_RANK

    @pl.when(kind == PACK_COPY)
    def _():
        o_ref[...] = a_ref[...].astype(BF16)

    @pl.when(kind == PACK_ZERO)
    def _():
        o_ref[...] = jnp.zeros_like(o_ref)

    @pl.when(kind == PACK_LOW_RANK)
    def _():
        o_ref[...] = jnp.zeros_like(o_ref)
        o_ref[:, :LANES] = jnp.where(lane < GLA_GATE_RANK, a_ref[:, :LANES], 0.0).astype(BF16)

    @pl.when(kind == PACK_SHIFTED)
    def _():
        cur = pltpu.roll(a_ref[:, :LANES], keep, 1)
        for k in range(PACK_W // LANES):
            nxt_src = a_ref[:, (k + 1) * LANES:(k + 2) * LANES] if (k + 1) * LANES < PACK_W else b_ref[...]
            nxt = pltpu.roll(nxt_src, keep, 1)
            o_ref[:, k * LANES:(k + 1) * LANES] = jnp.where(lane < keep, cur, nxt).astype(BF16)
            cur = nxt


def _pack_w_in(w_in):
    d, width = w_in.shape
    plan = _pack_plan()
    first = [c - c % PACK_W for c, _ in plan]
    assert all(c % PACK_W in (0, GLA_GATE_RANK) for c, _ in plan)
    table = jnp.asarray([[c // PACK_W for c in first],
                         [min((c + PACK_W) // LANES, (width - 1) // LANES) for c in first],
                         [kind for _, kind in plan]], jnp.int32)
    return pl.pallas_call(
        _pack_kernel,
        grid_spec=pltpu.PrefetchScalarGridSpec(
            num_scalar_prefetch=1,
            grid=(len(plan),),
            in_specs=[pl.BlockSpec((d, PACK_W), lambda i, t: (0, t[0, i])),
                      pl.BlockSpec((d, LANES), lambda i, t: (0, t[1, i]))],
            out_specs=pl.BlockSpec((d, PACK_W), lambda i, t: (0, i)),
        ),
        out_shape=jax.ShapeDtypeStruct((d, len(plan) * PACK_W), BF16),
        compiler_params=_cparams(("arbitrary",)),
        name="pack_w_in",
    )(table, w_in, w_in)


def kernel(x, c, positions, w_ada, b_ada, ln1_g, ln1_b, w_ffn1_gu, w_ffn1_down, w_in, w_alpha2,
           b_alpha, gla_norm_g, w_branch_a, w_branch_b, w_out, ln2_g, ln2_b, w_ffn2_gu, w_ffn2_down,
           ln3_g, ln3_b):
    bsz, seq, d = x.shape
    m = bsz * seq
    x2d = x.reshape(m, d)
    c_pad = jnp.pad(c, ((0, 8 - bsz % 8 if bsz % 8 else 0), (0, 0)))
    dilations = tuple(r for _, r in DIL_PATTERNS)

    half = DIL_HEAD_DIM // 2
    freq = ROPE_THETA ** (-jnp.arange(half, dtype=F32) / half)
    freq2 = jnp.concatenate([freq, freq]).reshape(1, DIL_HEAD_DIM)
    rope_t = _rope_tables(positions.reshape(m // PROJ_TM, PROJ_TM // LANES, LANES), freq2, dilations)

    for l in range(DEPTH):
        mods = _mods(c_pad, w_ada[l], b_ada[l].reshape(1, -1))[:bsz]
        mods3 = mods.reshape(bsz * N_MOD, 1, d)

        x1 = _ffn(x2d, mods3, 0, 1, 2, w_ffn1_gu[l].astype(BF16), w_ffn1_down[l].astype(BF16),
                  ln1_g[l].reshape(1, d), ln1_b[l].reshape(1, d), seq)

        w_all = _pack_w_in(w_in[l])
        p2d, lr2d = _proj(x1, mods3, 3, 4, w_all, seq)
        p3 = p2d.reshape(bsz, seq, PROJ_W)

        w_a2 = jnp.pad(w_alpha2[l], ((0, LR_PAD - GLA_GATE_RANK), (0, 0))).astype(BF16)
        o_a = _gla(p3, lr2d.reshape(bsz, seq, LR_PAD), w_a2, b_alpha[l].reshape(1, GLA_QK),
                   gla_norm_g[l].reshape(GLA_HEADS, 1, GLA_DV))

        o_groups, lse_groups = [], []
        for g, r in enumerate(dilations):
            qkv = _proj_dil(x1, mods3, 3, 4, w_all, g, rope_t[2 * g], rope_t[2 * g + 1],
                            bsz, seq, r)
            o_g, lse_g = _dil_group(qkv)
            o_groups.append(o_g)
            lse_groups.append(lse_g)

        x2 = _merge(o_a.reshape(m, GLA_VW), o_groups, lse_groups, p2d, x1, mods3, 5,
                    w_branch_a[l].astype(BF16), w_branch_b[l].astype(BF16), w_out[l].astype(BF16),
                    ln2_g[l].reshape(1, d), ln2_b[l].reshape(1, d), seq, dilations)

        x2d = _ffn(x2, mods3, 6, 7, 8, w_ffn2_gu[l].astype(BF16), w_ffn2_down[l].astype(BF16),
                   ln3_g[l].reshape(1, d), ln3_b[l].reshape(1, d), seq)
    return x2d.reshape(bsz, seq, d)
```

```python
import functools

import jax
import jax.numpy as jnp
from jax import lax
from jax.experimental import pallas as pl
from jax.experimental.pallas import tpu as pltpu

F32 = jnp.float32
BF16 = jnp.bfloat16

D_MODEL = 2048
DEPTH = 1
D_FF = 5632
N_MOD = 9
LN_EPS = 1e-5
DN_ALPHA = (2.0 * DEPTH) ** 0.25

GLA_HEADS = 4
GLA_DK = 256
GLA_DV = 512
GLA_GATE_RANK = 16
GLA_TAU = 16.0
GLA_CHUNK = 64
GLA_QK = GLA_HEADS * GLA_DK
GLA_VW = GLA_HEADS * GLA_DV

DIL_PATTERNS = ((128, 1), (512, 4), (2048, 16))
DIL_GROUPS = len(DIL_PATTERNS)
DIL_HEADS = 8
DIL_HEAD_DIM = 128
DIL_BLOCK = 128
DIL_W = DIL_GROUPS * DIL_HEADS * DIL_HEAD_DIM
DIL_OUT = DIL_HEADS * DIL_HEAD_DIM
ROPE_THETA = 10000.0

LANES = 128
LR_PAD = LANES

PROJ_W = 2 * GLA_QK + 2 * GLA_VW + 2 * D_MODEL
OFF_GQ = 0
OFF_GK = GLA_QK
OFF_GV = 2 * GLA_QK
OFF_GR = OFF_GV + GLA_VW
OFF_GA = OFF_GR + GLA_VW
OFF_GB = OFF_GA + D_MODEL
DIL_QKV = 3 * DIL_OUT
OFF_DIL = -(-PROJ_W // DIL_QKV) * DIL_QKV
OFF_LR = OFF_DIL + DIL_GROUPS * DIL_QKV

NEG_BIG = -1e30

VMEM_LIMIT = 56 * 1024 * 1024
VMEM_LIMIT_BIG = 60 * 1024 * 1024


def _cparams(sem, vmem=VMEM_LIMIT):
    return pltpu.CompilerParams(dimension_semantics=sem, vmem_limit_bytes=vmem)


def _sigmoid(x):
    return 1.0 / (1.0 + jnp.exp(-x))


def _layer_norm(y, g, b):
    mu = jnp.mean(y, axis=-1, keepdims=True)
    d = y - mu
    var = jnp.mean(d * d, axis=-1, keepdims=True)
    return d * lax.rsqrt(var + LN_EPS) * g + b


def _mods_kernel(c_ref, w_ref, b_ref, o_ref):
    c = c_ref[...]
    c_act = (c * _sigmoid(c)).astype(BF16)
    o_ref[...] = jnp.dot(c_act, w_ref[...].astype(BF16), preferred_element_type=F32) + b_ref[...]


def _mods(c_pad, w_ada, b_ada, tn=1024):
    rows, d = c_pad.shape
    n = w_ada.shape[1]
    return pl.pallas_call(
        _mods_kernel,
        grid=(n // tn,),
        in_specs=[
            pl.BlockSpec((rows, d), lambda j: (0, 0)),
            pl.BlockSpec((d, tn), lambda j: (0, j)),
            pl.BlockSpec((1, tn), lambda j: (0, j)),
        ],
        out_specs=pl.BlockSpec((rows, tn), lambda j: (0, j)),
        out_shape=jax.ShapeDtypeStruct((rows, n), F32),
        compiler_params=_cparams(("arbitrary",)),
        name="mods",
    )(c_pad, w_ada, b_ada)


FFN_ROW_CHUNK = 512


def _ffn_kernel(x_ref, sh_ref, sc_ref, g_ref, wg_ref, wu_ref, wd_ref, lng_ref, lnb_ref,
                o_ref, h_scr):
    j = pl.program_id(1)

    @pl.when(j == 0)
    def _():
        h_scr[...] = (x_ref[...] * (1.0 + sc_ref[0]) + sh_ref[0]).astype(BF16)
        o_ref[...] = jnp.zeros_like(o_ref)

    for r0 in range(0, o_ref.shape[0], FFN_ROW_CHUNK):
        rs = slice(r0, r0 + FFN_ROW_CHUNK)
        h = h_scr[rs, :]
        gate = jnp.dot(h, wg_ref[...], preferred_element_type=F32)
        up = jnp.dot(h, wu_ref[...], preferred_element_type=F32)
        act = (gate * _sigmoid(gate) * up).astype(BF16)
        o_ref[rs, :] += jnp.dot(act, wd_ref[...], preferred_element_type=F32)

    @pl.when(j == pl.num_programs(1) - 1)
    def _():
        y = DN_ALPHA * x_ref[...] + 0.5 * g_ref[0] * o_ref[...]
        o_ref[...] = _layer_norm(y, lng_ref[...], lnb_ref[...])


def _ffn(x2d, mods3, k_shift, k_scale, k_gate, w_gu, w_down, ln_g, ln_b, seq, tm=1024, tf=512):
    m, d = x2d.shape
    n_ff = w_down.shape[0]
    nj = n_ff // tf
    tiles_per_seq = seq // tm

    def mod_spec(k):
        return pl.BlockSpec((1, 1, d), lambda i, j: ((i // tiles_per_seq) * N_MOD + k, 0, 0))

    return pl.pallas_call(
        _ffn_kernel,
        grid=(m // tm, nj),
        in_specs=[
            pl.BlockSpec((tm, d), lambda i, j: (i, 0)),
            mod_spec(k_shift), mod_spec(k_scale), mod_spec(k_gate),
            pl.BlockSpec((d, tf), lambda i, j: (0, j)),
            pl.BlockSpec((d, tf), lambda i, j: (0, j + nj)),
            pl.BlockSpec((tf, d), lambda i, j: (j, 0)),
            pl.BlockSpec((1, d), lambda i, j: (0, 0)),
            pl.BlockSpec((1, d), lambda i, j: (0, 0)),
        ],
        out_specs=pl.BlockSpec((tm, d), lambda i, j: (i, 0)),
        out_shape=jax.ShapeDtypeStruct((m, d), F32),
        scratch_shapes=[pltpu.VMEM((tm, d), BF16)],
        compiler_params=_cparams(("parallel", "arbitrary"), VMEM_LIMIT_BIG),
        name="ffn",
    )(x2d, mods3, mods3, mods3, w_gu, w_gu, w_down, ln_g, ln_b)


PROJ_TM = 1024


def _rope_kernel(pos_ref, freq_ref, *out_refs, dilations):
    cos_ref, sin_ref = out_refs[0], out_refs[1]
    tm = cos_ref.shape[0]
    pos_t = pos_ref[0].astype(F32).T
    lane = lax.broadcasted_iota(jnp.int32, (LANES, DIL_HEAD_DIM), 1)
    for k in range(tm // LANES):
        ang = pos_t[:, k:k + 1] * freq_ref[...]
        s = jnp.sin(ang)
        cos_ref[k * LANES:(k + 1) * LANES, :] = jnp.cos(ang)
        sin_ref[k * LANES:(k + 1) * LANES, :] = jnp.where(lane < DIL_HEAD_DIM // 2, -s, s)
    for gi, r in enumerate(dilations[1:]):
        tl = tm // r
        for c in range(r):
            out_refs[2 + 2 * gi][c * tl:(c + 1) * tl, :] = cos_ref[pl.ds(c, tl, stride=r), :]
            out_refs[3 + 2 * gi][c * tl:(c + 1) * tl, :] = sin_ref[pl.ds(c, tl, stride=r), :]


def _rope_tables(pos3, freq2, dilations, tm=PROJ_TM):
    assert dilations[0] == 1
    m = pos3.shape[0] * tm
    n_out = 2 * len(dilations)
    return pl.pallas_call(
        functools.partial(_rope_kernel, dilations=dilations),
        grid=(m // tm,),
        in_specs=[
            pl.BlockSpec((1, tm // LANES, LANES), lambda i: (i, 0, 0)),
            pl.BlockSpec((1, DIL_HEAD_DIM), lambda i: (0, 0)),
        ],
        out_specs=[pl.BlockSpec((tm, DIL_HEAD_DIM), lambda i: (i, 0))] * n_out,
        out_shape=[jax.ShapeDtypeStruct((m, DIL_HEAD_DIM), F32)] * n_out,
        compiler_params=_cparams(("parallel",)),
        name="rope_tables",
    )(pos3, freq2)


def _proj_kernel(x_ref, sh_ref, sc_ref, w_ref, wlr_ref, o_ref, lr_ref, h_scr):
    @pl.when(pl.program_id(1) == 0)
    def _():
        h = (x_ref[...] * (1.0 + sc_ref[0]) + sh_ref[0]).astype(BF16)
        h_scr[...] = h
        lr_ref[...] = jnp.dot(h, wlr_ref[...], preferred_element_type=F32)

    o_ref[...] = jnp.dot(h_scr[...], w_ref[...], preferred_element_type=F32).astype(BF16)


def _proj(x2d, mods3, k_shift, k_scale, w_all, seq, tm=PROJ_TM, tn=2048):
    m, d = x2d.shape
    n = PROJ_W
    tiles_per_seq = seq // tm

    def mod_spec(k):
        return pl.BlockSpec((1, 1, d), lambda i, j: ((i // tiles_per_seq) * N_MOD + k, 0, 0))

    return pl.pallas_call(
        _proj_kernel,
        grid=(m // tm, n // tn),
        in_specs=[
            pl.BlockSpec((tm, d), lambda i, j: (i, 0)),
            mod_spec(k_shift), mod_spec(k_scale),
            pl.BlockSpec((d, tn), lambda i, j: (0, j)),
            pl.BlockSpec((d, LR_PAD), lambda i, j: (0, OFF_LR // LR_PAD)),
        ],
        out_specs=[
            pl.BlockSpec((tm, tn), lambda i, j: (i, j)),
            pl.BlockSpec((tm, LR_PAD), lambda i, j: (i, 0)),
        ],
        out_shape=[
            jax.ShapeDtypeStruct((m, n), BF16),
            jax.ShapeDtypeStruct((m, LR_PAD), F32),
        ],
        scratch_shapes=[pltpu.VMEM((tm, d), BF16)],
        compiler_params=_cparams(("parallel", "arbitrary")),
        name="proj",
    )(x2d, mods3, mods3, w_all, w_all)


def _proj_dil_kernel(x_ref, sh_ref, sc_ref, w_ref, cos_ref, sin_ref, o_ref, h_scr, slab_scr, *, r):
    tm, d = x_ref.shape
    tl = tm // r
    if r == 1:
        h_scr[...] = (x_ref[...] * (1.0 + sc_ref[0]) + sh_ref[0]).astype(BF16)
    else:
        for k in range(d // LANES):
            sl = slice(k * LANES, (k + 1) * LANES)
            slab_scr[k] = x_ref[:, sl] * (1.0 + sc_ref[0, :, sl]) + sh_ref[0, :, sl]
        for c in range(r):
            for k in range(d // LANES):
                sl = slice(k * LANES, (k + 1) * LANES)
                h_scr[c * tl:(c + 1) * tl, sl] = slab_scr[k, pl.ds(c, tl, stride=r), :].astype(BF16)
    _proj_dil_qkv(h_scr, w_ref, cos_ref, sin_ref, o_ref, r)


def _proj_dil_qkv(h_scr, w_ref, cos_ref, sin_ref, o_ref, r):
    tl = h_scr.shape[0] // r
    for part, scale in enumerate((DIL_HEAD_DIM ** -0.5, 1.0, None)):
        cols = slice(part * DIL_OUT, (part + 1) * DIL_OUT)
        t = jnp.dot(h_scr[...], w_ref[:, cols], preferred_element_type=F32)
        if scale is None:
            for c in range(r):
                o_ref[0, c, :, cols] = t[c * tl:(c + 1) * tl, :].astype(BF16)
            continue
        cos = cos_ref[...] * scale
        sin = sin_ref[...] * scale
        for s in range(DIL_HEADS):
            ts = t[:, s * DIL_HEAD_DIM:(s + 1) * DIL_HEAD_DIM]
            rot = (ts * cos + pltpu.roll(ts, DIL_HEAD_DIM // 2, 1) * sin).astype(BF16)
            lo = part * DIL_OUT + s * DIL_HEAD_DIM
            for c in range(r):
                o_ref[0, c, :, lo:lo + DIL_HEAD_DIM] = rot[c * tl:(c + 1) * tl, :]


def _proj_dil_dma_kernel(x_hbm, sh_ref, sc_ref, w_ref, cos_ref, sin_ref, o_ref, h_scr, x_buf, sem,
                         *, r):
    i = pl.program_id(0)
    tl = h_scr.shape[0] // r
    slot = i % 2

    def tile_copies(tile, buf_slot):
        return [pltpu.make_async_copy(x_hbm.at[pl.ds(tile * tl, tl), c], x_buf.at[buf_slot, c],
                                      sem.at[buf_slot]) for c in range(r)]

    @pl.when(i == 0)
    def _():
        for cp in tile_copies(0, 0):
            cp.start()

    @pl.when(i + 1 < pl.num_programs(0))
    def _():
        for cp in tile_copies(i + 1, 1 - slot):
            cp.start()

    for cp in tile_copies(i, slot):
        cp.wait()
    for c in range(r):
        h_scr[c * tl:(c + 1) * tl, :] = (x_buf[slot, c] * (1.0 + sc_ref[0]) + sh_ref[0]).astype(BF16)
    _proj_dil_qkv(h_scr, w_ref, cos_ref, sin_ref, o_ref, r)


SUBLANES_F32 = 8


def _proj_dil(x2d, mods3, k_shift, k_scale, w_all, g, cos_t, sin_t, bsz, seq, r, tm=PROJ_TM):
    m, d = x2d.shape
    tiles_per_seq = seq // tm
    tl = tm // r
    dma_gather = r % SUBLANES_F32 == 0

    def mod_spec(k):
        return pl.BlockSpec((1, 1, d), lambda i: ((i // tiles_per_seq) * N_MOD + k, 0, 0))

    if dma_gather:
        body = functools.partial(_proj_dil_dma_kernel, r=r)
        x_arg = x2d.reshape(m // r, r, d)
        x_spec = pl.BlockSpec(memory_space=pl.ANY)
        scratch = [pltpu.VMEM((tm, d), BF16), pltpu.VMEM((2, r, tl, d), F32),
                   pltpu.SemaphoreType.DMA((2,))]
        semantics = ("arbitrary",)
    else:
        body = functools.partial(_proj_dil_kernel, r=r)
        x_arg = x2d
        x_spec = pl.BlockSpec((tm, d), lambda i: (i, 0))
        scratch = [pltpu.VMEM((tm, d), BF16),
                   pltpu.VMEM((d // LANES, tm, LANES) if r > 1 else (1, 8, LANES), F32)]
        semantics = ("parallel",)

    return pl.pallas_call(
        body,
        grid=(m // tm,),
        in_specs=[
            x_spec,
            mod_spec(k_shift), mod_spec(k_scale),
            pl.BlockSpec((d, DIL_QKV), lambda i: (0, OFF_DIL // DIL_QKV + g),
                         pipeline_mode=pl.Buffered(1)),
            pl.BlockSpec((tm, DIL_HEAD_DIM), lambda i: (i, 0)),
            pl.BlockSpec((tm, DIL_HEAD_DIM), lambda i: (i, 0)),
        ],
        out_specs=pl.BlockSpec((1, r, tl, DIL_QKV),
                               lambda i: (i // tiles_per_seq, 0, i % tiles_per_seq, 0)),
        out_shape=jax.ShapeDtypeStruct((bsz, r, seq // r, DIL_QKV), BF16),
        scratch_shapes=scratch,
        compiler_params=_cparams(semantics, VMEM_LIMIT_BIG),
        name=f"proj_dil_r{r}",
    )(x_arg, mods3, mods3, w_all, cos_t, sin_t)


def _gla_kernel(q_ref, k_ref, v_ref, r_ref, lr_ref, wa_ref, ba_ref, ng_ref, o_ref,
                st_scr, b_scr, qd_scr, qh_scr, ki_scr, ke_scr, kp_scr, kh_scr, a_scr, *, n_chunks):
    c_len = GLA_CHUNK
    tc = q_ref.shape[1]
    hb = 4 * c_len
    pw = 2 * c_len
    n_pairs = n_chunks // 2

    @pl.when(pl.program_id(2) == 0)
    def _():
        st_scr[...] = jnp.zeros_like(st_scr)

    ti = lax.broadcasted_iota(jnp.int32, (hb, hb), 0)
    tj = lax.broadcasted_iota(jnp.int32, (hb, hb), 1)
    tri = jnp.logical_and(ti >= tj, ti // c_len == tj // c_len).astype(BF16)
    for s in range(tc // hb):
        rows = slice(s * hb, (s + 1) * hb)
        logits = jnp.dot(lr_ref[0, rows, :].astype(BF16), wa_ref[...],
                         preferred_element_type=F32) + ba_ref[...]
        log_a = (jnp.minimum(logits, 0.0) - jnp.log(1.0 + jnp.exp(-jnp.abs(logits)))) / GLA_TAU
        p1 = log_a.astype(BF16)
        r1 = log_a - p1.astype(F32)
        p2 = r1.astype(BF16)
        p3 = (r1 - p2.astype(F32)).astype(BF16)
        b_scr[rows, :] = (jnp.dot(tri, p1, preferred_element_type=F32)
                          + jnp.dot(tri, p2, preferred_element_type=F32)
                          + jnp.dot(tri, p3, preferred_element_type=F32))

    nt = (((1,), (1,)), ((), ()))

    def chunk_rows(c):
        return slice(c * c_len, (c + 1) * c_len)

    def pair_rows(p):
        return slice(p * pw, (p + 1) * pw)

    b_last = [b_scr[(c + 1) * c_len - 1:(c + 1) * c_len, :] for c in range(n_chunks)]
    h_log = [jnp.zeros_like(b_last[0])]
    for c in range(n_chunks):
        h_log.append(h_log[-1] + b_last[c])

    for c in range(n_chunks):
        rows = chunk_rows(c)
        b = b_scr[rows, :]
        q = q_ref[0, rows, :].astype(F32)
        k = k_ref[0, rows, :].astype(F32)
        qd = q * jnp.exp(b) * (GLA_DK ** -0.5)
        ke = k * jnp.exp(b_last[c] - b)
        qd_scr[rows, :] = qd
        qh_scr[rows, :] = (qd * jnp.exp(h_log[c])).astype(BF16)
        ki_scr[rows, :] = (k * jnp.exp(-b)).astype(BF16)
        ke_scr[rows, :] = ke.astype(BF16)
        kp_scr[rows, :] = (ke * jnp.exp(b_last[c + 1]) if c % 2 == 0 else ke).astype(BF16)
        kh_scr[rows, :] = (ke * jnp.exp(h_log[n_chunks] - h_log[c + 1])).astype(BF16)

    pi = lax.broadcasted_iota(jnp.int32, (pw, pw), 0)
    pj = lax.broadcasted_iota(jnp.int32, (pw, pw), 1)
    diag = jnp.logical_and(pi >= pj, pi // c_len == pj // c_len)
    lower = jnp.logical_and(pi >= c_len, pj < c_len)
    for p in range(n_pairs):
        first = chunk_rows(2 * p)
        keys = jnp.concatenate([ki_scr[pair_rows(p), :], ke_scr[first, :], ke_scr[first, :]], axis=0)
        s = lax.dot_general(qd_scr[pair_rows(p), :].astype(BF16), keys, nt,
                            preferred_element_type=F32)
        s = jnp.where(diag, s[:, :pw], jnp.where(lower, s[:, pw:], 0.0))
        a_scr[pair_rows(p), pair_rows(p)] = s.astype(BF16)

    for p in range(n_pairs - 1):
        c0 = 2 * p + 2
        lhs = jnp.concatenate(
            [(qd_scr[chunk_rows(c), :] * jnp.exp(h_log[c] - h_log[c0])).astype(BF16)
             for c in range(c0, n_chunks)], axis=0)
        s = lax.dot_general(lhs, kp_scr[pair_rows(p), :], nt, preferred_element_type=F32)
        a_scr[c0 * c_len:, pair_rows(p)] = s.astype(BF16)

    half = 2 * pw
    for hf in range(tc // half):
        a_scr[hf * half:hf * half + pw, hf * half + pw:(hf + 1) * half] = jnp.zeros((pw, pw), BF16)

    st = st_scr[...]
    st_b = st.astype(BF16)
    for hf in range(tc // half):
        rows = slice(hf * half, (hf + 1) * half)
        keys = (hf + 1) * half
        o = jnp.dot(a_scr[rows, :keys], v_ref[0, :keys, :], preferred_element_type=F32)
        o += jnp.dot(qh_scr[rows, :], st_b, preferred_element_type=F32)
        o = o * lax.rsqrt(jnp.mean(o * o, axis=-1, keepdims=True) + LN_EPS) * ng_ref[0]
        r = r_ref[0, rows, :].astype(F32)
        o_ref[0, rows, :] = (o * (r * _sigmoid(r))).astype(BF16)

    d_col = jnp.broadcast_to(jnp.exp(h_log[n_chunks]), (8, GLA_DK)).T[:, :1]
    st_scr[...] = st * d_col + lax.dot_general(
        kh_scr[...], v_ref[0], (((0,), (0,)), ((), ())), preferred_element_type=F32)


def _gla(p3, lr3, w_a2, b_a, norm_g3, tc=512):
    bsz, seq, _ = p3.shape
    nq, nv, nr = OFF_GK // GLA_DK, OFF_GV // GLA_DV, OFF_GR // GLA_DV
    return pl.pallas_call(
        functools.partial(_gla_kernel, n_chunks=tc // GLA_CHUNK),
        grid=(bsz, GLA_HEADS, seq // tc),
        in_specs=[
            pl.BlockSpec((1, tc, GLA_DK), lambda b, h, t: (b, t, h)),
            pl.BlockSpec((1, tc, GLA_DK), lambda b, h, t: (b, t, nq + h)),
            pl.BlockSpec((1, tc, GLA_DV), lambda b, h, t: (b, t, nv + h)),
            pl.BlockSpec((1, tc, GLA_DV), lambda b, h, t: (b, t, nr + h)),
            pl.BlockSpec((1, tc, LR_PAD), lambda b, h, t: (b, t, 0)),
            pl.BlockSpec((LR_PAD, GLA_DK), lambda b, h, t: (0, h)),
            pl.BlockSpec((1, GLA_DK), lambda b, h, t: (0, h)),
            pl.BlockSpec((1, 1, GLA_DV), lambda b, h, t: (h, 0, 0)),
        ],
        out_specs=pl.BlockSpec((1, tc, GLA_DV), lambda b, h, t: (b, t, h)),
        out_shape=jax.ShapeDtypeStruct((bsz, seq, GLA_VW), BF16),
        scratch_shapes=[pltpu.VMEM((GLA_DK, GLA_DV), F32), pltpu.VMEM((tc, GLA_DK), F32),
                        pltpu.VMEM((tc, GLA_DK), F32)]
                       + [pltpu.VMEM((tc, GLA_DK), BF16)] * 5 + [pltpu.VMEM((tc, tc), BF16)],
        compiler_params=_cparams(("parallel", "parallel", "arbitrary")),
        name="gla",
    )(p3, p3, p3, p3, lr3, w_a2, b_a, norm_g3)


def _dil_kernel(q_ref, k_ref, v_ref, kp_ref, vp_ref, o_ref, lse_ref, s_scr, m_scr, p_scr, *, nblk):
    n = pl.program_id(2)
    blk = DIL_BLOCK
    qi = lax.broadcasted_iota(jnp.int32, (blk, 2 * blk), 0)
    kj = lax.broadcasted_iota(jnp.int32, (blk, 2 * blk), 1)
    band = jnp.logical_or(jnp.logical_and(kj < blk, kj >= qi),
                          jnp.logical_and(kj >= blk, kj - blk <= qi))
    band_first = jnp.logical_and(band, jnp.logical_or(kj >= blk, n > 0))
    nt = (((1,), (1,)), ((), ()))

    def rows(i):
        return slice(i * blk, (i + 1) * blk)

    def cols(h):
        return slice(h * DIL_HEAD_DIM, (h + 1) * DIL_HEAD_DIM)

    for i in range(nblk):
        for h in range(DIL_HEADS):
            q = q_ref[0, 0, rows(i), cols(h)]
            k_prev = kp_ref[0, 0, :, cols(h)] if i == 0 else k_ref[0, 0, rows(i - 1), cols(h)]
            s_p = lax.dot_general(q, k_prev, nt, preferred_element_type=F32)
            s_c = lax.dot_general(q, k_ref[0, 0, rows(i), cols(h)], nt, preferred_element_type=F32)
            s = jnp.concatenate([s_p, s_c], axis=1)
            s_scr[rows(i * DIL_HEADS + h), :] = jnp.where(band_first if i == 0 else band, s, NEG_BIG)

    s_all = s_scr[...]
    m_all = jnp.max(s_all, axis=-1, keepdims=True)
    p_scr[...] = jnp.exp(s_all - m_all).astype(BF16)
    m_scr[...] = jnp.broadcast_to(m_all, m_scr.shape)

    lane = lax.broadcasted_iota(jnp.int32, (blk, LANES), 1)
    ones = jnp.ones((2 * blk, DIL_HEAD_DIM), BF16)
    for i in range(nblk):
        lse_tile = jnp.zeros((blk, LANES), F32)
        for h in range(DIL_HEADS):
            u = i * DIL_HEADS + h
            v_prev = vp_ref[0, 0, :, cols(h)] if i == 0 else v_ref[0, 0, rows(i - 1), cols(h)]
            v_aug = jnp.concatenate(
                [jnp.concatenate([v_prev, v_ref[0, 0, rows(i), cols(h)]], axis=0), ones], axis=1)
            acc = jnp.dot(p_scr[rows(u), :], v_aug, preferred_element_type=F32)
            den = acc[:, DIL_HEAD_DIM:]
            o_ref[0, 0, rows(i), cols(h)] = (acc[:, :DIL_HEAD_DIM] / den).astype(BF16)
            lse_tile = jnp.where(lane == h, m_scr[rows(u), :] + jnp.log(den), lse_tile)
        lse_ref[0, 0, rows(i), :] = lse_tile


def _dil_group(qkv, nblk=4):
    bsz, r, length, _ = qkv.shape
    nb = length // DIL_BLOCK
    nblk = min(nblk, nb)
    assert nb % nblk == 0
    tq = nblk * DIL_BLOCK
    units = nblk * DIL_HEADS * DIL_BLOCK

    def cur_spec(u):
        return pl.BlockSpec((1, 1, tq, DIL_OUT), lambda b, c, n: (b, c, n, u))

    def prev_spec(u):
        return pl.BlockSpec((1, 1, DIL_BLOCK, DIL_OUT),
                            lambda b, c, n: (b, c, jnp.maximum(n * nblk - 1, 0), u))

    return pl.pallas_call(
        functools.partial(_dil_kernel, nblk=nblk),
        grid=(bsz, r, nb // nblk),
        in_specs=[cur_spec(0), cur_spec(1), cur_spec(2), prev_spec(1), prev_spec(2)],
        out_specs=[
            pl.BlockSpec((1, 1, tq, DIL_OUT), lambda b, c, n: (b, c, n, 0)),
            pl.BlockSpec((1, 1, tq, LANES), lambda b, c, n: (b, c, n, 0)),
        ],
        out_shape=[
            jax.ShapeDtypeStruct((bsz, r, length, DIL_OUT), BF16),
            jax.ShapeDtypeStruct((bsz, r, length, LANES), F32),
        ],
        scratch_shapes=[pltpu.VMEM((units, 2 * DIL_BLOCK), F32),
                        pltpu.VMEM((units, LANES), F32),
                        pltpu.VMEM((units, 2 * DIL_BLOCK), BF16)],
        compiler_params=_cparams(("parallel", "parallel", "parallel")),
        name=f"dilattn_r{r}",
    )(qkv, qkv, qkv, qkv, qkv)


def _merge_kernel(oa_ref, o1_ref, o2_ref, o3_ref, l1_ref, l2_ref, l3_ref, ga_ref, gb_ref, x_ref,
                  g2_ref, wa_ref, wb_ref, wo_ref, lng_ref, lnb_ref, out_ref, l_scr, w_scr, ob_scr,
                  *, dilations):
    y_a = jnp.dot(oa_ref[...], wa_ref[...], preferred_element_type=F32)
    o_refs = (o1_ref, o2_ref, o3_ref)
    l_refs = (l1_ref, l2_ref, l3_ref)
    tm = oa_ref.shape[0]

    def tok_rows(c, r):
        return pl.ds(c, tm // r, stride=r) if r > 1 else slice(None)

    for g, r in enumerate(dilations):
        for c in range(r):
            l_scr[g, tok_rows(c, r), :] = l_refs[g][0, c]
    l1, l2, l3 = l_scr[0], l_scr[1], l_scr[2]
    mx = jnp.maximum(jnp.maximum(l1, l2), l3)
    e1, e2, e3 = jnp.exp(l1 - mx), jnp.exp(l2 - mx), jnp.exp(l3 - mx)
    inv = 1.0 / (e1 + e2 + e3)
    w_scr[0], w_scr[1], w_scr[2] = e1 * inv, e2 * inv, e3 * inv
    for g, r in enumerate(dilations):
        for c in range(r):
            rows = tok_rows(c, r)
            wr = w_scr[g, rows, :]
            for h in range(DIL_HEADS):
                sl = slice(h * DIL_HEAD_DIM, (h + 1) * DIL_HEAD_DIM)
                part = wr[:, h:h + 1] * o_refs[g][0, c, :, sl].astype(F32)
                if g == 0:
                    ob_scr[h, rows, :] = part
                else:
                    ob_scr[h, rows, :] += part
    o_b = jnp.concatenate([ob_scr[h].astype(BF16) for h in range(DIL_HEADS)], axis=1)
    y_b = jnp.dot(o_b, wb_ref[...], preferred_element_type=F32)
    merged = _sigmoid(ga_ref[...].astype(F32)) * y_a + _sigmoid(gb_ref[...].astype(F32)) * y_b
    mix = jnp.dot(merged.astype(BF16), wo_ref[...], preferred_element_type=F32)
    y = DN_ALPHA * x_ref[...] + g2_ref[0] * mix
    out_ref[...] = _layer_norm(y, lng_ref[...], lnb_ref[...])


def _merge(oa2d, o_groups, lse_groups, p2d, x2d, mods3, k_gate, w_a, w_b, w_o, ln_g, ln_b, seq,
           dilations, tm=256):
    m, d = x2d.shape
    tiles_per_seq = seq // tm
    resident = pl.Buffered(1)
    n_g = len(dilations)

    def row_spec(width, col=0):
        return pl.BlockSpec((tm, width), lambda i: (i, col))

    def class_spec(r, width):
        return pl.BlockSpec((1, r, tm // r, width),
                            lambda i: (i // tiles_per_seq, 0, i % tiles_per_seq, 0))

    def const_spec(shape):
        return pl.BlockSpec(shape, lambda i: (0, 0), pipeline_mode=resident)

    return pl.pallas_call(
        functools.partial(_merge_kernel, dilations=dilations),
        grid=(m // tm,),
        in_specs=[
            row_spec(GLA_VW),
            *[class_spec(r, DIL_OUT) for r in dilations],
            *[class_spec(r, LANES) for r in dilations],
            row_spec(D_MODEL, OFF_GA // D_MODEL), row_spec(D_MODEL, OFF_GB // D_MODEL),
            row_spec(d),
            pl.BlockSpec((1, 1, d), lambda i: ((i // tiles_per_seq) * N_MOD + k_gate, 0, 0)),
            const_spec(w_a.shape), const_spec(w_b.shape), const_spec(w_o.shape),
            const_spec((1, d)), const_spec((1, d)),
        ],
        out_specs=row_spec(d),
        out_shape=jax.ShapeDtypeStruct((m, d), F32),
        scratch_shapes=[pltpu.VMEM((n_g, tm, LANES), F32), pltpu.VMEM((n_g, tm, LANES), F32),
                        pltpu.VMEM((DIL_HEADS, tm, DIL_HEAD_DIM), F32)],
        compiler_params=_cparams(("parallel",)),
        name="merge",
    )(oa2d, *o_groups, *lse_groups, p2d, p2d, x2d, mods3, w_a, w_b, w_o, ln_g, ln_b)


PACK_W = 1024
PACK_COPY, PACK_SHIFTED, PACK_ZERO, PACK_LOW_RANK = range(4)


def _pack_plan():
    src, o = {}, 0
    for name, wdt in zip(("gq", "gk", "gv", "gr", "glr", "dq", "dk", "dv", "ga", "gb"),
                         (GLA_QK, GLA_QK, GLA_VW, GLA_VW, GLA_GATE_RANK, DIL_W, DIL_W, DIL_W,
                          D_MODEL, D_MODEL)):
        src[name] = (o, wdt)
        o += wdt
    plan = []

    def copy(first, width):
        for c in range(first, first + width, PACK_W):
            plan.append((c, PACK_COPY if c % PACK_W == 0 else PACK_SHIFTED))

    for name in ("gq", "gk", "gv", "gr", "ga", "gb"):
        copy(*src[name])
    plan += [(0, PACK_ZERO)] * ((OFF_DIL - PROJ_W) // PACK_W)
    for g in range(DIL_GROUPS):
        for name in ("dq", "dk", "dv"):
            copy(src[name][0] + g * DIL_OUT, DIL_OUT)
    plan.append((src["glr"][0], PACK_LOW_RANK))
    return plan


def _pack_kernel(plan_ref, a_ref, b_ref, o_ref):
    kind = plan_ref[2, pl.program_id(0)]
    rows = a_ref.shape[0]
    lane = lax.broadcasted_iota(jnp.int32, (rows, LANES), 1)
    keep = LANES - GLA_GATE_RANK

    @pl.when(kind == PACK_COPY)
    def _():
        o_ref[...] = a_ref[...].astype(BF16)

    @pl.when(kind == PACK_ZERO)
    def _():
        o_ref[...] = jnp.zeros_like(o_ref)

    @pl.when(kind == PACK_LOW_RANK)
    def _():
        o_ref[...] = jnp.zeros_like(o_ref)
        o_ref[:, :LANES] = jnp.where(lane < GLA_GATE_RANK, a_ref[:, :LANES], 0.0).astype(BF16)

    @pl.when(kind == PACK_SHIFTED)
    def _():
        cur = pltpu.roll(a_ref[:, :LANES], keep, 1)
        for k in range(PACK_W // LANES):
            nxt_src = a_ref[:, (k + 1) * LANES:(k + 2) * LANES] if (k + 1) * LANES < PACK_W else b_ref[...]
            nxt = pltpu.roll(nxt_src, keep, 1)
            o_ref[:, k * LANES:(k + 1) * LANES] = jnp.where(lane < keep, cur, nxt).astype(BF16)
            cur = nxt


def _pack_w_in(w_in, layer):
    _, d, width = w_in.shape
    plan = _pack_plan()
    first = [c - c % PACK_W for c, _ in plan]
    assert all(c % PACK_W in (0, GLA_GATE_RANK) for c, _ in plan)
    table = jnp.asarray([[c // PACK_W for c in first],
                         [min((c + PACK_W) // LANES, (width - 1) // LANES) for c in first],
                         [kind for _, kind in plan]], jnp.int32)
    return pl.pallas_call(
        _pack_kernel,
        grid_spec=pltpu.PrefetchScalarGridSpec(
            num_scalar_prefetch=1,
            grid=(len(plan),),
            in_specs=[pl.BlockSpec((None, d, PACK_W), lambda i, t: (layer, 0, t[0, i])),
                      pl.BlockSpec((None, d, LANES), lambda i, t: (layer, 0, t[1, i]))],
            out_specs=pl.BlockSpec((d, PACK_W), lambda i, t: (0, i)),
        ),
        out_shape=jax.ShapeDtypeStruct((d, len(plan) * PACK_W), BF16),
        compiler_params=_cparams(("arbitrary",)),
        name="pack_w_in",
    )(table, w_in, w_in)


def kernel(x, c, positions, w_ada, b_ada, ln1_g, ln1_b, w_ffn1_gu, w_ffn1_down, w_in, w_alpha2,
           b_alpha, gla_norm_g, w_branch_a, w_branch_b, w_out, ln2_g, ln2_b, w_ffn2_gu, w_ffn2_down,
           ln3_g, ln3_b):
    bsz, seq, d = x.shape
    m = bsz * seq
    x2d = x.reshape(m, d)
    c_pad = jnp.pad(c, ((0, 8 - bsz % 8 if bsz % 8 else 0), (0, 0)))
    dilations = tuple(r for _, r in DIL_PATTERNS)

    half = DIL_HEAD_DIM // 2
    freq = ROPE_THETA ** (-jnp.arange(half, dtype=F32) / half)
    freq2 = jnp.concatenate([freq, freq]).reshape(1, DIL_HEAD_DIM)
    rope_t = _rope_tables(positions.reshape(m // PROJ_TM, PROJ_TM // LANES, LANES), freq2, dilations)

    for l in range(DEPTH):
        mods = _mods(c_pad, w_ada[l], b_ada[l].reshape(1, -1))[:bsz]
        mods3 = mods.reshape(bsz * N_MOD, 1, d)

        x1 = _ffn(x2d, mods3, 0, 1, 2, w_ffn1_gu[l].astype(BF16), w_ffn1_down[l].astype(BF16),
                  ln1_g[l].reshape(1, d), ln1_b[l].reshape(1, d), seq)

        w_all = _pack_w_in(w_in, l)
        p2d, lr2d = _proj(x1, mods3, 3, 4, w_all, seq)
        p3 = p2d.reshape(bsz, seq, PROJ_W)

        w_a2 = jnp.pad(w_alpha2[l], ((0, LR_PAD - GLA_GATE_RANK), (0, 0))).astype(BF16)
        o_a = _gla(p3, lr2d.reshape(bsz, seq, LR_PAD), w_a2, b_alpha[l].reshape(1, GLA_QK),
                   gla_norm_g[l].reshape(GLA_HEADS, 1, GLA_DV))

        o_groups, lse_groups = [], []
        for g, r in enumerate(dilations):
            qkv = _proj_dil(x1, mods3, 3, 4, w_all, g, rope_t[2 * g], rope_t[2 * g + 1],
                            bsz, seq, r)
            o_g, lse_g = _dil_group(qkv)
            o_groups.append(o_g)
            lse_groups.append(lse_g)

        x2 = _merge(o_a.reshape(m, GLA_VW), o_groups, lse_groups, p2d, x1, mods3, 5,
                    w_branch_a[l].astype(BF16), w_branch_b[l].astype(BF16), w_out[l].astype(BF16),
                    ln2_g[l].reshape(1, d), ln2_b[l].reshape(1, d), seq, dilations)

        x2d = _ffn(x2, mods3, 6, 7, 8, w_ffn2_gu[l].astype(BF16), w_ffn2_down[l].astype(BF16),
                   ln3_g[l].reshape(1, d), ln3_b[l].reshape(1, d), seq)
    return x2d.reshape(bsz, seq, d)
```

```python
import functools

import jax
import jax.numpy as jnp
from jax import lax
from jax.experimental import pallas as pl
from jax.experimental.pallas import tpu as pltpu

F32 = jnp.float32
BF16 = jnp.bfloat16

D_MODEL = 2048
DEPTH = 1
D_FF = 5632
N_MOD = 9
LN_EPS = 1e-5
DN_ALPHA = (2.0 * DEPTH) ** 0.25

GLA_HEADS = 4
GLA_DK = 256
GLA_DV = 512
GLA_GATE_RANK = 16
GLA_TAU = 16.0
GLA_CHUNK = 64
GLA_QK = GLA_HEADS * GLA_DK
GLA_VW = GLA_HEADS * GLA_DV

DIL_PATTERNS = ((128, 1), (512, 4), (2048, 16))
DIL_GROUPS = len(DIL_PATTERNS)
DIL_HEADS = 8
DIL_HEAD_DIM = 128
DIL_BLOCK = 128
DIL_W = DIL_GROUPS * DIL_HEADS * DIL_HEAD_DIM
DIL_OUT = DIL_HEADS * DIL_HEAD_DIM
ROPE_THETA = 10000.0

LANES = 128
LR_PAD = LANES

PROJ_W = 2 * GLA_QK + 2 * GLA_VW + 2 * D_MODEL
OFF_GQ = 0
OFF_GK = GLA_QK
OFF_GV = 2 * GLA_QK
OFF_GR = OFF_GV + GLA_VW
OFF_GA = OFF_GR + GLA_VW
OFF_GB = OFF_GA + D_MODEL
DIL_QKV = 3 * DIL_OUT
OFF_DIL = -(-PROJ_W // DIL_QKV) * DIL_QKV
OFF_LR = OFF_DIL + DIL_GROUPS * DIL_QKV

NEG_BIG = -1e30

VMEM_LIMIT = 56 * 1024 * 1024
VMEM_LIMIT_BIG = 60 * 1024 * 1024


def _cparams(sem, vmem=VMEM_LIMIT):
    return pltpu.CompilerParams(dimension_semantics=sem, vmem_limit_bytes=vmem)


def _sigmoid(x):
    return 1.0 / (1.0 + jnp.exp(-x))


def _layer_norm(y, g, b):
    mu = jnp.mean(y, axis=-1, keepdims=True)
    d = y - mu
    var = jnp.mean(d * d, axis=-1, keepdims=True)
    return d * lax.rsqrt(var + LN_EPS) * g + b


def _mods_kernel(c_ref, w_ref, b_ref, o_ref):
    c = c_ref[...]
    c_act = (c * _sigmoid(c)).astype(BF16)
    o_ref[...] = jnp.dot(c_act, w_ref[...].astype(BF16), preferred_element_type=F32) + b_ref[...]


def _mods(c_pad, w_ada, b_ada, tn=1024):
    rows, d = c_pad.shape
    n = w_ada.shape[1]
    return pl.pallas_call(
        _mods_kernel,
        grid=(n // tn,),
        in_specs=[
            pl.BlockSpec((rows, d), lambda j: (0, 0)),
            pl.BlockSpec((d, tn), lambda j: (0, j)),
            pl.BlockSpec((1, tn), lambda j: (0, j)),
        ],
        out_specs=pl.BlockSpec((rows, tn), lambda j: (0, j)),
        out_shape=jax.ShapeDtypeStruct((rows, n), F32),
        compiler_params=_cparams(("arbitrary",)),
        name="mods",
    )(c_pad, w_ada, b_ada)


FFN_ROW_CHUNK = 512


def _ffn_kernel(x_ref, sh_ref, sc_ref, g_ref, wg_ref, wu_ref, wd_ref, lng_ref, lnb_ref,
                o_ref, h_scr):
    j = pl.program_id(1)

    @pl.when(j == 0)
    def _():
        h_scr[...] = (x_ref[...] * (1.0 + sc_ref[0]) + sh_ref[0]).astype(BF16)
        o_ref[...] = jnp.zeros_like(o_ref)

    for r0 in range(0, o_ref.shape[0], FFN_ROW_CHUNK):
        rs = slice(r0, r0 + FFN_ROW_CHUNK)
        h = h_scr[rs, :]
        gate = jnp.dot(h, wg_ref[...], preferred_element_type=F32)
        up = jnp.dot(h, wu_ref[...], preferred_element_type=F32)
        act = (gate * _sigmoid(gate) * up).astype(BF16)
        o_ref[rs, :] += jnp.dot(act, wd_ref[...], preferred_element_type=F32)

    @pl.when(j == pl.num_programs(1) - 1)
    def _():
        y = DN_ALPHA * x_ref[...] + 0.5 * g_ref[0] * o_ref[...]
        o_ref[...] = _layer_norm(y, lng_ref[...], lnb_ref[...])


def _ffn(x2d, mods3, k_shift, k_scale, k_gate, w_gu, w_down, ln_g, ln_b, seq, tm=1024, tf=512):
    m, d = x2d.shape
    n_ff = w_down.shape[0]
    nj = n_ff // tf
    tiles_per_seq = seq // tm

    def mod_spec(k):
        return pl.BlockSpec((1, 1, d), lambda i, j: ((i // tiles_per_seq) * N_MOD + k, 0, 0))

    return pl.pallas_call(
        _ffn_kernel,
        grid=(m // tm, nj),
        in_specs=[
            pl.BlockSpec((tm, d), lambda i, j: (i, 0)),
            mod_spec(k_shift), mod_spec(k_scale), mod_spec(k_gate),
            pl.BlockSpec((d, tf), lambda i, j: (0, j)),
            pl.BlockSpec((d, tf), lambda i, j: (0, j + nj)),
            pl.BlockSpec((tf, d), lambda i, j: (j, 0)),
            pl.BlockSpec((1, d), lambda i, j: (0, 0)),
            pl.BlockSpec((1, d), lambda i, j: (0, 0)),
        ],
        out_specs=pl.BlockSpec((tm, d), lambda i, j: (i, 0)),
        out_shape=jax.ShapeDtypeStruct((m, d), F32),
        scratch_shapes=[pltpu.VMEM((tm, d), BF16)],
        compiler_params=_cparams(("parallel", "arbitrary"), VMEM_LIMIT_BIG),
        name="ffn",
    )(x2d, mods3, mods3, mods3, w_gu, w_gu, w_down, ln_g, ln_b)


PROJ_TM = 1024


def _rope_kernel(pos_ref, freq_ref, *out_refs, dilations):
    cos_ref, sin_ref = out_refs[0], out_refs[1]
    tm = cos_ref.shape[0]
    pos_t = pos_ref[0].astype(F32).T
    lane = lax.broadcasted_iota(jnp.int32, (LANES, DIL_HEAD_DIM), 1)
    for k in range(tm // LANES):
        ang = pos_t[:, k:k + 1] * freq_ref[...]
        s = jnp.sin(ang)
        cos_ref[k * LANES:(k + 1) * LANES, :] = jnp.cos(ang)
        sin_ref[k * LANES:(k + 1) * LANES, :] = jnp.where(lane < DIL_HEAD_DIM // 2, -s, s)
    for gi, r in enumerate(dilations[1:]):
        tl = tm // r
        for c in range(r):
            out_refs[2 + 2 * gi][c * tl:(c + 1) * tl, :] = cos_ref[pl.ds(c, tl, stride=r), :]
            out_refs[3 + 2 * gi][c * tl:(c + 1) * tl, :] = sin_ref[pl.ds(c, tl, stride=r), :]


def _rope_tables(pos3, freq2, dilations, tm=PROJ_TM):
    assert dilations[0] == 1
    m = pos3.shape[0] * tm
    n_out = 2 * len(dilations)
    return pl.pallas_call(
        functools.partial(_rope_kernel, dilations=dilations),
        grid=(m // tm,),
        in_specs=[
            pl.BlockSpec((1, tm // LANES, LANES), lambda i: (i, 0, 0)),
            pl.BlockSpec((1, DIL_HEAD_DIM), lambda i: (0, 0)),
        ],
        out_specs=[pl.BlockSpec((tm, DIL_HEAD_DIM), lambda i: (i, 0))] * n_out,
        out_shape=[jax.ShapeDtypeStruct((m, DIL_HEAD_DIM), F32)] * n_out,
        compiler_params=_cparams(("parallel",)),
        name="rope_tables",
    )(pos3, freq2)


def _proj_kernel(x_ref, sh_ref, sc_ref, w_ref, wlr_ref, o_ref, lr_ref, h_scr):
    @pl.when(pl.program_id(1) == 0)
    def _():
        h = (x_ref[...] * (1.0 + sc_ref[0]) + sh_ref[0]).astype(BF16)
        h_scr[...] = h
        lr_ref[...] = jnp.dot(h, wlr_ref[...], preferred_element_type=F32)

    o_ref[...] = jnp.dot(h_scr[...], w_ref[...], preferred_element_type=F32).astype(BF16)


def _proj(x2d, mods3, k_shift, k_scale, w_all, seq, tm=PROJ_TM, tn=2048):
    m, d = x2d.shape
    n = PROJ_W
    tiles_per_seq = seq // tm

    def mod_spec(k):
        return pl.BlockSpec((1, 1, d), lambda i, j: ((i // tiles_per_seq) * N_MOD + k, 0, 0))

    return pl.pallas_call(
        _proj_kernel,
        grid=(m // tm, n // tn),
        in_specs=[
            pl.BlockSpec((tm, d), lambda i, j: (i, 0)),
            mod_spec(k_shift), mod_spec(k_scale),
            pl.BlockSpec((d, tn), lambda i, j: (0, j)),
            pl.BlockSpec((d, LR_PAD), lambda i, j: (0, OFF_LR // LR_PAD)),
        ],
        out_specs=[
            pl.BlockSpec((tm, tn), lambda i, j: (i, j)),
            pl.BlockSpec((tm, LR_PAD), lambda i, j: (i, 0)),
        ],
        out_shape=[
            jax.ShapeDtypeStruct((m, n), BF16),
            jax.ShapeDtypeStruct((m, LR_PAD), F32),
        ],
        scratch_shapes=[pltpu.VMEM((tm, d), BF16)],
        compiler_params=_cparams(("parallel", "arbitrary")),
        name="proj",
    )(x2d, mods3, mods3, w_all, w_all)


def _proj_dil_kernel(x_ref, sh_ref, sc_ref, w_ref, cos_ref, sin_ref, o_ref, h_scr, slab_scr, *, r):
    tm, d = x_ref.shape
    tl = tm // r
    if r == 1:
        h_scr[...] = (x_ref[...] * (1.0 + sc_ref[0]) + sh_ref[0]).astype(BF16)
    else:
        for k in range(d // LANES):
            sl = slice(k * LANES, (k + 1) * LANES)
            slab_scr[k] = x_ref[:, sl] * (1.0 + sc_ref[0, :, sl]) + sh_ref[0, :, sl]
        for c in range(r):
            for k in range(d // LANES):
                sl = slice(k * LANES, (k + 1) * LANES)
                h_scr[c * tl:(c + 1) * tl, sl] = slab_scr[k, pl.ds(c, tl, stride=r), :].astype(BF16)
    _proj_dil_qkv(h_scr, w_ref, cos_ref, sin_ref, o_ref, r)


def _proj_dil_qkv(h_scr, w_ref, cos_ref, sin_ref, o_ref, r):
    tl = h_scr.shape[0] // r
    for part, scale in enumerate((DIL_HEAD_DIM ** -0.5, 1.0, None)):
        cols = slice(part * DIL_OUT, (part + 1) * DIL_OUT)
        t = jnp.dot(h_scr[...], w_ref[:, cols], preferred_element_type=F32)
        if scale is None:
            for c in range(r):
                o_ref[0, c, :, cols] = t[c * tl:(c + 1) * tl, :].astype(BF16)
            continue
        cos = cos_ref[...] * scale
        sin = sin_ref[...] * scale
        for s in range(DIL_HEADS):
            ts = t[:, s * DIL_HEAD_DIM:(s + 1) * DIL_HEAD_DIM]
            rot = (ts * cos + pltpu.roll(ts, DIL_HEAD_DIM // 2, 1) * sin).astype(BF16)
            lo = part * DIL_OUT + s * DIL_HEAD_DIM
            for c in range(r):
                o_ref[0, c, :, lo:lo + DIL_HEAD_DIM] = rot[c * tl:(c + 1) * tl, :]


def _proj_dil_dma_kernel(x_hbm, sh_ref, sc_ref, w_ref, cos_ref, sin_ref, o_ref, h_scr, x_buf, sem,
                         *, r):
    i = pl.program_id(0)
    tl = h_scr.shape[0] // r
    slot = i % 2

    def tile_copies(tile, buf_slot):
        return [pltpu.make_async_copy(x_hbm.at[pl.ds(tile * tl, tl), c], x_buf.at[buf_slot, c],
                                      sem.at[buf_slot]) for c in range(r)]

    @pl.when(i == 0)
    def _():
        for cp in tile_copies(0, 0):
            cp.start()

    @pl.when(i + 1 < pl.num_programs(0))
    def _():
        for cp in tile_copies(i + 1, 1 - slot):
            cp.start()

    for cp in tile_copies(i, slot):
        cp.wait()
    for c in range(r):
        h_scr[c * tl:(c + 1) * tl, :] = (x_buf[slot, c] * (1.0 + sc_ref[0]) + sh_ref[0]).astype(BF16)
    _proj_dil_qkv(h_scr, w_ref, cos_ref, sin_ref, o_ref, r)


SUBLANES_F32 = 8


def _proj_dil(x2d, mods3, k_shift, k_scale, w_all, g, cos_t, sin_t, bsz, seq, r, tm=PROJ_TM):
    m, d = x2d.shape
    tiles_per_seq = seq // tm
    tl = tm // r
    dma_gather = r % SUBLANES_F32 == 0

    def mod_spec(k):
        return pl.BlockSpec((1, 1, d), lambda i: ((i // tiles_per_seq) * N_MOD + k, 0, 0))

    if dma_gather:
        body = functools.partial(_proj_dil_dma_kernel, r=r)
        x_arg = x2d.reshape(m // r, r, d)
        x_spec = pl.BlockSpec(memory_space=pl.ANY)
        scratch = [pltpu.VMEM((tm, d), BF16), pltpu.VMEM((2, r, tl, d), F32),
                   pltpu.SemaphoreType.DMA((2,))]
        semantics = ("arbitrary",)
    else:
        body = functools.partial(_proj_dil_kernel, r=r)
        x_arg = x2d
        x_spec = pl.BlockSpec((tm, d), lambda i: (i, 0))
        scratch = [pltpu.VMEM((tm, d), BF16),
                   pltpu.VMEM((d // LANES, tm, LANES) if r > 1 else (1, 8, LANES), F32)]
        semantics = ("parallel",)

    return pl.pallas_call(
        body,
        grid=(m // tm,),
        in_specs=[
            x_spec,
            mod_spec(k_shift), mod_spec(k_scale),
            pl.BlockSpec((d, DIL_QKV), lambda i: (0, OFF_DIL // DIL_QKV + g),
                         pipeline_mode=pl.Buffered(1)),
            pl.BlockSpec((tm, DIL_HEAD_DIM), lambda i: (i, 0)),
            pl.BlockSpec((tm, DIL_HEAD_DIM), lambda i: (i, 0)),
        ],
        out_specs=pl.BlockSpec((1, r, tl, DIL_QKV),
                               lambda i: (i // tiles_per_seq, 0, i % tiles_per_seq, 0)),
        out_shape=jax.ShapeDtypeStruct((bsz, r, seq // r, DIL_QKV), BF16),
        scratch_shapes=scratch,
        compiler_params=_cparams(semantics, VMEM_LIMIT_BIG),
        name=f"proj_dil_r{r}",
    )(x_arg, mods3, mods3, w_all, cos_t, sin_t)


def _gla_kernel(q_ref, k_ref, v_ref, r_ref, lr_ref, wa_ref, ba_ref, ng_ref, o_ref,
                st_scr, b_scr, qd_scr, qh_scr, ki_scr, ke_scr, kp_scr, kh_scr, a_scr, *, n_chunks):
    c_len = GLA_CHUNK
    tc = q_ref.shape[1]
    hb = 4 * c_len
    pw = 2 * c_len
    n_pairs = n_chunks // 2

    @pl.when(pl.program_id(2) == 0)
    def _():
        st_scr[...] = jnp.zeros_like(st_scr)

    ti = lax.broadcasted_iota(jnp.int32, (hb, hb), 0)
    tj = lax.broadcasted_iota(jnp.int32, (hb, hb), 1)
    tri = jnp.logical_and(ti >= tj, ti // c_len == tj // c_len).astype(BF16)
    for s in range(tc // hb):
        rows = slice(s * hb, (s + 1) * hb)
        logits = jnp.dot(lr_ref[0, rows, :].astype(BF16), wa_ref[...],
                         preferred_element_type=F32) + ba_ref[...]
        log_a = (jnp.minimum(logits, 0.0) - jnp.log(1.0 + jnp.exp(-jnp.abs(logits)))) / GLA_TAU
        p1 = log_a.astype(BF16)
        r1 = log_a - p1.astype(F32)
        p2 = r1.astype(BF16)
        p3 = (r1 - p2.astype(F32)).astype(BF16)
        b_scr[rows, :] = (jnp.dot(tri, p1, preferred_element_type=F32)
                          + jnp.dot(tri, p2, preferred_element_type=F32)
                          + jnp.dot(tri, p3, preferred_element_type=F32))

    nt = (((1,), (1,)), ((), ()))

    def chunk_rows(c):
        return slice(c * c_len, (c + 1) * c_len)

    def pair_rows(p):
        return slice(p * pw, (p + 1) * pw)

    b_last = [b_scr[(c + 1) * c_len - 1:(c + 1) * c_len, :] for c in range(n_chunks)]
    h_log = [jnp.zeros_like(b_last[0])]
    for c in range(n_chunks):
        h_log.append(h_log[-1] + b_last[c])

    for c in range(n_chunks):
        rows = chunk_rows(c)
        b = b_scr[rows, :]
        q = q_ref[0, rows, :].astype(F32)
        k = k_ref[0, rows, :].astype(F32)
        qd = q * jnp.exp(b) * (GLA_DK ** -0.5)
        ke = k * jnp.exp(b_last[c] - b)
        qd_scr[rows, :] = qd
        qh_scr[rows, :] = (qd * jnp.exp(h_log[c])).astype(BF16)
        ki_scr[rows, :] = (k * jnp.exp(-b)).astype(BF16)
        ke_scr[rows, :] = ke.astype(BF16)
        kp_scr[rows, :] = (ke * jnp.exp(b_last[c + 1]) if c % 2 == 0 else ke).astype(BF16)
        kh_scr[rows, :] = (ke * jnp.exp(h_log[n_chunks] - h_log[c + 1])).astype(BF16)

    pi = lax.broadcasted_iota(jnp.int32, (pw, pw), 0)
    pj = lax.broadcasted_iota(jnp.int32, (pw, pw), 1)
    diag = jnp.logical_and(pi >= pj, pi // c_len == pj // c_len)
    lower = jnp.logical_and(pi >= c_len, pj < c_len)
    for p in range(n_pairs):
        first = chunk_rows(2 * p)
        keys = jnp.concatenate([ki_scr[pair_rows(p), :], ke_scr[first, :], ke_scr[first, :]], axis=0)
        s = lax.dot_general(qd_scr[pair_rows(p), :].astype(BF16), keys, nt,
                            preferred_element_type=F32)
        s = jnp.where(diag, s[:, :pw], jnp.where(lower, s[:, pw:], 0.0))
        a_scr[pair_rows(p), pair_rows(p)] = s.astype(BF16)

    for p in range(n_pairs - 1):
        c0 = 2 * p + 2
        lhs = jnp.concatenate(
            [(qd_scr[chunk_rows(c), :] * jnp.exp(h_log[c] - h_log[c0])).astype(BF16)
             for c in range(c0, n_chunks)], axis=0)
        s = lax.dot_general(lhs, kp_scr[pair_rows(p), :], nt, preferred_element_type=F32)
        a_scr[c0 * c_len:, pair_rows(p)] = s.astype(BF16)

    half = 2 * pw
    for hf in range(tc // half):
        a_scr[hf * half:hf * half + pw, hf * half + pw:(hf + 1) * half] = jnp.zeros((pw, pw), BF16)

    st = st_scr[...]
    st_b = st.astype(BF16)
    for hf in range(tc // half):
        rows = slice(hf * half, (hf + 1) * half)
        keys = (hf + 1) * half
        o = jnp.dot(a_scr[rows, :keys], v_ref[0, :keys, :], preferred_element_type=F32)
        o += jnp.dot(qh_scr[rows, :], st_b, preferred_element_type=F32)
        o = o * lax.rsqrt(jnp.mean(o * o, axis=-1, keepdims=True) + LN_EPS) * ng_ref[0]
        r = r_ref[0, rows, :].astype(F32)
        o_ref[0, rows, :] = (o * (r * _sigmoid(r))).astype(BF16)

    d_col = jnp.broadcast_to(jnp.exp(h_log[n_chunks]), (8, GLA_DK)).T[:, :1]
    st_scr[...] = st * d_col + lax.dot_general(
        kh_scr[...], v_ref[0], (((0,), (0,)), ((), ())), preferred_element_type=F32)


def _gla(p3, lr3, w_a2, b_a, norm_g3, tc=512):
    bsz, seq, _ = p3.shape
    nq, nv, nr = OFF_GK // GLA_DK, OFF_GV // GLA_DV, OFF_GR // GLA_DV
    return pl.pallas_call(
        functools.partial(_gla_kernel, n_chunks=tc // GLA_CHUNK),
        grid=(bsz, GLA_HEADS, seq // tc),
        in_specs=[
            pl.BlockSpec((1, tc, GLA_DK), lambda b, h, t: (b, t, h)),
            pl.BlockSpec((1, tc, GLA_DK), lambda b, h, t: (b, t, nq + h)),
            pl.BlockSpec((1, tc, GLA_DV), lambda b, h, t: (b, t, nv + h)),
            pl.BlockSpec((1, tc, GLA_DV), lambda b, h, t: (b, t, nr + h)),
            pl.BlockSpec((1, tc, LR_PAD), lambda b, h, t: (b, t, 0)),
            pl.BlockSpec((LR_PAD, GLA_DK), lambda b, h, t: (0, h)),
            pl.BlockSpec((1, GLA_DK), lambda b, h, t: (0, h)),
            pl.BlockSpec((1, 1, GLA_DV), lambda b, h, t: (h, 0, 0)),
        ],
        out_specs=pl.BlockSpec((1, tc, GLA_DV), lambda b, h, t: (b, t, h)),
        out_shape=jax.ShapeDtypeStruct((bsz, seq, GLA_VW), BF16),
        scratch_shapes=[pltpu.VMEM((GLA_DK, GLA_DV), F32), pltpu.VMEM((tc, GLA_DK), F32),
                        pltpu.VMEM((tc, GLA_DK), F32)]
                       + [pltpu.VMEM((tc, GLA_DK), BF16)] * 5 + [pltpu.VMEM((tc, tc), BF16)],
        compiler_params=_cparams(("parallel", "parallel", "arbitrary")),
        name="gla",
    )(p3, p3, p3, p3, lr3, w_a2, b_a, norm_g3)


def _dil_kernel(q_ref, k_ref, v_ref, kp_ref, vp_ref, o_ref, lse_ref, s_scr, m_scr, p_scr, *, nblk):
    n = pl.program_id(2)
    blk = DIL_BLOCK
    qi = lax.broadcasted_iota(jnp.int32, (blk, 2 * blk), 0)
    kj = lax.broadcasted_iota(jnp.int32, (blk, 2 * blk), 1)
    band = jnp.logical_or(jnp.logical_and(kj < blk, kj >= qi),
                          jnp.logical_and(kj >= blk, kj - blk <= qi))
    band_first = jnp.logical_and(band, jnp.logical_or(kj >= blk, n > 0))
    nt = (((1,), (1,)), ((), ()))

    def rows(i):
        return slice(i * blk, (i + 1) * blk)

    def cols(h):
        return slice(h * DIL_HEAD_DIM, (h + 1) * DIL_HEAD_DIM)

    for i in range(nblk):
        for h in range(DIL_HEADS):
            q = q_ref[0, 0, rows(i), cols(h)]
            k_prev = kp_ref[0, 0, :, cols(h)] if i == 0 else k_ref[0, 0, rows(i - 1), cols(h)]
            s_p = lax.dot_general(q, k_prev, nt, preferred_element_type=F32)
            s_c = lax.dot_general(q, k_ref[0, 0, rows(i), cols(h)], nt, preferred_element_type=F32)
            s = jnp.concatenate([s_p, s_c], axis=1)
            s_scr[rows(i * DIL_HEADS + h), :] = jnp.where(band_first if i == 0 else band, s, NEG_BIG)

    s_all = s_scr[...]
    m_all = jnp.max(s_all, axis=-1, keepdims=True)
    p_scr[...] = jnp.exp(s_all - m_all).astype(BF16)
    m_scr[...] = jnp.broadcast_to(m_all, m_scr.shape)

    lane = lax.broadcasted_iota(jnp.int32, (blk, LANES), 1)
    ones = jnp.ones((2 * blk, DIL_HEAD_DIM), BF16)
    for i in range(nblk):
        lse_tile = jnp.zeros((blk, LANES), F32)
        for h in range(DIL_HEADS):
            u = i * DIL_HEADS + h
            v_prev = vp_ref[0, 0, :, cols(h)] if i == 0 else v_ref[0, 0, rows(i - 1), cols(h)]
            v_aug = jnp.concatenate(
                [jnp.concatenate([v_prev, v_ref[0, 0, rows(i), cols(h)]], axis=0), ones], axis=1)
            acc = jnp.dot(p_scr[rows(u), :], v_aug, preferred_element_type=F32)
            den = acc[:, DIL_HEAD_DIM:]
            o_ref[0, 0, rows(i), cols(h)] = (acc[:, :DIL_HEAD_DIM] / den).astype(BF16)
            lse_tile = jnp.where(lane == h, m_scr[rows(u), :] + jnp.log(den), lse_tile)
        lse_ref[0, 0, rows(i), :] = lse_tile


def _dil_group(qkv, nblk=4):
    bsz, r, length, _ = qkv.shape
    nb = length // DIL_BLOCK
    nblk = min(nblk, nb)
    assert nb % nblk == 0
    tq = nblk * DIL_BLOCK
    units = nblk * DIL_HEADS * DIL_BLOCK

    def cur_spec(u):
        return pl.BlockSpec((1, 1, tq, DIL_OUT), lambda b, c, n: (b, c, n, u))

    def prev_spec(u):
        return pl.BlockSpec((1, 1, DIL_BLOCK, DIL_OUT),
                            lambda b, c, n: (b, c, jnp.maximum(n * nblk - 1, 0), u))

    return pl.pallas_call(
        functools.partial(_dil_kernel, nblk=nblk),
        grid=(bsz, r, nb // nblk),
        in_specs=[cur_spec(0), cur_spec(1), cur_spec(2), prev_spec(1), prev_spec(2)],
        out_specs=[
            pl.BlockSpec((1, 1, tq, DIL_OUT), lambda b, c, n: (b, c, n, 0)),
            pl.BlockSpec((1, 1, tq, LANES), lambda b, c, n: (b, c, n, 0)),
        ],
        out_shape=[
            jax.ShapeDtypeStruct((bsz, r, length, DIL_OUT), BF16),
            jax.ShapeDtypeStruct((bsz, r, length, LANES), F32),
        ],
        scratch_shapes=[pltpu.VMEM((units, 2 * DIL_BLOCK), F32),
                        pltpu.VMEM((units, LANES), F32),
                        pltpu.VMEM((units, 2 * DIL_BLOCK), BF16)],
        compiler_params=_cparams(("parallel", "parallel", "parallel")),
        name=f"dilattn_r{r}",
    )(qkv, qkv, qkv, qkv, qkv)


def _merge_kernel(oa_ref, o1_ref, o2_ref, o3_ref, l1_ref, l2_ref, l3_ref, ga_ref, gb_ref, x_ref,
                  g2_ref, wa_ref, wb_ref, wo_ref, lng_ref, lnb_ref, out_ref, l_scr, w_scr, ob_scr,
                  *, dilations):
    y_a = jnp.dot(oa_ref[...], wa_ref[...], preferred_element_type=F32)
    o_refs = (o1_ref, o2_ref, o3_ref)
    l_refs = (l1_ref, l2_ref, l3_ref)
    tm = oa_ref.shape[0]

    def tok_rows(c, r):
        return pl.ds(c, tm // r, stride=r) if r > 1 else slice(None)

    for g, r in enumerate(dilations):
        for c in range(r):
            l_scr[g, tok_rows(c, r), :] = l_refs[g][0, c]
    l1, l2, l3 = l_scr[0], l_scr[1], l_scr[2]
    mx = jnp.maximum(jnp.maximum(l1, l2), l3)
    e1, e2, e3 = jnp.exp(l1 - mx), jnp.exp(l2 - mx), jnp.exp(l3 - mx)
    inv = 1.0 / (e1 + e2 + e3)
    w_scr[0], w_scr[1], w_scr[2] = e1 * inv, e2 * inv, e3 * inv
    for g, r in enumerate(dilations):
        for c in range(r):
            rows = tok_rows(c, r)
            wr = w_scr[g, rows, :]
            for h in range(DIL_HEADS):
                sl = slice(h * DIL_HEAD_DIM, (h + 1) * DIL_HEAD_DIM)
                part = wr[:, h:h + 1] * o_refs[g][0, c, :, sl].astype(F32)
                if g == 0:
                    ob_scr[h, rows, :] = part
                else:
                    ob_scr[h, rows, :] += part
    o_b = jnp.concatenate([ob_scr[h].astype(BF16) for h in range(DIL_HEADS)], axis=1)
    y_b = jnp.dot(o_b, wb_ref[...], preferred_element_type=F32)
    merged = _sigmoid(ga_ref[...].astype(F32)) * y_a + _sigmoid(gb_ref[...].astype(F32)) * y_b
    mix = jnp.dot(merged.astype(BF16), wo_ref[...], preferred_element_type=F32)
    y = DN_ALPHA * x_ref[...] + g2_ref[0] * mix
    out_ref[...] = _layer_norm(y, lng_ref[...], lnb_ref[...])


def _merge(oa2d, o_groups, lse_groups, p2d, x2d, mods3, k_gate, w_a, w_b, w_o, ln_g, ln_b, seq,
           dilations, tm=256):
    m, d = x2d.shape
    tiles_per_seq = seq // tm
    resident = pl.Buffered(1)
    n_g = len(dilations)

    def row_spec(width, col=0):
        return pl.BlockSpec((tm, width), lambda i: (i, col))

    def class_spec(r, width):
        return pl.BlockSpec((1, r, tm // r, width),
                            lambda i: (i // tiles_per_seq, 0, i % tiles_per_seq, 0))

    def const_spec(shape):
        return pl.BlockSpec(shape, lambda i: (0, 0), pipeline_mode=resident)

    return pl.pallas_call(
        functools.partial(_merge_kernel, dilations=dilations),
        grid=(m // tm,),
        in_specs=[
            row_spec(GLA_VW),
            *[class_spec(r, DIL_OUT) for r in dilations],
            *[class_spec(r, LANES) for r in dilations],
            row_spec(D_MODEL, OFF_GA // D_MODEL), row_spec(D_MODEL, OFF_GB // D_MODEL),
            row_spec(d),
            pl.BlockSpec((1, 1, d), lambda i: ((i // tiles_per_seq) * N_MOD + k_gate, 0, 0)),
            const_spec(w_a.shape), const_spec(w_b.shape), const_spec(w_o.shape),
            const_spec((1, d)), const_spec((1, d)),
        ],
        out_specs=row_spec(d),
        out_shape=jax.ShapeDtypeStruct((m, d), F32),
        scratch_shapes=[pltpu.VMEM((n_g, tm, LANES), F32), pltpu.VMEM((n_g, tm, LANES), F32),
                        pltpu.VMEM((DIL_HEADS, tm, DIL_HEAD_DIM), F32)],
        compiler_params=_cparams(("parallel",)),
        name="merge",
    )(oa2d, *o_groups, *lse_groups, p2d, p2d, x2d, mods3, w_a, w_b, w_o, ln_g, ln_b)


PACK_W = 1024
PACK_COPY, PACK_SHIFTED, PACK_ZERO, PACK_LOW_RANK = range(4)


def _pack_plan():
    src, o = {}, 0
    for name, wdt in zip(("gq", "gk", "gv", "gr", "glr", "dq", "dk", "dv", "ga", "gb"),
                         (GLA_QK, GLA_QK, GLA_VW, GLA_VW, GLA_GATE_RANK, DIL_W, DIL_W, DIL_W,
                          D_MODEL, D_MODEL)):
        src[name] = (o, wdt)
        o += wdt
    plan = []

    def copy(first, width):
        for c in range(first, first + width, PACK_W):
            plan.append((c, PACK_COPY if c % PACK_W == 0 else PACK_SHIFTED))

    for name in ("gq", "gk", "gv", "gr", "ga", "gb"):
        copy(*src[name])
    plan += [(0, PACK_ZERO)] * ((OFF_DIL - PROJ_W) // PACK_W)
    for g in range(DIL_GROUPS):
        for name in ("dq", "dk", "dv"):
            copy(src[name][0] + g * DIL_OUT, DIL_OUT)
    plan.append((src["glr"][0], PACK_LOW_RANK))
    return plan


def _pack_kernel(plan_ref, a_ref, b_ref, o_ref):
    kind = plan_ref[2, pl.program_id(0)]
    rank = GLA_GATE_RANK

    @pl.when(kind == PACK_COPY)
    def _():
        o_ref[...] = a_ref[...].T.astype(BF16)

    @pl.when(kind == PACK_SHIFTED)
    def _():
        o_ref[...] = jnp.concatenate([a_ref[rank:, :], b_ref[...]], axis=0).T.astype(BF16)

    @pl.when(kind == PACK_ZERO)
    def _():
        o_ref[...] = jnp.zeros_like(o_ref)

    @pl.when(kind == PACK_LOW_RANK)
    def _():
        lane = lax.broadcasted_iota(jnp.int32, (o_ref.shape[0], LANES), 1)
        o_ref[...] = jnp.zeros_like(o_ref)
        o_ref[:, :LANES] = jnp.where(lane < rank, a_ref[:LANES, :].T, 0.0).astype(BF16)


def _pack_w_in(w_in, layer):
    _, d, width = w_in.shape
    w_t = jnp.swapaxes(w_in, 1, 2)
    plan = _pack_plan()
    first = [c - c % PACK_W for c, _ in plan]
    rank = GLA_GATE_RANK
    assert all(c % PACK_W in (0, rank) for c, _ in plan) and rank % SUBLANES_F32 == 0
    table = jnp.asarray([[c // PACK_W for c in first],
                         [min(c + PACK_W, width - rank) // rank for c in first],
                         [kind for _, kind in plan]], jnp.int32)
    return pl.pallas_call(
        _pack_kernel,
        grid_spec=pltpu.PrefetchScalarGridSpec(
            num_scalar_prefetch=1,
            grid=(len(plan),),
            in_specs=[pl.BlockSpec((None, PACK_W, d), lambda i, t: (layer, t[0, i], 0)),
                      pl.BlockSpec((None, rank, d), lambda i, t: (layer, t[1, i], 0))],
            out_specs=pl.BlockSpec((d, PACK_W), lambda i, t: (0, i)),
        ),
        out_shape=jax.ShapeDtypeStruct((d, len(plan) * PACK_W), BF16),
        compiler_params=_cparams(("arbitrary",)),
        name="pack_w_in",
    )(table, w_t, w_t)


def kernel(x, c, positions, w_ada, b_ada, ln1_g, ln1_b, w_ffn1_gu, w_ffn1_down, w_in, w_alpha2,
           b_alpha, gla_norm_g, w_branch_a, w_branch_b, w_out, ln2_g, ln2_b, w_ffn2_gu, w_ffn2_down,
           ln3_g, ln3_b):
    bsz, seq, d = x.shape
    m = bsz * seq
    x2d = x.reshape(m, d)
    c_pad = jnp.pad(c, ((0, 8 - bsz % 8 if bsz % 8 else 0), (0, 0)))
    dilations = tuple(r for _, r in DIL_PATTERNS)

    half = DIL_HEAD_DIM // 2
    freq = ROPE_THETA ** (-jnp.arange(half, dtype=F32) / half)
    freq2 = jnp.concatenate([freq, freq]).reshape(1, DIL_HEAD_DIM)
    rope_t = _rope_tables(positions.reshape(m // PROJ_TM, PROJ_TM // LANES, LANES), freq2, dilations)

    for l in range(DEPTH):
        mods = _mods(c_pad, w_ada[l], b_ada[l].reshape(1, -1))[:bsz]
        mods3 = mods.reshape(bsz * N_MOD, 1, d)

        x1 = _ffn(x2d, mods3, 0, 1, 2, w_ffn1_gu[l].astype(BF16), w_ffn1_down[l].astype(BF16),
                  ln1_g[l].reshape(1, d), ln1_b[l].reshape(1, d), seq)

        w_all = _pack_w_in(w_in, l)
        p2d, lr2d = _proj(x1, mods3, 3, 4, w_all, seq)
        p3 = p2d.reshape(bsz, seq, PROJ_W)

        w_a2 = jnp.pad(w_alpha2[l], ((0, LR_PAD - GLA_GATE_RANK), (0, 0))).astype(BF16)
        o_a = _gla(p3, lr2d.reshape(bsz, seq, LR_PAD), w_a2, b_alpha[l].reshape(1, GLA_QK),
                   gla_norm_g[l].reshape(GLA_HEADS, 1, GLA_DV))

        o_groups, lse_groups = [], []
        for g, r in enumerate(dilations):
            qkv = _proj_dil(x1, mods3, 3, 4, w_all, g, rope_t[2 * g], rope_t[2 * g + 1],
                            bsz, seq, r)
            o_g, lse_g = _dil_group(qkv)
            o_groups.append(o_g)
            lse_groups.append(lse_g)

        x2 = _merge(o_a.reshape(m, GLA_VW), o_groups, lse_groups, p2d, x1, mods3, 5,
                    w_branch_a[l].astype(BF16), w_branch_b[l].astype(BF16), w_out[l].astype(BF16),
                    ln2_g[l].reshape(1, d), ln2_b[l].reshape(1, d), seq, dilations)

        x2d = _ffn(x2, mods3, 6, 7, 8, w_ffn2_gu[l].astype(BF16), w_ffn2_down[l].astype(BF16),
                   ln3_g[l].reshape(1, d), ln3_b[l].reshape(1, d), seq)
    return x2d.reshape(bsz, seq, d)
```

```python
import functools

import jax
import jax.numpy as jnp
from jax import lax
from jax.experimental import pallas as pl
from jax.experimental.pallas import tpu as pltpu

F32 = jnp.float32
BF16 = jnp.bfloat16

D_MODEL = 2048
DEPTH = 1
D_FF = 5632
N_MOD = 9
LN_EPS = 1e-5
DN_ALPHA = (2.0 * DEPTH) ** 0.25

GLA_HEADS = 4
GLA_DK = 256
GLA_DV = 512
GLA_GATE_RANK = 16
GLA_TAU = 16.0
GLA_CHUNK = 64
GLA_QK = GLA_HEADS * GLA_DK
GLA_VW = GLA_HEADS * GLA_DV

DIL_PATTERNS = ((128, 1), (512, 4), (2048, 16))
DIL_GROUPS = len(DIL_PATTERNS)
DIL_HEADS = 8
DIL_HEAD_DIM = 128
DIL_BLOCK = 128
DIL_W = DIL_GROUPS * DIL_HEADS * DIL_HEAD_DIM
DIL_OUT = DIL_HEADS * DIL_HEAD_DIM
ROPE_THETA = 10000.0

LANES = 128
LR_PAD = LANES

PROJ_W = 2 * GLA_QK + 2 * GLA_VW + 2 * D_MODEL
OFF_GQ = 0
OFF_GK = GLA_QK
OFF_GV = 2 * GLA_QK
OFF_GR = OFF_GV + GLA_VW
OFF_GA = OFF_GR + GLA_VW
OFF_GB = OFF_GA + D_MODEL
DIL_QKV = 3 * DIL_OUT
OFF_DIL = -(-PROJ_W // DIL_QKV) * DIL_QKV
OFF_LR = OFF_DIL + DIL_GROUPS * DIL_QKV

NEG_BIG = -1e30

VMEM_LIMIT = 56 * 1024 * 1024
VMEM_LIMIT_BIG = 60 * 1024 * 1024


def _cparams(sem, vmem=VMEM_LIMIT):
    return pltpu.CompilerParams(dimension_semantics=sem, vmem_limit_bytes=vmem)


def _sigmoid(x):
    return 1.0 / (1.0 + jnp.exp(-x))


def _layer_norm(y, g, b):
    mu = jnp.mean(y, axis=-1, keepdims=True)
    d = y - mu
    var = jnp.mean(d * d, axis=-1, keepdims=True)
    return d * lax.rsqrt(var + LN_EPS) * g + b


def _mods_kernel(c_ref, w_ref, b_ref, o_ref):
    c = c_ref[...]
    c_act = (c * _sigmoid(c)).astype(BF16)
    o_ref[...] = jnp.dot(c_act, w_ref[...].astype(BF16), preferred_element_type=F32) + b_ref[...]


def _mods(c_pad, w_ada, b_ada, tn=1024):
    rows, d = c_pad.shape
    n = w_ada.shape[1]
    return pl.pallas_call(
        _mods_kernel,
        grid=(n // tn,),
        in_specs=[
            pl.BlockSpec((rows, d), lambda j: (0, 0)),
            pl.BlockSpec((d, tn), lambda j: (0, j)),
            pl.BlockSpec((1, tn), lambda j: (0, j)),
        ],
        out_specs=pl.BlockSpec((rows, tn), lambda j: (0, j)),
        out_shape=jax.ShapeDtypeStruct((rows, n), F32),
        compiler_params=_cparams(("arbitrary",)),
        name="mods",
    )(c_pad, w_ada, b_ada)


FFN_ROW_CHUNK = 512


def _ffn_kernel(x_ref, sh_ref, sc_ref, g_ref, wg_ref, wu_ref, wd_ref, lng_ref, lnb_ref,
                o_ref, h_scr):
    j = pl.program_id(1)

    @pl.when(j == 0)
    def _():
        h_scr[...] = (x_ref[...] * (1.0 + sc_ref[0]) + sh_ref[0]).astype(BF16)
        o_ref[...] = jnp.zeros_like(o_ref)

    for r0 in range(0, o_ref.shape[0], FFN_ROW_CHUNK):
        rs = slice(r0, r0 + FFN_ROW_CHUNK)
        h = h_scr[rs, :]
        gate = jnp.dot(h, wg_ref[...], preferred_element_type=F32)
        up = jnp.dot(h, wu_ref[...], preferred_element_type=F32)
        act = (gate * _sigmoid(gate) * up).astype(BF16)
        o_ref[rs, :] += jnp.dot(act, wd_ref[...], preferred_element_type=F32)

    @pl.when(j == pl.num_programs(1) - 1)
    def _():
        y = DN_ALPHA * x_ref[...] + 0.5 * g_ref[0] * o_ref[...]
        o_ref[...] = _layer_norm(y, lng_ref[...], lnb_ref[...])


def _ffn(x2d, mods3, k_shift, k_scale, k_gate, w_gu, w_down, ln_g, ln_b, seq, tm=1024, tf=512):
    m, d = x2d.shape
    n_ff = w_down.shape[0]
    nj = n_ff // tf
    tiles_per_seq = seq // tm

    def mod_spec(k):
        return pl.BlockSpec((1, 1, d), lambda i, j: ((i // tiles_per_seq) * N_MOD + k, 0, 0))

    return pl.pallas_call(
        _ffn_kernel,
        grid=(m // tm, nj),
        in_specs=[
            pl.BlockSpec((tm, d), lambda i, j: (i, 0)),
            mod_spec(k_shift), mod_spec(k_scale), mod_spec(k_gate),
            pl.BlockSpec((d, tf), lambda i, j: (0, j)),
            pl.BlockSpec((d, tf), lambda i, j: (0, j + nj)),
            pl.BlockSpec((tf, d), lambda i, j: (j, 0)),
            pl.BlockSpec((1, d), lambda i, j: (0, 0)),
            pl.BlockSpec((1, d), lambda i, j: (0, 0)),
        ],
        out_specs=pl.BlockSpec((tm, d), lambda i, j: (i, 0)),
        out_shape=jax.ShapeDtypeStruct((m, d), F32),
        scratch_shapes=[pltpu.VMEM((tm, d), BF16)],
        compiler_params=_cparams(("parallel", "arbitrary"), VMEM_LIMIT_BIG),
        name="ffn",
    )(x2d, mods3, mods3, mods3, w_gu, w_gu, w_down, ln_g, ln_b)


PROJ_TM = 1024


def _rope_kernel(pos_ref, freq_ref, *out_refs, dilations):
    cos_ref, sin_ref = out_refs[0], out_refs[1]
    tm = cos_ref.shape[0]
    pos_t = pos_ref[0].astype(F32).T
    half = DIL_HEAD_DIM // 2
    low = lax.broadcasted_iota(jnp.int32, (LANES, DIL_HEAD_DIM), 1) < half
    n_blocks = tm // LANES
    for k in range(n_blocks // 2):
        k2 = k + n_blocks // 2
        ang = jnp.where(low, pos_t[:, k:k + 1], pos_t[:, k2:k2 + 1]) * freq_ref[...]
        c, s = jnp.cos(ang), jnp.sin(ang)
        c_sw, s_sw = pltpu.roll(c, half, 1), pltpu.roll(s, half, 1)
        cos_ref[k * LANES:(k + 1) * LANES, :] = jnp.where(low, c, c_sw)
        sin_ref[k * LANES:(k + 1) * LANES, :] = jnp.where(low, -s, s_sw)
        cos_ref[k2 * LANES:(k2 + 1) * LANES, :] = jnp.where(low, c_sw, c)
        sin_ref[k2 * LANES:(k2 + 1) * LANES, :] = jnp.where(low, -s_sw, s)
    for gi, r in enumerate(dilations[1:]):
        tl = tm // r
        for c in range(r):
            out_refs[2 + 2 * gi][c * tl:(c + 1) * tl, :] = cos_ref[pl.ds(c, tl, stride=r), :]
            out_refs[3 + 2 * gi][c * tl:(c + 1) * tl, :] = sin_ref[pl.ds(c, tl, stride=r), :]


def _rope_tables(pos3, freq2, dilations, tm=PROJ_TM):
    assert dilations[0] == 1
    m = pos3.shape[0] * tm
    n_out = 2 * len(dilations)
    return pl.pallas_call(
        functools.partial(_rope_kernel, dilations=dilations),
        grid=(m // tm,),
        in_specs=[
            pl.BlockSpec((1, tm // LANES, LANES), lambda i: (i, 0, 0)),
            pl.BlockSpec((1, DIL_HEAD_DIM), lambda i: (0, 0)),
        ],
        out_specs=[pl.BlockSpec((tm, DIL_HEAD_DIM), lambda i: (i, 0))] * n_out,
        out_shape=[jax.ShapeDtypeStruct((m, DIL_HEAD_DIM), F32)] * n_out,
        compiler_params=_cparams(("parallel",)),
        name="rope_tables",
    )(pos3, freq2)


def _proj_kernel(x_ref, sh_ref, sc_ref, w_ref, wlr_ref, o_ref, lr_ref, h_scr):
    @pl.when(pl.program_id(1) == 0)
    def _():
        h = (x_ref[...] * (1.0 + sc_ref[0]) + sh_ref[0]).astype(BF16)
        h_scr[...] = h
        lr_ref[...] = jnp.dot(h, wlr_ref[...], preferred_element_type=F32)

    o_ref[...] = jnp.dot(h_scr[...], w_ref[...], preferred_element_type=F32).astype(BF16)


def _proj(x2d, mods3, k_shift, k_scale, w_all, seq, tm=PROJ_TM, tn=2048):
    m, d = x2d.shape
    n = PROJ_W
    tiles_per_seq = seq // tm

    def mod_spec(k):
        return pl.BlockSpec((1, 1, d), lambda i, j: ((i // tiles_per_seq) * N_MOD + k, 0, 0))

    return pl.pallas_call(
        _proj_kernel,
        grid=(m // tm, n // tn),
        in_specs=[
            pl.BlockSpec((tm, d), lambda i, j: (i, 0)),
            mod_spec(k_shift), mod_spec(k_scale),
            pl.BlockSpec((d, tn), lambda i, j: (0, j)),
            pl.BlockSpec((d, LR_PAD), lambda i, j: (0, OFF_LR // LR_PAD)),
        ],
        out_specs=[
            pl.BlockSpec((tm, tn), lambda i, j: (i, j)),
            pl.BlockSpec((tm, LR_PAD), lambda i, j: (i, 0)),
        ],
        out_shape=[
            jax.ShapeDtypeStruct((m, n), BF16),
            jax.ShapeDtypeStruct((m, LR_PAD), F32),
        ],
        scratch_shapes=[pltpu.VMEM((tm, d), BF16)],
        compiler_params=_cparams(("parallel", "arbitrary")),
        name="proj",
    )(x2d, mods3, mods3, w_all, w_all)


def _proj_dil_kernel(x_ref, sh_ref, sc_ref, w_ref, cos_ref, sin_ref, o_ref, h_scr, slab_scr, *, r):
    tm, d = x_ref.shape
    tl = tm // r
    if r == 1:
        h_scr[...] = (x_ref[...] * (1.0 + sc_ref[0]) + sh_ref[0]).astype(BF16)
    else:
        for k in range(d // LANES):
            sl = slice(k * LANES, (k + 1) * LANES)
            slab_scr[k] = x_ref[:, sl] * (1.0 + sc_ref[0, :, sl]) + sh_ref[0, :, sl]
        for c in range(r):
            for k in range(d // LANES):
                sl = slice(k * LANES, (k + 1) * LANES)
                h_scr[c * tl:(c + 1) * tl, sl] = slab_scr[k, pl.ds(c, tl, stride=r), :].astype(BF16)
    _proj_dil_qkv(h_scr, w_ref, cos_ref, sin_ref, o_ref, r)


def _proj_dil_qkv(h_scr, w_ref, cos_ref, sin_ref, o_ref, r):
    tl = h_scr.shape[0] // r
    for part, scale in enumerate((DIL_HEAD_DIM ** -0.5, 1.0, None)):
        cols = slice(part * DIL_OUT, (part + 1) * DIL_OUT)
        t = jnp.dot(h_scr[...], w_ref[:, cols], preferred_element_type=F32)
        if scale is None:
            for c in range(r):
                o_ref[0, c, :, cols] = t[c * tl:(c + 1) * tl, :].astype(BF16)
            continue
        cos = cos_ref[...] * scale
        sin = sin_ref[...] * scale
        for s in range(DIL_HEADS):
            ts = t[:, s * DIL_HEAD_DIM:(s + 1) * DIL_HEAD_DIM]
            rot = (ts * cos + pltpu.roll(ts, DIL_HEAD_DIM // 2, 1) * sin).astype(BF16)
            lo = part * DIL_OUT + s * DIL_HEAD_DIM
            for c in range(r):
                o_ref[0, c, :, lo:lo + DIL_HEAD_DIM] = rot[c * tl:(c + 1) * tl, :]


def _proj_dil_dma_kernel(x_hbm, sh_ref, sc_ref, w_ref, cos_ref, sin_ref, o_ref, h_scr, x_buf, sem,
                         *, r):
    i = pl.program_id(0)
    tl = h_scr.shape[0] // r
    slot = i % 2

    def tile_copies(tile, buf_slot):
        return [pltpu.make_async_copy(x_hbm.at[pl.ds(tile * tl, tl), c], x_buf.at[buf_slot, c],
                                      sem.at[buf_slot]) for c in range(r)]

    @pl.when(i == 0)
    def _():
        for cp in tile_copies(0, 0):
            cp.start()

    @pl.when(i + 1 < pl.num_programs(0))
    def _():
        for cp in tile_copies(i + 1, 1 - slot):
            cp.start()

    for cp in tile_copies(i, slot):
        cp.wait()
    for c in range(r):
        h_scr[c * tl:(c + 1) * tl, :] = (x_buf[slot, c] * (1.0 + sc_ref[0]) + sh_ref[0]).astype(BF16)
    _proj_dil_qkv(h_scr, w_ref, cos_ref, sin_ref, o_ref, r)


SUBLANES_F32 = 8


def _proj_dil(x2d, mods3, k_shift, k_scale, w_all, g, cos_t, sin_t, bsz, seq, r, tm=PROJ_TM):
    m, d = x2d.shape
    tiles_per_seq = seq // tm
    tl = tm // r
    dma_gather = r % SUBLANES_F32 == 0

    def mod_spec(k):
        return pl.BlockSpec((1, 1, d), lambda i: ((i // tiles_per_seq) * N_MOD + k, 0, 0))

    if dma_gather:
        body = functools.partial(_proj_dil_dma_kernel, r=r)
        x_arg = x2d.reshape(m // r, r, d)
        x_spec = pl.BlockSpec(memory_space=pl.ANY)
        scratch = [pltpu.VMEM((tm, d), BF16), pltpu.VMEM((2, r, tl, d), F32),
                   pltpu.SemaphoreType.DMA((2,))]
        semantics = ("arbitrary",)
    else:
        body = functools.partial(_proj_dil_kernel, r=r)
        x_arg = x2d
        x_spec = pl.BlockSpec((tm, d), lambda i: (i, 0))
        scratch = [pltpu.VMEM((tm, d), BF16),
                   pltpu.VMEM((d // LANES, tm, LANES) if r > 1 else (1, 8, LANES), F32)]
        semantics = ("parallel",)

    return pl.pallas_call(
        body,
        grid=(m // tm,),
        in_specs=[
            x_spec,
            mod_spec(k_shift), mod_spec(k_scale),
            pl.BlockSpec((d, DIL_QKV), lambda i: (0, OFF_DIL // DIL_QKV + g),
                         pipeline_mode=pl.Buffered(1)),
            pl.BlockSpec((tm, DIL_HEAD_DIM), lambda i: (i, 0)),
            pl.BlockSpec((tm, DIL_HEAD_DIM), lambda i: (i, 0)),
        ],
        out_specs=pl.BlockSpec((1, r, tl, DIL_QKV),
                               lambda i: (i // tiles_per_seq, 0, i % tiles_per_seq, 0)),
        out_shape=jax.ShapeDtypeStruct((bsz, r, seq // r, DIL_QKV), BF16),
        scratch_shapes=scratch,
        compiler_params=_cparams(semantics, VMEM_LIMIT_BIG),
        name=f"proj_dil_r{r}",
    )(x_arg, mods3, mods3, w_all, cos_t, sin_t)


def _gla_kernel(q_ref, k_ref, v_ref, r_ref, lr_ref, wa_ref, ba_ref, ng_ref, o_ref,
                st_scr, b_scr, qd_scr, qh_scr, ki_scr, ke_scr, kp_scr, kh_scr, a_scr, *, n_chunks):
    c_len = GLA_CHUNK
    tc = q_ref.shape[1]
    hb = 4 * c_len
    pw = 2 * c_len
    n_pairs = n_chunks // 2

    @pl.when(pl.program_id(2) == 0)
    def _():
        st_scr[...] = jnp.zeros_like(st_scr)

    ti = lax.broadcasted_iota(jnp.int32, (hb, hb), 0)
    tj = lax.broadcasted_iota(jnp.int32, (hb, hb), 1)
    tri = jnp.logical_and(ti >= tj, ti // c_len == tj // c_len).astype(BF16)
    for s in range(tc // hb):
        rows = slice(s * hb, (s + 1) * hb)
        logits = jnp.dot(lr_ref[0, rows, :].astype(BF16), wa_ref[...],
                         preferred_element_type=F32) + ba_ref[...]
        log_a = (jnp.minimum(logits, 0.0) - jnp.log(1.0 + jnp.exp(-jnp.abs(logits)))) / GLA_TAU
        p1 = log_a.astype(BF16)
        r1 = log_a - p1.astype(F32)
        p2 = r1.astype(BF16)
        p3 = (r1 - p2.astype(F32)).astype(BF16)
        b_scr[rows, :] = (jnp.dot(tri, p1, preferred_element_type=F32)
                          + jnp.dot(tri, p2, preferred_element_type=F32)
                          + jnp.dot(tri, p3, preferred_element_type=F32))

    nt = (((1,), (1,)), ((), ()))

    def chunk_rows(c):
        return slice(c * c_len, (c + 1) * c_len)

    def pair_rows(p):
        return slice(p * pw, (p + 1) * pw)

    b_last = [b_scr[(c + 1) * c_len - 1:(c + 1) * c_len, :] for c in range(n_chunks)]
    h_log = [jnp.zeros_like(b_last[0])]
    for c in range(n_chunks):
        h_log.append(h_log[-1] + b_last[c])

    for c in range(n_chunks):
        rows = chunk_rows(c)
        b = b_scr[rows, :]
        q = q_ref[0, rows, :].astype(F32)
        k = k_ref[0, rows, :].astype(F32)
        qd = q * jnp.exp(b) * (GLA_DK ** -0.5)
        ke = k * jnp.exp(b_last[c] - b)
        qd_scr[rows, :] = qd
        qh_scr[rows, :] = (qd * jnp.exp(h_log[c])).astype(BF16)
        ki_scr[rows, :] = (k * jnp.exp(-b)).astype(BF16)
        ke_scr[rows, :] = ke.astype(BF16)
        kp_scr[rows, :] = (ke * jnp.exp(b_last[c + 1]) if c % 2 == 0 else ke).astype(BF16)
        kh_scr[rows, :] = (ke * jnp.exp(h_log[n_chunks] - h_log[c + 1])).astype(BF16)

    pi = lax.broadcasted_iota(jnp.int32, (pw, pw), 0)
    pj = lax.broadcasted_iota(jnp.int32, (pw, pw), 1)
    diag = jnp.logical_and(pi >= pj, pi // c_len == pj // c_len)
    lower = jnp.logical_and(pi >= c_len, pj < c_len)
    for p in range(n_pairs):
        first = chunk_rows(2 * p)
        keys = jnp.concatenate([ki_scr[pair_rows(p), :], ke_scr[first, :], ke_scr[first, :]], axis=0)
        s = lax.dot_general(qd_scr[pair_rows(p), :].astype(BF16), keys, nt,
                            preferred_element_type=F32)
        s = jnp.where(diag, s[:, :pw], jnp.where(lower, s[:, pw:], 0.0))
        a_scr[pair_rows(p), pair_rows(p)] = s.astype(BF16)

    for p in range(n_pairs - 1):
        c0 = 2 * p + 2
        lhs = jnp.concatenate(
            [(qd_scr[chunk_rows(c), :] * jnp.exp(h_log[c] - h_log[c0])).astype(BF16)
             for c in range(c0, n_chunks)], axis=0)
        s = lax.dot_general(lhs, kp_scr[pair_rows(p), :], nt, preferred_element_type=F32)
        a_scr[c0 * c_len:, pair_rows(p)] = s.astype(BF16)

    half = 2 * pw
    for hf in range(tc // half):
        a_scr[hf * half:hf * half + pw, hf * half + pw:(hf + 1) * half] = jnp.zeros((pw, pw), BF16)

    st = st_scr[...]
    st_b = st.astype(BF16)
    for hf in range(tc // half):
        rows = slice(hf * half, (hf + 1) * half)
        keys = (hf + 1) * half
        o = jnp.dot(a_scr[rows, :keys], v_ref[0, :keys, :], preferred_element_type=F32)
        o += jnp.dot(qh_scr[rows, :], st_b, preferred_element_type=F32)
        o = o * lax.rsqrt(jnp.mean(o * o, axis=-1, keepdims=True) + LN_EPS) * ng_ref[0]
        r = r_ref[0, rows, :].astype(F32)
        o_ref[0, rows, :] = (o * (r * _sigmoid(r))).astype(BF16)

    d_col = jnp.broadcast_to(jnp.exp(h_log[n_chunks]), (8, GLA_DK)).T[:, :1]
    st_scr[...] = st * d_col + lax.dot_general(
        kh_scr[...], v_ref[0], (((0,), (0,)), ((), ())), preferred_element_type=F32)


def _gla(p3, lr3, w_a2, b_a, norm_g3, tc=512):
    bsz, seq, _ = p3.shape
    nq, nv, nr = OFF_GK // GLA_DK, OFF_GV // GLA_DV, OFF_GR // GLA_DV
    return pl.pallas_call(
        functools.partial(_gla_kernel, n_chunks=tc // GLA_CHUNK),
        grid=(bsz, GLA_HEADS, seq // tc),
        in_specs=[
            pl.BlockSpec((1, tc, GLA_DK), lambda b, h, t: (b, t, h)),
            pl.BlockSpec((1, tc, GLA_DK), lambda b, h, t: (b, t, nq + h)),
            pl.BlockSpec((1, tc, GLA_DV), lambda b, h, t: (b, t, nv + h)),
            pl.BlockSpec((1, tc, GLA_DV), lambda b, h, t: (b, t, nr + h)),
            pl.BlockSpec((1, tc, LR_PAD), lambda b, h, t: (b, t, 0)),
            pl.BlockSpec((LR_PAD, GLA_DK), lambda b, h, t: (0, h)),
            pl.BlockSpec((1, GLA_DK), lambda b, h, t: (0, h)),
            pl.BlockSpec((1, 1, GLA_DV), lambda b, h, t: (h, 0, 0)),
        ],
        out_specs=pl.BlockSpec((1, tc, GLA_DV), lambda b, h, t: (b, t, h)),
        out_shape=jax.ShapeDtypeStruct((bsz, seq, GLA_VW), BF16),
        scratch_shapes=[pltpu.VMEM((GLA_DK, GLA_DV), F32), pltpu.VMEM((tc, GLA_DK), F32),
                        pltpu.VMEM((tc, GLA_DK), F32)]
                       + [pltpu.VMEM((tc, GLA_DK), BF16)] * 5 + [pltpu.VMEM((tc, tc), BF16)],
        compiler_params=_cparams(("parallel", "parallel", "arbitrary")),
        name="gla",
    )(p3, p3, p3, p3, lr3, w_a2, b_a, norm_g3)


def _dil_kernel(q_ref, k_ref, v_ref, kp_ref, vp_ref, o_ref, lse_ref, s_scr, m_scr, p_scr, *, nblk):
    n = pl.program_id(2)
    blk = DIL_BLOCK
    qi = lax.broadcasted_iota(jnp.int32, (blk, 2 * blk), 0)
    kj = lax.broadcasted_iota(jnp.int32, (blk, 2 * blk), 1)
    band = jnp.logical_or(jnp.logical_and(kj < blk, kj >= qi),
                          jnp.logical_and(kj >= blk, kj - blk <= qi))
    band_first = jnp.logical_and(band, jnp.logical_or(kj >= blk, n > 0))
    nt = (((1,), (1,)), ((), ()))

    def rows(i):
        return slice(i * blk, (i + 1) * blk)

    def cols(h):
        return slice(h * DIL_HEAD_DIM, (h + 1) * DIL_HEAD_DIM)

    for i in range(nblk):
        for h in range(DIL_HEADS):
            q = q_ref[0, 0, rows(i), cols(h)]
            k_prev = kp_ref[0, 0, :, cols(h)] if i == 0 else k_ref[0, 0, rows(i - 1), cols(h)]
            s_p = lax.dot_general(q, k_prev, nt, preferred_element_type=F32)
            s_c = lax.dot_general(q, k_ref[0, 0, rows(i), cols(h)], nt, preferred_element_type=F32)
            s = jnp.concatenate([s_p, s_c], axis=1)
            s_scr[rows(i * DIL_HEADS + h), :] = jnp.where(band_first if i == 0 else band, s, NEG_BIG)

    s_all = s_scr[...]
    m_all = jnp.max(s_all, axis=-1, keepdims=True)
    p_scr[...] = jnp.exp(s_all - m_all).astype(BF16)
    m_scr[...] = jnp.broadcast_to(m_all, m_scr.shape)

    lane = lax.broadcasted_iota(jnp.int32, (blk, LANES), 1)
    ones = jnp.ones((2 * blk, DIL_HEAD_DIM), BF16)
    for i in range(nblk):
        lse_tile = jnp.zeros((blk, LANES), F32)
        for h in range(DIL_HEADS):
            u = i * DIL_HEADS + h
            v_prev = vp_ref[0, 0, :, cols(h)] if i == 0 else v_ref[0, 0, rows(i - 1), cols(h)]
            v_aug = jnp.concatenate(
                [jnp.concatenate([v_prev, v_ref[0, 0, rows(i), cols(h)]], axis=0), ones], axis=1)
            acc = jnp.dot(p_scr[rows(u), :], v_aug, preferred_element_type=F32)
            den = acc[:, DIL_HEAD_DIM:]
            o_ref[0, 0, rows(i), cols(h)] = (acc[:, :DIL_HEAD_DIM] / den).astype(BF16)
            lse_tile = jnp.where(lane == h, m_scr[rows(u), :] + jnp.log(den), lse_tile)
        lse_ref[0, 0, rows(i), :] = lse_tile


def _dil_group(qkv, nblk=8):
    bsz, r, length, _ = qkv.shape
    nb = length // DIL_BLOCK
    nblk = min(nblk, nb)
    assert nb % nblk == 0
    tq = nblk * DIL_BLOCK
    units = nblk * DIL_HEADS * DIL_BLOCK

    def cur_spec(u):
        return pl.BlockSpec((1, 1, tq, DIL_OUT), lambda b, c, n: (b, c, n, u))

    def prev_spec(u):
        return pl.BlockSpec((1, 1, DIL_BLOCK, DIL_OUT),
                            lambda b, c, n: (b, c, jnp.maximum(n * nblk - 1, 0), u))

    return pl.pallas_call(
        functools.partial(_dil_kernel, nblk=nblk),
        grid=(bsz, r, nb // nblk),
        in_specs=[cur_spec(0), cur_spec(1), cur_spec(2), prev_spec(1), prev_spec(2)],
        out_specs=[
            pl.BlockSpec((1, 1, tq, DIL_OUT), lambda b, c, n: (b, c, n, 0)),
            pl.BlockSpec((1, 1, tq, LANES), lambda b, c, n: (b, c, n, 0)),
        ],
        out_shape=[
            jax.ShapeDtypeStruct((bsz, r, length, DIL_OUT), BF16),
            jax.ShapeDtypeStruct((bsz, r, length, LANES), F32),
        ],
        scratch_shapes=[pltpu.VMEM((units, 2 * DIL_BLOCK), F32),
                        pltpu.VMEM((units, LANES), F32),
                        pltpu.VMEM((units, 2 * DIL_BLOCK), BF16)],
        compiler_params=_cparams(("parallel", "parallel", "parallel")),
        name=f"dilattn_r{r}",
    )(qkv, qkv, qkv, qkv, qkv)


def _merge_kernel(oa_ref, o1_ref, o2_ref, o3_ref, l1_ref, l2_ref, l3_ref, ga_ref, gb_ref, x_ref,
                  g2_ref, wa_ref, wb_ref, wo_ref, lng_ref, lnb_ref, out_ref, l_scr, w_scr, ob_scr,
                  *, dilations):
    y_a = jnp.dot(oa_ref[...], wa_ref[...], preferred_element_type=F32)
    o_refs = (o1_ref, o2_ref, o3_ref)
    l_refs = (l1_ref, l2_ref, l3_ref)
    tm = oa_ref.shape[0]

    def tok_rows(c, r):
        return pl.ds(c, tm // r, stride=r) if r > 1 else slice(None)

    for g, r in enumerate(dilations):
        for c in range(r):
            l_scr[g, tok_rows(c, r), :] = l_refs[g][0, c]
    l1, l2, l3 = l_scr[0], l_scr[1], l_scr[2]
    mx = jnp.maximum(jnp.maximum(l1, l2), l3)
    e1, e2, e3 = jnp.exp(l1 - mx), jnp.exp(l2 - mx), jnp.exp(l3 - mx)
    inv = 1.0 / (e1 + e2 + e3)
    w_scr[0], w_scr[1], w_scr[2] = e1 * inv, e2 * inv, e3 * inv
    for g, r in enumerate(dilations):
        for c in range(r):
            rows = tok_rows(c, r)
            wr = w_scr[g, rows, :]
            for h in range(DIL_HEADS):
                sl = slice(h * DIL_HEAD_DIM, (h + 1) * DIL_HEAD_DIM)
                part = wr[:, h:h + 1] * o_refs[g][0, c, :, sl].astype(F32)
                if g == 0:
                    ob_scr[h, rows, :] = part
                else:
                    ob_scr[h, rows, :] += part
    o_b = jnp.concatenate([ob_scr[h].astype(BF16) for h in range(DIL_HEADS)], axis=1)
    y_b = jnp.dot(o_b, wb_ref[...], preferred_element_type=F32)
    merged = _sigmoid(ga_ref[...].astype(F32)) * y_a + _sigmoid(gb_ref[...].astype(F32)) * y_b
    mix = jnp.dot(merged.astype(BF16), wo_ref[...], preferred_element_type=F32)
    y = DN_ALPHA * x_ref[...] + g2_ref[0] * mix
    out_ref[...] = _layer_norm(y, lng_ref[...], lnb_ref[...])


def _merge(oa2d, o_groups, lse_groups, p2d, x2d, mods3, k_gate, w_a, w_b, w_o, ln_g, ln_b, seq,
           dilations, tm=256):
    m, d = x2d.shape
    tiles_per_seq = seq // tm
    resident = pl.Buffered(1)
    n_g = len(dilations)

    def row_spec(width, col=0):
        return pl.BlockSpec((tm, width), lambda i: (i, col))

    def class_spec(r, width):
        return pl.BlockSpec((1, r, tm // r, width),
                            lambda i: (i // tiles_per_seq, 0, i % tiles_per_seq, 0))

    def const_spec(shape):
        return pl.BlockSpec(shape, lambda i: (0, 0), pipeline_mode=resident)

    return pl.pallas_call(
        functools.partial(_merge_kernel, dilations=dilations),
        grid=(m // tm,),
        in_specs=[
            row_spec(GLA_VW),
            *[class_spec(r, DIL_OUT) for r in dilations],
            *[class_spec(r, LANES) for r in dilations],
            row_spec(D_MODEL, OFF_GA // D_MODEL), row_spec(D_MODEL, OFF_GB // D_MODEL),
            row_spec(d),
            pl.BlockSpec((1, 1, d), lambda i: ((i // tiles_per_seq) * N_MOD + k_gate, 0, 0)),
            const_spec(w_a.shape), const_spec(w_b.shape), const_spec(w_o.shape),
            const_spec((1, d)), const_spec((1, d)),
        ],
        out_specs=row_spec(d),
        out_shape=jax.ShapeDtypeStruct((m, d), F32),
        scratch_shapes=[pltpu.VMEM((n_g, tm, LANES), F32), pltpu.VMEM((n_g, tm, LANES), F32),
                        pltpu.VMEM((DIL_HEADS, tm, DIL_HEAD_DIM), F32)],
        compiler_params=_cparams(("parallel",)),
        name="merge",
    )(oa2d, *o_groups, *lse_groups, p2d, p2d, x2d, mods3, w_a, w_b, w_o, ln_g, ln_b)


PACK_W = 1024
PACK_COPY, PACK_SHIFTED, PACK_ZERO, PACK_LOW_RANK = range(4)


def _pack_plan():
    src, o = {}, 0
    for name, wdt in zip(("gq", "gk", "gv", "gr", "glr", "dq", "dk", "dv", "ga", "gb"),
                         (GLA_QK, GLA_QK, GLA_VW, GLA_VW, GLA_GATE_RANK, DIL_W, DIL_W, DIL_W,
                          D_MODEL, D_MODEL)):
        src[name] = (o, wdt)
        o += wdt
    plan = []

    def copy(first, width):
        for c in range(first, first + width, PACK_W):
            plan.append((c, PACK_COPY if c % PACK_W == 0 else PACK_SHIFTED))

    for name in ("gq", "gk", "gv", "gr", "ga", "gb"):
        copy(*src[name])
    plan += [(0, PACK_ZERO)] * ((OFF_DIL - PROJ_W) // PACK_W)
    for g in range(DIL_GROUPS):
        for name in ("dq", "dk", "dv"):
            copy(src[name][0] + g * DIL_OUT, DIL_OUT)
    plan.append((src["glr"][0], PACK_LOW_RANK))
    return plan


def _pack_kernel(plan_ref, a_ref, b_ref, o_ref):
    kind = plan_ref[2, pl.program_id(0)]
    rank = GLA_GATE_RANK

    @pl.when(kind == PACK_COPY)
    def _():
        o_ref[...] = a_ref[...].T.astype(BF16)

    @pl.when(kind == PACK_SHIFTED)
    def _():
        o_ref[...] = jnp.concatenate([a_ref[rank:, :], b_ref[...]], axis=0).T.astype(BF16)

    @pl.when(kind == PACK_ZERO)
    def _():
        o_ref[...] = jnp.zeros_like(o_ref)

    @pl.when(kind == PACK_LOW_RANK)
    def _():
        lane = lax.broadcasted_iota(jnp.int32, (o_ref.shape[0], LANES), 1)
        o_ref[...] = jnp.zeros_like(o_ref)
        o_ref[:, :LANES] = jnp.where(lane < rank, a_ref[:LANES, :].T, 0.0).astype(BF16)


def _pack_w_in(w_in, layer):
    _, d, width = w_in.shape
    w_t = jnp.swapaxes(w_in, 1, 2)
    plan = _pack_plan()
    first = [c - c % PACK_W for c, _ in plan]
    rank = GLA_GATE_RANK
    assert all(c % PACK_W in (0, rank) for c, _ in plan) and rank % SUBLANES_F32 == 0
    table = jnp.asarray([[c // PACK_W for c in first],
                         [min(c + PACK_W, width - rank) // rank for c in first],
                         [kind for _, kind in plan]], jnp.int32)
    return pl.pallas_call(
        _pack_kernel,
        grid_spec=pltpu.PrefetchScalarGridSpec(
            num_scalar_prefetch=1,
            grid=(len(plan),),
            in_specs=[pl.BlockSpec((None, PACK_W, d), lambda i, t: (layer, t[0, i], 0)),
                      pl.BlockSpec((None, rank, d), lambda i, t: (layer, t[1, i], 0))],
            out_specs=pl.BlockSpec((d, PACK_W), lambda i, t: (0, i)),
        ),
        out_shape=jax.ShapeDtypeStruct((d, len(plan) * PACK_W), BF16),
        compiler_params=_cparams(("arbitrary",)),
        name="pack_w_in",
    )(table, w_t, w_t)


def kernel(x, c, positions, w_ada, b_ada, ln1_g, ln1_b, w_ffn1_gu, w_ffn1_down, w_in, w_alpha2,
           b_alpha, gla_norm_g, w_branch_a, w_branch_b, w_out, ln2_g, ln2_b, w_ffn2_gu, w_ffn2_down,
           ln3_g, ln3_b):
    bsz, seq, d = x.shape
    m = bsz * seq
    x2d = x.reshape(m, d)
    c_pad = jnp.pad(c, ((0, 8 - bsz % 8 if bsz % 8 else 0), (0, 0)))
    dilations = tuple(r for _, r in DIL_PATTERNS)

    half = DIL_HEAD_DIM // 2
    freq = ROPE_THETA ** (-jnp.arange(half, dtype=F32) / half)
    freq2 = jnp.concatenate([freq, freq]).reshape(1, DIL_HEAD_DIM)
    rope_t = _rope_tables(positions.reshape(m // PROJ_TM, PROJ_TM // LANES, LANES), freq2, dilations)

    for l in range(DEPTH):
        mods = _mods(c_pad, w_ada[l], b_ada[l].reshape(1, -1))[:bsz]
        mods3 = mods.reshape(bsz * N_MOD, 1, d)

        x1 = _ffn(x2d, mods3, 0, 1, 2, w_ffn1_gu[l].astype(BF16), w_ffn1_down[l].astype(BF16),
                  ln1_g[l].reshape(1, d), ln1_b[l].reshape(1, d), seq)

        w_all = _pack_w_in(w_in, l)
        p2d, lr2d = _proj(x1, mods3, 3, 4, w_all, seq)
        p3 = p2d.reshape(bsz, seq, PROJ_W)

        w_a2 = jnp.pad(w_alpha2[l], ((0, LR_PAD - GLA_GATE_RANK), (0, 0))).astype(BF16)
        o_a = _gla(p3, lr2d.reshape(bsz, seq, LR_PAD), w_a2, b_alpha[l].reshape(1, GLA_QK),
                   gla_norm_g[l].reshape(GLA_HEADS, 1, GLA_DV))

        o_groups, lse_groups = [], []
        for g, r in enumerate(dilations):
            qkv = _proj_dil(x1, mods3, 3, 4, w_all, g, rope_t[2 * g], rope_t[2 * g + 1],
                            bsz, seq, r)
            o_g, lse_g = _dil_group(qkv)
            o_groups.append(o_g)
            lse_groups.append(lse_g)

        x2 = _merge(o_a.reshape(m, GLA_VW), o_groups, lse_groups, p2d, x1, mods3, 5,
                    w_branch_a[l].astype(BF16), w_branch_b[l].astype(BF16), w_out[l].astype(BF16),
                    ln2_g[l].reshape(1, d), ln2_b[l].reshape(1, d), seq, dilations)

        x2d = _ffn(x2, mods3, 6, 7, 8, w_ffn2_gu[l].astype(BF16), w_ffn2_down[l].astype(BF16),
                   ln3_g[l].reshape(1, d), ln3_b[l].reshape(1, d), seq)
    return x2d.reshape(bsz, seq, d)
```

```python
import functools

import jax
import jax.numpy as jnp
from jax import lax
from jax.experimental import pallas as pl
from jax.experimental.pallas import tpu as pltpu

F32 = jnp.float32
BF16 = jnp.bfloat16

D_MODEL = 2048
DEPTH = 1
D_FF = 5632
N_MOD = 9
LN_EPS = 1e-5
DN_ALPHA = (2.0 * DEPTH) ** 0.25

GLA_HEADS = 4
GLA_DK = 256
GLA_DV = 512
GLA_GATE_RANK = 16
GLA_TAU = 16.0
GLA_CHUNK = 64
GLA_QK = GLA_HEADS * GLA_DK
GLA_VW = GLA_HEADS * GLA_DV

DIL_PATTERNS = ((128, 1), (512, 4), (2048, 16))
DIL_GROUPS = len(DIL_PATTERNS)
DIL_HEADS = 8
DIL_HEAD_DIM = 128
DIL_BLOCK = 128
DIL_W = DIL_GROUPS * DIL_HEADS * DIL_HEAD_DIM
DIL_OUT = DIL_HEADS * DIL_HEAD_DIM
ROPE_THETA = 10000.0

LANES = 128
LR_PAD = LANES

PROJ_W = 2 * GLA_QK + 2 * GLA_VW + 2 * D_MODEL
OFF_GQ = 0
OFF_GK = GLA_QK
OFF_GV = 2 * GLA_QK
OFF_GR = OFF_GV + GLA_VW
OFF_GA = OFF_GR + GLA_VW
OFF_GB = OFF_GA + D_MODEL
DIL_QKV = 3 * DIL_OUT
OFF_DIL = -(-PROJ_W // DIL_QKV) * DIL_QKV
OFF_LR = OFF_DIL + DIL_GROUPS * DIL_QKV

NEG_BIG = -1e30

VMEM_LIMIT = 56 * 1024 * 1024
VMEM_LIMIT_BIG = 60 * 1024 * 1024


def _cparams(sem, vmem=VMEM_LIMIT):
    return pltpu.CompilerParams(dimension_semantics=sem, vmem_limit_bytes=vmem)


def _sigmoid(x):
    return 1.0 / (1.0 + jnp.exp(-x))


def _deepnorm(x, update, g, b):
    y = x + update
    mu = jnp.mean(y, axis=-1, keepdims=True)
    d = y - mu
    var = jnp.mean(d * d, axis=-1, keepdims=True)
    return d * lax.rsqrt(var + LN_EPS / (DN_ALPHA * DN_ALPHA)) * g + b


def _mods_kernel(c_ref, w_ref, b_ref, o_ref):
    c = c_ref[...]
    c_act = (c * _sigmoid(c)).astype(BF16)
    o_ref[...] = jnp.dot(c_act, w_ref[...].astype(BF16), preferred_element_type=F32) + b_ref[...]


def _mods(c_pad, w_ada, b_ada, tn=1024):
    rows, d = c_pad.shape
    n = w_ada.shape[1]
    return pl.pallas_call(
        _mods_kernel,
        grid=(n // tn,),
        in_specs=[
            pl.BlockSpec((rows, d), lambda j: (0, 0)),
            pl.BlockSpec((d, tn), lambda j: (0, j)),
            pl.BlockSpec((1, tn), lambda j: (0, j)),
        ],
        out_specs=pl.BlockSpec((rows, tn), lambda j: (0, j)),
        out_shape=jax.ShapeDtypeStruct((rows, n), F32),
        compiler_params=_cparams(("arbitrary",)),
        name="mods",
    )(c_pad, w_ada, b_ada)


FFN_ROW_CHUNK = 512


def _ffn_kernel(x_ref, sh_ref, sc_ref, g_ref, wg_ref, wu_ref, wd_ref, lng_ref, lnb_ref,
                o_ref, h_scr):
    j = pl.program_id(1)

    @pl.when(j == 0)
    def _():
        h_scr[...] = (x_ref[...] * (1.0 + sc_ref[0]) + sh_ref[0]).astype(BF16)
        o_ref[...] = jnp.zeros_like(o_ref)

    for r0 in range(0, o_ref.shape[0], FFN_ROW_CHUNK):
        rs = slice(r0, r0 + FFN_ROW_CHUNK)
        h = h_scr[rs, :]
        gate = jnp.dot(h, wg_ref[...], preferred_element_type=F32)
        up = jnp.dot(h, wu_ref[...], preferred_element_type=F32)
        act = (gate * _sigmoid(gate) * up).astype(BF16)
        o_ref[rs, :] += jnp.dot(act, wd_ref[...], preferred_element_type=F32)

    @pl.when(j == pl.num_programs(1) - 1)
    def _():
        o_ref[...] = _deepnorm(x_ref[...], (0.5 / DN_ALPHA) * g_ref[0] * o_ref[...],
                               lng_ref[...], lnb_ref[...])


def _ffn(x2d, mods3, k_shift, k_scale, k_gate, w_gu, w_down, ln_g, ln_b, seq, tm=1024, tf=512):
    m, d = x2d.shape
    n_ff = w_down.shape[0]
    nj = n_ff // tf
    tiles_per_seq = seq // tm

    def mod_spec(k):
        return pl.BlockSpec((1, 1, d), lambda i, j: ((i // tiles_per_seq) * N_MOD + k, 0, 0))

    return pl.pallas_call(
        _ffn_kernel,
        grid=(m // tm, nj),
        in_specs=[
            pl.BlockSpec((tm, d), lambda i, j: (i, 0)),
            mod_spec(k_shift), mod_spec(k_scale), mod_spec(k_gate),
            pl.BlockSpec((d, tf), lambda i, j: (0, j)),
            pl.BlockSpec((d, tf), lambda i, j: (0, j + nj)),
            pl.BlockSpec((tf, d), lambda i, j: (j, 0)),
            pl.BlockSpec((1, d), lambda i, j: (0, 0)),
            pl.BlockSpec((1, d), lambda i, j: (0, 0)),
        ],
        out_specs=pl.BlockSpec((tm, d), lambda i, j: (i, 0)),
        out_shape=jax.ShapeDtypeStruct((m, d), F32),
        scratch_shapes=[pltpu.VMEM((tm, d), BF16)],
        compiler_params=_cparams(("parallel", "arbitrary"), VMEM_LIMIT_BIG),
        name="ffn",
    )(x2d, mods3, mods3, mods3, w_gu, w_gu, w_down, ln_g, ln_b)


PROJ_TM = 1024


def _rope_kernel(pos_ref, freq_ref, *out_refs, dilations):
    cos_ref, sin_ref = out_refs[0], out_refs[1]
    tm = cos_ref.shape[0]
    pos_t = pos_ref[0].astype(F32).T
    half = DIL_HEAD_DIM // 2
    low = lax.broadcasted_iota(jnp.int32, (LANES, DIL_HEAD_DIM), 1) < half
    n_blocks = tm // LANES
    for k in range(n_blocks // 2):
        k2 = k + n_blocks // 2
        ang = jnp.where(low, pos_t[:, k:k + 1], pos_t[:, k2:k2 + 1]) * freq_ref[...]
        c, s = jnp.cos(ang), jnp.sin(ang)
        c_sw, s_sw = pltpu.roll(c, half, 1), pltpu.roll(s, half, 1)
        cos_ref[k * LANES:(k + 1) * LANES, :] = jnp.where(low, c, c_sw)
        sin_ref[k * LANES:(k + 1) * LANES, :] = jnp.where(low, -s, s_sw)
        cos_ref[k2 * LANES:(k2 + 1) * LANES, :] = jnp.where(low, c_sw, c)
        sin_ref[k2 * LANES:(k2 + 1) * LANES, :] = jnp.where(low, -s_sw, s)
    for gi, r in enumerate(dilations[1:]):
        tl = tm // r
        for c in range(r):
            out_refs[2 + 2 * gi][c * tl:(c + 1) * tl, :] = cos_ref[pl.ds(c, tl, stride=r), :]
            out_refs[3 + 2 * gi][c * tl:(c + 1) * tl, :] = sin_ref[pl.ds(c, tl, stride=r), :]


def _rope_tables(pos3, freq2, dilations, tm=PROJ_TM):
    assert dilations[0] == 1
    m = pos3.shape[0] * tm
    n_out = 2 * len(dilations)
    return pl.pallas_call(
        functools.partial(_rope_kernel, dilations=dilations),
        grid=(m // tm,),
        in_specs=[
            pl.BlockSpec((1, tm // LANES, LANES), lambda i: (i, 0, 0)),
            pl.BlockSpec((1, DIL_HEAD_DIM), lambda i: (0, 0)),
        ],
        out_specs=[pl.BlockSpec((tm, DIL_HEAD_DIM), lambda i: (i, 0))] * n_out,
        out_shape=[jax.ShapeDtypeStruct((m, DIL_HEAD_DIM), F32)] * n_out,
        compiler_params=_cparams(("parallel",)),
        name="rope_tables",
    )(pos3, freq2)


def _proj_kernel(x_ref, sh_ref, sc_ref, w_ref, wlr_ref, o_ref, lr_ref, h_scr):
    @pl.when(pl.program_id(1) == 0)
    def _():
        h = (x_ref[...] * (1.0 + sc_ref[0]) + sh_ref[0]).astype(BF16)
        h_scr[...] = h
        lr_ref[...] = jnp.dot(h, wlr_ref[...], preferred_element_type=F32)

    o_ref[...] = jnp.dot(h_scr[...], w_ref[...], preferred_element_type=F32).astype(BF16)


def _proj(x2d, mods3, k_shift, k_scale, w_all, seq, tm=PROJ_TM, tn=2048):
    m, d = x2d.shape
    n = PROJ_W
    tiles_per_seq = seq // tm

    def mod_spec(k):
        return pl.BlockSpec((1, 1, d), lambda i, j: ((i // tiles_per_seq) * N_MOD + k, 0, 0))

    return pl.pallas_call(
        _proj_kernel,
        grid=(m // tm, n // tn),
        in_specs=[
            pl.BlockSpec((tm, d), lambda i, j: (i, 0)),
            mod_spec(k_shift), mod_spec(k_scale),
            pl.BlockSpec((d, tn), lambda i, j: (0, j)),
            pl.BlockSpec((d, LR_PAD), lambda i, j: (0, OFF_LR // LR_PAD)),
        ],
        out_specs=[
            pl.BlockSpec((tm, tn), lambda i, j: (i, j)),
            pl.BlockSpec((tm, LR_PAD), lambda i, j: (i, 0)),
        ],
        out_shape=[
            jax.ShapeDtypeStruct((m, n), BF16),
            jax.ShapeDtypeStruct((m, LR_PAD), F32),
        ],
        scratch_shapes=[pltpu.VMEM((tm, d), BF16)],
        compiler_params=_cparams(("parallel", "arbitrary")),
        name="proj",
    )(x2d, mods3, mods3, w_all, w_all)


def _proj_dil_kernel(x_ref, sh_ref, sc_ref, w_ref, cos_ref, sin_ref, o_ref, h_scr, slab_scr, *, r):
    tm, d = x_ref.shape
    tl = tm // r
    if r == 1:
        h_scr[...] = (x_ref[...] * (1.0 + sc_ref[0]) + sh_ref[0]).astype(BF16)
    else:
        for k in range(d // LANES):
            sl = slice(k * LANES, (k + 1) * LANES)
            slab_scr[k] = x_ref[:, sl] * (1.0 + sc_ref[0, :, sl]) + sh_ref[0, :, sl]
        for c in range(r):
            for k in range(d // LANES):
                sl = slice(k * LANES, (k + 1) * LANES)
                h_scr[c * tl:(c + 1) * tl, sl] = slab_scr[k, pl.ds(c, tl, stride=r), :].astype(BF16)
    _proj_dil_qkv(h_scr, w_ref, cos_ref, sin_ref, o_ref, r)


def _proj_dil_qkv(h_scr, w_ref, cos_ref, sin_ref, o_ref, r):
    tl = h_scr.shape[0] // r
    for part, scale in enumerate((DIL_HEAD_DIM ** -0.5, 1.0, None)):
        cols = slice(part * DIL_OUT, (part + 1) * DIL_OUT)
        t = jnp.dot(h_scr[...], w_ref[:, cols], preferred_element_type=F32)
        if scale is None:
            for c in range(r):
                o_ref[0, c, :, cols] = t[c * tl:(c + 1) * tl, :].astype(BF16)
            continue
        cos = cos_ref[...] * scale
        sin = sin_ref[...] * scale
        for s in range(DIL_HEADS):
            ts = t[:, s * DIL_HEAD_DIM:(s + 1) * DIL_HEAD_DIM]
            rot = (ts * cos + pltpu.roll(ts, DIL_HEAD_DIM // 2, 1) * sin).astype(BF16)
            lo = part * DIL_OUT + s * DIL_HEAD_DIM
            for c in range(r):
                o_ref[0, c, :, lo:lo + DIL_HEAD_DIM] = rot[c * tl:(c + 1) * tl, :]


def _proj_dil_dma_kernel(x_hbm, sh_ref, sc_ref, w_ref, cos_ref, sin_ref, o_ref, h_scr, x_buf, sem,
                         *, r):
    i = pl.program_id(0)
    tl = h_scr.shape[0] // r
    slot = i % 2

    def tile_copies(tile, buf_slot):
        return [pltpu.make_async_copy(x_hbm.at[pl.ds(tile * tl, tl), c], x_buf.at[buf_slot, c],
                                      sem.at[buf_slot]) for c in range(r)]

    @pl.when(i == 0)
    def _():
        for cp in tile_copies(0, 0):
            cp.start()

    @pl.when(i + 1 < pl.num_programs(0))
    def _():
        for cp in tile_copies(i + 1, 1 - slot):
            cp.start()

    for cp in tile_copies(i, slot):
        cp.wait()
    for c in range(r):
        h_scr[c * tl:(c + 1) * tl, :] = (x_buf[slot, c] * (1.0 + sc_ref[0]) + sh_ref[0]).astype(BF16)
    _proj_dil_qkv(h_scr, w_ref, cos_ref, sin_ref, o_ref, r)


SUBLANES_F32 = 8


def _proj_dil(x2d, mods3, k_shift, k_scale, w_all, g, cos_t, sin_t, bsz, seq, r, tm=PROJ_TM):
    m, d = x2d.shape
    tiles_per_seq = seq // tm
    tl = tm // r
    dma_gather = r % SUBLANES_F32 == 0

    def mod_spec(k):
        return pl.BlockSpec((1, 1, d), lambda i: ((i // tiles_per_seq) * N_MOD + k, 0, 0))

    if dma_gather:
        body = functools.partial(_proj_dil_dma_kernel, r=r)
        x_arg = x2d.reshape(m // r, r, d)
        x_spec = pl.BlockSpec(memory_space=pl.ANY)
        scratch = [pltpu.VMEM((tm, d), BF16), pltpu.VMEM((2, r, tl, d), F32),
                   pltpu.SemaphoreType.DMA((2,))]
        semantics = ("arbitrary",)
    else:
        body = functools.partial(_proj_dil_kernel, r=r)
        x_arg = x2d
        x_spec = pl.BlockSpec((tm, d), lambda i: (i, 0))
        scratch = [pltpu.VMEM((tm, d), BF16),
                   pltpu.VMEM((d // LANES, tm, LANES) if r > 1 else (1, 8, LANES), F32)]
        semantics = ("parallel",)

    return pl.pallas_call(
        body,
        grid=(m // tm,),
        in_specs=[
            x_spec,
            mod_spec(k_shift), mod_spec(k_scale),
            pl.BlockSpec((d, DIL_QKV), lambda i: (0, OFF_DIL // DIL_QKV + g),
                         pipeline_mode=pl.Buffered(1)),
            pl.BlockSpec((tm, DIL_HEAD_DIM), lambda i: (i, 0)),
            pl.BlockSpec((tm, DIL_HEAD_DIM), lambda i: (i, 0)),
        ],
        out_specs=pl.BlockSpec((1, r, tl, DIL_QKV),
                               lambda i: (i // tiles_per_seq, 0, i % tiles_per_seq, 0)),
        out_shape=jax.ShapeDtypeStruct((bsz, r, seq // r, DIL_QKV), BF16),
        scratch_shapes=scratch,
        compiler_params=_cparams(semantics, VMEM_LIMIT_BIG),
        name=f"proj_dil_r{r}",
    )(x_arg, mods3, mods3, w_all, cos_t, sin_t)


def _gla_kernel(q_ref, k_ref, v_ref, r_ref, lr_ref, wa_ref, ba_ref, ng_ref, o_ref,
                st_scr, b_scr, qd_scr, qh_scr, ki_scr, ke_scr, kp_scr, kh_scr, a_scr, *, n_chunks):
    c_len = GLA_CHUNK
    tc = q_ref.shape[1]
    hb = 4 * c_len
    pw = 2 * c_len
    n_pairs = n_chunks // 2

    @pl.when(pl.program_id(2) == 0)
    def _():
        st_scr[...] = jnp.zeros_like(st_scr)

    ti = lax.broadcasted_iota(jnp.int32, (hb, hb), 0)
    tj = lax.broadcasted_iota(jnp.int32, (hb, hb), 1)
    tri = jnp.logical_and(ti >= tj, ti // c_len == tj // c_len).astype(BF16)
    for s in range(tc // hb):
        rows = slice(s * hb, (s + 1) * hb)
        logits = jnp.dot(lr_ref[0, rows, :].astype(BF16), wa_ref[...],
                         preferred_element_type=F32) + ba_ref[...]
        log_a = (jnp.minimum(logits, 0.0) - jnp.log(1.0 + jnp.exp(-jnp.abs(logits)))) / GLA_TAU
        p1 = log_a.astype(BF16)
        r1 = log_a - p1.astype(F32)
        p2 = r1.astype(BF16)
        p3 = (r1 - p2.astype(F32)).astype(BF16)
        b_scr[rows, :] = (jnp.dot(tri, p1, preferred_element_type=F32)
                          + jnp.dot(tri, p2, preferred_element_type=F32)
                          + jnp.dot(tri, p3, preferred_element_type=F32))

    nt = (((1,), (1,)), ((), ()))

    def chunk_rows(c):
        return slice(c * c_len, (c + 1) * c_len)

    def pair_rows(p):
        return slice(p * pw, (p + 1) * pw)

    b_last = [b_scr[(c + 1) * c_len - 1:(c + 1) * c_len, :] for c in range(n_chunks)]
    h_log = [jnp.zeros_like(b_last[0])]
    for c in range(n_chunks):
        h_log.append(h_log[-1] + b_last[c])

    for c in range(n_chunks):
        rows = chunk_rows(c)
        b = b_scr[rows, :]
        q = q_ref[0, rows, :].astype(F32)
        k = k_ref[0, rows, :].astype(F32)
        qd = q * jnp.exp(b) * (GLA_DK ** -0.5)
        ke = k * jnp.exp(b_last[c] - b)
        qd_scr[rows, :] = qd
        qh_scr[rows, :] = (qd * jnp.exp(h_log[c])).astype(BF16)
        ki_scr[rows, :] = (k * jnp.exp(-b)).astype(BF16)
        ke_scr[rows, :] = ke.astype(BF16)
        kp_scr[rows, :] = (ke * jnp.exp(b_last[c + 1]) if c % 2 == 0 else ke).astype(BF16)
        kh_scr[rows, :] = (ke * jnp.exp(h_log[n_chunks] - h_log[c + 1])).astype(BF16)

    pi = lax.broadcasted_iota(jnp.int32, (pw, pw), 0)
    pj = lax.broadcasted_iota(jnp.int32, (pw, pw), 1)
    diag = jnp.logical_and(pi >= pj, pi // c_len == pj // c_len)
    lower = jnp.logical_and(pi >= c_len, pj < c_len)
    for p in range(n_pairs):
        first = chunk_rows(2 * p)
        keys = jnp.concatenate([ki_scr[pair_rows(p), :], ke_scr[first, :], ke_scr[first, :]], axis=0)
        s = lax.dot_general(qd_scr[pair_rows(p), :].astype(BF16), keys, nt,
                            preferred_element_type=F32)
        s = jnp.where(diag, s[:, :pw], jnp.where(lower, s[:, pw:], 0.0))
        a_scr[pair_rows(p), pair_rows(p)] = s.astype(BF16)

    for p in range(n_pairs - 1):
        c0 = 2 * p + 2
        lhs = jnp.concatenate(
            [(qd_scr[chunk_rows(c), :] * jnp.exp(h_log[c] - h_log[c0])).astype(BF16)
             for c in range(c0, n_chunks)], axis=0)
        s = lax.dot_general(lhs, kp_scr[pair_rows(p), :], nt, preferred_element_type=F32)
        a_scr[c0 * c_len:, pair_rows(p)] = s.astype(BF16)

    half = 2 * pw
    for hf in range(tc // half):
        a_scr[hf * half:hf * half + pw, hf * half + pw:(hf + 1) * half] = jnp.zeros((pw, pw), BF16)

    st = st_scr[...]
    st_b = st.astype(BF16)
    for hf in range(tc // half):
        rows = slice(hf * half, (hf + 1) * half)
        keys = (hf + 1) * half
        o = jnp.dot(a_scr[rows, :keys], v_ref[0, :keys, :], preferred_element_type=F32)
        o += jnp.dot(qh_scr[rows, :], st_b, preferred_element_type=F32)
        o = o * lax.rsqrt(jnp.mean(o * o, axis=-1, keepdims=True) + LN_EPS) * ng_ref[0]
        r = r_ref[0, rows, :].astype(F32)
        o_ref[0, rows, :] = (o * (r * _sigmoid(r))).astype(BF16)

    d_col = jnp.broadcast_to(jnp.exp(h_log[n_chunks]), (8, GLA_DK)).T[:, :1]
    st_scr[...] = st * d_col + lax.dot_general(
        kh_scr[...], v_ref[0], (((0,), (0,)), ((), ())), preferred_element_type=F32)


def _gla(p3, lr3, w_a2, b_a, norm_g3, tc=1024):
    bsz, seq, _ = p3.shape
    nq, nv, nr = OFF_GK // GLA_DK, OFF_GV // GLA_DV, OFF_GR // GLA_DV
    return pl.pallas_call(
        functools.partial(_gla_kernel, n_chunks=tc // GLA_CHUNK),
        grid=(bsz, GLA_HEADS, seq // tc),
        in_specs=[
            pl.BlockSpec((1, tc, GLA_DK), lambda b, h, t: (b, t, h)),
            pl.BlockSpec((1, tc, GLA_DK), lambda b, h, t: (b, t, nq + h)),
            pl.BlockSpec((1, tc, GLA_DV), lambda b, h, t: (b, t, nv + h)),
            pl.BlockSpec((1, tc, GLA_DV), lambda b, h, t: (b, t, nr + h)),
            pl.BlockSpec((1, tc, LR_PAD), lambda b, h, t: (b, t, 0)),
            pl.BlockSpec((LR_PAD, GLA_DK), lambda b, h, t: (0, h)),
            pl.BlockSpec((1, GLA_DK), lambda b, h, t: (0, h)),
            pl.BlockSpec((1, 1, GLA_DV), lambda b, h, t: (h, 0, 0)),
        ],
        out_specs=pl.BlockSpec((1, tc, GLA_DV), lambda b, h, t: (b, t, h)),
        out_shape=jax.ShapeDtypeStruct((bsz, seq, GLA_VW), BF16),
        scratch_shapes=[pltpu.VMEM((GLA_DK, GLA_DV), F32), pltpu.VMEM((tc, GLA_DK), F32),
                        pltpu.VMEM((tc, GLA_DK), F32)]
                       + [pltpu.VMEM((tc, GLA_DK), BF16)] * 5 + [pltpu.VMEM((tc, tc), BF16)],
        compiler_params=_cparams(("parallel", "parallel", "arbitrary")),
        name="gla",
    )(p3, p3, p3, p3, lr3, w_a2, b_a, norm_g3)


def _dil_kernel(q_ref, k_ref, v_ref, kp_ref, vp_ref, o_ref, lse_ref, s_scr, m_scr, p_scr, *, nblk):
    n = pl.program_id(2)
    blk = DIL_BLOCK
    qi = lax.broadcasted_iota(jnp.int32, (blk, 2 * blk), 0)
    kj = lax.broadcasted_iota(jnp.int32, (blk, 2 * blk), 1)
    band = jnp.logical_or(jnp.logical_and(kj < blk, kj >= qi),
                          jnp.logical_and(kj >= blk, kj - blk <= qi))
    band_first = jnp.logical_and(band, jnp.logical_or(kj >= blk, n > 0))
    nt = (((1,), (1,)), ((), ()))

    def rows(i):
        return slice(i * blk, (i + 1) * blk)

    def cols(h):
        return slice(h * DIL_HEAD_DIM, (h + 1) * DIL_HEAD_DIM)

    for i in range(nblk):
        for h in range(DIL_HEADS):
            q = q_ref[0, 0, rows(i), cols(h)]
            k_prev = kp_ref[0, 0, :, cols(h)] if i == 0 else k_ref[0, 0, rows(i - 1), cols(h)]
            s_p = lax.dot_general(q, k_prev, nt, preferred_element_type=F32)
            s_c = lax.dot_general(q, k_ref[0, 0, rows(i), cols(h)], nt, preferred_element_type=F32)
            s = jnp.concatenate([s_p, s_c], axis=1)
            s_scr[rows(i * DIL_HEADS + h), :] = jnp.where(band_first if i == 0 else band, s, NEG_BIG)

    s_all = s_scr[...]
    m_all = jnp.max(s_all, axis=-1, keepdims=True)
    p_scr[...] = jnp.exp(s_all - m_all).astype(BF16)
    m_scr[...] = jnp.broadcast_to(m_all, m_scr.shape)

    lane = lax.broadcasted_iota(jnp.int32, (blk, LANES), 1)
    ones = jnp.ones((2 * blk, DIL_HEAD_DIM), BF16)
    for i in range(nblk):
        lse_tile = jnp.zeros((blk, LANES), F32)
        for h in range(DIL_HEADS):
            u = i * DIL_HEADS + h
            v_prev = vp_ref[0, 0, :, cols(h)] if i == 0 else v_ref[0, 0, rows(i - 1), cols(h)]
            v_aug = jnp.concatenate(
                [jnp.concatenate([v_prev, v_ref[0, 0, rows(i), cols(h)]], axis=0), ones], axis=1)
            acc = jnp.dot(p_scr[rows(u), :], v_aug, preferred_element_type=F32)
            den = acc[:, DIL_HEAD_DIM:]
            o_ref[0, 0, rows(i), cols(h)] = (acc[:, :DIL_HEAD_DIM] / den).astype(BF16)
            lse_tile = jnp.where(lane == h, m_scr[rows(u), :] + jnp.log(den), lse_tile)
        lse_ref[0, 0, rows(i), :] = lse_tile


def _dil_group(qkv, nblk=8):
    bsz, r, length, _ = qkv.shape
    nb = length // DIL_BLOCK
    nblk = min(nblk, nb)
    assert nb % nblk == 0
    tq = nblk * DIL_BLOCK
    units = nblk * DIL_HEADS * DIL_BLOCK

    def cur_spec(u):
        return pl.BlockSpec((1, 1, tq, DIL_OUT), lambda b, c, n: (b, c, n, u))

    def prev_spec(u):
        return pl.BlockSpec((1, 1, DIL_BLOCK, DIL_OUT),
                            lambda b, c, n: (b, c, jnp.maximum(n * nblk - 1, 0), u))

    return pl.pallas_call(
        functools.partial(_dil_kernel, nblk=nblk),
        grid=(bsz, r, nb // nblk),
        in_specs=[cur_spec(0), cur_spec(1), cur_spec(2), prev_spec(1), prev_spec(2)],
        out_specs=[
            pl.BlockSpec((1, 1, tq, DIL_OUT), lambda b, c, n: (b, c, n, 0)),
            pl.BlockSpec((1, 1, tq, LANES), lambda b, c, n: (b, c, n, 0)),
        ],
        out_shape=[
            jax.ShapeDtypeStruct((bsz, r, length, DIL_OUT), BF16),
            jax.ShapeDtypeStruct((bsz, r, length, LANES), F32),
        ],
        scratch_shapes=[pltpu.VMEM((units, 2 * DIL_BLOCK), F32),
                        pltpu.VMEM((units, LANES), F32),
                        pltpu.VMEM((units, 2 * DIL_BLOCK), BF16)],
        compiler_params=_cparams(("parallel", "parallel", "parallel")),
        name=f"dilattn_r{r}",
    )(qkv, qkv, qkv, qkv, qkv)


def _merge_kernel(oa_ref, o1_ref, o2_ref, o3_ref, l1_ref, l2_ref, l3_ref, ga_ref, gb_ref, x_ref,
                  g2_ref, wa_ref, wb_ref, wo_ref, lng_ref, lnb_ref, out_ref, l_scr, w_scr, ob_scr,
                  *, dilations):
    y_a = jnp.dot(oa_ref[...], wa_ref[...], preferred_element_type=F32)
    o_refs = (o1_ref, o2_ref, o3_ref)
    l_refs = (l1_ref, l2_ref, l3_ref)
    tm = oa_ref.shape[0]

    def tok_rows(c, r):
        return pl.ds(c, tm // r, stride=r) if r > 1 else slice(None)

    for g, r in enumerate(dilations):
        for c in range(r):
            l_scr[g, tok_rows(c, r), :] = l_refs[g][0, c]
    l1, l2, l3 = l_scr[0], l_scr[1], l_scr[2]
    mx = jnp.maximum(jnp.maximum(l1, l2), l3)
    e1, e2, e3 = jnp.exp(l1 - mx), jnp.exp(l2 - mx), jnp.exp(l3 - mx)
    inv = 1.0 / (e1 + e2 + e3)
    w_scr[0], w_scr[1], w_scr[2] = e1 * inv, e2 * inv, e3 * inv
    for g, r in enumerate(dilations):
        for c in range(r):
            rows = tok_rows(c, r)
            wr = w_scr[g, rows, :]
            for h in range(DIL_HEADS):
                sl = slice(h * DIL_HEAD_DIM, (h + 1) * DIL_HEAD_DIM)
                part = wr[:, h:h + 1] * o_refs[g][0, c, :, sl].astype(F32)
                if g == 0:
                    ob_scr[h, rows, :] = part
                else:
                    ob_scr[h, rows, :] += part
    o_b = jnp.concatenate([ob_scr[h].astype(BF16) for h in range(DIL_HEADS)], axis=1)
    y_b = jnp.dot(o_b, wb_ref[...], preferred_element_type=F32)
    merged = _sigmoid(ga_ref[...].astype(F32)) * y_a + _sigmoid(gb_ref[...].astype(F32)) * y_b
    mix = jnp.dot(merged.astype(BF16), wo_ref[...], preferred_element_type=F32)
    out_ref[...] = _deepnorm(x_ref[...], (1.0 / DN_ALPHA) * g2_ref[0] * mix, lng_ref[...], lnb_ref[...])


def _merge(oa2d, o_groups, lse_groups, p2d, x2d, mods3, k_gate, w_a, w_b, w_o, ln_g, ln_b, seq,
           dilations, tm=256):
    m, d = x2d.shape
    tiles_per_seq = seq // tm
    resident = pl.Buffered(1)
    n_g = len(dilations)

    def row_spec(width, col=0):
        return pl.BlockSpec((tm, width), lambda i: (i, col))

    def class_spec(r, width):
        return pl.BlockSpec((1, r, tm // r, width),
                            lambda i: (i // tiles_per_seq, 0, i % tiles_per_seq, 0))

    def const_spec(shape):
        return pl.BlockSpec(shape, lambda i: (0, 0), pipeline_mode=resident)

    return pl.pallas_call(
        functools.partial(_merge_kernel, dilations=dilations),
        grid=(m // tm,),
        in_specs=[
            row_spec(GLA_VW),
            *[class_spec(r, DIL_OUT) for r in dilations],
            *[class_spec(r, LANES) for r in dilations],
            row_spec(D_MODEL, OFF_GA // D_MODEL), row_spec(D_MODEL, OFF_GB // D_MODEL),
            row_spec(d),
            pl.BlockSpec((1, 1, d), lambda i: ((i // tiles_per_seq) * N_MOD + k_gate, 0, 0)),
            const_spec(w_a.shape), const_spec(w_b.shape), const_spec(w_o.shape),
            const_spec((1, d)), const_spec((1, d)),
        ],
        out_specs=row_spec(d),
        out_shape=jax.ShapeDtypeStruct((m, d), F32),
        scratch_shapes=[pltpu.VMEM((n_g, tm, LANES), F32), pltpu.VMEM((n_g, tm, LANES), F32),
                        pltpu.VMEM((DIL_HEADS, tm, DIL_HEAD_DIM), F32)],
        compiler_params=_cparams(("parallel",)),
        name="merge",
    )(oa2d, *o_groups, *lse_groups, p2d, p2d, x2d, mods3, w_a, w_b, w_o, ln_g, ln_b)


PACK_W = 1024
PACK_COPY, PACK_SHIFTED, PACK_ZERO, PACK_LOW_RANK = range(4)


def _pack_plan():
    src, o = {}, 0
    for name, wdt in zip(("gq", "gk", "gv", "gr", "glr", "dq", "dk", "dv", "ga", "gb"),
                         (GLA_QK, GLA_QK, GLA_VW, GLA_VW, GLA_GATE_RANK, DIL_W, DIL_W, DIL_W,
                          D_MODEL, D_MODEL)):
        src[name] = (o, wdt)
        o += wdt
    plan = []

    def copy(first, width):
        for c in range(first, first + width, PACK_W):
            plan.append((c, PACK_COPY if c % PACK_W == 0 else PACK_SHIFTED))

    for name in ("gq", "gk", "gv", "gr", "ga", "gb"):
        copy(*src[name])
    plan += [(0, PACK_ZERO)] * ((OFF_DIL - PROJ_W) // PACK_W)
    for g in range(DIL_GROUPS):
        for name in ("dq", "dk", "dv"):
            copy(src[name][0] + g * DIL_OUT, DIL_OUT)
    plan.append((src["glr"][0], PACK_LOW_RANK))
    return plan


def _pack_kernel(plan_ref, a_ref, b_ref, o_ref):
    kind = plan_ref[2, pl.program_id(0)]
    rank = GLA_GATE_RANK

    @pl.when(kind == PACK_COPY)
    def _():
        o_ref[...] = a_ref[...].T.astype(BF16)

    @pl.when(kind == PACK_SHIFTED)
    def _():
        o_ref[...] = jnp.concatenate([a_ref[rank:, :], b_ref[...]], axis=0).T.astype(BF16)

    @pl.when(kind == PACK_ZERO)
    def _():
        o_ref[...] = jnp.zeros_like(o_ref)

    @pl.when(kind == PACK_LOW_RANK)
    def _():
        lane = lax.broadcasted_iota(jnp.int32, (o_ref.shape[0], LANES), 1)
        o_ref[...] = jnp.zeros_like(o_ref)
        o_ref[:, :LANES] = jnp.where(lane < rank, a_ref[:LANES, :].T, 0.0).astype(BF16)


def _pack_w_in(w_in, layer):
    _, d, width = w_in.shape
    w_t = jnp.swapaxes(w_in, 1, 2)
    plan = _pack_plan()
    first = [c - c % PACK_W for c, _ in plan]
    rank = GLA_GATE_RANK
    assert all(c % PACK_W in (0, rank) for c, _ in plan) and rank % SUBLANES_F32 == 0
    table = jnp.asarray([[c // PACK_W for c in first],
                         [min(c + PACK_W, width - rank) // rank for c in first],
                         [kind for _, kind in plan]], jnp.int32)
    return pl.pallas_call(
        _pack_kernel,
        grid_spec=pltpu.PrefetchScalarGridSpec(
            num_scalar_prefetch=1,
            grid=(len(plan),),
            in_specs=[pl.BlockSpec((None, PACK_W, d), lambda i, t: (layer, t[0, i], 0)),
                      pl.BlockSpec((None, rank, d), lambda i, t: (layer, t[1, i], 0))],
            out_specs=pl.BlockSpec((d, PACK_W), lambda i, t: (0, i)),
        ),
        out_shape=jax.ShapeDtypeStruct((d, len(plan) * PACK_W), BF16),
        compiler_params=_cparams(("arbitrary",)),
        name="pack_w_in",
    )(table, w_t, w_t)


def kernel(x, c, positions, w_ada, b_ada, ln1_g, ln1_b, w_ffn1_gu, w_ffn1_down, w_in, w_alpha2,
           b_alpha, gla_norm_g, w_branch_a, w_branch_b, w_out, ln2_g, ln2_b, w_ffn2_gu, w_ffn2_down,
           ln3_g, ln3_b):
    bsz, seq, d = x.shape
    m = bsz * seq
    x2d = x.reshape(m, d)
    c_pad = jnp.pad(c, ((0, 8 - bsz % 8 if bsz % 8 else 0), (0, 0)))
    dilations = tuple(r for _, r in DIL_PATTERNS)

    half = DIL_HEAD_DIM // 2
    freq = ROPE_THETA ** (-jnp.arange(half, dtype=F32) / half)
    freq2 = jnp.concatenate([freq, freq]).reshape(1, DIL_HEAD_DIM)
    rope_t = _rope_tables(positions.reshape(m // PROJ_TM, PROJ_TM // LANES, LANES), freq2, dilations)

    for l in range(DEPTH):
        mods = _mods(c_pad, w_ada[l], b_ada[l].reshape(1, -1))[:bsz]
        mods3 = mods.reshape(bsz * N_MOD, 1, d)

        x1 = _ffn(x2d, mods3, 0, 1, 2, w_ffn1_gu[l].astype(BF16), w_ffn1_down[l].astype(BF16),
                  ln1_g[l].reshape(1, d), ln1_b[l].reshape(1, d), seq)

        w_all = _pack_w_in(w_in, l)
        p2d, lr2d = _proj(x1, mods3, 3, 4, w_all, seq)
        p3 = p2d.reshape(bsz, seq, PROJ_W)

        w_a2 = jnp.pad(w_alpha2[l], ((0, LR_PAD - GLA_GATE_RANK), (0, 0))).astype(BF16)
        o_a = _gla(p3, lr2d.reshape(bsz, seq, LR_PAD), w_a2, b_alpha[l].reshape(1, GLA_QK),
                   gla_norm_g[l].reshape(GLA_HEADS, 1, GLA_DV))

        o_groups, lse_groups = [], []
        for g, r in enumerate(dilations):
            qkv = _proj_dil(x1, mods3, 3, 4, w_all, g, rope_t[2 * g], rope_t[2 * g + 1],
                            bsz, seq, r)
            o_g, lse_g = _dil_group(qkv)
            o_groups.append(o_g)
            lse_groups.append(lse_g)

        x2 = _merge(o_a.reshape(m, GLA_VW), o_groups, lse_groups, p2d, x1, mods3, 5,
                    w_branch_a[l].astype(BF16), w_branch_b[l].astype(BF16), w_out[l].astype(BF16),
                    ln2_g[l].reshape(1, d), ln2_b[l].reshape(1, d), seq, dilations)

        x2d = _ffn(x2, mods3, 6, 7, 8, w_ffn2_gu[l].astype(BF16), w_ffn2_down[l].astype(BF16),
                   ln3_g[l].reshape(1, d), ln3_b[l].reshape(1, d), seq)
    return x2d.reshape(bsz, seq, d)
```

```python
import functools

import jax
import jax.numpy as jnp
from jax import lax
from jax.experimental import pallas as pl
from jax.experimental.pallas import tpu as pltpu

F32 = jnp.float32
BF16 = jnp.bfloat16

D_MODEL = 2048
DEPTH = 1
N_MOD = 9
LN_EPS = 1e-5
DN_ALPHA = (2.0 * DEPTH) ** 0.25

GLA_HEADS = 4
GLA_DK = 256
GLA_DV = 512
GLA_GATE_RANK = 16
GLA_TAU = 16.0
GLA_CHUNK = 64
GLA_QK = GLA_HEADS * GLA_DK
GLA_VW = GLA_HEADS * GLA_DV

DIL_PATTERNS = ((128, 1), (512, 4), (2048, 16))
DIL_GROUPS = len(DIL_PATTERNS)
DIL_HEADS = 8
DIL_HEAD_DIM = 128
DIL_BLOCK = 128
DIL_W = DIL_GROUPS * DIL_HEADS * DIL_HEAD_DIM
DIL_OUT = DIL_HEADS * DIL_HEAD_DIM
ROPE_THETA = 10000.0

LANES = 128
SUBLANES_F32 = 8
LR_PAD = LANES

PROJ_W = 2 * GLA_QK + 2 * GLA_VW + 2 * D_MODEL
OFF_GK = GLA_QK
OFF_GV = 2 * GLA_QK
OFF_GR = OFF_GV + GLA_VW
OFF_GA = OFF_GR + GLA_VW
OFF_GB = OFF_GA + D_MODEL
DIL_QKV = 3 * DIL_OUT
OFF_DIL = -(-PROJ_W // DIL_QKV) * DIL_QKV
OFF_LR = OFF_DIL + DIL_GROUPS * DIL_QKV

NEG_BIG = -1e30

VMEM_LIMIT = 56 * 1024 * 1024
VMEM_LIMIT_BIG = 60 * 1024 * 1024


def _cparams(sem, vmem=VMEM_LIMIT):
    return pltpu.CompilerParams(dimension_semantics=sem, vmem_limit_bytes=vmem)


def _sigmoid(x):
    return 1.0 / (1.0 + jnp.exp(-x))


def _deepnorm(x, update, g, b):
    y = x + update
    mu = jnp.mean(y, axis=-1, keepdims=True)
    d = y - mu
    var = jnp.mean(d * d, axis=-1, keepdims=True)
    return d * lax.rsqrt(var + LN_EPS / (DN_ALPHA * DN_ALPHA)) * g + b


def _mods_kernel(c_ref, w_ref, b_ref, o_ref):
    c = c_ref[...]
    c_act = (c * _sigmoid(c)).astype(BF16)
    o_ref[...] = jnp.dot(c_act, w_ref[...].astype(BF16), preferred_element_type=F32) + b_ref[...]


def _mods(c_pad, w_ada, b_ada, tn=1024):
    rows, d = c_pad.shape
    n = w_ada.shape[1]
    return pl.pallas_call(
        _mods_kernel,
        grid=(n // tn,),
        in_specs=[
            pl.BlockSpec((rows, d), lambda j: (0, 0)),
            pl.BlockSpec((d, tn), lambda j: (0, j)),
            pl.BlockSpec((1, tn), lambda j: (0, j)),
        ],
        out_specs=pl.BlockSpec((rows, tn), lambda j: (0, j)),
        out_shape=jax.ShapeDtypeStruct((rows, n), F32),
        compiler_params=_cparams(("arbitrary",)),
        name="mods",
    )(c_pad, w_ada, b_ada)


FFN_ROW_CHUNK = 512


def _ffn_kernel(x_ref, sh_ref, sc_ref, g_ref, wg_ref, wu_ref, wd_ref, lng_ref, lnb_ref,
                o_ref, h_scr):
    j = pl.program_id(1)

    @pl.when(j == 0)
    def _():
        h_scr[...] = (x_ref[...] * (1.0 + sc_ref[0]) + sh_ref[0]).astype(BF16)
        o_ref[...] = jnp.zeros_like(o_ref)

    for r0 in range(0, o_ref.shape[0], FFN_ROW_CHUNK):
        rs = slice(r0, r0 + FFN_ROW_CHUNK)
        h = h_scr[rs, :]
        gate = jnp.dot(h, wg_ref[...], preferred_element_type=F32)
        up = jnp.dot(h, wu_ref[...], preferred_element_type=F32)
        act = (gate * _sigmoid(gate) * up).astype(BF16)
        o_ref[rs, :] += jnp.dot(act, wd_ref[...], preferred_element_type=F32)

    @pl.when(j == pl.num_programs(1) - 1)
    def _():
        o_ref[...] = _deepnorm(x_ref[...], (0.5 / DN_ALPHA) * g_ref[0] * o_ref[...],
                               lng_ref[...], lnb_ref[...])


def _ffn(x2d, mods3, k_shift, k_scale, k_gate, w_gu, w_down, ln_g, ln_b, seq, tm=1024, tf=512):
    m, d = x2d.shape
    n_ff = w_down.shape[0]
    nj = n_ff // tf
    tiles_per_seq = seq // tm

    def mod_spec(k):
        return pl.BlockSpec((1, 1, d), lambda i, j: ((i // tiles_per_seq) * N_MOD + k, 0, 0))

    return pl.pallas_call(
        _ffn_kernel,
        grid=(m // tm, nj),
        in_specs=[
            pl.BlockSpec((tm, d), lambda i, j: (i, 0)),
            mod_spec(k_shift), mod_spec(k_scale), mod_spec(k_gate),
            pl.BlockSpec((d, tf), lambda i, j: (0, j)),
            pl.BlockSpec((d, tf), lambda i, j: (0, j + nj)),
            pl.BlockSpec((tf, d), lambda i, j: (j, 0)),
            pl.BlockSpec((1, d), lambda i, j: (0, 0)),
            pl.BlockSpec((1, d), lambda i, j: (0, 0)),
        ],
        out_specs=pl.BlockSpec((tm, d), lambda i, j: (i, 0)),
        out_shape=jax.ShapeDtypeStruct((m, d), F32),
        scratch_shapes=[pltpu.VMEM((tm, d), BF16)],
        compiler_params=_cparams(("parallel", "arbitrary"), VMEM_LIMIT_BIG),
        name="ffn",
    )(x2d, mods3, mods3, mods3, w_gu, w_gu, w_down, ln_g, ln_b)


PROJ_TM = 1024


def _rope_kernel(pos_ref, freq_ref, *out_refs, dilations):
    cos_ref, sin_ref = out_refs[0], out_refs[1]
    tm = cos_ref.shape[0]
    pos_t = pos_ref[0].astype(F32).T
    half = DIL_HEAD_DIM // 2
    low = lax.broadcasted_iota(jnp.int32, (LANES, DIL_HEAD_DIM), 1) < half
    n_blocks = tm // LANES
    for k in range(n_blocks // 2):
        k2 = k + n_blocks // 2
        ang = jnp.where(low, pos_t[:, k:k + 1], pos_t[:, k2:k2 + 1]) * freq_ref[...]
        c, s = jnp.cos(ang), jnp.sin(ang)
        c_sw, s_sw = pltpu.roll(c, half, 1), pltpu.roll(s, half, 1)
        cos_ref[k * LANES:(k + 1) * LANES, :] = jnp.where(low, c, c_sw)
        sin_ref[k * LANES:(k + 1) * LANES, :] = jnp.where(low, -s, s_sw)
        cos_ref[k2 * LANES:(k2 + 1) * LANES, :] = jnp.where(low, c_sw, c)
        sin_ref[k2 * LANES:(k2 + 1) * LANES, :] = jnp.where(low, -s_sw, s)
    for gi, r in enumerate(dilations[1:]):
        tl = tm // r
        for c in range(r):
            out_refs[2 + 2 * gi][c * tl:(c + 1) * tl, :] = cos_ref[pl.ds(c, tl, stride=r), :]
            out_refs[3 + 2 * gi][c * tl:(c + 1) * tl, :] = sin_ref[pl.ds(c, tl, stride=r), :]


def _rope_tables(pos3, freq2, dilations, tm=PROJ_TM):
    assert dilations[0] == 1
    m = pos3.shape[0] * tm
    n_out = 2 * len(dilations)
    return pl.pallas_call(
        functools.partial(_rope_kernel, dilations=dilations),
        grid=(m // tm,),
        in_specs=[
            pl.BlockSpec((1, tm // LANES, LANES), lambda i: (i, 0, 0)),
            pl.BlockSpec((1, DIL_HEAD_DIM), lambda i: (0, 0)),
        ],
        out_specs=[pl.BlockSpec((tm, DIL_HEAD_DIM), lambda i: (i, 0))] * n_out,
        out_shape=[jax.ShapeDtypeStruct((m, DIL_HEAD_DIM), F32)] * n_out,
        compiler_params=_cparams(("parallel",)),
        name="rope_tables",
    )(pos3, freq2)


def _proj_kernel(x_ref, sh_ref, sc_ref, w_ref, wlr_ref, o_ref, lr_ref, h_scr):
    @pl.when(pl.program_id(1) == 0)
    def _():
        h = (x_ref[...] * (1.0 + sc_ref[0]) + sh_ref[0]).astype(BF16)
        h_scr[...] = h
        lr_ref[...] = jnp.dot(h, wlr_ref[...], preferred_element_type=F32)

    o_ref[...] = jnp.dot(h_scr[...], w_ref[...], preferred_element_type=F32).astype(BF16)


def _proj(x2d, mods3, k_shift, k_scale, w_all, seq, tm=PROJ_TM, tn=2048):
    m, d = x2d.shape
    n = PROJ_W
    tiles_per_seq = seq // tm

    def mod_spec(k):
        return pl.BlockSpec((1, 1, d), lambda i, j: ((i // tiles_per_seq) * N_MOD + k, 0, 0))

    return pl.pallas_call(
        _proj_kernel,
        grid=(m // tm, n // tn),
        in_specs=[
            pl.BlockSpec((tm, d), lambda i, j: (i, 0)),
            mod_spec(k_shift), mod_spec(k_scale),
            pl.BlockSpec((d, tn), lambda i, j: (0, j)),
            pl.BlockSpec((d, LR_PAD), lambda i, j: (0, OFF_LR // LR_PAD)),
        ],
        out_specs=[
            pl.BlockSpec((tm, tn), lambda i, j: (i, j)),
            pl.BlockSpec((tm, LR_PAD), lambda i, j: (i, 0)),
        ],
        out_shape=[
            jax.ShapeDtypeStruct((m, n), BF16),
            jax.ShapeDtypeStruct((m, LR_PAD), F32),
        ],
        scratch_shapes=[pltpu.VMEM((tm, d), BF16)],
        compiler_params=_cparams(("parallel", "arbitrary")),
        name="proj",
    )(x2d, mods3, mods3, w_all, w_all)


def _proj_dil_kernel(x_ref, sh_ref, sc_ref, w_ref, cos_ref, sin_ref, o_ref, h_scr, slab_scr, *, r):
    tm, d = x_ref.shape
    tl = tm // r
    if r == 1:
        h_scr[...] = (x_ref[...] * (1.0 + sc_ref[0]) + sh_ref[0]).astype(BF16)
    else:
        for k in range(d // LANES):
            sl = slice(k * LANES, (k + 1) * LANES)
            slab_scr[k] = x_ref[:, sl] * (1.0 + sc_ref[0, :, sl]) + sh_ref[0, :, sl]
        for c in range(r):
            for k in range(d // LANES):
                sl = slice(k * LANES, (k + 1) * LANES)
                h_scr[c * tl:(c + 1) * tl, sl] = slab_scr[k, pl.ds(c, tl, stride=r), :].astype(BF16)
    _proj_dil_qkv(h_scr, w_ref, cos_ref, sin_ref, o_ref, r)


def _proj_dil_qkv(h_scr, w_ref, cos_ref, sin_ref, o_ref, r):
    tl = h_scr.shape[0] // r
    for part, scale in enumerate((DIL_HEAD_DIM ** -0.5, 1.0, None)):
        cols = slice(part * DIL_OUT, (part + 1) * DIL_OUT)
        t = jnp.dot(h_scr[...], w_ref[:, cols], preferred_element_type=F32)
        if scale is None:
            for c in range(r):
                o_ref[0, c, :, cols] = t[c * tl:(c + 1) * tl, :].astype(BF16)
            continue
        cos = cos_ref[...] * scale
        sin = sin_ref[...] * scale
        for s in range(DIL_HEADS):
            ts = t[:, s * DIL_HEAD_DIM:(s + 1) * DIL_HEAD_DIM]
            rot = (ts * cos + pltpu.roll(ts, DIL_HEAD_DIM // 2, 1) * sin).astype(BF16)
            lo = part * DIL_OUT + s * DIL_HEAD_DIM
            for c in range(r):
                o_ref[0, c, :, lo:lo + DIL_HEAD_DIM] = rot[c * tl:(c + 1) * tl, :]


def _proj_dil_dma_kernel(x_hbm, sh_ref, sc_ref, w_ref, cos_ref, sin_ref, o_ref, h_scr, x_buf, sem,
                         *, r):
    i = pl.program_id(0)
    tl = h_scr.shape[0] // r
    slot = i % 2

    def tile_copies(tile, buf_slot):
        return [pltpu.make_async_copy(x_hbm.at[pl.ds(tile * tl, tl), c], x_buf.at[buf_slot, c],
                                      sem.at[buf_slot]) for c in range(r)]

    @pl.when(i == 0)
    def _():
        for cp in tile_copies(0, 0):
            cp.start()

    @pl.when(i + 1 < pl.num_programs(0))
    def _():
        for cp in tile_copies(i + 1, 1 - slot):
            cp.start()

    for cp in tile_copies(i, slot):
        cp.wait()
    for c in range(r):
        h_scr[c * tl:(c + 1) * tl, :] = (x_buf[slot, c] * (1.0 + sc_ref[0]) + sh_ref[0]).astype(BF16)
    _proj_dil_qkv(h_scr, w_ref, cos_ref, sin_ref, o_ref, r)


def _proj_dil(x2d, mods3, k_shift, k_scale, w_all, g, cos_t, sin_t, bsz, seq, r, tm=PROJ_TM):
    m, d = x2d.shape
    tiles_per_seq = seq // tm
    tl = tm // r
    dma_gather = r % SUBLANES_F32 == 0

    def mod_spec(k):
        return pl.BlockSpec((1, 1, d), lambda i: ((i // tiles_per_seq) * N_MOD + k, 0, 0))

    if dma_gather:
        body = functools.partial(_proj_dil_dma_kernel, r=r)
        x_arg = x2d.reshape(m // r, r, d)
        x_spec = pl.BlockSpec(memory_space=pl.ANY)
        scratch = [pltpu.VMEM((tm, d), BF16), pltpu.VMEM((2, r, tl, d), F32),
                   pltpu.SemaphoreType.DMA((2,))]
        semantics = ("arbitrary",)
    else:
        body = functools.partial(_proj_dil_kernel, r=r)
        x_arg = x2d
        x_spec = pl.BlockSpec((tm, d), lambda i: (i, 0))
        scratch = [pltpu.VMEM((tm, d), BF16),
                   pltpu.VMEM((d // LANES, tm, LANES) if r > 1 else (1, SUBLANES_F32, LANES), F32)]
        semantics = ("parallel",)

    return pl.pallas_call(
        body,
        grid=(m // tm,),
        in_specs=[
            x_spec,
            mod_spec(k_shift), mod_spec(k_scale),
            pl.BlockSpec((d, DIL_QKV), lambda i: (0, OFF_DIL // DIL_QKV + g),
                         pipeline_mode=pl.Buffered(1)),
            pl.BlockSpec((tm, DIL_HEAD_DIM), lambda i: (i, 0)),
            pl.BlockSpec((tm, DIL_HEAD_DIM), lambda i: (i, 0)),
        ],
        out_specs=pl.BlockSpec((1, r, tl, DIL_QKV),
                               lambda i: (i // tiles_per_seq, 0, i % tiles_per_seq, 0)),
        out_shape=jax.ShapeDtypeStruct((bsz, r, seq // r, DIL_QKV), BF16),
        scratch_shapes=scratch,
        compiler_params=_cparams(semantics, VMEM_LIMIT_BIG),
        name=f"proj_dil_r{r}",
    )(x_arg, mods3, mods3, w_all, cos_t, sin_t)


def _gla_kernel(q_ref, k_ref, v_ref, r_ref, lr_ref, wa_ref, ba_ref, ng_ref, o_ref,
                st_scr, b_scr, qd_scr, qh_scr, ki_scr, ke_scr, kp_scr, kh_scr, a_scr, *, n_chunks):
    c_len = GLA_CHUNK
    tc = q_ref.shape[1]
    hb = 4 * c_len
    pw = 2 * c_len
    n_pairs = n_chunks // 2

    @pl.when(pl.program_id(2) == 0)
    def _():
        st_scr[...] = jnp.zeros_like(st_scr)

    ti = lax.broadcasted_iota(jnp.int32, (hb, hb), 0)
    tj = lax.broadcasted_iota(jnp.int32, (hb, hb), 1)
    tri = jnp.logical_and(ti >= tj, ti // c_len == tj // c_len).astype(BF16)
    for s in range(tc // hb):
        rows = slice(s * hb, (s + 1) * hb)
        logits = jnp.dot(lr_ref[0, rows, :].astype(BF16), wa_ref[...],
                         preferred_element_type=F32) + ba_ref[...]
        log_a = (jnp.minimum(logits, 0.0) - jnp.log(1.0 + jnp.exp(-jnp.abs(logits)))) / GLA_TAU
        p1 = log_a.astype(BF16)
        r1 = log_a - p1.astype(F32)
        p2 = r1.astype(BF16)
        p3 = (r1 - p2.astype(F32)).astype(BF16)
        b_scr[rows, :] = (jnp.dot(tri, p1, preferred_element_type=F32)
                          + jnp.dot(tri, p2, preferred_element_type=F32)
                          + jnp.dot(tri, p3, preferred_element_type=F32))

    nt = (((1,), (1,)), ((), ()))

    def chunk_rows(c):
        return slice(c * c_len, (c + 1) * c_len)

    def pair_rows(p):
        return slice(p * pw, (p + 1) * pw)

    b_last = [b_scr[(c + 1) * c_len - 1:(c + 1) * c_len, :] for c in range(n_chunks)]
    h_log = [jnp.zeros_like(b_last[0])]
    for c in range(n_chunks):
        h_log.append(h_log[-1] + b_last[c])

    for c in range(n_chunks):
        rows = chunk_rows(c)
        b = b_scr[rows, :]
        q = q_ref[0, rows, :].astype(F32)
        k = k_ref[0, rows, :].astype(F32)
        qd = q * jnp.exp(b) * (GLA_DK ** -0.5)
        ke = k * jnp.exp(b_last[c] - b)
        qd_scr[rows, :] = qd
        qh_scr[rows, :] = (qd * jnp.exp(h_log[c])).astype(BF16)
        ki_scr[rows, :] = (k * jnp.exp(-b)).astype(BF16)
        ke_scr[rows, :] = ke.astype(BF16)
        kp_scr[rows, :] = (ke * jnp.exp(b_last[c + 1]) if c % 2 == 0 else ke).astype(BF16)
        kh_scr[rows, :] = (ke * jnp.exp(h_log[n_chunks] - h_log[c + 1])).astype(BF16)

    pi = lax.broadcasted_iota(jnp.int32, (pw, pw), 0)
    pj = lax.broadcasted_iota(jnp.int32, (pw, pw), 1)
    diag = jnp.logical_and(pi >= pj, pi // c_len == pj // c_len)
    lower = jnp.logical_and(pi >= c_len, pj < c_len)
    for p in range(n_pairs):
        first = chunk_rows(2 * p)
        keys = jnp.concatenate([ki_scr[pair_rows(p), :], ke_scr[first, :], ke_scr[first, :]], axis=0)
        s = lax.dot_general(qd_scr[pair_rows(p), :].astype(BF16), keys, nt,
                            preferred_element_type=F32)
        s = jnp.where(diag, s[:, :pw], jnp.where(lower, s[:, pw:], 0.0))
        a_scr[pair_rows(p), pair_rows(p)] = s.astype(BF16)

    for p in range(n_pairs - 1):
        c0 = 2 * p + 2
        lhs = jnp.concatenate(
            [(qd_scr[chunk_rows(c), :] * jnp.exp(h_log[c] - h_log[c0])).astype(BF16)
             for c in range(c0, n_chunks)], axis=0)
        s = lax.dot_general(lhs, kp_scr[pair_rows(p), :], nt, preferred_element_type=F32)
        a_scr[c0 * c_len:, pair_rows(p)] = s.astype(BF16)

    half = 2 * pw
    for hf in range(tc // half):
        a_scr[hf * half:hf * half + pw, hf * half + pw:(hf + 1) * half] = jnp.zeros((pw, pw), BF16)

    st = st_scr[...]
    st_b = st.astype(BF16)
    for hf in range(tc // half):
        rows = slice(hf * half, (hf + 1) * half)
        keys = (hf + 1) * half
        o = jnp.dot(a_scr[rows, :keys], v_ref[0, :keys, :], preferred_element_type=F32)
        o += jnp.dot(qh_scr[rows, :], st_b, preferred_element_type=F32)
        o = o * lax.rsqrt(jnp.mean(o * o, axis=-1, keepdims=True) + LN_EPS) * ng_ref[0]
        r = r_ref[0, rows, :].astype(F32)
        o_ref[0, rows, :] = (o * (r * _sigmoid(r))).astype(BF16)

    d_col = jnp.broadcast_to(jnp.exp(h_log[n_chunks]), (SUBLANES_F32, GLA_DK)).T[:, :1]
    st_scr[...] = st * d_col + lax.dot_general(
        kh_scr[...], v_ref[0], (((0,), (0,)), ((), ())), preferred_element_type=F32)


def _gla(p3, lr3, w_a2, b_a, norm_g3, tc=1024):
    bsz, seq, _ = p3.shape
    nq, nv, nr = OFF_GK // GLA_DK, OFF_GV // GLA_DV, OFF_GR // GLA_DV
    return pl.pallas_call(
        functools.partial(_gla_kernel, n_chunks=tc // GLA_CHUNK),
        grid=(bsz, GLA_HEADS, seq // tc),
        in_specs=[
            pl.BlockSpec((1, tc, GLA_DK), lambda b, h, t: (b, t, h)),
            pl.BlockSpec((1, tc, GLA_DK), lambda b, h, t: (b, t, nq + h)),
            pl.BlockSpec((1, tc, GLA_DV), lambda b, h, t: (b, t, nv + h)),
            pl.BlockSpec((1, tc, GLA_DV), lambda b, h, t: (b, t, nr + h)),
            pl.BlockSpec((1, tc, LR_PAD), lambda b, h, t: (b, t, 0)),
            pl.BlockSpec((LR_PAD, GLA_DK), lambda b, h, t: (0, h)),
            pl.BlockSpec((1, GLA_DK), lambda b, h, t: (0, h)),
            pl.BlockSpec((1, 1, GLA_DV), lambda b, h, t: (h, 0, 0)),
        ],
        out_specs=pl.BlockSpec((1, tc, GLA_DV), lambda b, h, t: (b, t, h)),
        out_shape=jax.ShapeDtypeStruct((bsz, seq, GLA_VW), BF16),
        scratch_shapes=[pltpu.VMEM((GLA_DK, GLA_DV), F32), pltpu.VMEM((tc, GLA_DK), F32),
                        pltpu.VMEM((tc, GLA_DK), F32)]
                       + [pltpu.VMEM((tc, GLA_DK), BF16)] * 5 + [pltpu.VMEM((tc, tc), BF16)],
        compiler_params=_cparams(("parallel", "parallel", "arbitrary")),
        name="gla",
    )(p3, p3, p3, p3, lr3, w_a2, b_a, norm_g3)


def _dil_kernel(q_ref, k_ref, v_ref, kp_ref, vp_ref, o_ref, lse_ref, s_scr, m_scr, p_scr, *, nblk):
    n = pl.program_id(2)
    blk = DIL_BLOCK
    qi = lax.broadcasted_iota(jnp.int32, (blk, 2 * blk), 0)
    kj = lax.broadcasted_iota(jnp.int32, (blk, 2 * blk), 1)
    band = jnp.logical_or(jnp.logical_and(kj < blk, kj >= qi),
                          jnp.logical_and(kj >= blk, kj - blk <= qi))
    band_first = jnp.logical_and(band, jnp.logical_or(kj >= blk, n > 0))
    nt = (((1,), (1,)), ((), ()))

    def rows(i):
        return slice(i * blk, (i + 1) * blk)

    def cols(h):
        return slice(h * DIL_HEAD_DIM, (h + 1) * DIL_HEAD_DIM)

    for i in range(nblk):
        for h in range(DIL_HEADS):
            q = q_ref[0, 0, rows(i), cols(h)]
            k_prev = kp_ref[0, 0, :, cols(h)] if i == 0 else k_ref[0, 0, rows(i - 1), cols(h)]
            s_p = lax.dot_general(q, k_prev, nt, preferred_element_type=F32)
            s_c = lax.dot_general(q, k_ref[0, 0, rows(i), cols(h)], nt, preferred_element_type=F32)
            s = jnp.concatenate([s_p, s_c], axis=1)
            s_scr[rows(i * DIL_HEADS + h), :] = jnp.where(band_first if i == 0 else band, s, NEG_BIG)

    s_all = s_scr[...]
    m_all = jnp.max(s_all, axis=-1, keepdims=True)
    p_scr[...] = jnp.exp(s_all - m_all).astype(BF16)
    m_scr[...] = jnp.broadcast_to(m_all, m_scr.shape)

    lane = lax.broadcasted_iota(jnp.int32, (blk, LANES), 1)
    ones = jnp.ones((2 * blk, DIL_HEAD_DIM), BF16)
    for i in range(nblk):
        lse_tile = jnp.zeros((blk, LANES), F32)
        for h in range(DIL_HEADS):
            u = i * DIL_HEADS + h
            v_prev = vp_ref[0, 0, :, cols(h)] if i == 0 else v_ref[0, 0, rows(i - 1), cols(h)]
            v_aug = jnp.concatenate(
                [jnp.concatenate([v_prev, v_ref[0, 0, rows(i), cols(h)]], axis=0), ones], axis=1)
            acc = jnp.dot(p_scr[rows(u), :], v_aug, preferred_element_type=F32)
            den = acc[:, DIL_HEAD_DIM:]
            o_ref[0, 0, rows(i), cols(h)] = (acc[:, :DIL_HEAD_DIM] / den).astype(BF16)
            lse_tile = jnp.where(lane == h, m_scr[rows(u), :] + jnp.log(den), lse_tile)
        lse_ref[0, 0, rows(i), :] = lse_tile


def _dil_group(qkv, nblk=8):
    bsz, r, length, _ = qkv.shape
    nb = length // DIL_BLOCK
    nblk = min(nblk, nb)
    assert nb % nblk == 0
    tq = nblk * DIL_BLOCK
    units = nblk * DIL_HEADS * DIL_BLOCK

    def cur_spec(u):
        return pl.BlockSpec((1, 1, tq, DIL_OUT), lambda b, c, n: (b, c, n, u))

    def prev_spec(u):
        return pl.BlockSpec((1, 1, DIL_BLOCK, DIL_OUT),
                            lambda b, c, n: (b, c, jnp.maximum(n * nblk - 1, 0), u))

    return pl.pallas_call(
        functools.partial(_dil_kernel, nblk=nblk),
        grid=(bsz, r, nb // nblk),
        in_specs=[cur_spec(0), cur_spec(1), cur_spec(2), prev_spec(1), prev_spec(2)],
        out_specs=[
            pl.BlockSpec((1, 1, tq, DIL_OUT), lambda b, c, n: (b, c, n, 0)),
            pl.BlockSpec((1, 1, tq, LANES), lambda b, c, n: (b, c, n, 0)),
        ],
        out_shape=[
            jax.ShapeDtypeStruct((bsz, r, length, DIL_OUT), BF16),
            jax.ShapeDtypeStruct((bsz, r, length, LANES), F32),
        ],
        scratch_shapes=[pltpu.VMEM((units, 2 * DIL_BLOCK), F32),
                        pltpu.VMEM((units, LANES), F32),
                        pltpu.VMEM((units, 2 * DIL_BLOCK), BF16)],
        compiler_params=_cparams(("parallel", "parallel", "parallel")),
        name=f"dilattn_r{r}",
    )(qkv, qkv, qkv, qkv, qkv)


def _merge_kernel(oa_ref, o1_ref, o2_ref, o3_ref, l1_ref, l2_ref, l3_ref, ga_ref, gb_ref, x_ref,
                  g2_ref, wa_ref, wb_ref, wo_ref, lng_ref, lnb_ref, out_ref, l_scr, w_scr, ob_scr,
                  *, dilations):
    y_a = jnp.dot(oa_ref[...], wa_ref[...], preferred_element_type=F32)
    o_refs = (o1_ref, o2_ref, o3_ref)
    l_refs = (l1_ref, l2_ref, l3_ref)
    tm = oa_ref.shape[0]

    def tok_rows(c, r):
        return pl.ds(c, tm // r, stride=r) if r > 1 else slice(None)

    for g, r in enumerate(dilations):
        for c in range(r):
            l_scr[g, tok_rows(c, r), :] = l_refs[g][0, c]
    l1, l2, l3 = l_scr[0], l_scr[1], l_scr[2]
    mx = jnp.maximum(jnp.maximum(l1, l2), l3)
    e1, e2, e3 = jnp.exp(l1 - mx), jnp.exp(l2 - mx), jnp.exp(l3 - mx)
    inv = 1.0 / (e1 + e2 + e3)
    w_scr[0], w_scr[1], w_scr[2] = e1 * inv, e2 * inv, e3 * inv
    for g, r in enumerate(dilations):
        for c in range(r):
            rows = tok_rows(c, r)
            wr = w_scr[g, rows, :]
            for h in range(DIL_HEADS):
                sl = slice(h * DIL_HEAD_DIM, (h + 1) * DIL_HEAD_DIM)
                part = wr[:, h:h + 1] * o_refs[g][0, c, :, sl].astype(F32)
                if g == 0:
                    ob_scr[h, rows, :] = part
                else:
                    ob_scr[h, rows, :] += part
    o_b = jnp.concatenate([ob_scr[h].astype(BF16) for h in range(DIL_HEADS)], axis=1)
    y_b = jnp.dot(o_b, wb_ref[...], preferred_element_type=F32)
    merged = _sigmoid(ga_ref[...].astype(F32)) * y_a + _sigmoid(gb_ref[...].astype(F32)) * y_b
    mix = jnp.dot(merged.astype(BF16), wo_ref[...], preferred_element_type=F32)
    out_ref[...] = _deepnorm(x_ref[...], (1.0 / DN_ALPHA) * g2_ref[0] * mix, lng_ref[...], lnb_ref[...])


def _merge(oa2d, o_groups, lse_groups, p2d, x2d, mods3, k_gate, w_a, w_b, w_o, ln_g, ln_b, seq,
           dilations, tm=256):
    m, d = x2d.shape
    tiles_per_seq = seq // tm
    resident = pl.Buffered(1)
    n_g = len(dilations)

    def row_spec(width, col=0):
        return pl.BlockSpec((tm, width), lambda i: (i, col))

    def class_spec(r, width):
        return pl.BlockSpec((1, r, tm // r, width),
                            lambda i: (i // tiles_per_seq, 0, i % tiles_per_seq, 0))

    def const_spec(shape):
        return pl.BlockSpec(shape, lambda i: (0, 0), pipeline_mode=resident)

    return pl.pallas_call(
        functools.partial(_merge_kernel, dilations=dilations),
        grid=(m // tm,),
        in_specs=[
            row_spec(GLA_VW),
            *[class_spec(r, DIL_OUT) for r in dilations],
            *[class_spec(r, LANES) for r in dilations],
            row_spec(D_MODEL, OFF_GA // D_MODEL), row_spec(D_MODEL, OFF_GB // D_MODEL),
            row_spec(d),
            pl.BlockSpec((1, 1, d), lambda i: ((i // tiles_per_seq) * N_MOD + k_gate, 0, 0)),
            const_spec(w_a.shape), const_spec(w_b.shape), const_spec(w_o.shape),
            const_spec((1, d)), const_spec((1, d)),
        ],
        out_specs=row_spec(d),
        out_shape=jax.ShapeDtypeStruct((m, d), F32),
        scratch_shapes=[pltpu.VMEM((n_g, tm, LANES), F32), pltpu.VMEM((n_g, tm, LANES), F32),
                        pltpu.VMEM((DIL_HEADS, tm, DIL_HEAD_DIM), F32)],
        compiler_params=_cparams(("parallel",)),
        name="merge",
    )(oa2d, *o_groups, *lse_groups, p2d, p2d, x2d, mods3, w_a, w_b, w_o, ln_g, ln_b)


PACK_W = 1024
PACK_COPY, PACK_SHIFTED, PACK_ZERO, PACK_LOW_RANK = range(4)


def _pack_plan():
    src, o = {}, 0
    for name, wdt in zip(("gq", "gk", "gv", "gr", "glr", "dq", "dk", "dv", "ga", "gb"),
                         (GLA_QK, GLA_QK, GLA_VW, GLA_VW, GLA_GATE_RANK, DIL_W, DIL_W, DIL_W,
                          D_MODEL, D_MODEL)):
        src[name] = (o, wdt)
        o += wdt
    plan = []

    def copy(first, width):
        for c in range(first, first + width, PACK_W):
            plan.append((c, PACK_COPY if c % PACK_W == 0 else PACK_SHIFTED))

    for name in ("gq", "gk", "gv", "gr", "ga", "gb"):
        copy(*src[name])
    plan += [(0, PACK_ZERO)] * ((OFF_DIL - PROJ_W) // PACK_W)
    for g in range(DIL_GROUPS):
        for name in ("dq", "dk", "dv"):
            copy(src[name][0] + g * DIL_OUT, DIL_OUT)
    plan.append((src["glr"][0], PACK_LOW_RANK))
    return plan


def _pack_kernel(plan_ref, a_ref, b_ref, o_ref):
    kind = plan_ref[2, pl.program_id(0)]
    rank = GLA_GATE_RANK

    @pl.when(kind == PACK_COPY)
    def _():
        o_ref[...] = a_ref[...].T.astype(BF16)

    @pl.when(kind == PACK_SHIFTED)
    def _():
        o_ref[...] = jnp.concatenate([a_ref[rank:, :], b_ref[...]], axis=0).T.astype(BF16)

    @pl.when(kind == PACK_ZERO)
    def _():
        o_ref[...] = jnp.zeros_like(o_ref)

    @pl.when(kind == PACK_LOW_RANK)
    def _():
        lane = lax.broadcasted_iota(jnp.int32, (o_ref.shape[0], LANES), 1)
        o_ref[...] = jnp.zeros_like(o_ref)
        o_ref[:, :LANES] = jnp.where(lane < rank, a_ref[:LANES, :].T, 0.0).astype(BF16)


def _pack_w_in(w_in, layer):
    _, d, width = w_in.shape
    w_t = jnp.swapaxes(w_in, 1, 2)
    plan = _pack_plan()
    first = [c - c % PACK_W for c, _ in plan]
    rank = GLA_GATE_RANK
    assert all(c % PACK_W in (0, rank) for c, _ in plan) and rank % SUBLANES_F32 == 0
    table = jnp.asarray([[c // PACK_W for c in first],
                         [min(c + PACK_W, width - rank) // rank for c in first],
                         [kind for _, kind in plan]], jnp.int32)
    return pl.pallas_call(
        _pack_kernel,
        grid_spec=pltpu.PrefetchScalarGridSpec(
            num_scalar_prefetch=1,
            grid=(len(plan),),
            in_specs=[pl.BlockSpec((None, PACK_W, d), lambda i, t: (layer, t[0, i], 0)),
                      pl.BlockSpec((None, rank, d), lambda i, t: (layer, t[1, i], 0))],
            out_specs=pl.BlockSpec((d, PACK_W), lambda i, t: (0, i)),
        ),
        out_shape=jax.ShapeDtypeStruct((d, len(plan) * PACK_W), BF16),
        compiler_params=_cparams(("arbitrary",)),
        name="pack_w_in",
    )(table, w_t, w_t)


def kernel(x, c, positions, w_ada, b_ada, ln1_g, ln1_b, w_ffn1_gu, w_ffn1_down, w_in, w_alpha2,
           b_alpha, gla_norm_g, w_branch_a, w_branch_b, w_out, ln2_g, ln2_b, w_ffn2_gu, w_ffn2_down,
           ln3_g, ln3_b):
    bsz, seq, d = x.shape
    m = bsz * seq
    x2d = x.reshape(m, d)
    c_pad = jnp.pad(c, ((0, 8 - bsz % 8 if bsz % 8 else 0), (0, 0)))
    dilations = tuple(r for _, r in DIL_PATTERNS)

    half = DIL_HEAD_DIM // 2
    freq = ROPE_THETA ** (-jnp.arange(half, dtype=F32) / half)
    freq2 = jnp.concatenate([freq, freq]).reshape(1, DIL_HEAD_DIM)
    rope_t = _rope_tables(positions.reshape(m // PROJ_TM, PROJ_TM // LANES, LANES), freq2, dilations)

    for l in range(DEPTH):
        mods = _mods(c_pad, w_ada[l], b_ada[l].reshape(1, -1))[:bsz]
        mods3 = mods.reshape(bsz * N_MOD, 1, d)

        x1 = _ffn(x2d, mods3, 0, 1, 2, w_ffn1_gu[l].astype(BF16), w_ffn1_down[l].astype(BF16),
                  ln1_g[l].reshape(1, d), ln1_b[l].reshape(1, d), seq)

        w_all = _pack_w_in(w_in, l)
        p2d, lr2d = _proj(x1, mods3, 3, 4, w_all, seq)
        p3 = p2d.reshape(bsz, seq, PROJ_W)

        w_a2 = jnp.pad(w_alpha2[l], ((0, LR_PAD - GLA_GATE_RANK), (0, 0))).astype(BF16)
        o_a = _gla(p3, lr2d.reshape(bsz, seq, LR_PAD), w_a2, b_alpha[l].reshape(1, GLA_QK),
                   gla_norm_g[l].reshape(GLA_HEADS, 1, GLA_DV))

        o_groups, lse_groups = [], []
        for g, r in enumerate(dilations):
            qkv = _proj_dil(x1, mods3, 3, 4, w_all, g, rope_t[2 * g], rope_t[2 * g + 1],
                            bsz, seq, r)
            o_g, lse_g = _dil_group(qkv)
            o_groups.append(o_g)
            lse_groups.append(lse_g)

        x2 = _merge(o_a.reshape(m, GLA_VW), o_groups, lse_groups, p2d, x1, mods3, 5,
                    w_branch_a[l].astype(BF16), w_branch_b[l].astype(BF16), w_out[l].astype(BF16),
                    ln2_g[l].reshape(1, d), ln2_b[l].reshape(1, d), seq, dilations)

        x2d = _ffn(x2, mods3, 6, 7, 8, w_ffn2_gu[l].astype(BF16), w_ffn2_down[l].astype(BF16),
                   ln3_g[l].reshape(1, d), ln3_b[l].reshape(1, d), seq)
    return x2d.reshape(bsz, seq, d)
```

```python
import functools

import jax
import jax.numpy as jnp
from jax import lax
from jax.experimental import pallas as pl
from jax.experimental.pallas import tpu as pltpu

F32 = jnp.float32
BF16 = jnp.bfloat16

D_MODEL = 2048
DEPTH = 1
N_MOD = 9
LN_EPS = 1e-5
DN_ALPHA = (2.0 * DEPTH) ** 0.25

GLA_HEADS = 4
GLA_DK = 256
GLA_DV = 512
GLA_GATE_RANK = 16
GLA_TAU = 16.0
GLA_CHUNK = 64
GLA_QK = GLA_HEADS * GLA_DK
GLA_VW = GLA_HEADS * GLA_DV

DIL_PATTERNS = ((128, 1), (512, 4), (2048, 16))
DIL_GROUPS = len(DIL_PATTERNS)
DIL_HEADS = 8
DIL_HEAD_DIM = 128
DIL_BLOCK = 128
DIL_W = DIL_GROUPS * DIL_HEADS * DIL_HEAD_DIM
DIL_OUT = DIL_HEADS * DIL_HEAD_DIM
ROPE_THETA = 10000.0

LANES = 128
SUBLANES_F32 = 8
LR_PAD = LANES

PROJ_W = 2 * GLA_QK + 2 * GLA_VW + 2 * D_MODEL
OFF_GK = GLA_QK
OFF_GV = 2 * GLA_QK
OFF_GR = OFF_GV + GLA_VW
OFF_GA = OFF_GR + GLA_VW
OFF_GB = OFF_GA + D_MODEL
DIL_QKV = 3 * DIL_OUT
OFF_DIL = -(-PROJ_W // DIL_QKV) * DIL_QKV
OFF_LR = OFF_DIL + DIL_GROUPS * DIL_QKV

NEG_BIG = -1e30

VMEM_LIMIT = 56 * 1024 * 1024
VMEM_LIMIT_BIG = 60 * 1024 * 1024


def _cparams(sem, vmem=VMEM_LIMIT):
    return pltpu.CompilerParams(dimension_semantics=sem, vmem_limit_bytes=vmem)


def _sigmoid(x):
    return 1.0 / (1.0 + jnp.exp(-x))


def _deepnorm(x, update, g, b):
    y = x + update
    mu = jnp.mean(y, axis=-1, keepdims=True)
    d = y - mu
    var = jnp.mean(d * d, axis=-1, keepdims=True)
    return d * lax.rsqrt(var + LN_EPS / (DN_ALPHA * DN_ALPHA)) * g + b


def _mods_kernel(c_ref, w_ref, b_ref, o_ref):
    c = c_ref[...]
    c_act = (c * _sigmoid(c)).astype(BF16)
    o_ref[...] = jnp.dot(c_act, w_ref[...].astype(BF16), preferred_element_type=F32) + b_ref[...]


def _mods(c_pad, w_ada, b_ada, tn=1024):
    rows, d = c_pad.shape
    n = w_ada.shape[1]
    return pl.pallas_call(
        _mods_kernel,
        grid=(n // tn,),
        in_specs=[
            pl.BlockSpec((rows, d), lambda j: (0, 0)),
            pl.BlockSpec((d, tn), lambda j: (0, j)),
            pl.BlockSpec((1, tn), lambda j: (0, j)),
        ],
        out_specs=pl.BlockSpec((rows, tn), lambda j: (0, j)),
        out_shape=jax.ShapeDtypeStruct((rows, n), F32),
        compiler_params=_cparams(("arbitrary",)),
        name="mods",
    )(c_pad, w_ada, b_ada)


FFN_ROW_CHUNK = 512


def _ffn_kernel(x_ref, sh_ref, sc_ref, g_ref, wg_ref, wu_ref, wd_ref, lng_ref, lnb_ref,
                o_ref, h_scr):
    j = pl.program_id(1)

    @pl.when(j == 0)
    def _():
        h_scr[...] = (x_ref[...] * (1.0 + sc_ref[0]) + sh_ref[0]).astype(BF16)
        o_ref[...] = jnp.zeros_like(o_ref)

    for r0 in range(0, o_ref.shape[0], FFN_ROW_CHUNK):
        rs = slice(r0, r0 + FFN_ROW_CHUNK)
        h = h_scr[rs, :]
        gate = jnp.dot(h, wg_ref[...], preferred_element_type=F32)
        up = jnp.dot(h, wu_ref[...], preferred_element_type=F32)
        act = (gate * _sigmoid(gate) * up).astype(BF16)
        o_ref[rs, :] += jnp.dot(act, wd_ref[...], preferred_element_type=F32)

    @pl.when(j == pl.num_programs(1) - 1)
    def _():
        o_ref[...] = _deepnorm(x_ref[...], (0.5 / DN_ALPHA) * g_ref[0] * o_ref[...],
                               lng_ref[...], lnb_ref[...])


def _ffn(x2d, mods3, k_shift, k_scale, k_gate, w_gu, w_down, ln_g, ln_b, seq, tm=1024, tf=512):
    m, d = x2d.shape
    n_ff = w_down.shape[0]
    nj = n_ff // tf
    tiles_per_seq = seq // tm

    def mod_spec(k):
        return pl.BlockSpec((1, 1, d), lambda i, j: ((i // tiles_per_seq) * N_MOD + k, 0, 0))

    return pl.pallas_call(
        _ffn_kernel,
        grid=(m // tm, nj),
        in_specs=[
            pl.BlockSpec((tm, d), lambda i, j: (i, 0)),
            mod_spec(k_shift), mod_spec(k_scale), mod_spec(k_gate),
            pl.BlockSpec((d, tf), lambda i, j: (0, j)),
            pl.BlockSpec((d, tf), lambda i, j: (0, j + nj)),
            pl.BlockSpec((tf, d), lambda i, j: (j, 0)),
            pl.BlockSpec((1, d), lambda i, j: (0, 0)),
            pl.BlockSpec((1, d), lambda i, j: (0, 0)),
        ],
        out_specs=pl.BlockSpec((tm, d), lambda i, j: (i, 0)),
        out_shape=jax.ShapeDtypeStruct((m, d), F32),
        scratch_shapes=[pltpu.VMEM((tm, d), BF16)],
        compiler_params=_cparams(("parallel", "arbitrary"), VMEM_LIMIT_BIG),
        name="ffn",
    )(x2d, mods3, mods3, mods3, w_gu, w_gu, w_down, ln_g, ln_b)


PROJ_TM = 1024


def _rope_kernel(pos_ref, freq_ref, *out_refs, dilations):
    cos_ref, sin_ref = out_refs[0], out_refs[1]
    tm = cos_ref.shape[0]
    pos_t = pos_ref[0].astype(F32).T
    half = DIL_HEAD_DIM // 2
    low = lax.broadcasted_iota(jnp.int32, (LANES, DIL_HEAD_DIM), 1) < half
    n_blocks = tm // LANES
    for k in range(n_blocks // 2):
        k2 = k + n_blocks // 2
        ang = jnp.where(low, pos_t[:, k:k + 1], pos_t[:, k2:k2 + 1]) * freq_ref[...]
        c, s = jnp.cos(ang), jnp.sin(ang)
        c_sw, s_sw = pltpu.roll(c, half, 1), pltpu.roll(s, half, 1)
        cos_ref[k * LANES:(k + 1) * LANES, :] = jnp.where(low, c, c_sw)
        sin_ref[k * LANES:(k + 1) * LANES, :] = jnp.where(low, -s, s_sw)
        cos_ref[k2 * LANES:(k2 + 1) * LANES, :] = jnp.where(low, c_sw, c)
        sin_ref[k2 * LANES:(k2 + 1) * LANES, :] = jnp.where(low, -s_sw, s)
    for gi, r in enumerate(dilations[1:]):
        tl = tm // r
        for c in range(r):
            out_refs[2 + 2 * gi][c * tl:(c + 1) * tl, :] = cos_ref[pl.ds(c, tl, stride=r), :]
            out_refs[3 + 2 * gi][c * tl:(c + 1) * tl, :] = sin_ref[pl.ds(c, tl, stride=r), :]


def _rope_tables(pos3, freq2, dilations, tm=PROJ_TM):
    assert dilations[0] == 1
    m = pos3.shape[0] * tm
    n_out = 2 * len(dilations)
    return pl.pallas_call(
        functools.partial(_rope_kernel, dilations=dilations),
        grid=(m // tm,),
        in_specs=[
            pl.BlockSpec((1, tm // LANES, LANES), lambda i: (i, 0, 0)),
            pl.BlockSpec((1, DIL_HEAD_DIM), lambda i: (0, 0)),
        ],
        out_specs=[pl.BlockSpec((tm, DIL_HEAD_DIM), lambda i: (i, 0))] * n_out,
        out_shape=[jax.ShapeDtypeStruct((m, DIL_HEAD_DIM), F32)] * n_out,
        compiler_params=_cparams(("parallel",)),
        name="rope_tables",
    )(pos3, freq2)


def _proj_kernel(x_ref, sh_ref, sc_ref, w_ref, wlr_ref, o_ref, lr_ref, h_ref):
    @pl.when(pl.program_id(1) == 0)
    def _():
        h = (x_ref[...] * (1.0 + sc_ref[0]) + sh_ref[0]).astype(BF16)
        h_ref[...] = h
        lr_ref[...] = jnp.dot(h, wlr_ref[...], preferred_element_type=F32)

    o_ref[...] = jnp.dot(h_ref[...], w_ref[...], preferred_element_type=F32).astype(BF16)


def _proj(x2d, mods3, k_shift, k_scale, w_all, seq, tm=PROJ_TM, tn=2048):
    m, d = x2d.shape
    n = PROJ_W
    tiles_per_seq = seq // tm

    def mod_spec(k):
        return pl.BlockSpec((1, 1, d), lambda i, j: ((i // tiles_per_seq) * N_MOD + k, 0, 0))

    return pl.pallas_call(
        _proj_kernel,
        grid=(m // tm, n // tn),
        in_specs=[
            pl.BlockSpec((tm, d), lambda i, j: (i, 0)),
            mod_spec(k_shift), mod_spec(k_scale),
            pl.BlockSpec((d, tn), lambda i, j: (0, j)),
            pl.BlockSpec((d, LR_PAD), lambda i, j: (0, OFF_LR // LR_PAD)),
        ],
        out_specs=[
            pl.BlockSpec((tm, tn), lambda i, j: (i, j)),
            pl.BlockSpec((tm, LR_PAD), lambda i, j: (i, 0)),
            pl.BlockSpec((tm, d), lambda i, j: (i, 0)),
        ],
        out_shape=[
            jax.ShapeDtypeStruct((m, n), BF16),
            jax.ShapeDtypeStruct((m, LR_PAD), F32),
            jax.ShapeDtypeStruct((m, d), BF16),
        ],
        compiler_params=_cparams(("parallel", "arbitrary")),
        name="proj",
    )(x2d, mods3, mods3, w_all, w_all)


def _proj_dil_kernel(x_ref, sh_ref, sc_ref, w_ref, cos_ref, sin_ref, o_ref, h_scr, slab_scr, *, r):
    tm, d = x_ref.shape
    tl = tm // r
    if r == 1:
        h_scr[...] = (x_ref[...] * (1.0 + sc_ref[0]) + sh_ref[0]).astype(BF16)
    else:
        for k in range(d // LANES):
            sl = slice(k * LANES, (k + 1) * LANES)
            slab_scr[k] = x_ref[:, sl] * (1.0 + sc_ref[0, :, sl]) + sh_ref[0, :, sl]
        for c in range(r):
            for k in range(d // LANES):
                sl = slice(k * LANES, (k + 1) * LANES)
                h_scr[c * tl:(c + 1) * tl, sl] = slab_scr[k, pl.ds(c, tl, stride=r), :].astype(BF16)
    _proj_dil_qkv(h_scr, w_ref, cos_ref, sin_ref, o_ref, r)


def _proj_dil_h_kernel(h_ref, w_ref, cos_ref, sin_ref, o_ref):
    _proj_dil_qkv(h_ref, w_ref, cos_ref, sin_ref, o_ref, 1)


def _proj_dil_qkv(h_scr, w_ref, cos_ref, sin_ref, o_ref, r):
    tl = h_scr.shape[0] // r
    for part, scale in enumerate((DIL_HEAD_DIM ** -0.5, 1.0, None)):
        cols = slice(part * DIL_OUT, (part + 1) * DIL_OUT)
        t = jnp.dot(h_scr[...], w_ref[:, cols], preferred_element_type=F32)
        if scale is None:
            for c in range(r):
                o_ref[0, c, :, cols] = t[c * tl:(c + 1) * tl, :].astype(BF16)
            continue
        cos = cos_ref[...] * scale
        sin = sin_ref[...] * scale
        for s in range(DIL_HEADS):
            ts = t[:, s * DIL_HEAD_DIM:(s + 1) * DIL_HEAD_DIM]
            rot = (ts * cos + pltpu.roll(ts, DIL_HEAD_DIM // 2, 1) * sin).astype(BF16)
            lo = part * DIL_OUT + s * DIL_HEAD_DIM
            for c in range(r):
                o_ref[0, c, :, lo:lo + DIL_HEAD_DIM] = rot[c * tl:(c + 1) * tl, :]


def _proj_dil_dma_kernel(x_hbm, sh_ref, sc_ref, w_ref, cos_ref, sin_ref, o_ref, h_scr, x_buf, sem,
                         *, r):
    i = pl.program_id(0)
    tl = h_scr.shape[0] // r
    slot = i % 2

    def tile_copies(tile, buf_slot):
        return [pltpu.make_async_copy(x_hbm.at[pl.ds(tile * tl, tl), c], x_buf.at[buf_slot, c],
                                      sem.at[buf_slot]) for c in range(r)]

    @pl.when(i == 0)
    def _():
        for cp in tile_copies(0, 0):
            cp.start()

    @pl.when(i + 1 < pl.num_programs(0))
    def _():
        for cp in tile_copies(i + 1, 1 - slot):
            cp.start()

    for cp in tile_copies(i, slot):
        cp.wait()
    for c in range(r):
        h_scr[c * tl:(c + 1) * tl, :] = (x_buf[slot, c] * (1.0 + sc_ref[0]) + sh_ref[0]).astype(BF16)
    _proj_dil_qkv(h_scr, w_ref, cos_ref, sin_ref, o_ref, r)


def _proj_dil(x2d, h2d, mods3, k_shift, k_scale, w_all, g, cos_t, sin_t, bsz, seq, r, tm=PROJ_TM):
    m, d = x2d.shape
    tiles_per_seq = seq // tm
    tl = tm // r
    dma_gather = r % SUBLANES_F32 == 0

    def mod_spec(k):
        return pl.BlockSpec((1, 1, d), lambda i: ((i // tiles_per_seq) * N_MOD + k, 0, 0))

    if dma_gather:
        body = functools.partial(_proj_dil_dma_kernel, r=r)
        x_arg = x2d.reshape(m // r, r, d)
        x_spec = pl.BlockSpec(memory_space=pl.ANY)
        scratch = [pltpu.VMEM((tm, d), BF16), pltpu.VMEM((2, r, tl, d), F32),
                   pltpu.SemaphoreType.DMA((2,))]
        semantics = ("arbitrary",)
    else:
        body = functools.partial(_proj_dil_kernel, r=r)
        x_arg = x2d
        x_spec = pl.BlockSpec((tm, d), lambda i: (i, 0))
        scratch = [pltpu.VMEM((tm, d), BF16),
                   pltpu.VMEM((d // LANES, tm, LANES) if r > 1 else (1, SUBLANES_F32, LANES), F32)]
        semantics = ("parallel",)

    w_spec = pl.BlockSpec((d, DIL_QKV), lambda i: (0, OFF_DIL // DIL_QKV + g),
                          pipeline_mode=pl.Buffered(1))
    table_spec = pl.BlockSpec((tm, DIL_HEAD_DIM), lambda i: (i, 0))
    if r == 1:
        body, scratch = _proj_dil_h_kernel, []
        operands = (h2d, w_all, cos_t, sin_t)
        in_specs = [pl.BlockSpec((tm, d), lambda i: (i, 0)), w_spec, table_spec, table_spec]
    else:
        operands = (x_arg, mods3, mods3, w_all, cos_t, sin_t)
        in_specs = [x_spec, mod_spec(k_shift), mod_spec(k_scale), w_spec, table_spec, table_spec]

    return pl.pallas_call(
        body,
        grid=(m // tm,),
        in_specs=in_specs,
        out_specs=pl.BlockSpec((1, r, tl, DIL_QKV),
                               lambda i: (i // tiles_per_seq, 0, i % tiles_per_seq, 0)),
        out_shape=jax.ShapeDtypeStruct((bsz, r, seq // r, DIL_QKV), BF16),
        scratch_shapes=scratch,
        compiler_params=_cparams(semantics, VMEM_LIMIT_BIG),
        name=f"proj_dil_r{r}",
    )(*operands)


def _gla_kernel(q_ref, k_ref, v_ref, r_ref, lr_ref, wa_ref, ba_ref, ng_ref, o_ref,
                st_scr, b_scr, qd_scr, qh_scr, ki_scr, ke_scr, kp_scr, kh_scr, a_scr, *, n_chunks):
    c_len = GLA_CHUNK
    tc = q_ref.shape[1]
    hb = 4 * c_len
    pw = 2 * c_len
    n_pairs = n_chunks // 2

    @pl.when(pl.program_id(2) == 0)
    def _():
        st_scr[...] = jnp.zeros_like(st_scr)

    ti = lax.broadcasted_iota(jnp.int32, (hb, hb), 0)
    tj = lax.broadcasted_iota(jnp.int32, (hb, hb), 1)
    tri = jnp.logical_and(ti >= tj, ti // c_len == tj // c_len).astype(BF16)
    for s in range(tc // hb):
        rows = slice(s * hb, (s + 1) * hb)
        logits = jnp.dot(lr_ref[0, rows, :].astype(BF16), wa_ref[...],
                         preferred_element_type=F32) + ba_ref[...]
        log_a = (jnp.minimum(logits, 0.0) - jnp.log(1.0 + jnp.exp(-jnp.abs(logits)))) / GLA_TAU
        p1 = log_a.astype(BF16)
        r1 = log_a - p1.astype(F32)
        p2 = r1.astype(BF16)
        p3 = (r1 - p2.astype(F32)).astype(BF16)
        b_scr[rows, :] = (jnp.dot(tri, p1, preferred_element_type=F32)
                          + jnp.dot(tri, p2, preferred_element_type=F32)
                          + jnp.dot(tri, p3, preferred_element_type=F32))

    nt = (((1,), (1,)), ((), ()))

    def chunk_rows(c):
        return slice(c * c_len, (c + 1) * c_len)

    def pair_rows(p):
        return slice(p * pw, (p + 1) * pw)

    b_last = [b_scr[(c + 1) * c_len - 1:(c + 1) * c_len, :] for c in range(n_chunks)]
    h_log = [jnp.zeros_like(b_last[0])]
    for c in range(n_chunks):
        h_log.append(h_log[-1] + b_last[c])

    for c in range(n_chunks):
        rows = chunk_rows(c)
        b = b_scr[rows, :]
        q = q_ref[0, rows, :].astype(F32)
        k = k_ref[0, rows, :].astype(F32)
        qd = q * jnp.exp(b) * (GLA_DK ** -0.5)
        ke = k * jnp.exp(b_last[c] - b)
        qd_scr[rows, :] = qd
        qh_scr[rows, :] = (qd * jnp.exp(h_log[c])).astype(BF16)
        ki_scr[rows, :] = (k * jnp.exp(-b)).astype(BF16)
        ke_scr[rows, :] = ke.astype(BF16)
        kp_scr[rows, :] = (ke * jnp.exp(b_last[c + 1]) if c % 2 == 0 else ke).astype(BF16)
        kh_scr[rows, :] = (ke * jnp.exp(h_log[n_chunks] - h_log[c + 1])).astype(BF16)

    pi = lax.broadcasted_iota(jnp.int32, (pw, pw), 0)
    pj = lax.broadcasted_iota(jnp.int32, (pw, pw), 1)
    diag = jnp.logical_and(pi >= pj, pi // c_len == pj // c_len)
    lower = jnp.logical_and(pi >= c_len, pj < c_len)
    for p in range(n_pairs):
        first = chunk_rows(2 * p)
        keys = jnp.concatenate([ki_scr[pair_rows(p), :], ke_scr[first, :], ke_scr[first, :]], axis=0)
        s = lax.dot_general(qd_scr[pair_rows(p), :].astype(BF16), keys, nt,
                            preferred_element_type=F32)
        s = jnp.where(diag, s[:, :pw], jnp.where(lower, s[:, pw:], 0.0))
        a_scr[pair_rows(p), pair_rows(p)] = s.astype(BF16)

    for p in range(n_pairs - 1):
        c0 = 2 * p + 2
        lhs = jnp.concatenate(
            [(qd_scr[chunk_rows(c), :] * jnp.exp(h_log[c] - h_log[c0])).astype(BF16)
             for c in range(c0, n_chunks)], axis=0)
        s = lax.dot_general(lhs, kp_scr[pair_rows(p), :], nt, preferred_element_type=F32)
        a_scr[c0 * c_len:, pair_rows(p)] = s.astype(BF16)

    half = 2 * pw
    for hf in range(tc // half):
        a_scr[hf * half:hf * half + pw, hf * half + pw:(hf + 1) * half] = jnp.zeros((pw, pw), BF16)

    st = st_scr[...]
    st_b = st.astype(BF16)
    for hf in range(tc // half):
        rows = slice(hf * half, (hf + 1) * half)
        keys = (hf + 1) * half
        o = jnp.dot(a_scr[rows, :keys], v_ref[0, :keys, :], preferred_element_type=F32)
        o += jnp.dot(qh_scr[rows, :], st_b, preferred_element_type=F32)
        o = o * lax.rsqrt(jnp.mean(o * o, axis=-1, keepdims=True) + LN_EPS) * ng_ref[0]
        r = r_ref[0, rows, :].astype(F32)
        o_ref[0, rows, :] = (o * (r * _sigmoid(r))).astype(BF16)

    d_col = jnp.broadcast_to(jnp.exp(h_log[n_chunks]), (SUBLANES_F32, GLA_DK)).T[:, :1]
    st_scr[...] = st * d_col + lax.dot_general(
        kh_scr[...], v_ref[0], (((0,), (0,)), ((), ())), preferred_element_type=F32)


def _gla(p3, lr3, w_a2, b_a, norm_g3, tc=1024):
    bsz, seq, _ = p3.shape
    nq, nv, nr = OFF_GK // GLA_DK, OFF_GV // GLA_DV, OFF_GR // GLA_DV
    return pl.pallas_call(
        functools.partial(_gla_kernel, n_chunks=tc // GLA_CHUNK),
        grid=(bsz, GLA_HEADS, seq // tc),
        in_specs=[
            pl.BlockSpec((1, tc, GLA_DK), lambda b, h, t: (b, t, h)),
            pl.BlockSpec((1, tc, GLA_DK), lambda b, h, t: (b, t, nq + h)),
            pl.BlockSpec((1, tc, GLA_DV), lambda b, h, t: (b, t, nv + h)),
            pl.BlockSpec((1, tc, GLA_DV), lambda b, h, t: (b, t, nr + h)),
            pl.BlockSpec((1, tc, LR_PAD), lambda b, h, t: (b, t, 0)),
            pl.BlockSpec((LR_PAD, GLA_DK), lambda b, h, t: (0, h)),
            pl.BlockSpec((1, GLA_DK), lambda b, h, t: (0, h)),
            pl.BlockSpec((1, 1, GLA_DV), lambda b, h, t: (h, 0, 0)),
        ],
        out_specs=pl.BlockSpec((1, tc, GLA_DV), lambda b, h, t: (b, t, h)),
        out_shape=jax.ShapeDtypeStruct((bsz, seq, GLA_VW), BF16),
        scratch_shapes=[pltpu.VMEM((GLA_DK, GLA_DV), F32), pltpu.VMEM((tc, GLA_DK), F32),
                        pltpu.VMEM((tc, GLA_DK), F32)]
                       + [pltpu.VMEM((tc, GLA_DK), BF16)] * 5 + [pltpu.VMEM((tc, tc), BF16)],
        compiler_params=_cparams(("parallel", "parallel", "arbitrary")),
        name="gla",
    )(p3, p3, p3, p3, lr3, w_a2, b_a, norm_g3)


def _dil_kernel(q_ref, k_ref, v_ref, kp_ref, vp_ref, o_ref, lse_ref, s_scr, m_scr, p_scr, *, nblk):
    n = pl.program_id(2)
    blk = DIL_BLOCK
    qi = lax.broadcasted_iota(jnp.int32, (blk, 2 * blk), 0)
    kj = lax.broadcasted_iota(jnp.int32, (blk, 2 * blk), 1)
    band = jnp.logical_or(jnp.logical_and(kj < blk, kj >= qi),
                          jnp.logical_and(kj >= blk, kj - blk <= qi))
    band_first = jnp.logical_and(band, jnp.logical_or(kj >= blk, n > 0))
    nt = (((1,), (1,)), ((), ()))

    def rows(i):
        return slice(i * blk, (i + 1) * blk)

    def cols(h):
        return slice(h * DIL_HEAD_DIM, (h + 1) * DIL_HEAD_DIM)

    for i in range(nblk):
        for h in range(DIL_HEADS):
            q = q_ref[0, 0, rows(i), cols(h)]
            k_prev = kp_ref[0, 0, :, cols(h)] if i == 0 else k_ref[0, 0, rows(i - 1), cols(h)]
            s_p = lax.dot_general(q, k_prev, nt, preferred_element_type=F32)
            s_c = lax.dot_general(q, k_ref[0, 0, rows(i), cols(h)], nt, preferred_element_type=F32)
            s = jnp.concatenate([s_p, s_c], axis=1)
            s_scr[rows(i * DIL_HEADS + h), :] = jnp.where(band_first if i == 0 else band, s, NEG_BIG)

    s_all = s_scr[...]
    m_all = jnp.max(s_all, axis=-1, keepdims=True)
    p_scr[...] = jnp.exp(s_all - m_all).astype(BF16)
    m_scr[...] = jnp.broadcast_to(m_all, m_scr.shape)

    lane = lax.broadcasted_iota(jnp.int32, (blk, LANES), 1)
    ones = jnp.ones((2 * blk, DIL_HEAD_DIM), BF16)
    for i in range(nblk):
        lse_tile = jnp.zeros((blk, LANES), F32)
        for h in range(DIL_HEADS):
            u = i * DIL_HEADS + h
            v_prev = vp_ref[0, 0, :, cols(h)] if i == 0 else v_ref[0, 0, rows(i - 1), cols(h)]
            v_aug = jnp.concatenate(
                [jnp.concatenate([v_prev, v_ref[0, 0, rows(i), cols(h)]], axis=0), ones], axis=1)
            acc = jnp.dot(p_scr[rows(u), :], v_aug, preferred_element_type=F32)
            den = acc[:, DIL_HEAD_DIM:]
            o_ref[0, 0, rows(i), cols(h)] = (acc[:, :DIL_HEAD_DIM] / den).astype(BF16)
            lse_tile = jnp.where(lane == h, m_scr[rows(u), :] + jnp.log(den), lse_tile)
        lse_ref[0, 0, rows(i), :] = lse_tile


def _dil_group(qkv, nblk=8):
    bsz, r, length, _ = qkv.shape
    nb = length // DIL_BLOCK
    nblk = min(nblk, nb)
    assert nb % nblk == 0
    tq = nblk * DIL_BLOCK
    units = nblk * DIL_HEADS * DIL_BLOCK

    def cur_spec(u):
        return pl.BlockSpec((1, 1, tq, DIL_OUT), lambda b, c, n: (b, c, n, u))

    def prev_spec(u):
        return pl.BlockSpec((1, 1, DIL_BLOCK, DIL_OUT),
                            lambda b, c, n: (b, c, jnp.maximum(n * nblk - 1, 0), u))

    return pl.pallas_call(
        functools.partial(_dil_kernel, nblk=nblk),
        grid=(bsz, r, nb // nblk),
        in_specs=[cur_spec(0), cur_spec(1), cur_spec(2), prev_spec(1), prev_spec(2)],
        out_specs=[
            pl.BlockSpec((1, 1, tq, DIL_OUT), lambda b, c, n: (b, c, n, 0)),
            pl.BlockSpec((1, 1, tq, LANES), lambda b, c, n: (b, c, n, 0)),
        ],
        out_shape=[
            jax.ShapeDtypeStruct((bsz, r, length, DIL_OUT), BF16),
            jax.ShapeDtypeStruct((bsz, r, length, LANES), F32),
        ],
        scratch_shapes=[pltpu.VMEM((units, 2 * DIL_BLOCK), F32),
                        pltpu.VMEM((units, LANES), F32),
                        pltpu.VMEM((units, 2 * DIL_BLOCK), BF16)],
        compiler_params=_cparams(("parallel", "parallel", "parallel")),
        name=f"dilattn_r{r}",
    )(qkv, qkv, qkv, qkv, qkv)


def _merge_kernel(oa_ref, o1_ref, o2_ref, o3_ref, l1_ref, l2_ref, l3_ref, ga_ref, gb_ref, x_ref,
                  g2_ref, wa_ref, wb_ref, wo_ref, lng_ref, lnb_ref, out_ref, l_scr, w_scr, ob_scr,
                  *, dilations):
    y_a = jnp.dot(oa_ref[...], wa_ref[...], preferred_element_type=F32)
    o_refs = (o1_ref, o2_ref, o3_ref)
    l_refs = (l1_ref, l2_ref, l3_ref)
    tm = oa_ref.shape[0]

    def tok_rows(c, r):
        return pl.ds(c, tm // r, stride=r) if r > 1 else slice(None)

    for g, r in enumerate(dilations):
        for c in range(r):
            l_scr[g, tok_rows(c, r), :] = l_refs[g][0, c]
    l1, l2, l3 = l_scr[0], l_scr[1], l_scr[2]
    mx = jnp.maximum(jnp.maximum(l1, l2), l3)
    e1, e2, e3 = jnp.exp(l1 - mx), jnp.exp(l2 - mx), jnp.exp(l3 - mx)
    inv = 1.0 / (e1 + e2 + e3)
    w_scr[0], w_scr[1], w_scr[2] = e1 * inv, e2 * inv, e3 * inv
    for g, r in enumerate(dilations):
        for c in range(r):
            rows = tok_rows(c, r)
            wr = w_scr[g, rows, :]
            for h in range(DIL_HEADS):
                sl = slice(h * DIL_HEAD_DIM, (h + 1) * DIL_HEAD_DIM)
                part = wr[:, h:h + 1] * o_refs[g][0, c, :, sl].astype(F32)
                if g == 0:
                    ob_scr[h, rows, :] = part
                else:
                    ob_scr[h, rows, :] += part
    o_b = jnp.concatenate([ob_scr[h].astype(BF16) for h in range(DIL_HEADS)], axis=1)
    y_b = jnp.dot(o_b, wb_ref[...], preferred_element_type=F32)
    merged = _sigmoid(ga_ref[...].astype(F32)) * y_a + _sigmoid(gb_ref[...].astype(F32)) * y_b
    mix = jnp.dot(merged.astype(BF16), wo_ref[...], preferred_element_type=F32)
    out_ref[...] = _deepnorm(x_ref[...], (1.0 / DN_ALPHA) * g2_ref[0] * mix, lng_ref[...], lnb_ref[...])


def _merge(oa2d, o_groups, lse_groups, p2d, x2d, mods3, k_gate, w_a, w_b, w_o, ln_g, ln_b, seq,
           dilations, tm=256):
    m, d = x2d.shape
    tiles_per_seq = seq // tm
    resident = pl.Buffered(1)
    n_g = len(dilations)

    def row_spec(width, col=0):
        return pl.BlockSpec((tm, width), lambda i: (i, col))

    def class_spec(r, width):
        return pl.BlockSpec((1, r, tm // r, width),
                            lambda i: (i // tiles_per_seq, 0, i % tiles_per_seq, 0))

    def const_spec(shape):
        return pl.BlockSpec(shape, lambda i: (0, 0), pipeline_mode=resident)

    return pl.pallas_call(
        functools.partial(_merge_kernel, dilations=dilations),
        grid=(m // tm,),
        in_specs=[
            row_spec(GLA_VW),
            *[class_spec(r, DIL_OUT) for r in dilations],
            *[class_spec(r, LANES) for r in dilations],
            row_spec(D_MODEL, OFF_GA // D_MODEL), row_spec(D_MODEL, OFF_GB // D_MODEL),
            row_spec(d),
            pl.BlockSpec((1, 1, d), lambda i: ((i // tiles_per_seq) * N_MOD + k_gate, 0, 0)),
            const_spec(w_a.shape), const_spec(w_b.shape), const_spec(w_o.shape),
            const_spec((1, d)), const_spec((1, d)),
        ],
        out_specs=row_spec(d),
        out_shape=jax.ShapeDtypeStruct((m, d), F32),
        scratch_shapes=[pltpu.VMEM((n_g, tm, LANES), F32), pltpu.VMEM((n_g, tm, LANES), F32),
                        pltpu.VMEM((DIL_HEADS, tm, DIL_HEAD_DIM), F32)],
        compiler_params=_cparams(("parallel",)),
        name="merge",
    )(oa2d, *o_groups, *lse_groups, p2d, p2d, x2d, mods3, w_a, w_b, w_o, ln_g, ln_b)


PACK_W = 1024
PACK_COPY, PACK_SHIFTED, PACK_ZERO, PACK_LOW_RANK = range(4)


def _pack_plan():
    src, o = {}, 0
    for name, wdt in zip(("gq", "gk", "gv", "gr", "glr", "dq", "dk", "dv", "ga", "gb"),
                         (GLA_QK, GLA_QK, GLA_VW, GLA_VW, GLA_GATE_RANK, DIL_W, DIL_W, DIL_W,
                          D_MODEL, D_MODEL)):
        src[name] = (o, wdt)
        o += wdt
    plan = []

    def copy(first, width):
        for c in range(first, first + width, PACK_W):
            plan.append((c, PACK_COPY if c % PACK_W == 0 else PACK_SHIFTED))

    for name in ("gq", "gk", "gv", "gr", "ga", "gb"):
        copy(*src[name])
    plan += [(0, PACK_ZERO)] * ((OFF_DIL - PROJ_W) // PACK_W)
    for g in range(DIL_GROUPS):
        for name in ("dq", "dk", "dv"):
            copy(src[name][0] + g * DIL_OUT, DIL_OUT)
    plan.append((src["glr"][0], PACK_LOW_RANK))
    return plan


def _pack_kernel(plan_ref, a_ref, b_ref, o_ref):
    kind = plan_ref[2, pl.program_id(0)]
    rank = GLA_GATE_RANK

    @pl.when(kind == PACK_COPY)
    def _():
        o_ref[...] = a_ref[...].T.astype(BF16)

    @pl.when(kind == PACK_SHIFTED)
    def _():
        o_ref[...] = jnp.concatenate([a_ref[rank:, :], b_ref[...]], axis=0).T.astype(BF16)

    @pl.when(kind == PACK_ZERO)
    def _():
        o_ref[...] = jnp.zeros_like(o_ref)

    @pl.when(kind == PACK_LOW_RANK)
    def _():
        lane = lax.broadcasted_iota(jnp.int32, (o_ref.shape[0], LANES), 1)
        o_ref[...] = jnp.zeros_like(o_ref)
        o_ref[:, :LANES] = jnp.where(lane < rank, a_ref[:LANES, :].T, 0.0).astype(BF16)


def _pack_w_in(w_in, layer):
    _, d, width = w_in.shape
    w_t = jnp.swapaxes(w_in, 1, 2)
    plan = _pack_plan()
    first = [c - c % PACK_W for c, _ in plan]
    rank = GLA_GATE_RANK
    assert all(c % PACK_W in (0, rank) for c, _ in plan) and rank % SUBLANES_F32 == 0
    table = jnp.asarray([[c // PACK_W for c in first],
                         [min(c + PACK_W, width - rank) // rank for c in first],
                         [kind for _, kind in plan]], jnp.int32)
    return pl.pallas_call(
        _pack_kernel,
        grid_spec=pltpu.PrefetchScalarGridSpec(
            num_scalar_prefetch=1,
            grid=(len(plan),),
            in_specs=[pl.BlockSpec((None, PACK_W, d), lambda i, t: (layer, t[0, i], 0)),
                      pl.BlockSpec((None, rank, d), lambda i, t: (layer, t[1, i], 0))],
            out_specs=pl.BlockSpec((d, PACK_W), lambda i, t: (0, i)),
        ),
        out_shape=jax.ShapeDtypeStruct((d, len(plan) * PACK_W), BF16),
        compiler_params=_cparams(("arbitrary",)),
        name="pack_w_in",
    )(table, w_t, w_t)


def kernel(x, c, positions, w_ada, b_ada, ln1_g, ln1_b, w_ffn1_gu, w_ffn1_down, w_in, w_alpha2,
           b_alpha, gla_norm_g, w_branch_a, w_branch_b, w_out, ln2_g, ln2_b, w_ffn2_gu, w_ffn2_down,
           ln3_g, ln3_b):
    bsz, seq, d = x.shape
    m = bsz * seq
    x2d = x.reshape(m, d)
    c_pad = jnp.pad(c, ((0, 8 - bsz % 8 if bsz % 8 else 0), (0, 0)))
    dilations = tuple(r for _, r in DIL_PATTERNS)

    half = DIL_HEAD_DIM // 2
    freq = ROPE_THETA ** (-jnp.arange(half, dtype=F32) / half)
    freq2 = jnp.concatenate([freq, freq]).reshape(1, DIL_HEAD_DIM)
    rope_t = _rope_tables(positions.reshape(m // PROJ_TM, PROJ_TM // LANES, LANES), freq2, dilations)

    for l in range(DEPTH):
        mods = _mods(c_pad, w_ada[l], b_ada[l].reshape(1, -1))[:bsz]
        mods3 = mods.reshape(bsz * N_MOD, 1, d)

        x1 = _ffn(x2d, mods3, 0, 1, 2, w_ffn1_gu[l].astype(BF16), w_ffn1_down[l].astype(BF16),
                  ln1_g[l].reshape(1, d), ln1_b[l].reshape(1, d), seq)

        w_all = _pack_w_in(w_in, l)
        p2d, lr2d, h2 = _proj(x1, mods3, 3, 4, w_all, seq)
        p3 = p2d.reshape(bsz, seq, PROJ_W)

        w_a2 = jnp.pad(w_alpha2[l], ((0, LR_PAD - GLA_GATE_RANK), (0, 0))).astype(BF16)
        o_a = _gla(p3, lr2d.reshape(bsz, seq, LR_PAD), w_a2, b_alpha[l].reshape(1, GLA_QK),
                   gla_norm_g[l].reshape(GLA_HEADS, 1, GLA_DV))

        o_groups, lse_groups = [], []
        for g, r in enumerate(dilations):
            qkv = _proj_dil(x1, h2, mods3, 3, 4, w_all, g, rope_t[2 * g], rope_t[2 * g + 1],
                            bsz, seq, r)
            o_g, lse_g = _dil_group(qkv)
            o_groups.append(o_g)
            lse_groups.append(lse_g)

        x2 = _merge(o_a.reshape(m, GLA_VW), o_groups, lse_groups, p2d, x1, mods3, 5,
                    w_branch_a[l].astype(BF16), w_branch_b[l].astype(BF16), w_out[l].astype(BF16),
                    ln2_g[l].reshape(1, d), ln2_b[l].reshape(1, d), seq, dilations)

        x2d = _ffn(x2, mods3, 6, 7, 8, w_ffn2_gu[l].astype(BF16), w_ffn2_down[l].astype(BF16),
                   ln3_g[l].reshape(1, d), ln3_b[l].reshape(1, d), seq)
    return x2d.reshape(bsz, seq, d)
```
